```python
import jax, jax.numpy as jnp
from jax import lax
import numpy as np

D_MODEL = 1024
BATCH = 8
SEQ = 4096
DEPTH = 1

EXPAND = 2
D_MIX = EXPAND * D_MODEL
D_CONV = D_MIX // 2
D_GMLP = D_MIX - D_CONV
CONV_GROUPS = 8
CONV_GROUP_DIM = D_CONV // CONV_GROUPS
GMLP_HEADS = 8
GMLP_HEAD_DIM = D_GMLP // GMLP_HEADS
CONV_WIDTH = 31
CONV_HALF = CONV_WIDTH // 2
CHUNK = 128
EPS = 1e-6
D_IN = 3 * D_CONV + 3 * D_GMLP
SPLITS = (D_CONV, 2 * D_CONV, 3 * D_CONV, 3 * D_CONV + D_GMLP, 3 * D_CONV + 2 * D_GMLP)

kernel_name = "hybrid_conv_gmlp_adaln_encoder"


def rms_norm(x, g):
    xf = x.astype(jnp.float32)
    xf = xf * lax.rsqrt(jnp.mean(xf * xf, axis=-1, keepdims=True) + EPS)
    return (xf * g.astype(jnp.float32)).astype(x.dtype)


def layer_norm(x, g, b):
    xf = x.astype(jnp.float32)
    mu = jnp.mean(xf, axis=-1, keepdims=True)
    var = jnp.mean(jnp.square(xf - mu), axis=-1, keepdims=True)
    y = (xf - mu) * lax.rsqrt(var + EPS) * g.astype(jnp.float32) + b.astype(jnp.float32)
    return y.astype(x.dtype)


def _fwd_setup_inputs(seed: int = 0) -> dict:
    key = jax.random.key(seed)
    ks = jax.random.split(key, 16)
    L = DEPTH
    nrm = jax.random.normal
    return {
        "x": nrm(ks[0], (BATCH, SEQ, D_MODEL), jnp.float32),
        "c": nrm(ks[1], (BATCH, D_MODEL), jnp.float32),
        "w_ada": nrm(ks[2], (L, D_MODEL, 3 * D_MODEL), jnp.float32) * D_MODEL ** -0.5,
        "b_ada": nrm(ks[3], (L, 3 * D_MODEL), jnp.float32) * 0.02,
        "norm_g": 1.0 + 0.02 * nrm(ks[4], (L, D_MODEL), jnp.float32),
        "w_in": nrm(ks[5], (L, D_MODEL, D_IN), jnp.float32) * D_MODEL ** -0.5,
        "conv_w": nrm(ks[6], (L, CONV_WIDTH, 1, D_CONV), jnp.float32) * CONV_WIDTH ** -0.5,
        "conv_b": nrm(ks[7], (L, D_CONV), jnp.float32) * 0.02,
        "conv_ln_g": 1.0 + 0.02 * nrm(ks[8], (L, D_CONV), jnp.float32),
        "conv_ln_b": nrm(ks[9], (L, D_CONV), jnp.float32) * 0.02,
        "sg_ln_g": 1.0 + 0.02 * nrm(ks[10], (L, D_GMLP), jnp.float32),
        "sg_ln_b": nrm(ks[11], (L, D_GMLP), jnp.float32) * 0.02,
        "w_s": nrm(ks[12], (L, GMLP_HEADS, CHUNK, CHUNK), jnp.float32) * CHUNK ** -0.5,
        "b_s": 1.0 + 0.02 * nrm(ks[13], (L, GMLP_HEADS, CHUNK), jnp.float32),
        "w_out": nrm(ks[14], (L, D_MIX, D_MODEL), jnp.float32) * D_MIX ** -0.5,
        "final_g": 1.0 + 0.02 * nrm(ks[15], (D_MODEL,), jnp.float32),
    }


def conv_group(a, a_glu, a_gate, conv_w, conv_b, ln_g, ln_b):
    a = a * jax.nn.sigmoid(a_glu)
    a = lax.conv_general_dilated(
        a, conv_w.astype(a.dtype), window_strides=(1,),
        padding=[(CONV_HALF, CONV_HALF)],
        dimension_numbers=("NWC", "WIO", "NWC"),
        feature_group_count=D_CONV) + conv_b
    a = jax.nn.silu(layer_norm(a, ln_g, ln_b))
    return a * jax.nn.silu(a_gate)


def gmlp_group(u, v, b_gate, ln_g, ln_b, w_s, b_s):
    B, S, _ = v.shape
    v = layer_norm(v, ln_g, ln_b)
    v = v.reshape(B, S // CHUNK, CHUNK, GMLP_HEADS, GMLP_HEAD_DIM)
    v = jnp.einsum("hpq,bnqhd->bnphd", w_s.astype(v.dtype), v) \
        + jnp.transpose(b_s)[None, None, :, :, None]
    v = v.reshape(B, S, D_GMLP)
    return u * v * jax.nn.silu(b_gate)


def _fwd_reference(x, c, w_ada, b_ada, norm_g, w_in, conv_w, conv_b, conv_ln_g, conv_ln_b,
              sg_ln_g, sg_ln_b, w_s, b_s, w_out, final_g):
    c_act = jax.nn.silu(c)
    for l in range(DEPTH):
        mod = jnp.einsum("bd,de->be", c_act, w_ada[l]) + b_ada[l]
        shift, scale, gate = jnp.split(mod, 3, axis=-1)
        h = rms_norm(x, norm_g[l]) * (1.0 + scale[:, None, :]) + shift[:, None, :]
        z = jnp.einsum("bsd,de->bse", h, w_in[l])
        a, a_glu, a_gate, u, v, b_gate = jnp.split(z, SPLITS, axis=-1)
        y_a = conv_group(a, a_glu, a_gate, conv_w[l], conv_b[l], conv_ln_g[l], conv_ln_b[l])
        y_b = gmlp_group(u, v, b_gate, sg_ln_g[l], sg_ln_b[l], w_s[l], b_s[l])
        y = jnp.einsum("bse,ed->bsd", jnp.concatenate([y_a, y_b], axis=-1), w_out[l])
        x = x + gate[:, None, :] * y
    return rms_norm(x, final_g)


import jax as _jax
import jax.numpy as _jnp

TWIN_FORMAT = 'train_step'
FWD_PARAMS = ['x', 'c', 'w_ada', 'b_ada', 'norm_g', 'w_in', 'conv_w', 'conv_b', 'conv_ln_g', 'conv_ln_b', 'sg_ln_g', 'sg_ln_b', 'w_s', 'b_s', 'w_out', 'final_g']
TWIN_WEIGHTS = ['w_ada', 'b_ada', 'norm_g', 'w_in', 'conv_w', 'conv_b', 'conv_ln_g', 'conv_ln_b', 'sg_ln_g', 'sg_ln_b', 'w_s', 'b_s', 'w_out', 'final_g']
TWIN_DIFF_INPUT = 'x'
TWIN_INPUTS = ['x', 'c', 'w_ada', 'b_ada', 'norm_g', 'w_in', 'conv_w', 'conv_b', 'conv_ln_g', 'conv_ln_b', 'sg_ln_g', 'sg_ln_b', 'w_s', 'b_s', 'w_out', 'final_g', 'loss_target', 'm_w_ada', 'm_b_ada', 'm_norm_g', 'm_w_in', 'm_conv_w', 'm_conv_b', 'm_conv_ln_g', 'm_conv_ln_b', 'm_sg_ln_g', 'm_sg_ln_b', 'm_w_s', 'm_b_s', 'm_w_out', 'm_final_g', 'v_w_ada', 'v_b_ada', 'v_norm_g', 'v_w_in', 'v_conv_w', 'v_conv_b', 'v_conv_ln_g', 'v_conv_ln_b', 'v_sg_ln_g', 'v_sg_ln_b', 'v_w_s', 'v_b_s', 'v_w_out', 'v_final_g']
TWIN_OUTPUTS = ['loss', 'grad_x', 'grad_w_ada', 'grad_b_ada', 'grad_norm_g', 'grad_w_in', 'grad_conv_w', 'grad_conv_b', 'grad_conv_ln_g', 'grad_conv_ln_b', 'grad_sg_ln_g', 'grad_sg_ln_b', 'grad_w_s', 'grad_b_s', 'grad_w_out', 'grad_final_g', 'delta_w_ada', 'delta_b_ada', 'delta_norm_g', 'delta_w_in', 'delta_conv_w', 'delta_conv_b', 'delta_conv_ln_g', 'delta_conv_ln_b', 'delta_sg_ln_g', 'delta_sg_ln_b', 'delta_w_s', 'delta_b_s', 'delta_w_out', 'delta_final_g', 'new_m_w_ada', 'new_m_b_ada', 'new_m_norm_g', 'new_m_w_in', 'new_m_conv_w', 'new_m_conv_b', 'new_m_conv_ln_g', 'new_m_conv_ln_b', 'new_m_sg_ln_g', 'new_m_sg_ln_b', 'new_m_w_s', 'new_m_b_s', 'new_m_w_out', 'new_m_final_g', 'new_v_w_ada', 'new_v_b_ada', 'new_v_norm_g', 'new_v_w_in', 'new_v_conv_w', 'new_v_conv_b', 'new_v_conv_ln_g', 'new_v_conv_ln_b', 'new_v_sg_ln_g', 'new_v_sg_ln_b', 'new_v_w_s', 'new_v_b_s', 'new_v_w_out', 'new_v_final_g']
TWIN_LEAF_KINDS = {'loss': 'loss', 'grad_x': 'grad_x', 'grad_w_ada': 'grad_w', 'grad_b_ada': 'grad_w', 'grad_norm_g': 'grad_w', 'grad_w_in': 'grad_w', 'grad_conv_w': 'grad_w', 'grad_conv_b': 'grad_w', 'grad_conv_ln_g': 'grad_w', 'grad_conv_ln_b': 'grad_w', 'grad_sg_ln_g': 'grad_w', 'grad_sg_ln_b': 'grad_w', 'grad_w_s': 'grad_w', 'grad_b_s': 'grad_w', 'grad_w_out': 'grad_w', 'grad_final_g': 'grad_w', 'delta_w_ada': 'delta_w', 'delta_b_ada': 'delta_w', 'delta_norm_g': 'delta_w', 'delta_w_in': 'delta_w', 'delta_conv_w': 'delta_w', 'delta_conv_b': 'delta_w', 'delta_conv_ln_g': 'delta_w', 'delta_conv_ln_b': 'delta_w', 'delta_sg_ln_g': 'delta_w', 'delta_sg_ln_b': 'delta_w', 'delta_w_s': 'delta_w', 'delta_b_s': 'delta_w', 'delta_w_out': 'delta_w', 'delta_final_g': 'delta_w', 'new_m_w_ada': 'new_m', 'new_m_b_ada': 'new_m', 'new_m_norm_g': 'new_m', 'new_m_w_in': 'new_m', 'new_m_conv_w': 'new_m', 'new_m_conv_b': 'new_m', 'new_m_conv_ln_g': 'new_m', 'new_m_conv_ln_b': 'new_m', 'new_m_sg_ln_g': 'new_m', 'new_m_sg_ln_b': 'new_m', 'new_m_w_s': 'new_m', 'new_m_b_s': 'new_m', 'new_m_w_out': 'new_m', 'new_m_final_g': 'new_m', 'new_v_w_ada': 'new_v', 'new_v_b_ada': 'new_v', 'new_v_norm_g': 'new_v', 'new_v_w_in': 'new_v', 'new_v_conv_w': 'new_v', 'new_v_conv_b': 'new_v', 'new_v_conv_ln_g': 'new_v', 'new_v_conv_ln_b': 'new_v', 'new_v_sg_ln_g': 'new_v', 'new_v_sg_ln_b': 'new_v', 'new_v_w_s': 'new_v', 'new_v_b_s': 'new_v', 'new_v_w_out': 'new_v', 'new_v_final_g': 'new_v'}


def _forward(args):
    return _fwd_reference(*[args[k] for k in FWD_PARAMS])


def _output_shape():
    out = _jax.eval_shape(lambda: _forward(_fwd_setup_inputs(0)))
    return out.shape, out.dtype

N_MICROBATCH = 1
ADAM_LR = 0.001
ADAM_B1 = 0.9
ADAM_B2 = 0.999
ADAM_EPS = 1e-08
ADAM_WD = 0.01
ADAM_STEP = 10
PER_EXAMPLE_BATCH_AXIS = {'x': 0, 'c': 0, 'loss_target': 0}
SHARED_INPUTS = []
_WEIGHT_DTYPES = {'w_ada': _jnp.float32, 'b_ada': _jnp.float32, 'norm_g': _jnp.float32, 'w_in': _jnp.float32, 'conv_w': _jnp.float32, 'conv_b': _jnp.float32, 'conv_ln_g': _jnp.float32, 'conv_ln_b': _jnp.float32, 'sg_ln_g': _jnp.float32, 'sg_ln_b': _jnp.float32, 'w_s': _jnp.float32, 'b_s': _jnp.float32, 'w_out': _jnp.float32, 'final_g': _jnp.float32}
MOMENT_SCALE = {'w_ada': 1.143819e-01, 'b_ada': 2.078001e-01, 'norm_g': 1.467693e-01, 'w_in': 6.893904e-02, 'conv_w': 3.389688e-02, 'conv_b': 4.930119e-02, 'conv_ln_g': 3.947191e-02, 'conv_ln_b': 3.729092e-02, 'sg_ln_g': 8.076762e-02, 'sg_ln_b': 7.612168e-02, 'w_s': 7.151820e-02, 'b_s': 7.082810e-02, 'w_out': 1.047647e-01, 'final_g': 3.295085e+01}


def _to_microbatches(a, axis):
    t = _jnp.moveaxis(a, axis, 0)
    t = t.reshape((N_MICROBATCH, t.shape[0] // N_MICROBATCH) + t.shape[1:])
    return _jnp.moveaxis(t, 1, axis + 1)


def setup_inputs(seed: int = 0) -> dict:
    inp = _fwd_setup_inputs(seed)
    key = _jax.random.fold_in(_jax.random.key(seed), 7919)
    shape, _ = _output_shape()
    out = dict(inp)
    out["loss_target"] = _jax.random.normal(_jax.random.fold_in(key, 0), shape, _jnp.float32)
    for i, name in enumerate(TWIN_WEIGHTS):
        w = inp[name].astype(_jnp.float32)
        if MOMENT_SCALE is None:
            s = _jnp.sqrt(_jnp.mean(_jnp.square(w)) + 1e-30)
        else:
            s = MOMENT_SCALE[name]
        km, kv = _jax.random.split(_jax.random.fold_in(key, i + 1))
        out[name] = w
        out["m_" + name] = s * _jax.random.normal(km, w.shape, _jnp.float32)
        out["v_" + name] = (s * s) * _jax.random.uniform(kv, w.shape, _jnp.float32, 0.5, 1.5)
    if N_MICROBATCH > 1:
        for name, axis in PER_EXAMPLE_BATCH_AXIS.items():
            out[name] = _to_microbatches(out[name], axis)
    return {'x': out['x'], 'c': out['c'], 'w_ada': out['w_ada'], 'b_ada': out['b_ada'], 'norm_g': out['norm_g'], 'w_in': out['w_in'], 'conv_w': out['conv_w'], 'conv_b': out['conv_b'], 'conv_ln_g': out['conv_ln_g'], 'conv_ln_b': out['conv_ln_b'], 'sg_ln_g': out['sg_ln_g'], 'sg_ln_b': out['sg_ln_b'], 'w_s': out['w_s'], 'b_s': out['b_s'], 'w_out': out['w_out'], 'final_g': out['final_g'], 'loss_target': out['loss_target'], 'm_w_ada': out['m_w_ada'], 'm_b_ada': out['m_b_ada'], 'm_norm_g': out['m_norm_g'], 'm_w_in': out['m_w_in'], 'm_conv_w': out['m_conv_w'], 'm_conv_b': out['m_conv_b'], 'm_conv_ln_g': out['m_conv_ln_g'], 'm_conv_ln_b': out['m_conv_ln_b'], 'm_sg_ln_g': out['m_sg_ln_g'], 'm_sg_ln_b': out['m_sg_ln_b'], 'm_w_s': out['m_w_s'], 'm_b_s': out['m_b_s'], 'm_w_out': out['m_w_out'], 'm_final_g': out['m_final_g'], 'v_w_ada': out['v_w_ada'], 'v_b_ada': out['v_b_ada'], 'v_norm_g': out['v_norm_g'], 'v_w_in': out['v_w_in'], 'v_conv_w': out['v_conv_w'], 'v_conv_b': out['v_conv_b'], 'v_conv_ln_g': out['v_conv_ln_g'], 'v_conv_ln_b': out['v_conv_ln_b'], 'v_sg_ln_g': out['v_sg_ln_g'], 'v_sg_ln_b': out['v_sg_ln_b'], 'v_w_s': out['v_w_s'], 'v_b_s': out['v_b_s'], 'v_w_out': out['v_w_out'], 'v_final_g': out['v_final_g']}


def _loss(weights, diff, rest, loss_target):
    with _jax.named_scope("forward"):
        args = {**rest, TWIN_DIFF_INPUT: diff, **{k: w.astype(_WEIGHT_DTYPES[k]) for k, w in weights.items()}}
        y = _forward(args)
    with _jax.named_scope("loss_head"):
        err = _jnp.square(y.astype(_jnp.float32) - loss_target)
        return 0.5 * _jnp.sum(_jnp.mean(err, axis=-1)) if err.ndim else 0.5 * err


def _adamw(w, g, m, v):
    m = ADAM_B1 * m + (1.0 - ADAM_B1) * g
    v = ADAM_B2 * v + (1.0 - ADAM_B2) * _jnp.square(g)
    m_hat = m / (1.0 - ADAM_B1 ** ADAM_STEP)
    v_hat = v / (1.0 - ADAM_B2 ** ADAM_STEP)
    delta = -ADAM_LR * (m_hat / (_jnp.sqrt(v_hat) + ADAM_EPS) + ADAM_WD * w)
    return delta, m, v


def reference(x, c, w_ada, b_ada, norm_g, w_in, conv_w, conv_b, conv_ln_g, conv_ln_b, sg_ln_g, sg_ln_b, w_s, b_s, w_out, final_g, loss_target, m_w_ada, m_b_ada, m_norm_g, m_w_in, m_conv_w, m_conv_b, m_conv_ln_g, m_conv_ln_b, m_sg_ln_g, m_sg_ln_b, m_w_s, m_b_s, m_w_out, m_final_g, v_w_ada, v_b_ada, v_norm_g, v_w_in, v_conv_w, v_conv_b, v_conv_ln_g, v_conv_ln_b, v_sg_ln_g, v_sg_ln_b, v_w_s, v_b_s, v_w_out, v_final_g):
    given = dict(x=x, c=c, w_ada=w_ada, b_ada=b_ada, norm_g=norm_g, w_in=w_in, conv_w=conv_w, conv_b=conv_b, conv_ln_g=conv_ln_g, conv_ln_b=conv_ln_b, sg_ln_g=sg_ln_g, sg_ln_b=sg_ln_b, w_s=w_s, b_s=b_s, w_out=w_out, final_g=final_g, loss_target=loss_target, m_w_ada=m_w_ada, m_b_ada=m_b_ada, m_norm_g=m_norm_g, m_w_in=m_w_in, m_conv_w=m_conv_w, m_conv_b=m_conv_b, m_conv_ln_g=m_conv_ln_g, m_conv_ln_b=m_conv_ln_b, m_sg_ln_g=m_sg_ln_g, m_sg_ln_b=m_sg_ln_b, m_w_s=m_w_s, m_b_s=m_b_s, m_w_out=m_w_out, m_final_g=m_final_g, v_w_ada=v_w_ada, v_b_ada=v_b_ada, v_norm_g=v_norm_g, v_w_in=v_w_in, v_conv_w=v_conv_w, v_conv_b=v_conv_b, v_conv_ln_g=v_conv_ln_g, v_conv_ln_b=v_conv_ln_b, v_sg_ln_g=v_sg_ln_g, v_sg_ln_b=v_sg_ln_b, v_w_s=v_w_s, v_b_s=v_b_s, v_w_out=v_w_out, v_final_g=v_final_g)
    weights = {n: given[n] for n in TWIN_WEIGHTS}
    shared = {n: given[n] for n in SHARED_INPUTS}
    per_example = {n: given[n] for n in ['x', 'c']}
    grad_fn = _jax.value_and_grad(_loss, argnums=(0, 1))

    def one_microbatch(ex, loss_target):
        ex = dict(ex)
        diff = ex.pop(TWIN_DIFF_INPUT)
        return grad_fn(weights, diff, {**shared, **ex}, loss_target)

    if N_MICROBATCH == 1:
        loss, (grad_w, grad_x) = one_microbatch(per_example, given["loss_target"])
    else:
        def body(carry, xs):
            loss_sum, grad_sum = carry
            l_k, (gw_k, gx_k) = one_microbatch(xs[0], xs[1])
            with _jax.named_scope("update"):
                return (loss_sum + l_k, _jax.tree.map(_jnp.add, grad_sum, gw_k)), gx_k

        init = (_jnp.zeros((), _jnp.float32), _jax.tree.map(_jnp.zeros_like, weights))
        (loss, grad_w), grad_x = _jax.lax.scan(body, init, (per_example, given["loss_target"]))
    with _jax.named_scope("update"):
        delta_w, new_m, new_v = {}, {}, {}
        for n in TWIN_WEIGHTS:
            delta_w[n], new_m[n], new_v[n] = _adamw(weights[n], grad_w[n], given["m_" + n], given["v_" + n])
    return (loss, grad_x, *[grad_w[n] for n in TWIN_WEIGHTS], *[delta_w[n] for n in TWIN_WEIGHTS],
            *[new_m[n] for n in TWIN_WEIGHTS], *[new_v[n] for n in TWIN_WEIGHTS])
```

```python
import functools

import jax
import jax.numpy as jnp
from jax import lax
from jax.experimental import pallas as pl
from jax.experimental.pallas import tpu as pltpu

F32 = jnp.float32
BF16 = jnp.bfloat16
MESH = pl.DeviceIdType.MESH

D = 1024
D_IN = 6 * D
D_MIX = 2 * D
N_DEV = 8
W_IN_BLK = D_IN // N_DEV
W_OUT_BLK = D_MIX // N_DEV
W_ADA_BLK = 3 * D // N_DEV
CONV_BLK = D // N_DEV
CONV_W = 31
CONV_HALF = CONV_W // 2
CONV_ROWS = 32
HALO = 16
CHUNK = 128
HEADS = 8
HEAD_DIM = 128
EPS = 1e-6
ROW_TILE = 256
SMALL_ROWS = 40
VMEM_LIMIT = 56 * 1024 * 1024

ADAM_LR = 0.001
ADAM_B1 = 0.9
ADAM_B2 = 0.999
ADAM_EPS = 1e-08
ADAM_WD = 0.01
ADAM_STEP = 10

VM = pl.BlockSpec(memory_space=pltpu.VMEM)


def _params(grid_rank=0, **kw):
    sem = ("arbitrary",) * grid_rank if grid_rank else None
    return pltpu.CompilerParams(dimension_semantics=sem, vmem_limit_bytes=VMEM_LIMIT, **kw)


def _sigmoid(t):
    return jax.nn.sigmoid(t)


def _dsilu(t, sig):
    return sig * (1.0 + t * (1.0 - sig))


def _mesh_pos():
    return lax.axis_index("x"), lax.axis_index("y"), lax.axis_index("c")


def _allgather_blocks(bufs, send_sems, recv_sems):
    x, y, c = _mesh_pos()
    me, sib = (x, y, c), (x, y, 1 - c)
    chips = [(x, 1 - y), (1 - x, y), (1 - x, 1 - y)]
    n = len(bufs)

    def copy(a, k, block, to):
        px, py, pc = block
        ref = bufs[a].at[4 * px + 2 * py + pc]
        return pltpu.make_async_remote_copy(
            src_ref=ref, dst_ref=ref, send_sem=send_sems.at[7 * a + k], recv_sem=recv_sems.at[7 * a + k],
            device_id=to, device_id_type=MESH)

    first = []
    for a in range(n):
        first.append(copy(a, 0, me, sib))
        first += [copy(a, 1 + j, me, (*chip, c)) for j, chip in enumerate(chips)]
    for cp in first:
        cp.start()
    passed = []
    for j, chip in enumerate(chips):
        for a in range(n):
            copy(a, 1 + j, (*chip, c), me).wait_recv()
            fwd = copy(a, 4 + j, (*chip, c), sib)
            fwd.start()
            passed.append(fwd)
    for a in range(n):
        copy(a, 0, sib, me).wait_recv()
    for j, chip in enumerate(chips):
        for a in range(n):
            copy(a, 4 + j, (*chip, 1 - c), me).wait_recv()
    for cp in first + passed:
        cp.wait_send()


def _my_block():
    x, y, c = _mesh_pos()
    return 4 * x + 2 * y + c


def _gather_params(w_in, w_out, conv_w, c_rep):
    def body(w_in_ref, w_out_ref, cw_ref, c_ref, win_g, wout_g, cw_g, c_g, send_sems, recv_sems):
        me = _my_block()
        win_g[me] = w_in_ref[...].astype(BF16)
        wout_g[me] = w_out_ref[...].astype(BF16)
        cw_g[me] = cw_ref[...]
        c_g[me] = c_ref[...]
        _allgather_blocks([c_g, cw_g, wout_g, win_g], send_sems, recv_sems)

    return pl.pallas_call(
        body, name="gather_params",
        out_shape=(jax.ShapeDtypeStruct((N_DEV,) + w_in.shape, BF16),
                   jax.ShapeDtypeStruct((N_DEV,) + w_out.shape, BF16),
                   jax.ShapeDtypeStruct((N_DEV,) + conv_w.shape, F32),
                   jax.ShapeDtypeStruct((N_DEV,) + c_rep.shape, F32)),
        in_specs=[VM] * 4, out_specs=(VM,) * 4,
        scratch_shapes=[pltpu.SemaphoreType.DMA((28,)), pltpu.SemaphoreType.DMA((28,))],
        compiler_params=_params(),
    )(w_in, w_out, conv_w, c_rep)


def _mod_exchange(c_all, w_ada, b_ada):
    def body(c_ref, w_ref, b_ref, mod_ref, part, land, send_sems, recv_sems):
        x, y, c = _mesh_pos()
        me = 4 * x + 2 * y + c
        w = w_ref[...]
        for b in range(N_DEV):
            cb = c_ref[b]
            act = cb * _sigmoid(cb)
            part[b] = jnp.dot(act, w, preferred_element_type=F32, precision=lax.Precision.HIGHEST)
        land[me] = part[me]

        def copy(b):
            return pltpu.make_async_remote_copy(
                src_ref=part.at[b], dst_ref=land.at[me], send_sem=send_sems.at[b], recv_sem=recv_sems.at[me],
                device_id=(b // 4, (b // 2) % 2, b % 2), device_id_type=MESH)

        def arrival(b):
            return pltpu.make_async_remote_copy(
                src_ref=part.at[b], dst_ref=land.at[b], send_sem=send_sems.at[b], recv_sem=recv_sems.at[b],
                device_id=(b // 4, (b // 2) % 2, b % 2), device_id_type=MESH)

        for b in range(N_DEV):
            @pl.when(b != me)
            def _():
                copy(b).start()
        for b in range(N_DEV):
            @pl.when(b != me)
            def _():
                arrival(b).wait_recv()
                copy(b).wait_send()
        for b in range(N_DEV):
            cols = slice(b * W_ADA_BLK, (b + 1) * W_ADA_BLK)
            mod_ref[:, cols] = land[b] + b_ref[:, cols]

    return pl.pallas_call(
        body, name="mod_exchange",
        out_shape=jax.ShapeDtypeStruct((8, 3 * D), F32),
        in_specs=[VM] * 3, out_specs=VM,
        scratch_shapes=[pltpu.VMEM((N_DEV, 8, W_ADA_BLK), F32), pltpu.VMEM((N_DEV, 8, W_ADA_BLK), F32),
                        pltpu.SemaphoreType.DMA((N_DEV,)), pltpu.SemaphoreType.DMA((N_DEV,))],
        compiler_params=_params(),
    )(c_all, w_ada, b_ada)


def _rms_modulate(x, mod_ref):
    shift = mod_ref[0:1, 0:D]
    scale = mod_ref[0:1, D:2 * D]
    r = lax.rsqrt(jnp.mean(x * x, axis=-1, keepdims=True) + EPS)
    xn = x * r
    return xn, r, shift, scale


def _fwd_in_proj(x, mod, norm_g, win_g):
    s = x.shape[0]
    t = ROW_TILE

    def body(x_ref, mod_ref, ng_ref, w_ref, z_ref, ht_ref):
        xn, _, shift, scale = _rms_modulate(x_ref[...], mod_ref)
        h = xn * ng_ref[...] * (1.0 + scale) + shift
        hb = h.astype(BF16)
        ht_ref[...] = h.T.astype(BF16)
        for j in range(N_DEV):
            z_ref[:, j * W_IN_BLK:(j + 1) * W_IN_BLK] = jnp.dot(hb, w_ref[j], preferred_element_type=F32)

    return pl.pallas_call(
        body, name="fwd_in_proj", grid=(s // t,),
        out_shape=(jax.ShapeDtypeStruct((s, D_IN), F32), jax.ShapeDtypeStruct((D, s), BF16)),
        in_specs=[pl.BlockSpec((t, D), lambda i: (i, 0)), VM, VM, VM],
        out_specs=(pl.BlockSpec((t, D_IN), lambda i: (i, 0)), pl.BlockSpec((D, t), lambda i: (0, i))),
        compiler_params=_params(1),
    )(x, mod, norm_g, win_g)


def _halo_specs(t, s, width):
    per = t // HALO
    last = s // HALO - 1
    prev = pl.BlockSpec((HALO, width), lambda i: (jnp.maximum(i * per - 1, 0), 0))
    nxt = pl.BlockSpec((HALO, width), lambda i: (jnp.minimum((i + 1) * per, last), 0))
    return prev, nxt


def _glu(ref):
    return ref[:, 0:D] * _sigmoid(ref[:, D:2 * D])


def _layer_norm_stats(v):
    mu = jnp.mean(v, axis=-1, keepdims=True)
    cen = v - mu
    rstd = lax.rsqrt(jnp.mean(cen * cen, axis=-1, keepdims=True) + EPS)
    return cen * rstd, rstd


def _layer_norm_bwd(dy_hat, hat, rstd):
    m1 = jnp.mean(dy_hat, axis=-1, keepdims=True)
    m2 = jnp.mean(dy_hat * hat, axis=-1, keepdims=True)
    return rstd * (dy_hat - m1 - hat * m2)


def _colsum(v):
    return jnp.sum(v, axis=0, keepdims=True)


def _mix_and_head(z, x, tgt, mod, conv_w_g, conv_b, cln_g, cln_b, sln_g, sln_b, final_g, ws_b, wst_b, bs_full, wout):
    s = x.shape[0]
    t = ROW_TILE
    n_chunks = t // CHUNK
    n_steps = s // t
    prev_spec, next_spec = _halo_specs(t, s, 2 * D)

    def body(z_ref, zp_ref, zn_ref, x_ref, tgt_ref, mod_ref, cw_ref, cb_ref, clg_ref, clb_ref, slg_ref, slb_ref, fg_ref,
             ws_ref, wst_ref, bs_ref, wout_ref,
             dx2_ref, ycatt_ref, dy_ref, dcv_ref, dzr_ref, acc_ref, gws_ref, gbs_ref,
             gext, cv, vs, dvn, ycat, gbs_acc):
        i = pl.program_id(0)

        @pl.when(i == 0)
        def _():
            acc_ref[...] = jnp.zeros_like(acc_ref)
            gws_ref[...] = jnp.zeros_like(gws_ref)
            gbs_acc[...] = jnp.zeros_like(gbs_acc)

        gext[0:HALO, :] = jnp.where(i > 0, _glu(zp_ref), 0.0)
        gext[HALO:HALO + t, :] = _glu(z_ref)
        gext[HALO + t:2 * HALO + t, :] = jnp.where(i < n_steps - 1, _glu(zn_ref), 0.0)
        for blk in range(N_DEV):
            cols = slice(blk * CONV_BLK, (blk + 1) * CONV_BLK)
            acc = jnp.zeros((t, CONV_BLK), F32)
            for k in range(CONV_W):
                acc = acc + gext[pl.ds(HALO - CONV_HALF + k, t), cols] * cw_ref[blk, k:k + 1, :]
            cv[:, cols] = acc + cb_ref[:, cols]
        ln_hat, ln_rstd = _layer_norm_stats(cv[...])
        ln_a = ln_hat * clg_ref[...] + clb_ref[...]
        sig_ln = _sigmoid(ln_a)
        sa = ln_a * sig_ln
        a_gate = z_ref[:, 2 * D:3 * D]
        sig_ag = _sigmoid(a_gate)
        s_gate = a_gate * sig_ag
        ya = sa * s_gate

        v_hat, v_rstd = _layer_norm_stats(z_ref[:, 4 * D:5 * D])
        vn = v_hat * slg_ref[...] + slb_ref[...]
        vnb = vn.astype(BF16)
        for n in range(n_chunks):
            rows = slice(n * CHUNK, (n + 1) * CHUNK)
            for h in range(HEADS):
                cols = slice(h * HEAD_DIM, (h + 1) * HEAD_DIM)
                vs[rows, cols] = jnp.dot(ws_ref[h], vnb[rows, cols], preferred_element_type=F32) + bs_ref[:, cols]
        u = z_ref[:, 3 * D:4 * D]
        b_gate = z_ref[:, 5 * D:6 * D]
        sig_bg = _sigmoid(b_gate)
        s_bg = b_gate * sig_bg
        vsv = vs[...]
        yb = u * vsv * s_bg

        ycat[:, 0:D] = ya.astype(BF16)
        ycat[:, D:2 * D] = yb.astype(BF16)
        ycatt_ref[0:D, :] = ya.T.astype(BF16)
        ycatt_ref[D:2 * D, :] = yb.T.astype(BF16)
        y = jnp.dot(ycat[...], wout_ref[...], preferred_element_type=F32)
        gate = mod_ref[0:1, 2 * D:3 * D]
        x2 = x_ref[...] + gate * y
        r2 = lax.rsqrt(jnp.mean(x2 * x2, axis=-1, keepdims=True) + EPS)
        x2n = x2 * r2
        fg = fg_ref[...]
        diff = x2n * fg - tgt_ref[...]
        acc_ref[7:8, :] += _colsum(diff * diff)
        dout = diff * (1.0 / D)
        acc_ref[0:1, :] += _colsum(dout * x2n)
        dx2n = dout * fg
        dx2 = r2 * (dx2n - x2n * jnp.mean(dx2n * x2n, axis=-1, keepdims=True))
        dx2_ref[...] = dx2
        acc_ref[1:2, :] += _colsum(dx2 * y)
        dyb16 = (dx2 * gate).astype(BF16)
        dy_ref[...] = dyb16
        dycat = lax.dot_general(dyb16, wout_ref[...], (((1,), (1,)), ((), ())), preferred_element_type=F32)
        dya = dycat[:, 0:D]
        dyb = dycat[:, D:2 * D]

        du = dyb * vsv * s_bg
        dvs = dyb * u * s_bg
        dbg = dyb * u * vsv * _dsilu(b_gate, sig_bg)
        dvsb = dvs.astype(BF16)
        gbs = gbs_acc[...]
        for n in range(n_chunks):
            rows = slice(n * CHUNK, (n + 1) * CHUNK)
            gbs = gbs + dvs[rows, :]
            for h in range(HEADS):
                cols = slice(h * HEAD_DIM, (h + 1) * HEAD_DIM)
                gws_ref[h] += lax.dot_general(dvsb[rows, cols], vnb[rows, cols], (((1,), (1,)), ((), ())),
                                              preferred_element_type=F32)
                dvn[rows, cols] = jnp.dot(wst_ref[h], dvsb[rows, cols], preferred_element_type=F32)
        gbs_acc[...] = gbs

        @pl.when(i == n_steps - 1)
        def _():
            for h in range(HEADS):
                gbs_ref[:, h:h + 1] = jnp.sum(gbs_acc[:, h * HEAD_DIM:(h + 1) * HEAD_DIM], axis=1, keepdims=True)

        dvnv = dvn[...]
        acc_ref[5:6, :] += _colsum(dvnv * v_hat)
        acc_ref[6:7, :] += _colsum(dvnv)
        dv = _layer_norm_bwd(dvnv * slg_ref[...], v_hat, v_rstd)

        dsa = dya * s_gate
        dagate = dya * sa * _dsilu(a_gate, sig_ag)
        dln = dsa * _dsilu(ln_a, sig_ln)
        acc_ref[3:4, :] += _colsum(dln * ln_hat)
        acc_ref[4:5, :] += _colsum(dln)
        dcv = _layer_norm_bwd(dln * clg_ref[...], ln_hat, ln_rstd)
        acc_ref[2:3, :] += _colsum(dcv)
        dcv_ref[...] = dcv

        dzr_ref[:, 0:D] = dagate.astype(BF16)
        dzr_ref[:, D:2 * D] = du.astype(BF16)
        dzr_ref[:, 2 * D:3 * D] = dv.astype(BF16)
        dzr_ref[:, 3 * D:4 * D] = dbg.astype(BF16)

    row = lambda w: pl.BlockSpec((t, w), lambda i: (i, 0))
    const = lambda shape: pl.BlockSpec(shape, lambda i: (0,) * len(shape))
    return pl.pallas_call(
        body, name="mix_and_head", grid=(n_steps,),
        out_shape=(jax.ShapeDtypeStruct((s, D), F32),
                   jax.ShapeDtypeStruct((D_MIX, s), BF16),
                   jax.ShapeDtypeStruct((s, D), BF16),
                   jax.ShapeDtypeStruct((s, D), F32),
                   jax.ShapeDtypeStruct((s, 4 * D), BF16),
                   jax.ShapeDtypeStruct((8, D), F32),
                   jax.ShapeDtypeStruct((HEADS, CHUNK, CHUNK), F32),
                   jax.ShapeDtypeStruct((CHUNK, HEADS), F32)),
        in_specs=[row(D_IN), prev_spec, next_spec, row(D), row(D)] + [VM] * 12,
        out_specs=(row(D), pl.BlockSpec((D_MIX, t), lambda i: (0, i)), row(D), row(D), row(4 * D),
                   const((8, D)), const((HEADS, CHUNK, CHUNK)), const((CHUNK, HEADS))),
        scratch_shapes=[pltpu.VMEM((t + 2 * HALO, D), F32), pltpu.VMEM((t, D), F32), pltpu.VMEM((t, D), F32),
                        pltpu.VMEM((t, D), F32), pltpu.VMEM((t, D_MIX), BF16), pltpu.VMEM((CHUNK, D), F32)],
        compiler_params=_params(1),
    )(z, z, z, x, tgt, mod, conv_w_g, conv_b, cln_g, cln_b, sln_g, sln_b, final_g, ws_b, wst_b, bs_full, wout)


def _bwd_in_proj(z, dcv, dz_rest, x, dx2, mod, norm_g, conv_w_g, win_g):
    s = x.shape[0]
    t = ROW_TILE
    n_steps = s // t
    zp_spec, zn_spec = _halo_specs(t, s, 2 * D)
    dp_spec, dn_spec = _halo_specs(t, s, D)

    def body(z_ref, zp_ref, zn_ref, dcv_ref, dcvp_ref, dcvn_ref, dzr_ref, x_ref, dx2_ref, mod_ref, ng_ref, cw_ref, w_ref,
             gx_ref, dz_ref, acc_ref, gcw_ref,
             gext, dext, dg, taps):
        i = pl.program_id(0)

        @pl.when(i == 0)
        def _():
            acc_ref[...] = jnp.zeros_like(acc_ref)
            gcw_ref[...] = jnp.zeros_like(gcw_ref)

        not_first = i > 0
        not_last = i < n_steps - 1
        gext[0:HALO, :] = jnp.where(not_first, _glu(zp_ref), 0.0)
        gext[HALO:HALO + t, :] = _glu(z_ref)
        gext[HALO + t:2 * HALO + t, :] = jnp.where(not_last, _glu(zn_ref), 0.0)
        dext[0:HALO, :] = jnp.where(not_first, dcvp_ref[...], 0.0)
        dext[HALO:HALO + t, :] = dcv_ref[...]
        dext[HALO + t:2 * HALO + t, :] = jnp.where(not_last, dcvn_ref[...], 0.0)

        taps[...] = jnp.zeros_like(taps)
        for blk in range(N_DEV):
            cols = slice(blk * CONV_BLK, (blk + 1) * CONV_BLK)
            d_here = dcv_ref[:, cols]
            acc = jnp.zeros((t, CONV_BLK), F32)
            for k in range(CONV_W):
                acc = acc + dext[pl.ds(HALO + CONV_HALF - k, t), cols] * cw_ref[blk, k:k + 1, :]
                taps[k:k + 1, :] = _colsum(gext[pl.ds(HALO - CONV_HALF + k, t), cols] * d_here)
            dg[:, cols] = acc
            gcw_ref[blk] += taps[...]

        a = z_ref[:, 0:D]
        sig = _sigmoid(z_ref[:, D:2 * D])
        dgv = dg[...]
        dz_ref[:, 0:D] = (dgv * sig).astype(BF16)
        dz_ref[:, D:2 * D] = (dgv * a * sig * (1.0 - sig)).astype(BF16)
        dz_ref[:, 2 * D:6 * D] = dzr_ref[...]

        dh = jnp.zeros((t, D), F32)
        for j in range(N_DEV):
            dh = dh + lax.dot_general(dz_ref[:, j * W_IN_BLK:(j + 1) * W_IN_BLK], w_ref[j], (((1,), (1,)), ((), ())),
                                      preferred_element_type=F32)

        xn, r, _, scale = _rms_modulate(x_ref[...], mod_ref)
        ng = ng_ref[...]
        one_scale = 1.0 + scale
        dh_xn = dh * xn
        acc_ref[0:1, :] += _colsum(dh_xn * one_scale)
        acc_ref[1:2, :] += _colsum(dh)
        acc_ref[2:3, :] += _colsum(dh_xn * ng)
        dxn = dh * (ng * one_scale)
        gx_ref[...] = dx2_ref[...] + r * (dxn - xn * jnp.mean(dxn * xn, axis=-1, keepdims=True))

    row = lambda w: pl.BlockSpec((t, w), lambda i: (i, 0))
    const = lambda shape: pl.BlockSpec(shape, lambda i: (0,) * len(shape))
    return pl.pallas_call(
        body, name="bwd_in_proj", grid=(n_steps,),
        out_shape=(jax.ShapeDtypeStruct((s, D), F32), jax.ShapeDtypeStruct((s, D_IN), BF16),
                   jax.ShapeDtypeStruct((8, D), F32), jax.ShapeDtypeStruct((N_DEV, CONV_ROWS, CONV_BLK), F32)),
        in_specs=[pl.BlockSpec((t, 2 * D), lambda i: (i, 0)), zp_spec, zn_spec, row(D), dp_spec, dn_spec, row(4 * D),
                  row(D), row(D), VM, VM, VM, VM],
        out_specs=(row(D), row(D_IN), const((8, D)), const((N_DEV, CONV_ROWS, CONV_BLK))),
        scratch_shapes=[pltpu.VMEM((t + 2 * HALO, D), F32), pltpu.VMEM((t + 2 * HALO, D), F32), pltpu.VMEM((t, D), F32),
                        pltpu.VMEM((CONV_ROWS, CONV_BLK), F32)],
        compiler_params=_params(1),
    )(z, z, z, dcv, dcv, dcv, dz_rest, x, dx2, mod, norm_g, conv_w_g, win_g)


def _wgrad_in(ht, dz):
    s = dz.shape[0]

    def body(ht_ref, dz_ref, g_ref):
        g_ref[0] = jnp.dot(ht_ref[...], dz_ref[...], preferred_element_type=F32).astype(BF16)

    return pl.pallas_call(
        body, name="wgrad_in", grid=(N_DEV,),
        out_shape=jax.ShapeDtypeStruct((N_DEV, D, W_IN_BLK), BF16),
        in_specs=[VM, pl.BlockSpec((s, W_IN_BLK), lambda j: (0, j))],
        out_specs=pl.BlockSpec((1, D, W_IN_BLK), lambda j: (j, 0, 0)),
        compiler_params=_params(1),
    )(ht, dz)


def _wgrad_out(ycatt, dy):
    s = dy.shape[0]

    def body(yt_ref, dy_ref, g_ref):
        g_ref[0] = jnp.dot(yt_ref[...], dy_ref[...], preferred_element_type=F32).astype(BF16)

    return pl.pallas_call(
        body, name="wgrad_out", grid=(N_DEV,),
        out_shape=jax.ShapeDtypeStruct((N_DEV, W_OUT_BLK, D), BF16),
        in_specs=[pl.BlockSpec((W_OUT_BLK, s), lambda j: (j, 0)), VM],
        out_specs=pl.BlockSpec((1, W_OUT_BLK, D), lambda j: (j, 0, 0)),
        compiler_params=_params(1),
    )(ycatt, dy)


def _grad_reduce(g_in, g_out):
    row_chunk = 128

    def body(gin_ref, gout_ref, oin_ref, oout_ref, l1_in, l1_out, b2_in, b2_out, l2_in, l2_out,
             s1_send, s1_recv, s2_send, s2_recv):
        x, y, c = _mesh_pos()
        p = 2 * x + y
        sib = (x, y, 1 - c)
        chip_of = [(x, y), (x, 1 - y), (1 - x, y), (1 - x, 1 - y)]
        pr = [p, 2 * x + (1 - y), 2 * (1 - x) + y, 2 * (1 - x) + (1 - y)]
        pays = [(gin_ref, l1_in, b2_in, l2_in, oin_ref), (gout_ref, l1_out, b2_out, l2_out, oout_ref)]

        def step1(a, r):
            g, l1 = pays[a][0], pays[a][1]
            return pltpu.make_async_remote_copy(
                src_ref=g.at[2 * pr[r] + (1 - c)], dst_ref=l1.at[r], send_sem=s1_send.at[4 * a + r],
                recv_sem=s1_recv.at[4 * a + r], device_id=sib, device_id_type=MESH)

        def step2(a, r):
            b2, l2 = pays[a][2], pays[a][3]
            return pltpu.make_async_remote_copy(
                src_ref=b2.at[r - 1], dst_ref=l2.at[r - 1], send_sem=s2_send.at[3 * a + r - 1],
                recv_sem=s2_recv.at[3 * a + r - 1], device_id=(*chip_of[r], c), device_id_type=MESH)

        for r in (3, 1, 2, 0):
            for a in range(2):
                step1(a, r).start()

        def chunks(a):
            return pays[a][0].shape[1] // row_chunk

        for r in (3, 1, 2):
            for a in range(2):
                g, l1, b2 = pays[a][0], pays[a][1], pays[a][2]
                step1(a, r).wait_recv()
                mine = 2 * pr[r] + c

                def add(q, carry, g=g, l1=l1, b2=b2, mine=mine, r=r):
                    rows = pl.ds(pl.multiple_of(q * row_chunk, row_chunk), row_chunk)
                    b2[r - 1, rows, :] = (g[mine, rows, :].astype(F32) + l1[r, rows, :].astype(F32)).astype(BF16)
                    return carry

                lax.fori_loop(0, chunks(a), add, 0)
                step2(a, r).start()

        for a in range(2):
            step1(a, 0).wait_recv()
        for r in (1, 2, 3):
            for a in range(2):
                step2(a, r).wait_recv()
        for a in range(2):
            g, l1, l2, out = pays[a][0], pays[a][1], pays[a][3], pays[a][4]
            mine = 2 * p + c

            def total(q, carry, g=g, l1=l1, l2=l2, out=out, mine=mine):
                rows = pl.ds(pl.multiple_of(q * row_chunk, row_chunk), row_chunk)
                acc = g[mine, rows, :].astype(F32) + l1[0, rows, :].astype(F32)
                for r in range(3):
                    acc = acc + l2[r, rows, :].astype(F32)
                out[rows, :] = acc
                return carry

            lax.fori_loop(0, chunks(a), total, 0)
        for r in range(4):
            for a in range(2):
                step1(a, r).wait_send()
        for r in (1, 2, 3):
            for a in range(2):
                step2(a, r).wait_send()

    blk_in, blk_out = g_in.shape[1:], g_out.shape[1:]
    return pl.pallas_call(
        body, name="grad_reduce",
        out_shape=(jax.ShapeDtypeStruct(blk_in, F32), jax.ShapeDtypeStruct(blk_out, F32)),
        in_specs=[VM, VM], out_specs=(VM, VM),
        scratch_shapes=[pltpu.VMEM((4,) + blk_in, BF16), pltpu.VMEM((4,) + blk_out, BF16),
                        pltpu.VMEM((3,) + blk_in, BF16), pltpu.VMEM((3,) + blk_out, BF16),
                        pltpu.VMEM((3,) + blk_in, BF16), pltpu.VMEM((3,) + blk_out, BF16),
                        pltpu.SemaphoreType.DMA((8,)), pltpu.SemaphoreType.DMA((8,)),
                        pltpu.SemaphoreType.DMA((6,)), pltpu.SemaphoreType.DMA((6,))],
        compiler_params=_params(),
    )(g_in, g_out)


def _small_reduce(small, gws, dmod):
    def body(small_ref, gws_ref, dmod_ref, osmall_ref, ogws_ref, odmod_ref, obada_ref,
             small_g, gws_g, dmod_g, send_sems, recv_sems):
        me = _my_block()
        small_g[me] = small_ref[...]
        gws_g[me] = gws_ref[...]
        dmod_g[me] = dmod_ref[...]
        _allgather_blocks([dmod_g, small_g, gws_g], send_sems, recv_sems)
        tot_small, tot_gws, tot_dmod = small_g[0], gws_g[0], dmod_g[0]
        for b in range(1, N_DEV):
            tot_small = tot_small + small_g[b]
            tot_gws = tot_gws + gws_g[b]
            tot_dmod = tot_dmod + dmod_g[b]
        osmall_ref[...] = tot_small
        ogws_ref[...] = tot_gws
        obada_ref[...] = tot_dmod
        for b in range(N_DEV):
            odmod_ref[b:b + 1, :] = dmod_g[b, 0:1, :]

    return pl.pallas_call(
        body, name="small_reduce",
        out_shape=(jax.ShapeDtypeStruct(small.shape, F32), jax.ShapeDtypeStruct(gws.shape, F32),
                   jax.ShapeDtypeStruct((N_DEV, 3 * D), F32), jax.ShapeDtypeStruct((8, 3 * D), F32)),
        in_specs=[VM] * 3, out_specs=(VM,) * 4,
        scratch_shapes=[pltpu.VMEM((N_DEV,) + small.shape, F32), pltpu.VMEM((N_DEV,) + gws.shape, F32),
                        pltpu.VMEM((N_DEV,) + dmod.shape, F32),
                        pltpu.SemaphoreType.DMA((21,)), pltpu.SemaphoreType.DMA((21,))],
        compiler_params=_params(),
    )(small, gws, dmod)


def _wgrad_ada(c_all, dmod_cols):
    def body(c_ref, dm_ref, g_ref, act):
        for b in range(N_DEV):
            cb = c_ref[b, 0:1, :]
            act[b:b + 1, :] = cb * _sigmoid(cb)
        g_ref[...] = lax.dot_general(act[...], dm_ref[...], (((0,), (0,)), ((), ())), preferred_element_type=F32,
                                     precision=lax.Precision.HIGHEST)

    return pl.pallas_call(
        body, name="wgrad_ada", out_shape=jax.ShapeDtypeStruct((D, W_ADA_BLK), F32),
        in_specs=[VM, VM], out_specs=VM, scratch_shapes=[pltpu.VMEM((N_DEV, D), F32)],
        compiler_params=_params(),
    )(c_all, dmod_cols)


def _adamw_math(w, g, m, v):
    m = ADAM_B1 * m + (1.0 - ADAM_B1) * g
    v = ADAM_B2 * v + (1.0 - ADAM_B2) * (g * g)
    m_hat = m / (1.0 - ADAM_B1 ** ADAM_STEP)
    v_hat = v / (1.0 - ADAM_B2 ** ADAM_STEP)
    delta = -ADAM_LR * (m_hat / (jnp.sqrt(v_hat) + ADAM_EPS) + ADAM_WD * w)
    return delta, m, v


def _adamw(name, w, g, m, v, row_block=None):
    shape = w.shape

    def body(w_ref, g_ref, m_ref, v_ref, d_ref, nm_ref, nv_ref):
        d_ref[...], nm_ref[...], nv_ref[...] = _adamw_math(w_ref[...], g_ref[...], m_ref[...], v_ref[...])

    out_shape = (jax.ShapeDtypeStruct(shape, F32),) * 3
    if row_block is None:
        return pl.pallas_call(body, name=name, out_shape=out_shape, in_specs=[VM] * 4, out_specs=(VM,) * 3,
                              compiler_params=_params())(w, g, m, v)
    spec = pl.BlockSpec((row_block, shape[1]), lambda i: (i, 0))
    return pl.pallas_call(body, name=name, grid=(shape[0] // row_block,), out_shape=out_shape,
                          in_specs=[spec] * 4, out_specs=(spec,) * 3, compiler_params=_params(1))(w, g, m, v)


def kernel(x, c, w_ada, b_ada, norm_g, w_in, conv_w, conv_b, conv_ln_g, conv_ln_b, sg_ln_g, sg_ln_b, w_s, b_s, w_out, final_g, loss_target, m_w_ada, m_b_ada, m_norm_g, m_w_in, m_conv_w, m_conv_b, m_conv_ln_g, m_conv_ln_b, m_sg_ln_g, m_sg_ln_b, m_w_s, m_b_s, m_w_out, m_final_g, v_w_ada, v_b_ada, v_norm_g, v_w_in, v_conv_w, v_conv_b, v_conv_ln_g, v_conv_ln_b, v_sg_ln_g, v_sg_ln_b, v_w_s, v_b_s, v_w_out, v_final_g):
    me = 4 * lax.axis_index("x") + 2 * lax.axis_index("y") + lax.axis_index("c")
    x2d, tgt2d = x[0], loss_target[0]
    row1 = lambda a: a.reshape(1, D)
    taps = lambda a: jnp.pad(a.reshape(CONV_W, CONV_BLK), ((0, CONV_ROWS - CONV_W), (0, 0)))

    win_g, wout_g, cw_g, c_all = _gather_params(w_in[0], w_out[0], taps(conv_w), jnp.broadcast_to(c, (8, D)))
    mod = _mod_exchange(c_all, w_ada[0], b_ada)
    ws_b = w_s[0].astype(BF16)
    wst_b = jnp.swapaxes(w_s[0], 1, 2).astype(BF16)
    bs_full = jnp.repeat(b_s[0].T, HEAD_DIM, axis=1)

    z, ht = _fwd_in_proj(x2d, mod, norm_g, win_g)
    dx2, ycatt, dy, dcv, dz_rest, acc_a, gws, gbs = _mix_and_head(
        z, x2d, tgt2d, mod, cw_g, conv_b, conv_ln_g, conv_ln_b, sg_ln_g, sg_ln_b, row1(final_g), ws_b, wst_b, bs_full,
        wout_g.reshape(D_MIX, D))
    grad_x, dz, acc_b, gcw = _bwd_in_proj(z, dcv, dz_rest, x2d, dx2, mod, norm_g, cw_g, win_g)
    gin_part = _wgrad_in(ht, dz)
    gout_part = _wgrad_out(ycatt, dy)

    g_w_in, g_w_out = _grad_reduce(gin_part, gout_part)
    gbs_row = gbs.T.reshape(1, D)
    small = jnp.concatenate(
        [acc_b[0:1], acc_a[2:7], acc_a[0:1], gbs_row, jnp.transpose(gcw, (1, 0, 2)).reshape(CONV_ROWS, D)], axis=0)
    dmod_row = jnp.concatenate([acc_b[1:2], acc_b[2:3], acc_a[1:2]], axis=1)
    small_sum, gws_sum, dmod_all, bada8 = _small_reduce(small, gws.reshape(HEADS * CHUNK, CHUNK),
                                                        jnp.broadcast_to(dmod_row, (8, 3 * D)))
    loss = lax.psum(0.5 / D * jnp.sum(acc_a[7]), ("x", "y", "c"))

    g_w_ada = _wgrad_ada(c_all, lax.dynamic_slice(dmod_all, (0, me * W_ADA_BLK), (N_DEV, W_ADA_BLK)))
    g_b_ada = bada8[0:1]
    g_vec = small_sum[0:8]
    g_conv_w = lax.dynamic_slice(small_sum, (8, me * CONV_BLK), (CONV_W, CONV_BLK))

    vec = lambda *a: jnp.concatenate([row1(t) for t in a], axis=0)
    d_vec, m_vec, v_vec = _adamw(
        "adamw_vectors",
        vec(norm_g, conv_b, conv_ln_g, conv_ln_b, sg_ln_g, sg_ln_b, final_g, b_s), g_vec,
        vec(m_norm_g, m_conv_b, m_conv_ln_g, m_conv_ln_b, m_sg_ln_g, m_sg_ln_b, m_final_g, m_b_s),
        vec(v_norm_g, v_conv_b, v_conv_ln_g, v_conv_ln_b, v_sg_ln_g, v_sg_ln_b, v_final_g, v_b_s))
    flat_ws = lambda a: a.reshape(HEADS * CHUNK, CHUNK)
    upd = {
        "w_ada": _adamw("adamw_w_ada", w_ada[0], g_w_ada, m_w_ada[0], v_w_ada[0], 256),
        "b_ada": _adamw("adamw_b_ada", b_ada, g_b_ada, m_b_ada, v_b_ada),
        "w_in": _adamw("adamw_w_in", w_in[0], g_w_in, m_w_in[0], v_w_in[0], 256),
        "conv_w": _adamw("adamw_conv_w", conv_w.reshape(CONV_W, CONV_BLK), g_conv_w, m_conv_w.reshape(CONV_W, CONV_BLK),
                         v_conv_w.reshape(CONV_W, CONV_BLK)),
        "w_s": _adamw("adamw_w_s", flat_ws(w_s), gws_sum, flat_ws(m_w_s), flat_ws(v_w_s)),
        "w_out": _adamw("adamw_w_out", w_out[0], g_w_out, m_w_out[0], v_w_out[0], 128),
    }
    grads = {"w_ada": g_w_ada, "b_ada": g_b_ada, "w_in": g_w_in, "conv_w": g_conv_w, "w_s": gws_sum, "w_out": g_w_out}
    vec_names = ["norm_g", "conv_b", "conv_ln_g", "conv_ln_b", "sg_ln_g", "sg_ln_b", "final_g", "b_s"]
    shapes = {"w_ada": w_ada.shape, "b_ada": b_ada.shape, "norm_g": norm_g.shape, "w_in": w_in.shape,
              "conv_w": conv_w.shape, "conv_b": conv_b.shape, "conv_ln_g": conv_ln_g.shape, "conv_ln_b": conv_ln_b.shape,
              "sg_ln_g": sg_ln_g.shape, "sg_ln_b": sg_ln_b.shape, "w_s": w_s.shape, "b_s": b_s.shape,
              "w_out": w_out.shape, "final_g": final_g.shape}
    order = ["w_ada", "b_ada", "norm_g", "w_in", "conv_w", "conv_b", "conv_ln_g", "conv_ln_b", "sg_ln_g", "sg_ln_b",
             "w_s", "b_s", "w_out", "final_g"]

    def leaf(kind, name):
        if name in vec_names:
            src = (g_vec, d_vec, m_vec, v_vec)[kind]
            val = src[vec_names.index(name)]
        elif kind == 0:
            val = grads[name]
        else:
            val = upd[name][kind - 1]
        return val.reshape(shapes[name])

    outs = [loss, grad_x.reshape(x.shape)]
    for kind in range(4):
        outs += [leaf(kind, n) for n in order]
    return tuple(outs)
```

```python
import functools

import jax
import jax.numpy as jnp
from jax import lax
from jax.experimental import pallas as pl
from jax.experimental.pallas import tpu as pltpu

F32 = jnp.float32
BF16 = jnp.bfloat16
MESH = pl.DeviceIdType.MESH

D = 1024
D_IN = 6 * D
D_MIX = 2 * D
N_DEV = 8
W_IN_BLK = D_IN // N_DEV
W_OUT_BLK = D_MIX // N_DEV
W_ADA_BLK = 3 * D // N_DEV
CONV_BLK = D // N_DEV
CONV_W = 31
CONV_HALF = CONV_W // 2
CONV_ROWS = 32
HALO = 16
CHUNK = 128
HEADS = 8
HEAD_DIM = 128
EPS = 1e-6
ROW_TILE = 256
SMALL_ROWS = 40
VMEM_LIMIT = 56 * 1024 * 1024

ADAM_LR = 0.001
ADAM_B1 = 0.9
ADAM_B2 = 0.999
ADAM_EPS = 1e-08
ADAM_WD = 0.01
ADAM_STEP = 10

VM = pl.BlockSpec(memory_space=pltpu.VMEM)


def _params(grid_rank=0, **kw):
    sem = ("arbitrary",) * grid_rank if grid_rank else None
    return pltpu.CompilerParams(dimension_semantics=sem, vmem_limit_bytes=VMEM_LIMIT, **kw)


def _sigmoid(t):
    return jax.nn.sigmoid(t)


def _dsilu(t, sig):
    return sig * (1.0 + t * (1.0 - sig))


def _mesh_pos():
    return lax.axis_index("x"), lax.axis_index("y"), lax.axis_index("c")


def _allgather_blocks(bufs, send_sems, recv_sems):
    x, y, c = _mesh_pos()
    me, sib = (x, y, c), (x, y, 1 - c)
    chips = [(x, 1 - y), (1 - x, y), (1 - x, 1 - y)]
    n = len(bufs)

    def copy(a, k, block, to):
        px, py, pc = block
        ref = bufs[a].at[4 * px + 2 * py + pc]
        return pltpu.make_async_remote_copy(
            src_ref=ref, dst_ref=ref, send_sem=send_sems.at[7 * a + k], recv_sem=recv_sems.at[7 * a + k],
            device_id=to, device_id_type=MESH)

    first = []
    for a in range(n):
        first.append(copy(a, 0, me, sib))
        first += [copy(a, 1 + j, me, (*chip, c)) for j, chip in enumerate(chips)]
    for cp in first:
        cp.start()
    passed = []
    for j, chip in enumerate(chips):
        for a in range(n):
            copy(a, 1 + j, (*chip, c), me).wait_recv()
            fwd = copy(a, 4 + j, (*chip, c), sib)
            fwd.start()
            passed.append(fwd)
    for a in range(n):
        copy(a, 0, sib, me).wait_recv()
    for j, chip in enumerate(chips):
        for a in range(n):
            copy(a, 4 + j, (*chip, 1 - c), me).wait_recv()
    for cp in first + passed:
        cp.wait_send()


def _my_block():
    x, y, c = _mesh_pos()
    return 4 * x + 2 * y + c


def _gather_params(w_in, w_out, conv_w, c_rep):
    def body(w_in_ref, w_out_ref, cw_ref, c_ref, win_g, wout_g, cw_g, c_g, send_sems, recv_sems):
        me = _my_block()
        win_g[me] = w_in_ref[...].astype(BF16)
        wout_g[me] = w_out_ref[...].astype(BF16)
        cw_g[me] = cw_ref[...]
        c_g[me] = c_ref[...]
        _allgather_blocks([c_g, cw_g, wout_g, win_g], send_sems, recv_sems)

    return pl.pallas_call(
        body, name="gather_params",
        out_shape=(jax.ShapeDtypeStruct((N_DEV,) + w_in.shape, BF16),
                   jax.ShapeDtypeStruct((N_DEV,) + w_out.shape, BF16),
                   jax.ShapeDtypeStruct((N_DEV,) + conv_w.shape, F32),
                   jax.ShapeDtypeStruct((N_DEV,) + c_rep.shape, F32)),
        in_specs=[VM] * 4, out_specs=(VM,) * 4,
        scratch_shapes=[pltpu.SemaphoreType.DMA((28,)), pltpu.SemaphoreType.DMA((28,))],
        compiler_params=_params(),
    )(w_in, w_out, conv_w, c_rep)


def _mod_exchange(c_all, w_ada, b_ada):
    def body(c_ref, w_ref, b_ref, mod_ref, part, land, send_sems, recv_sems):
        x, y, c = _mesh_pos()
        me = 4 * x + 2 * y + c
        w = w_ref[...]
        for b in range(N_DEV):
            cb = c_ref[b]
            act = cb * _sigmoid(cb)
            part[b] = jnp.dot(act, w, preferred_element_type=F32, precision=lax.Precision.HIGHEST)
        land[me] = part[me]

        def copy(b):
            return pltpu.make_async_remote_copy(
                src_ref=part.at[b], dst_ref=land.at[me], send_sem=send_sems.at[b], recv_sem=recv_sems.at[me],
                device_id=(b // 4, (b // 2) % 2, b % 2), device_id_type=MESH)

        def arrival(b):
            return pltpu.make_async_remote_copy(
                src_ref=part.at[b], dst_ref=land.at[b], send_sem=send_sems.at[b], recv_sem=recv_sems.at[b],
                device_id=(b // 4, (b // 2) % 2, b % 2), device_id_type=MESH)

        for b in range(N_DEV):
            @pl.when(b != me)
            def _():
                copy(b).start()
        for b in range(N_DEV):
            @pl.when(b != me)
            def _():
                arrival(b).wait_recv()
                copy(b).wait_send()
        for b in range(N_DEV):
            cols = slice(b * W_ADA_BLK, (b + 1) * W_ADA_BLK)
            mod_ref[:, cols] = land[b] + b_ref[:, cols]

    return pl.pallas_call(
        body, name="mod_exchange",
        out_shape=jax.ShapeDtypeStruct((8, 3 * D), F32),
        in_specs=[VM] * 3, out_specs=VM,
        scratch_shapes=[pltpu.VMEM((N_DEV, 8, W_ADA_BLK), F32), pltpu.VMEM((N_DEV, 8, W_ADA_BLK), F32),
                        pltpu.SemaphoreType.DMA((N_DEV,)), pltpu.SemaphoreType.DMA((N_DEV,))],
        compiler_params=_params(),
    )(c_all, w_ada, b_ada)


def _rms_modulate(x, mod_ref):
    shift = mod_ref[0:1, 0:D]
    scale = mod_ref[0:1, D:2 * D]
    r = lax.rsqrt(jnp.mean(x * x, axis=-1, keepdims=True) + EPS)
    xn = x * r
    return xn, r, shift, scale


def _fwd_in_proj(x, mod, norm_g, win_g):
    s = x.shape[0]
    t = ROW_TILE

    def body(x_ref, mod_ref, ng_ref, w_ref, z_ref, ht_ref):
        xn, _, shift, scale = _rms_modulate(x_ref[...], mod_ref)
        h = xn * ng_ref[...] * (1.0 + scale) + shift
        hb = h.astype(BF16)
        ht_ref[...] = h.T.astype(BF16)
        for j in range(N_DEV):
            z_ref[:, j * W_IN_BLK:(j + 1) * W_IN_BLK] = jnp.dot(hb, w_ref[j], preferred_element_type=F32)

    return pl.pallas_call(
        body, name="fwd_in_proj", grid=(s // t,),
        out_shape=(jax.ShapeDtypeStruct((s, D_IN), F32), jax.ShapeDtypeStruct((D, s), BF16)),
        in_specs=[pl.BlockSpec((t, D), lambda i: (i, 0)), VM, VM, VM],
        out_specs=(pl.BlockSpec((t, D_IN), lambda i: (i, 0)), pl.BlockSpec((D, t), lambda i: (0, i))),
        compiler_params=_params(1),
    )(x, mod, norm_g, win_g)


def _halo_specs(t, s, width):
    per = t // HALO
    last = s // HALO - 1
    prev = pl.BlockSpec((HALO, width), lambda i: (jnp.maximum(i * per - 1, 0), 0))
    nxt = pl.BlockSpec((HALO, width), lambda i: (jnp.minimum((i + 1) * per, last), 0))
    return prev, nxt


def _glu(ref):
    return ref[:, 0:D] * _sigmoid(ref[:, D:2 * D])


SUB = 8
CONV_PHASE_ROWS = ROW_TILE + SUB


def _conv_taps(ext, qbuf, cols, tap, t):
    out = None
    for b in range(SUB):
        q = None
        for a in range((CONV_W - b + SUB - 1) // SUB):
            term = ext[SUB * a:SUB * a + t + SUB, cols] * tap(SUB * a + b)
            q = term if q is None else q + term
        qbuf[b] = q
        shifted = qbuf[b, pl.ds(b + 1, t), :]
        out = shifted if out is None else out + shifted
    return out


def _layer_norm_stats(v):
    mu = jnp.mean(v, axis=-1, keepdims=True)
    cen = v - mu
    rstd = lax.rsqrt(jnp.mean(cen * cen, axis=-1, keepdims=True) + EPS)
    return cen * rstd, rstd


def _layer_norm_bwd(dy_hat, hat, rstd):
    m1 = jnp.mean(dy_hat, axis=-1, keepdims=True)
    m2 = jnp.mean(dy_hat * hat, axis=-1, keepdims=True)
    return rstd * (dy_hat - m1 - hat * m2)


def _colsum(v):
    return jnp.sum(v, axis=0, keepdims=True)


def _mix_and_head(z, x, tgt, mod, conv_w_g, conv_b, cln_g, cln_b, sln_g, sln_b, final_g, ws_b, wst_b, bs_full, wout):
    s = x.shape[0]
    t = ROW_TILE
    n_chunks = t // CHUNK
    n_steps = s // t
    prev_spec, next_spec = _halo_specs(t, s, 2 * D)

    def body(z_ref, zp_ref, zn_ref, x_ref, tgt_ref, mod_ref, cw_ref, cb_ref, clg_ref, clb_ref, slg_ref, slb_ref, fg_ref,
             ws_ref, wst_ref, bs_ref, wout_ref,
             dx2_ref, ycatt_ref, dy_ref, dcv_ref, dzr_ref, acc_ref, gws_ref, gbs_ref,
             gext, cv, vs, dvn, ycat, gbs_acc, qbuf):
        i = pl.program_id(0)

        @pl.when(i == 0)
        def _():
            acc_ref[...] = jnp.zeros_like(acc_ref)
            gws_ref[...] = jnp.zeros_like(gws_ref)
            gbs_acc[...] = jnp.zeros_like(gbs_acc)

        gext[0:HALO, :] = jnp.where(i > 0, _glu(zp_ref), 0.0)
        gext[HALO:HALO + t, :] = _glu(z_ref)
        gext[HALO + t:2 * HALO + t, :] = jnp.where(i < n_steps - 1, _glu(zn_ref), 0.0)
        for blk in range(N_DEV):
            cols = slice(blk * CONV_BLK, (blk + 1) * CONV_BLK)
            cv[:, cols] = _conv_taps(gext, qbuf, cols, lambda k, blk=blk: cw_ref[blk, k:k + 1, :], t) + cb_ref[:, cols]
        ln_hat, ln_rstd = _layer_norm_stats(cv[...])
        ln_a = ln_hat * clg_ref[...] + clb_ref[...]
        sig_ln = _sigmoid(ln_a)
        sa = ln_a * sig_ln
        a_gate = z_ref[:, 2 * D:3 * D]
        sig_ag = _sigmoid(a_gate)
        s_gate = a_gate * sig_ag
        ya = sa * s_gate

        v_hat, v_rstd = _layer_norm_stats(z_ref[:, 4 * D:5 * D])
        vn = v_hat * slg_ref[...] + slb_ref[...]
        vnb = vn.astype(BF16)
        for n in range(n_chunks):
            rows = slice(n * CHUNK, (n + 1) * CHUNK)
            for h in range(HEADS):
                cols = slice(h * HEAD_DIM, (h + 1) * HEAD_DIM)
                vs[rows, cols] = jnp.dot(ws_ref[h], vnb[rows, cols], preferred_element_type=F32) + bs_ref[:, cols]
        u = z_ref[:, 3 * D:4 * D]
        b_gate = z_ref[:, 5 * D:6 * D]
        sig_bg = _sigmoid(b_gate)
        s_bg = b_gate * sig_bg
        vsv = vs[...]
        yb = u * vsv * s_bg

        ycat[:, 0:D] = ya.astype(BF16)
        ycat[:, D:2 * D] = yb.astype(BF16)
        ycatt_ref[0:D, :] = ya.T.astype(BF16)
        ycatt_ref[D:2 * D, :] = yb.T.astype(BF16)
        y = jnp.dot(ycat[...], wout_ref[...], preferred_element_type=F32)
        gate = mod_ref[0:1, 2 * D:3 * D]
        x2 = x_ref[...] + gate * y
        r2 = lax.rsqrt(jnp.mean(x2 * x2, axis=-1, keepdims=True) + EPS)
        x2n = x2 * r2
        fg = fg_ref[...]
        diff = x2n * fg - tgt_ref[...]
        acc_ref[7:8, :] += _colsum(diff * diff)
        dout = diff * (1.0 / D)
        acc_ref[0:1, :] += _colsum(dout * x2n)
        dx2n = dout * fg
        dx2 = r2 * (dx2n - x2n * jnp.mean(dx2n * x2n, axis=-1, keepdims=True))
        dx2_ref[...] = dx2
        acc_ref[1:2, :] += _colsum(dx2 * y)
        dyb16 = (dx2 * gate).astype(BF16)
        dy_ref[...] = dyb16
        dycat = lax.dot_general(dyb16, wout_ref[...], (((1,), (1,)), ((), ())), preferred_element_type=F32)
        dya = dycat[:, 0:D]
        dyb = dycat[:, D:2 * D]

        du = dyb * vsv * s_bg
        dvs = dyb * u * s_bg
        dbg = dyb * u * vsv * _dsilu(b_gate, sig_bg)
        dvsb = dvs.astype(BF16)
        gbs = gbs_acc[...]
        for n in range(n_chunks):
            rows = slice(n * CHUNK, (n + 1) * CHUNK)
            gbs = gbs + dvs[rows, :]
            for h in range(HEADS):
                cols = slice(h * HEAD_DIM, (h + 1) * HEAD_DIM)
                gws_ref[h] += lax.dot_general(dvsb[rows, cols], vnb[rows, cols], (((1,), (1,)), ((), ())),
                                              preferred_element_type=F32)
                dvn[rows, cols] = jnp.dot(wst_ref[h], dvsb[rows, cols], preferred_element_type=F32)
        gbs_acc[...] = gbs

        @pl.when(i == n_steps - 1)
        def _():
            for h in range(HEADS):
                gbs_ref[:, h:h + 1] = jnp.sum(gbs_acc[:, h * HEAD_DIM:(h + 1) * HEAD_DIM], axis=1, keepdims=True)

        dvnv = dvn[...]
        acc_ref[5:6, :] += _colsum(dvnv * v_hat)
        acc_ref[6:7, :] += _colsum(dvnv)
        dv = _layer_norm_bwd(dvnv * slg_ref[...], v_hat, v_rstd)

        dsa = dya * s_gate
        dagate = dya * sa * _dsilu(a_gate, sig_ag)
        dln = dsa * _dsilu(ln_a, sig_ln)
        acc_ref[3:4, :] += _colsum(dln * ln_hat)
        acc_ref[4:5, :] += _colsum(dln)
        dcv = _layer_norm_bwd(dln * clg_ref[...], ln_hat, ln_rstd)
        acc_ref[2:3, :] += _colsum(dcv)
        dcv_ref[...] = dcv

        dzr_ref[:, 0:D] = dagate.astype(BF16)
        dzr_ref[:, D:2 * D] = du.astype(BF16)
        dzr_ref[:, 2 * D:3 * D] = dv.astype(BF16)
        dzr_ref[:, 3 * D:4 * D] = dbg.astype(BF16)

    row = lambda w: pl.BlockSpec((t, w), lambda i: (i, 0))
    const = lambda shape: pl.BlockSpec(shape, lambda i: (0,) * len(shape))
    return pl.pallas_call(
        body, name="mix_and_head", grid=(n_steps,),
        out_shape=(jax.ShapeDtypeStruct((s, D), F32),
                   jax.ShapeDtypeStruct((D_MIX, s), BF16),
                   jax.ShapeDtypeStruct((s, D), BF16),
                   jax.ShapeDtypeStruct((s, D), F32),
                   jax.ShapeDtypeStruct((s, 4 * D), BF16),
                   jax.ShapeDtypeStruct((8, D), F32),
                   jax.ShapeDtypeStruct((HEADS, CHUNK, CHUNK), F32),
                   jax.ShapeDtypeStruct((CHUNK, HEADS), F32)),
        in_specs=[row(D_IN), prev_spec, next_spec, row(D), row(D)] + [VM] * 12,
        out_specs=(row(D), pl.BlockSpec((D_MIX, t), lambda i: (0, i)), row(D), row(D), row(4 * D),
                   const((8, D)), const((HEADS, CHUNK, CHUNK)), const((CHUNK, HEADS))),
        scratch_shapes=[pltpu.VMEM((t + 2 * HALO, D), F32), pltpu.VMEM((t, D), F32), pltpu.VMEM((t, D), F32),
                        pltpu.VMEM((t, D), F32), pltpu.VMEM((t, D_MIX), BF16), pltpu.VMEM((CHUNK, D), F32),
                        pltpu.VMEM((SUB, CONV_PHASE_ROWS, CONV_BLK), F32)],
        compiler_params=_params(1),
    )(z, z, z, x, tgt, mod, conv_w_g, conv_b, cln_g, cln_b, sln_g, sln_b, final_g, ws_b, wst_b, bs_full, wout)


def _bwd_in_proj(z, dcv, dz_rest, x, dx2, mod, norm_g, conv_w_g, win_g):
    s = x.shape[0]
    t = ROW_TILE
    n_steps = s // t
    zp_spec, zn_spec = _halo_specs(t, s, 2 * D)
    dp_spec, dn_spec = _halo_specs(t, s, D)

    def body(z_ref, zp_ref, zn_ref, dcv_ref, dcvp_ref, dcvn_ref, dzr_ref, x_ref, dx2_ref, mod_ref, ng_ref, cw_ref, w_ref,
             gx_ref, dz_ref, acc_ref, gcw_ref,
             gext, dext, dg, taps, dpad, qbuf):
        i = pl.program_id(0)

        @pl.when(i == 0)
        def _():
            acc_ref[...] = jnp.zeros_like(acc_ref)
            gcw_ref[...] = jnp.zeros_like(gcw_ref)

        not_first = i > 0
        not_last = i < n_steps - 1
        gext[0:HALO, :] = jnp.where(not_first, _glu(zp_ref), 0.0)
        gext[HALO:HALO + t, :] = _glu(z_ref)
        gext[HALO + t:2 * HALO + t, :] = jnp.where(not_last, _glu(zn_ref), 0.0)
        dext[0:HALO, :] = jnp.where(not_first, dcvp_ref[...], 0.0)
        dext[HALO:HALO + t, :] = dcv_ref[...]
        dext[HALO + t:2 * HALO + t, :] = jnp.where(not_last, dcvn_ref[...], 0.0)

        taps[...] = jnp.zeros_like(taps)
        dpad[0:SUB, :] = jnp.zeros((SUB, D), F32)
        dpad[SUB:SUB + t, :] = dcv_ref[...]
        dpad[SUB + t:2 * SUB + t, :] = jnp.zeros((SUB, D), F32)
        for blk in range(N_DEV):
            cols = slice(blk * CONV_BLK, (blk + 1) * CONV_BLK)
            dg[:, cols] = _conv_taps(dext, qbuf, cols, lambda k, blk=blk: cw_ref[blk, CONV_W - 1 - k:CONV_W - k, :], t)
            for b in range(SUB):
                dshift = dpad[pl.ds(SUB - 1 - b, t + SUB), cols]
                for a in range((CONV_W - b + SUB - 1) // SUB):
                    k = SUB * a + b
                    taps[k:k + 1, :] = _colsum(gext[SUB * a:SUB * a + t + SUB, cols] * dshift)
            gcw_ref[blk] += taps[...]

        a = z_ref[:, 0:D]
        sig = _sigmoid(z_ref[:, D:2 * D])
        dgv = dg[...]
        dz_ref[:, 0:D] = (dgv * sig).astype(BF16)
        dz_ref[:, D:2 * D] = (dgv * a * sig * (1.0 - sig)).astype(BF16)
        dz_ref[:, 2 * D:6 * D] = dzr_ref[...]

        dh = jnp.zeros((t, D), F32)
        for j in range(N_DEV):
            dh = dh + lax.dot_general(dz_ref[:, j * W_IN_BLK:(j + 1) * W_IN_BLK], w_ref[j], (((1,), (1,)), ((), ())),
                                      preferred_element_type=F32)

        xn, r, _, scale = _rms_modulate(x_ref[...], mod_ref)
        ng = ng_ref[...]
        one_scale = 1.0 + scale
        dh_xn = dh * xn
        acc_ref[0:1, :] += _colsum(dh_xn * one_scale)
        acc_ref[1:2, :] += _colsum(dh)
        acc_ref[2:3, :] += _colsum(dh_xn * ng)
        dxn = dh * (ng * one_scale)
        gx_ref[...] = dx2_ref[...] + r * (dxn - xn * jnp.mean(dxn * xn, axis=-1, keepdims=True))

    row = lambda w: pl.BlockSpec((t, w), lambda i: (i, 0))
    const = lambda shape: pl.BlockSpec(shape, lambda i: (0,) * len(shape))
    return pl.pallas_call(
        body, name="bwd_in_proj", grid=(n_steps,),
        out_shape=(jax.ShapeDtypeStruct((s, D), F32), jax.ShapeDtypeStruct((s, D_IN), BF16),
                   jax.ShapeDtypeStruct((8, D), F32), jax.ShapeDtypeStruct((N_DEV, CONV_ROWS, CONV_BLK), F32)),
        in_specs=[pl.BlockSpec((t, 2 * D), lambda i: (i, 0)), zp_spec, zn_spec, row(D), dp_spec, dn_spec, row(4 * D),
                  row(D), row(D), VM, VM, VM, VM],
        out_specs=(row(D), row(D_IN), const((8, D)), const((N_DEV, CONV_ROWS, CONV_BLK))),
        scratch_shapes=[pltpu.VMEM((t + 2 * HALO, D), F32), pltpu.VMEM((t + 2 * HALO, D), F32), pltpu.VMEM((t, D), F32),
                        pltpu.VMEM((CONV_ROWS, CONV_BLK), F32), pltpu.VMEM((t + 2 * SUB, D), F32),
                        pltpu.VMEM((SUB, CONV_PHASE_ROWS, CONV_BLK), F32)],
        compiler_params=_params(1),
    )(z, z, z, dcv, dcv, dcv, dz_rest, x, dx2, mod, norm_g, conv_w_g, win_g)


def _wgrad_in(ht, dz):
    s = dz.shape[0]

    def body(ht_ref, dz_ref, g_ref):
        g_ref[0] = jnp.dot(ht_ref[...], dz_ref[...], preferred_element_type=F32).astype(BF16)

    return pl.pallas_call(
        body, name="wgrad_in", grid=(N_DEV,),
        out_shape=jax.ShapeDtypeStruct((N_DEV, D, W_IN_BLK), BF16),
        in_specs=[VM, pl.BlockSpec((s, W_IN_BLK), lambda j: (0, j))],
        out_specs=pl.BlockSpec((1, D, W_IN_BLK), lambda j: (j, 0, 0)),
        compiler_params=_params(1),
    )(ht, dz)


def _wgrad_out(ycatt, dy):
    s = dy.shape[0]

    def body(yt_ref, dy_ref, g_ref):
        g_ref[0] = jnp.dot(yt_ref[...], dy_ref[...], preferred_element_type=F32).astype(BF16)

    return pl.pallas_call(
        body, name="wgrad_out", grid=(N_DEV,),
        out_shape=jax.ShapeDtypeStruct((N_DEV, W_OUT_BLK, D), BF16),
        in_specs=[pl.BlockSpec((W_OUT_BLK, s), lambda j: (j, 0)), VM],
        out_specs=pl.BlockSpec((1, W_OUT_BLK, D), lambda j: (j, 0, 0)),
        compiler_params=_params(1),
    )(ycatt, dy)


def _grad_reduce(g_in, g_out):
    row_chunk = 128

    def body(gin_ref, gout_ref, oin_ref, oout_ref, l1_in, l1_out, b2_in, b2_out, l2_in, l2_out,
             s1_send, s1_recv, s2_send, s2_recv):
        x, y, c = _mesh_pos()
        p = 2 * x + y
        sib = (x, y, 1 - c)
        chip_of = [(x, y), (x, 1 - y), (1 - x, y), (1 - x, 1 - y)]
        pr = [p, 2 * x + (1 - y), 2 * (1 - x) + y, 2 * (1 - x) + (1 - y)]
        pays = [(gin_ref, l1_in, b2_in, l2_in, oin_ref), (gout_ref, l1_out, b2_out, l2_out, oout_ref)]

        def step1(a, r):
            g, l1 = pays[a][0], pays[a][1]
            return pltpu.make_async_remote_copy(
                src_ref=g.at[2 * pr[r] + (1 - c)], dst_ref=l1.at[r], send_sem=s1_send.at[4 * a + r],
                recv_sem=s1_recv.at[4 * a + r], device_id=sib, device_id_type=MESH)

        def step2(a, r):
            b2, l2 = pays[a][2], pays[a][3]
            return pltpu.make_async_remote_copy(
                src_ref=b2.at[r - 1], dst_ref=l2.at[r - 1], send_sem=s2_send.at[3 * a + r - 1],
                recv_sem=s2_recv.at[3 * a + r - 1], device_id=(*chip_of[r], c), device_id_type=MESH)

        for r in (3, 1, 2, 0):
            for a in range(2):
                step1(a, r).start()

        def chunks(a):
            return pays[a][0].shape[1] // row_chunk

        for r in (3, 1, 2):
            for a in range(2):
                g, l1, b2 = pays[a][0], pays[a][1], pays[a][2]
                step1(a, r).wait_recv()
                mine = 2 * pr[r] + c

                def add(q, carry, g=g, l1=l1, b2=b2, mine=mine, r=r):
                    rows = pl.ds(pl.multiple_of(q * row_chunk, row_chunk), row_chunk)
                    b2[r - 1, rows, :] = (g[mine, rows, :].astype(F32) + l1[r, rows, :].astype(F32)).astype(BF16)
                    return carry

                lax.fori_loop(0, chunks(a), add, 0)
                step2(a, r).start()

        for a in range(2):
            step1(a, 0).wait_recv()
        for r in (1, 2, 3):
            for a in range(2):
                step2(a, r).wait_recv()
        for a in range(2):
            g, l1, l2, out = pays[a][0], pays[a][1], pays[a][3], pays[a][4]
            mine = 2 * p + c

            def total(q, carry, g=g, l1=l1, l2=l2, out=out, mine=mine):
                rows = pl.ds(pl.multiple_of(q * row_chunk, row_chunk), row_chunk)
                acc = g[mine, rows, :].astype(F32) + l1[0, rows, :].astype(F32)
                for r in range(3):
                    acc = acc + l2[r, rows, :].astype(F32)
                out[rows, :] = acc
                return carry

            lax.fori_loop(0, chunks(a), total, 0)
        for r in range(4):
            for a in range(2):
                step1(a, r).wait_send()
        for r in (1, 2, 3):
            for a in range(2):
                step2(a, r).wait_send()

    blk_in, blk_out = g_in.shape[1:], g_out.shape[1:]
    return pl.pallas_call(
        body, name="grad_reduce",
        out_shape=(jax.ShapeDtypeStruct(blk_in, F32), jax.ShapeDtypeStruct(blk_out, F32)),
        in_specs=[VM, VM], out_specs=(VM, VM),
        scratch_shapes=[pltpu.VMEM((4,) + blk_in, BF16), pltpu.VMEM((4,) + blk_out, BF16),
                        pltpu.VMEM((3,) + blk_in, BF16), pltpu.VMEM((3,) + blk_out, BF16),
                        pltpu.VMEM((3,) + blk_in, BF16), pltpu.VMEM((3,) + blk_out, BF16),
                        pltpu.SemaphoreType.DMA((8,)), pltpu.SemaphoreType.DMA((8,)),
                        pltpu.SemaphoreType.DMA((6,)), pltpu.SemaphoreType.DMA((6,))],
        compiler_params=_params(),
    )(g_in, g_out)


def _small_reduce(small, gws, dmod):
    def body(small_ref, gws_ref, dmod_ref, osmall_ref, ogws_ref, odmod_ref, obada_ref,
             small_g, gws_g, dmod_g, send_sems, recv_sems):
        me = _my_block()
        small_g[me] = small_ref[...]
        gws_g[me] = gws_ref[...]
        dmod_g[me] = dmod_ref[...]
        _allgather_blocks([dmod_g, small_g, gws_g], send_sems, recv_sems)
        tot_small, tot_gws, tot_dmod = small_g[0], gws_g[0], dmod_g[0]
        for b in range(1, N_DEV):
            tot_small = tot_small + small_g[b]
            tot_gws = tot_gws + gws_g[b]
            tot_dmod = tot_dmod + dmod_g[b]
        osmall_ref[...] = tot_small
        ogws_ref[...] = tot_gws
        obada_ref[...] = tot_dmod
        for b in range(N_DEV):
            odmod_ref[b:b + 1, :] = dmod_g[b, 0:1, :]

    return pl.pallas_call(
        body, name="small_reduce",
        out_shape=(jax.ShapeDtypeStruct(small.shape, F32), jax.ShapeDtypeStruct(gws.shape, F32),
                   jax.ShapeDtypeStruct((N_DEV, 3 * D), F32), jax.ShapeDtypeStruct((8, 3 * D), F32)),
        in_specs=[VM] * 3, out_specs=(VM,) * 4,
        scratch_shapes=[pltpu.VMEM((N_DEV,) + small.shape, F32), pltpu.VMEM((N_DEV,) + gws.shape, F32),
                        pltpu.VMEM((N_DEV,) + dmod.shape, F32),
                        pltpu.SemaphoreType.DMA((21,)), pltpu.SemaphoreType.DMA((21,))],
        compiler_params=_params(),
    )(small, gws, dmod)


def _wgrad_ada(c_all, dmod_cols):
    def body(c_ref, dm_ref, g_ref, act):
        for b in range(N_DEV):
            cb = c_ref[b, 0:1, :]
            act[b:b + 1, :] = cb * _sigmoid(cb)
        g_ref[...] = lax.dot_general(act[...], dm_ref[...], (((0,), (0,)), ((), ())), preferred_element_type=F32,
                                     precision=lax.Precision.HIGHEST)

    return pl.pallas_call(
        body, name="wgrad_ada", out_shape=jax.ShapeDtypeStruct((D, W_ADA_BLK), F32),
        in_specs=[VM, VM], out_specs=VM, scratch_shapes=[pltpu.VMEM((N_DEV, D), F32)],
        compiler_params=_params(),
    )(c_all, dmod_cols)


def _adamw_math(w, g, m, v):
    m = ADAM_B1 * m + (1.0 - ADAM_B1) * g
    v = ADAM_B2 * v + (1.0 - ADAM_B2) * (g * g)
    m_hat = m / (1.0 - ADAM_B1 ** ADAM_STEP)
    v_hat = v / (1.0 - ADAM_B2 ** ADAM_STEP)
    delta = -ADAM_LR * (m_hat / (jnp.sqrt(v_hat) + ADAM_EPS) + ADAM_WD * w)
    return delta, m, v


def _adamw(name, w, g, m, v, row_block=None):
    shape = w.shape

    def body(w_ref, g_ref, m_ref, v_ref, d_ref, nm_ref, nv_ref):
        d_ref[...], nm_ref[...], nv_ref[...] = _adamw_math(w_ref[...], g_ref[...], m_ref[...], v_ref[...])

    out_shape = (jax.ShapeDtypeStruct(shape, F32),) * 3
    if row_block is None:
        return pl.pallas_call(body, name=name, out_shape=out_shape, in_specs=[VM] * 4, out_specs=(VM,) * 3,
                              compiler_params=_params())(w, g, m, v)
    spec = pl.BlockSpec((row_block, shape[1]), lambda i: (i, 0))
    return pl.pallas_call(body, name=name, grid=(shape[0] // row_block,), out_shape=out_shape,
                          in_specs=[spec] * 4, out_specs=(spec,) * 3, compiler_params=_params(1))(w, g, m, v)


def kernel(x, c, w_ada, b_ada, norm_g, w_in, conv_w, conv_b, conv_ln_g, conv_ln_b, sg_ln_g, sg_ln_b, w_s, b_s, w_out, final_g, loss_target, m_w_ada, m_b_ada, m_norm_g, m_w_in, m_conv_w, m_conv_b, m_conv_ln_g, m_conv_ln_b, m_sg_ln_g, m_sg_ln_b, m_w_s, m_b_s, m_w_out, m_final_g, v_w_ada, v_b_ada, v_norm_g, v_w_in, v_conv_w, v_conv_b, v_conv_ln_g, v_conv_ln_b, v_sg_ln_g, v_sg_ln_b, v_w_s, v_b_s, v_w_out, v_final_g):
    me = 4 * lax.axis_index("x") + 2 * lax.axis_index("y") + lax.axis_index("c")
    x2d, tgt2d = x[0], loss_target[0]
    row1 = lambda a: a.reshape(1, D)
    taps = lambda a: jnp.pad(a.reshape(CONV_W, CONV_BLK), ((0, CONV_ROWS - CONV_W), (0, 0)))

    win_g, wout_g, cw_g, c_all = _gather_params(w_in[0], w_out[0], taps(conv_w), jnp.broadcast_to(c, (8, D)))
    mod = _mod_exchange(c_all, w_ada[0], b_ada)
    ws_b = w_s[0].astype(BF16)
    wst_b = jnp.swapaxes(w_s[0], 1, 2).astype(BF16)
    bs_full = jnp.repeat(b_s[0].T, HEAD_DIM, axis=1)

    z, ht = _fwd_in_proj(x2d, mod, norm_g, win_g)
    dx2, ycatt, dy, dcv, dz_rest, acc_a, gws, gbs = _mix_and_head(
        z, x2d, tgt2d, mod, cw_g, conv_b, conv_ln_g, conv_ln_b, sg_ln_g, sg_ln_b, row1(final_g), ws_b, wst_b, bs_full,
        wout_g.reshape(D_MIX, D))
    grad_x, dz, acc_b, gcw = _bwd_in_proj(z, dcv, dz_rest, x2d, dx2, mod, norm_g, cw_g, win_g)
    gin_part = _wgrad_in(ht, dz)
    gout_part = _wgrad_out(ycatt, dy)

    g_w_in, g_w_out = _grad_reduce(gin_part, gout_part)
    gbs_row = gbs.T.reshape(1, D)
    small = jnp.concatenate(
        [acc_b[0:1], acc_a[2:7], acc_a[0:1], gbs_row, jnp.transpose(gcw, (1, 0, 2)).reshape(CONV_ROWS, D)], axis=0)
    dmod_row = jnp.concatenate([acc_b[1:2], acc_b[2:3], acc_a[1:2]], axis=1)
    small_sum, gws_sum, dmod_all, bada8 = _small_reduce(small, gws.reshape(HEADS * CHUNK, CHUNK),
                                                        jnp.broadcast_to(dmod_row, (8, 3 * D)))
    loss = lax.psum(0.5 / D * jnp.sum(acc_a[7]), ("x", "y", "c"))

    g_w_ada = _wgrad_ada(c_all, lax.dynamic_slice(dmod_all, (0, me * W_ADA_BLK), (N_DEV, W_ADA_BLK)))
    g_b_ada = bada8[0:1]
    g_vec = small_sum[0:8]
    g_conv_w = lax.dynamic_slice(small_sum, (8, me * CONV_BLK), (CONV_W, CONV_BLK))

    vec = lambda *a: jnp.concatenate([row1(t) for t in a], axis=0)
    d_vec, m_vec, v_vec = _adamw(
        "adamw_vectors",
        vec(norm_g, conv_b, conv_ln_g, conv_ln_b, sg_ln_g, sg_ln_b, final_g, b_s), g_vec,
        vec(m_norm_g, m_conv_b, m_conv_ln_g, m_conv_ln_b, m_sg_ln_g, m_sg_ln_b, m_final_g, m_b_s),
        vec(v_norm_g, v_conv_b, v_conv_ln_g, v_conv_ln_b, v_sg_ln_g, v_sg_ln_b, v_final_g, v_b_s))
    flat_ws = lambda a: a.reshape(HEADS * CHUNK, CHUNK)
    upd = {
        "w_ada": _adamw("adamw_w_ada", w_ada[0], g_w_ada, m_w_ada[0], v_w_ada[0], 256),
        "b_ada": _adamw("adamw_b_ada", b_ada, g_b_ada, m_b_ada, v_b_ada),
        "w_in": _adamw("adamw_w_in", w_in[0], g_w_in, m_w_in[0], v_w_in[0], 256),
        "conv_w": _adamw("adamw_conv_w", conv_w.reshape(CONV_W, CONV_BLK), g_conv_w, m_conv_w.reshape(CONV_W, CONV_BLK),
                         v_conv_w.reshape(CONV_W, CONV_BLK)),
        "w_s": _adamw("adamw_w_s", flat_ws(w_s), gws_sum, flat_ws(m_w_s), flat_ws(v_w_s)),
        "w_out": _adamw("adamw_w_out", w_out[0], g_w_out, m_w_out[0], v_w_out[0], 128),
    }
    grads = {"w_ada": g_w_ada, "b_ada": g_b_ada, "w_in": g_w_in, "conv_w": g_conv_w, "w_s": gws_sum, "w_out": g_w_out}
    vec_names = ["norm_g", "conv_b", "conv_ln_g", "conv_ln_b", "sg_ln_g", "sg_ln_b", "final_g", "b_s"]
    shapes = {"w_ada": w_ada.shape, "b_ada": b_ada.shape, "norm_g": norm_g.shape, "w_in": w_in.shape,
              "conv_w": conv_w.shape, "conv_b": conv_b.shape, "conv_ln_g": conv_ln_g.shape, "conv_ln_b": conv_ln_b.shape,
              "sg_ln_g": sg_ln_g.shape, "sg_ln_b": sg_ln_b.shape, "w_s": w_s.shape, "b_s": b_s.shape,
              "w_out": w_out.shape, "final_g": final_g.shape}
    order = ["w_ada", "b_ada", "norm_g", "w_in", "conv_w", "conv_b", "conv_ln_g", "conv_ln_b", "sg_ln_g", "sg_ln_b",
             "w_s", "b_s", "w_out", "final_g"]

    def leaf(kind, name):
        if name in vec_names:
            src = (g_vec, d_vec, m_vec, v_vec)[kind]
            val = src[vec_names.index(name)]
        elif kind == 0:
            val = grads[name]
        else:
            val = upd[name][kind - 1]
        return val.reshape(shapes[name])

    outs = [loss, grad_x.reshape(x.shape)]
    for kind in range(4):
        outs += [leaf(kind, n) for n in order]
    return tuple(outs)
```

```python
import functools

import jax
import jax.numpy as jnp
from jax import lax
from jax.experimental import pallas as pl
from jax.experimental.pallas import tpu as pltpu

F32 = jnp.float32
BF16 = jnp.bfloat16
MESH = pl.DeviceIdType.MESH

D = 1024
D_IN = 6 * D
D_MIX = 2 * D
N_DEV = 8
W_IN_BLK = D_IN // N_DEV
W_OUT_BLK = D_MIX // N_DEV
W_ADA_BLK = 3 * D // N_DEV
CONV_BLK = D // N_DEV
CONV_W = 31
CONV_HALF = CONV_W // 2
CONV_ROWS = 32
HALO = 16
CHUNK = 128
HEADS = 8
HEAD_DIM = 128
EPS = 1e-6
ROW_TILE = 256
SMALL_ROWS = 40
VMEM_LIMIT = 56 * 1024 * 1024

ADAM_LR = 0.001
ADAM_B1 = 0.9
ADAM_B2 = 0.999
ADAM_EPS = 1e-08
ADAM_WD = 0.01
ADAM_STEP = 10

VM = pl.BlockSpec(memory_space=pltpu.VMEM)


def _params(grid_rank=0, **kw):
    sem = ("arbitrary",) * grid_rank if grid_rank else None
    return pltpu.CompilerParams(dimension_semantics=sem, vmem_limit_bytes=VMEM_LIMIT, **kw)


def _sigmoid(t):
    return jax.nn.sigmoid(t)


def _dsilu(t, sig):
    return sig * (1.0 + t * (1.0 - sig))


def _mesh_pos():
    return lax.axis_index("x"), lax.axis_index("y"), lax.axis_index("c")


def _allgather_blocks(bufs, send_sems, recv_sems):
    _allgather_start(bufs, send_sems, recv_sems)
    _allgather_pass_on(bufs, send_sems, recv_sems)
    _allgather_finish(bufs, send_sems, recv_sems)


def _allgather_copy(bufs, send_sems, recv_sems, a, k, block, to):
    px, py, pc = block
    ref = bufs[a].at[4 * px + 2 * py + pc]
    return pltpu.make_async_remote_copy(
        src_ref=ref, dst_ref=ref, send_sem=send_sems.at[7 * a + k], recv_sem=recv_sems.at[7 * a + k],
        device_id=to, device_id_type=MESH)


def _allgather_places():
    x, y, c = _mesh_pos()
    return (x, y, c), (x, y, 1 - c), [(x, 1 - y), (1 - x, y), (1 - x, 1 - y)], c


def _allgather_start(bufs, send_sems, recv_sems):
    me, sib, chips, c = _allgather_places()
    for a in range(len(bufs)):
        _allgather_copy(bufs, send_sems, recv_sems, a, 0, me, sib).start()
        for j, chip in enumerate(chips):
            _allgather_copy(bufs, send_sems, recv_sems, a, 1 + j, me, (*chip, c)).start()


def _allgather_pass_on(bufs, send_sems, recv_sems):
    me, sib, chips, c = _allgather_places()
    for j, chip in enumerate(chips):
        for a in range(len(bufs)):
            _allgather_copy(bufs, send_sems, recv_sems, a, 1 + j, (*chip, c), me).wait_recv()
            _allgather_copy(bufs, send_sems, recv_sems, a, 4 + j, (*chip, c), sib).start()


def _allgather_finish(bufs, send_sems, recv_sems):
    me, sib, chips, c = _allgather_places()
    for a in range(len(bufs)):
        _allgather_copy(bufs, send_sems, recv_sems, a, 0, sib, me).wait_recv()
    for j, chip in enumerate(chips):
        for a in range(len(bufs)):
            _allgather_copy(bufs, send_sems, recv_sems, a, 4 + j, (*chip, 1 - c), me).wait_recv()
    for a in range(len(bufs)):
        _allgather_copy(bufs, send_sems, recv_sems, a, 0, me, sib).wait_send()
        for j, chip in enumerate(chips):
            _allgather_copy(bufs, send_sems, recv_sems, a, 1 + j, me, (*chip, c)).wait_send()
            _allgather_copy(bufs, send_sems, recv_sems, a, 4 + j, (*chip, c), sib).wait_send()


def _my_block():
    x, y, c = _mesh_pos()
    return 4 * x + 2 * y + c


def _gather_params(w_in, w_out, conv_w, c_rep):
    def body(w_in_ref, w_out_ref, cw_ref, c_ref, win_g, wout_g, cw_g, c_g, send_sems, recv_sems):
        me = _my_block()
        win_g[me] = w_in_ref[...].astype(BF16)
        wout_g[me] = w_out_ref[...].astype(BF16)
        cw_g[me] = cw_ref[...]
        c_g[me] = c_ref[...]
        _allgather_blocks([c_g, cw_g, wout_g, win_g], send_sems, recv_sems)

    return pl.pallas_call(
        body, name="gather_params",
        out_shape=(jax.ShapeDtypeStruct((N_DEV,) + w_in.shape, BF16),
                   jax.ShapeDtypeStruct((N_DEV,) + w_out.shape, BF16),
                   jax.ShapeDtypeStruct((N_DEV,) + conv_w.shape, F32),
                   jax.ShapeDtypeStruct((N_DEV,) + c_rep.shape, F32)),
        in_specs=[VM] * 4, out_specs=(VM,) * 4,
        scratch_shapes=[pltpu.SemaphoreType.DMA((28,)), pltpu.SemaphoreType.DMA((28,))],
        compiler_params=_params(),
    )(w_in, w_out, conv_w, c_rep)


def _mod_exchange(c_all, w_ada, b_ada):
    def body(c_ref, w_ref, b_ref, mod_ref, part, land, send_sems, recv_sems):
        x, y, c = _mesh_pos()
        me = 4 * x + 2 * y + c
        w = w_ref[...]
        for b in range(N_DEV):
            cb = c_ref[b]
            act = cb * _sigmoid(cb)
            part[b] = jnp.dot(act, w, preferred_element_type=F32, precision=lax.Precision.HIGHEST)
        land[me] = part[me]

        def copy(b):
            return pltpu.make_async_remote_copy(
                src_ref=part.at[b], dst_ref=land.at[me], send_sem=send_sems.at[b], recv_sem=recv_sems.at[me],
                device_id=(b // 4, (b // 2) % 2, b % 2), device_id_type=MESH)

        def arrival(b):
            return pltpu.make_async_remote_copy(
                src_ref=part.at[b], dst_ref=land.at[b], send_sem=send_sems.at[b], recv_sem=recv_sems.at[b],
                device_id=(b // 4, (b // 2) % 2, b % 2), device_id_type=MESH)

        for b in range(N_DEV):
            @pl.when(b != me)
            def _():
                copy(b).start()
        for b in range(N_DEV):
            @pl.when(b != me)
            def _():
                arrival(b).wait_recv()
                copy(b).wait_send()
        for b in range(N_DEV):
            cols = slice(b * W_ADA_BLK, (b + 1) * W_ADA_BLK)
            mod_ref[:, cols] = land[b] + b_ref[:, cols]

    return pl.pallas_call(
        body, name="mod_exchange",
        out_shape=jax.ShapeDtypeStruct((8, 3 * D), F32),
        in_specs=[VM] * 3, out_specs=VM,
        scratch_shapes=[pltpu.VMEM((N_DEV, 8, W_ADA_BLK), F32), pltpu.VMEM((N_DEV, 8, W_ADA_BLK), F32),
                        pltpu.SemaphoreType.DMA((N_DEV,)), pltpu.SemaphoreType.DMA((N_DEV,))],
        compiler_params=_params(),
    )(c_all, w_ada, b_ada)


def _rms_modulate(x, mod_ref):
    shift = mod_ref[0:1, 0:D]
    scale = mod_ref[0:1, D:2 * D]
    r = lax.rsqrt(jnp.mean(x * x, axis=-1, keepdims=True) + EPS)
    xn = x * r
    return xn, r, shift, scale


def _fwd_in_proj(x, mod, norm_g, win_g):
    s = x.shape[0]
    t = ROW_TILE

    def body(x_ref, mod_ref, ng_ref, w_ref, z_ref, ht_ref):
        xn, _, shift, scale = _rms_modulate(x_ref[...], mod_ref)
        h = xn * ng_ref[...] * (1.0 + scale) + shift
        hb = h.astype(BF16)
        ht_ref[...] = h.T.astype(BF16)
        for j in range(N_DEV):
            z_ref[:, j * W_IN_BLK:(j + 1) * W_IN_BLK] = jnp.dot(hb, w_ref[j], preferred_element_type=F32)

    return pl.pallas_call(
        body, name="fwd_in_proj", grid=(s // t,),
        out_shape=(jax.ShapeDtypeStruct((s, D_IN), F32), jax.ShapeDtypeStruct((D, s), BF16)),
        in_specs=[pl.BlockSpec((t, D), lambda i: (i, 0)), VM, VM, VM],
        out_specs=(pl.BlockSpec((t, D_IN), lambda i: (i, 0)), pl.BlockSpec((D, t), lambda i: (0, i))),
        compiler_params=_params(1),
    )(x, mod, norm_g, win_g)


def _halo_specs(t, s, width):
    per = t // HALO
    last = s // HALO - 1
    prev = pl.BlockSpec((HALO, width), lambda i: (jnp.maximum(i * per - 1, 0), 0))
    nxt = pl.BlockSpec((HALO, width), lambda i: (jnp.minimum((i + 1) * per, last), 0))
    return prev, nxt


def _glu(ref):
    return ref[:, 0:D] * _sigmoid(ref[:, D:2 * D])


SUB = 8
CONV_PHASE_ROWS = ROW_TILE + SUB


def _conv_taps(ext, qbuf, cols, tap, t):
    out = None
    for b in range(SUB):
        q = None
        for a in range((CONV_W - b + SUB - 1) // SUB):
            term = ext[SUB * a:SUB * a + t + SUB, cols] * tap(SUB * a + b)
            q = term if q is None else q + term
        qbuf[b] = q
        shifted = qbuf[b, pl.ds(b + 1, t), :]
        out = shifted if out is None else out + shifted
    return out


def _layer_norm_stats(v):
    mu = jnp.mean(v, axis=-1, keepdims=True)
    cen = v - mu
    rstd = lax.rsqrt(jnp.mean(cen * cen, axis=-1, keepdims=True) + EPS)
    return cen * rstd, rstd


def _layer_norm_bwd(dy_hat, hat, rstd):
    m1 = jnp.mean(dy_hat, axis=-1, keepdims=True)
    m2 = jnp.mean(dy_hat * hat, axis=-1, keepdims=True)
    return rstd * (dy_hat - m1 - hat * m2)


def _colsum(v):
    return jnp.sum(v, axis=0, keepdims=True)


def _mix_and_head(z, x, tgt, mod, conv_w_g, conv_b, cln_g, cln_b, sln_g, sln_b, final_g, ws_b, wst_b, bs_full, wout):
    s = x.shape[0]
    t = ROW_TILE
    n_chunks = t // CHUNK
    n_steps = s // t
    prev_spec, next_spec = _halo_specs(t, s, 2 * D)

    def body(z_ref, zp_ref, zn_ref, x_ref, tgt_ref, mod_ref, cw_ref, cb_ref, clg_ref, clb_ref, slg_ref, slb_ref, fg_ref,
             ws_ref, wst_ref, bs_ref, wout_ref,
             dx2_ref, ycatt_ref, dy_ref, dcv_ref, dzr_ref, acc_ref, gws_ref, gbs_ref,
             gext, cv, vs, dvn, ycat, gbs_acc, qbuf):
        i = pl.program_id(0)

        @pl.when(i == 0)
        def _():
            acc_ref[...] = jnp.zeros_like(acc_ref)
            gws_ref[...] = jnp.zeros_like(gws_ref)
            gbs_acc[...] = jnp.zeros_like(gbs_acc)

        gext[0:HALO, :] = jnp.where(i > 0, _glu(zp_ref), 0.0)
        gext[HALO:HALO + t, :] = _glu(z_ref)
        gext[HALO + t:2 * HALO + t, :] = jnp.where(i < n_steps - 1, _glu(zn_ref), 0.0)
        for blk in range(N_DEV):
            cols = slice(blk * CONV_BLK, (blk + 1) * CONV_BLK)
            cv[:, cols] = _conv_taps(gext, qbuf, cols, lambda k, blk=blk: cw_ref[blk, k:k + 1, :], t) + cb_ref[:, cols]
        ln_hat, ln_rstd = _layer_norm_stats(cv[...])
        ln_a = ln_hat * clg_ref[...] + clb_ref[...]
        sig_ln = _sigmoid(ln_a)
        sa = ln_a * sig_ln
        a_gate = z_ref[:, 2 * D:3 * D]
        sig_ag = _sigmoid(a_gate)
        s_gate = a_gate * sig_ag
        ya = sa * s_gate

        v_hat, v_rstd = _layer_norm_stats(z_ref[:, 4 * D:5 * D])
        vn = v_hat * slg_ref[...] + slb_ref[...]
        vnb = vn.astype(BF16)
        for n in range(n_chunks):
            rows = slice(n * CHUNK, (n + 1) * CHUNK)
            for h in range(HEADS):
                cols = slice(h * HEAD_DIM, (h + 1) * HEAD_DIM)
                vs[rows, cols] = jnp.dot(ws_ref[h], vnb[rows, cols], preferred_element_type=F32) + bs_ref[:, cols]
        u = z_ref[:, 3 * D:4 * D]
        b_gate = z_ref[:, 5 * D:6 * D]
        sig_bg = _sigmoid(b_gate)
        s_bg = b_gate * sig_bg
        vsv = vs[...]
        yb = u * vsv * s_bg

        ycat[:, 0:D] = ya.astype(BF16)
        ycat[:, D:2 * D] = yb.astype(BF16)
        ycatt_ref[0:D, :] = ya.T.astype(BF16)
        ycatt_ref[D:2 * D, :] = yb.T.astype(BF16)
        y = jnp.dot(ycat[...], wout_ref[...], preferred_element_type=F32)
        gate = mod_ref[0:1, 2 * D:3 * D]
        x2 = x_ref[...] + gate * y
        r2 = lax.rsqrt(jnp.mean(x2 * x2, axis=-1, keepdims=True) + EPS)
        x2n = x2 * r2
        fg = fg_ref[...]
        diff = x2n * fg - tgt_ref[...]
        acc_ref[7:8, :] += _colsum(diff * diff)
        dout = diff * (1.0 / D)
        acc_ref[0:1, :] += _colsum(dout * x2n)
        dx2n = dout * fg
        dx2 = r2 * (dx2n - x2n * jnp.mean(dx2n * x2n, axis=-1, keepdims=True))
        dx2_ref[...] = dx2
        acc_ref[1:2, :] += _colsum(dx2 * y)
        dyb16 = (dx2 * gate).astype(BF16)
        dy_ref[...] = dyb16
        dycat = lax.dot_general(dyb16, wout_ref[...], (((1,), (1,)), ((), ())), preferred_element_type=F32)
        dya = dycat[:, 0:D]
        dyb = dycat[:, D:2 * D]

        du = dyb * vsv * s_bg
        dvs = dyb * u * s_bg
        dbg = dyb * u * vsv * _dsilu(b_gate, sig_bg)
        dvsb = dvs.astype(BF16)
        gbs = gbs_acc[...]
        for n in range(n_chunks):
            rows = slice(n * CHUNK, (n + 1) * CHUNK)
            gbs = gbs + dvs[rows, :]
            for h in range(HEADS):
                cols = slice(h * HEAD_DIM, (h + 1) * HEAD_DIM)
                gws_ref[h] += lax.dot_general(dvsb[rows, cols], vnb[rows, cols], (((1,), (1,)), ((), ())),
                                              preferred_element_type=F32)
                dvn[rows, cols] = jnp.dot(wst_ref[h], dvsb[rows, cols], preferred_element_type=F32)
        gbs_acc[...] = gbs

        @pl.when(i == n_steps - 1)
        def _():
            for h in range(HEADS):
                gbs_ref[:, h:h + 1] = jnp.sum(gbs_acc[:, h * HEAD_DIM:(h + 1) * HEAD_DIM], axis=1, keepdims=True)

        dvnv = dvn[...]
        acc_ref[5:6, :] += _colsum(dvnv * v_hat)
        acc_ref[6:7, :] += _colsum(dvnv)
        dv = _layer_norm_bwd(dvnv * slg_ref[...], v_hat, v_rstd)

        dsa = dya * s_gate
        dagate = dya * sa * _dsilu(a_gate, sig_ag)
        dln = dsa * _dsilu(ln_a, sig_ln)
        acc_ref[3:4, :] += _colsum(dln * ln_hat)
        acc_ref[4:5, :] += _colsum(dln)
        dcv = _layer_norm_bwd(dln * clg_ref[...], ln_hat, ln_rstd)
        acc_ref[2:3, :] += _colsum(dcv)
        dcv_ref[...] = dcv

        dzr_ref[:, 0:D] = dagate.astype(BF16)
        dzr_ref[:, D:2 * D] = du.astype(BF16)
        dzr_ref[:, 2 * D:3 * D] = dv.astype(BF16)
        dzr_ref[:, 3 * D:4 * D] = dbg.astype(BF16)

    row = lambda w: pl.BlockSpec((t, w), lambda i: (i, 0))
    const = lambda shape: pl.BlockSpec(shape, lambda i: (0,) * len(shape))
    return pl.pallas_call(
        body, name="mix_and_head", grid=(n_steps,),
        out_shape=(jax.ShapeDtypeStruct((s, D), F32),
                   jax.ShapeDtypeStruct((D_MIX, s), BF16),
                   jax.ShapeDtypeStruct((s, D), BF16),
                   jax.ShapeDtypeStruct((s, D), F32),
                   jax.ShapeDtypeStruct((s, 4 * D), BF16),
                   jax.ShapeDtypeStruct((8, D), F32),
                   jax.ShapeDtypeStruct((HEADS, CHUNK, CHUNK), F32),
                   jax.ShapeDtypeStruct((CHUNK, HEADS), F32)),
        in_specs=[row(D_IN), prev_spec, next_spec, row(D), row(D)] + [VM] * 12,
        out_specs=(row(D), pl.BlockSpec((D_MIX, t), lambda i: (0, i)), row(D), row(D), row(4 * D),
                   const((8, D)), const((HEADS, CHUNK, CHUNK)), const((CHUNK, HEADS))),
        scratch_shapes=[pltpu.VMEM((t + 2 * HALO, D), F32), pltpu.VMEM((t, D), F32), pltpu.VMEM((t, D), F32),
                        pltpu.VMEM((t, D), F32), pltpu.VMEM((t, D_MIX), BF16), pltpu.VMEM((CHUNK, D), F32),
                        pltpu.VMEM((SUB, CONV_PHASE_ROWS, CONV_BLK), F32)],
        compiler_params=_params(1),
    )(z, z, z, x, tgt, mod, conv_w_g, conv_b, cln_g, cln_b, sln_g, sln_b, final_g, ws_b, wst_b, bs_full, wout)


def _bwd_in_proj(z, dcv, dz_rest, x, dx2, mod, norm_g, conv_w_g, win_g):
    s = x.shape[0]
    t = ROW_TILE
    n_steps = s // t
    zp_spec, zn_spec = _halo_specs(t, s, 2 * D)
    dp_spec, dn_spec = _halo_specs(t, s, D)

    def body(z_ref, zp_ref, zn_ref, dcv_ref, dcvp_ref, dcvn_ref, dzr_ref, x_ref, dx2_ref, mod_ref, ng_ref, cw_ref, w_ref,
             gx_ref, dz_ref, acc_ref, gcw_ref,
             gext, dext, dg, taps, dpad, qbuf):
        i = pl.program_id(0)

        @pl.when(i == 0)
        def _():
            acc_ref[...] = jnp.zeros_like(acc_ref)
            gcw_ref[...] = jnp.zeros_like(gcw_ref)

        not_first = i > 0
        not_last = i < n_steps - 1
        gext[0:HALO, :] = jnp.where(not_first, _glu(zp_ref), 0.0)
        gext[HALO:HALO + t, :] = _glu(z_ref)
        gext[HALO + t:2 * HALO + t, :] = jnp.where(not_last, _glu(zn_ref), 0.0)
        dext[0:HALO, :] = jnp.where(not_first, dcvp_ref[...], 0.0)
        dext[HALO:HALO + t, :] = dcv_ref[...]
        dext[HALO + t:2 * HALO + t, :] = jnp.where(not_last, dcvn_ref[...], 0.0)

        taps[...] = jnp.zeros_like(taps)
        dpad[0:SUB, :] = jnp.zeros((SUB, D), F32)
        dpad[SUB:SUB + t, :] = dcv_ref[...]
        dpad[SUB + t:2 * SUB + t, :] = jnp.zeros((SUB, D), F32)
        for blk in range(N_DEV):
            cols = slice(blk * CONV_BLK, (blk + 1) * CONV_BLK)
            dg[:, cols] = _conv_taps(dext, qbuf, cols, lambda k, blk=blk: cw_ref[blk, CONV_W - 1 - k:CONV_W - k, :], t)
            for b in range(SUB):
                dshift = dpad[pl.ds(SUB - 1 - b, t + SUB), cols]
                for a in range((CONV_W - b + SUB - 1) // SUB):
                    k = SUB * a + b
                    taps[k:k + 1, :] = _colsum(gext[SUB * a:SUB * a + t + SUB, cols] * dshift)
            gcw_ref[blk] += taps[...]

        a = z_ref[:, 0:D]
        sig = _sigmoid(z_ref[:, D:2 * D])
        dgv = dg[...]
        dz_ref[:, 0:D] = (dgv * sig).astype(BF16)
        dz_ref[:, D:2 * D] = (dgv * a * sig * (1.0 - sig)).astype(BF16)
        dz_ref[:, 2 * D:6 * D] = dzr_ref[...]

        dh = jnp.zeros((t, D), F32)
        for j in range(N_DEV):
            dh = dh + lax.dot_general(dz_ref[:, j * W_IN_BLK:(j + 1) * W_IN_BLK], w_ref[j], (((1,), (1,)), ((), ())),
                                      preferred_element_type=F32)

        xn, r, _, scale = _rms_modulate(x_ref[...], mod_ref)
        ng = ng_ref[...]
        one_scale = 1.0 + scale
        dh_xn = dh * xn
        acc_ref[0:1, :] += _colsum(dh_xn * one_scale)
        acc_ref[1:2, :] += _colsum(dh)
        acc_ref[2:3, :] += _colsum(dh_xn * ng)
        dxn = dh * (ng * one_scale)
        gx_ref[...] = dx2_ref[...] + r * (dxn - xn * jnp.mean(dxn * xn, axis=-1, keepdims=True))

    row = lambda w: pl.BlockSpec((t, w), lambda i: (i, 0))
    const = lambda shape: pl.BlockSpec(shape, lambda i: (0,) * len(shape))
    return pl.pallas_call(
        body, name="bwd_in_proj", grid=(n_steps,),
        out_shape=(jax.ShapeDtypeStruct((s, D), F32), jax.ShapeDtypeStruct((s, D_IN), BF16),
                   jax.ShapeDtypeStruct((8, D), F32), jax.ShapeDtypeStruct((N_DEV, CONV_ROWS, CONV_BLK), F32)),
        in_specs=[pl.BlockSpec((t, 2 * D), lambda i: (i, 0)), zp_spec, zn_spec, row(D), dp_spec, dn_spec, row(4 * D),
                  row(D), row(D), VM, VM, VM, VM],
        out_specs=(row(D), row(D_IN), const((8, D)), const((N_DEV, CONV_ROWS, CONV_BLK))),
        scratch_shapes=[pltpu.VMEM((t + 2 * HALO, D), F32), pltpu.VMEM((t + 2 * HALO, D), F32), pltpu.VMEM((t, D), F32),
                        pltpu.VMEM((CONV_ROWS, CONV_BLK), F32), pltpu.VMEM((t + 2 * SUB, D), F32),
                        pltpu.VMEM((SUB, CONV_PHASE_ROWS, CONV_BLK), F32)],
        compiler_params=_params(1),
    )(z, z, z, dcv, dcv, dcv, dz_rest, x, dx2, mod, norm_g, conv_w_g, win_g)


WGRAD_K = 1024
_OTHER_CHIPS = [(3, True), (3, False), (1, True), (1, False), (2, True), (2, False)]
_OWN_CHIP = [(0, True), (0, False)]
WGRAD_ROLES = ([("out",) + e for e in _OTHER_CHIPS] + [("in",) + e for e in _OTHER_CHIPS]
               + [("out",) + e for e in _OWN_CHIP] + [("in",) + e for e in _OWN_CHIP])


def _wgrad_schedule():
    x, y, c = _mesh_pos()
    chip = [2 * x + y, 2 * x + (1 - y), 2 * (1 - x) + y, 2 * (1 - x) + (1 - y)]
    blk = lambda r, sibling: 2 * chip[r] + ((1 - c) if sibling else c)
    out_blk, in_blk, is_out = [], [], []
    last = {"out": blk(*_OTHER_CHIPS[0]), "in": blk(*_OTHER_CHIPS[0])}
    for kind, r, sibling in WGRAD_ROLES:
        last[kind] = blk(r, sibling)
        out_blk.append(last["out"])
        in_blk.append(last["in"])
        is_out.append(1 if kind == "out" else 0)
    as_vec = lambda v: jnp.stack([jnp.asarray(e, jnp.int32) for e in v])
    return as_vec(out_blk), as_vec(in_blk), as_vec(is_out)


def _wgrad_reduce(ht, dz, ycatt, dy, small, gws, dmod):
    s = dz.shape[0]
    n_kc = s // WGRAD_K
    n_steps = len(WGRAD_ROLES)
    first_in = [k for k, role in enumerate(WGRAD_ROLES) if role[0] == "in"][0]
    blk_in, blk_out = (D, W_IN_BLK), (W_OUT_BLK, D)

    def body(out_blk, in_blk, is_out, ht_ref, dz_ref, yt_ref, dy_ref, small_ref, gws_ref, dmod_ref,
             oin_ref, oout_ref, osmall_ref, ogws_ref, odmod_ref, obada_ref,
             acc_in, acc_out, p1_in, p1_out, l1_in, l1_out, l2_in, l2_out, small_g, gws_g, dmod_g,
             s1_send, s1_recv, s2_send, s2_recv, ag_send, ag_recv):
        step, kc = pl.program_id(0), pl.program_id(1)
        x, y, c = _mesh_pos()
        sib = (x, y, 1 - c)
        chip_of = [(x, y), (x, 1 - y), (1 - x, y), (1 - x, 1 - y)]
        pay = {"in": (0, acc_in, p1_in, l1_in, l2_in, oin_ref), "out": (1, acc_out, p1_out, l1_out, l2_out, oout_ref)}
        ag_bufs = [dmod_g, small_g, gws_g]
        last_kc = kc == n_kc - 1

        @pl.when((step == 0) & (kc == 0))
        def _():
            me = 4 * x + 2 * y + c
            small_g[me] = small_ref[...]
            gws_g[me] = gws_ref[...]
            dmod_g[me] = dmod_ref[...]
            _allgather_start(ag_bufs, ag_send, ag_recv)

        @pl.when((step == 4) & (kc == 0))
        def _():
            _allgather_pass_on(ag_bufs, ag_send, ag_recv)

        def accumulate(acc, prod):
            @pl.when(kc == 0)
            def _():
                acc[...] = prod

            @pl.when(kc != 0)
            def _():
                acc[...] += prod

        @pl.when(is_out[step] == 1)
        def _():
            accumulate(acc_out, jnp.dot(yt_ref[...], dy_ref[...], preferred_element_type=F32))

        @pl.when(is_out[step] == 0)
        def _():
            accumulate(acc_in, jnp.dot(ht_ref[...], dz_ref[...], preferred_element_type=F32))

        def step1(kind, r):
            a, _, p1, l1, _, _ = pay[kind]
            return pltpu.make_async_remote_copy(
                src_ref=p1.at[r], dst_ref=l1.at[r], send_sem=s1_send.at[4 * a + r], recv_sem=s1_recv.at[4 * a + r],
                device_id=sib, device_id_type=MESH)

        def step2(kind, r):
            a, _, _, l1, l2, _ = pay[kind]
            return pltpu.make_async_remote_copy(
                src_ref=l1.at[r], dst_ref=l2.at[r - 1], send_sem=s2_send.at[3 * a + r - 1],
                recv_sem=s2_recv.at[3 * a + r - 1], device_id=(*chip_of[r], c), device_id_type=MESH)

        for k, (kind, r, sibling) in enumerate(WGRAD_ROLES):
            @pl.when((step == k) & last_kc)
            def _(kind=kind, r=r, sibling=sibling):
                _, acc, p1, l1, _, out = pay[kind]
                if sibling:
                    p1[r] = acc[...].astype(BF16)
                    step1(kind, r).start()
                else:
                    step1(kind, r).wait_recv()
                    both = acc[...] + l1[r].astype(F32)
                    if r:
                        l1[r] = both.astype(BF16)
                        step2(kind, r).start()
                    else:
                        out[...] = both

        @pl.when((step == n_steps - 1) & last_kc)
        def _():
            for kind in ("out", "in"):
                _, _, _, _, l2, out = pay[kind]
                total = out[...]
                for r in (1, 2, 3):
                    step2(kind, r).wait_recv()
                    total = total + l2[r - 1].astype(F32)
                out[...] = total
            for kind in ("out", "in"):
                for r in range(4):
                    step1(kind, r).wait_send()
                for r in (1, 2, 3):
                    step2(kind, r).wait_send()
            _allgather_finish(ag_bufs, ag_send, ag_recv)
            tot_small, tot_gws, tot_dmod = small_g[0], gws_g[0], dmod_g[0]
            for b in range(1, N_DEV):
                tot_small = tot_small + small_g[b]
                tot_gws = tot_gws + gws_g[b]
                tot_dmod = tot_dmod + dmod_g[b]
            osmall_ref[...] = tot_small
            ogws_ref[...] = tot_gws
            obada_ref[...] = tot_dmod
            for b in range(N_DEV):
                odmod_ref[b:b + 1, :] = dmod_g[b, 0:1, :]

    def kc_of(working, step, kc, hold_first):
        held = jnp.where(step < hold_first, 0, n_kc - 1)
        return jnp.where(working, kc, held)

    out_kc = lambda i, kc, ob, ib, io: kc_of(io[i] == 1, i, kc, 0)
    in_kc = lambda i, kc, ob, ib, io: kc_of(io[i] == 0, i, kc, first_in)
    grid_spec = pltpu.PrefetchScalarGridSpec(
        num_scalar_prefetch=3, grid=(n_steps, n_kc),
        in_specs=[pl.BlockSpec((D, WGRAD_K), lambda i, kc, ob, ib, io: (0, in_kc(i, kc, ob, ib, io))),
                  pl.BlockSpec((WGRAD_K, W_IN_BLK), lambda i, kc, ob, ib, io: (in_kc(i, kc, ob, ib, io), ib[i])),
                  pl.BlockSpec((W_OUT_BLK, WGRAD_K), lambda i, kc, ob, ib, io: (ob[i], out_kc(i, kc, ob, ib, io))),
                  pl.BlockSpec((WGRAD_K, D), lambda i, kc, ob, ib, io: (out_kc(i, kc, ob, ib, io), 0)),
                  VM, VM, VM],
        out_specs=(VM,) * 6,
        scratch_shapes=[pltpu.VMEM(blk_in, F32), pltpu.VMEM(blk_out, F32),
                        pltpu.VMEM((4,) + blk_in, BF16), pltpu.VMEM((4,) + blk_out, BF16),
                        pltpu.VMEM((4,) + blk_in, BF16), pltpu.VMEM((4,) + blk_out, BF16),
                        pltpu.VMEM((3,) + blk_in, BF16), pltpu.VMEM((3,) + blk_out, BF16),
                        pltpu.VMEM((N_DEV,) + small.shape, F32), pltpu.VMEM((N_DEV,) + gws.shape, F32),
                        pltpu.VMEM((N_DEV,) + dmod.shape, F32),
                        pltpu.SemaphoreType.DMA((8,)), pltpu.SemaphoreType.DMA((8,)),
                        pltpu.SemaphoreType.DMA((6,)), pltpu.SemaphoreType.DMA((6,)),
                        pltpu.SemaphoreType.DMA((21,)), pltpu.SemaphoreType.DMA((21,))])
    return pl.pallas_call(
        body, name="wgrad_reduce", grid_spec=grid_spec,
        out_shape=(jax.ShapeDtypeStruct(blk_in, F32), jax.ShapeDtypeStruct(blk_out, F32),
                   jax.ShapeDtypeStruct(small.shape, F32), jax.ShapeDtypeStruct(gws.shape, F32),
                   jax.ShapeDtypeStruct((N_DEV, 3 * D), F32), jax.ShapeDtypeStruct((8, 3 * D), F32)),
        compiler_params=_params(2),
    )(*_wgrad_schedule(), ht, dz, ycatt, dy, small, gws, dmod)


def _wgrad_ada(c_all, dmod_cols):
    def body(c_ref, dm_ref, g_ref, act):
        for b in range(N_DEV):
            cb = c_ref[b, 0:1, :]
            act[b:b + 1, :] = cb * _sigmoid(cb)
        g_ref[...] = lax.dot_general(act[...], dm_ref[...], (((0,), (0,)), ((), ())), preferred_element_type=F32,
                                     precision=lax.Precision.HIGHEST)

    return pl.pallas_call(
        body, name="wgrad_ada", out_shape=jax.ShapeDtypeStruct((D, W_ADA_BLK), F32),
        in_specs=[VM, VM], out_specs=VM, scratch_shapes=[pltpu.VMEM((N_DEV, D), F32)],
        compiler_params=_params(),
    )(c_all, dmod_cols)


def _adamw_math(w, g, m, v):
    m = ADAM_B1 * m + (1.0 - ADAM_B1) * g
    v = ADAM_B2 * v + (1.0 - ADAM_B2) * (g * g)
    m_hat = m / (1.0 - ADAM_B1 ** ADAM_STEP)
    v_hat = v / (1.0 - ADAM_B2 ** ADAM_STEP)
    delta = -ADAM_LR * (m_hat / (jnp.sqrt(v_hat) + ADAM_EPS) + ADAM_WD * w)
    return delta, m, v


def _adamw(name, w, g, m, v, row_block=None):
    shape = w.shape

    def body(w_ref, g_ref, m_ref, v_ref, d_ref, nm_ref, nv_ref):
        d_ref[...], nm_ref[...], nv_ref[...] = _adamw_math(w_ref[...], g_ref[...], m_ref[...], v_ref[...])

    out_shape = (jax.ShapeDtypeStruct(shape, F32),) * 3
    if row_block is None:
        return pl.pallas_call(body, name=name, out_shape=out_shape, in_specs=[VM] * 4, out_specs=(VM,) * 3,
                              compiler_params=_params())(w, g, m, v)
    spec = pl.BlockSpec((row_block, shape[1]), lambda i: (i, 0))
    return pl.pallas_call(body, name=name, grid=(shape[0] // row_block,), out_shape=out_shape,
                          in_specs=[spec] * 4, out_specs=(spec,) * 3, compiler_params=_params(1))(w, g, m, v)


def kernel(x, c, w_ada, b_ada, norm_g, w_in, conv_w, conv_b, conv_ln_g, conv_ln_b, sg_ln_g, sg_ln_b, w_s, b_s, w_out, final_g, loss_target, m_w_ada, m_b_ada, m_norm_g, m_w_in, m_conv_w, m_conv_b, m_conv_ln_g, m_conv_ln_b, m_sg_ln_g, m_sg_ln_b, m_w_s, m_b_s, m_w_out, m_final_g, v_w_ada, v_b_ada, v_norm_g, v_w_in, v_conv_w, v_conv_b, v_conv_ln_g, v_conv_ln_b, v_sg_ln_g, v_sg_ln_b, v_w_s, v_b_s, v_w_out, v_final_g):
    me = 4 * lax.axis_index("x") + 2 * lax.axis_index("y") + lax.axis_index("c")
    x2d, tgt2d = x[0], loss_target[0]
    row1 = lambda a: a.reshape(1, D)
    taps = lambda a: jnp.pad(a.reshape(CONV_W, CONV_BLK), ((0, CONV_ROWS - CONV_W), (0, 0)))

    win_g, wout_g, cw_g, c_all = _gather_params(w_in[0], w_out[0], taps(conv_w), jnp.broadcast_to(c, (8, D)))
    mod = _mod_exchange(c_all, w_ada[0], b_ada)
    ws_b = w_s[0].astype(BF16)
    wst_b = jnp.swapaxes(w_s[0], 1, 2).astype(BF16)
    bs_full = jnp.repeat(b_s[0].T, HEAD_DIM, axis=1)

    z, ht = _fwd_in_proj(x2d, mod, norm_g, win_g)
    dx2, ycatt, dy, dcv, dz_rest, acc_a, gws, gbs = _mix_and_head(
        z, x2d, tgt2d, mod, cw_g, conv_b, conv_ln_g, conv_ln_b, sg_ln_g, sg_ln_b, row1(final_g), ws_b, wst_b, bs_full,
        wout_g.reshape(D_MIX, D))
    grad_x, dz, acc_b, gcw = _bwd_in_proj(z, dcv, dz_rest, x2d, dx2, mod, norm_g, cw_g, win_g)

    gbs_row = gbs.T.reshape(1, D)
    small = jnp.concatenate(
        [acc_b[0:1], acc_a[2:7], acc_a[0:1], gbs_row, jnp.transpose(gcw, (1, 0, 2)).reshape(CONV_ROWS, D)], axis=0)
    dmod_row = jnp.concatenate([acc_b[1:2], acc_b[2:3], acc_a[1:2]], axis=1)
    g_w_in, g_w_out, small_sum, gws_sum, dmod_all, bada8 = _wgrad_reduce(
        ht, dz, ycatt, dy, small, gws.reshape(HEADS * CHUNK, CHUNK), jnp.broadcast_to(dmod_row, (8, 3 * D)))
    loss = lax.psum(0.5 / D * jnp.sum(acc_a[7]), ("x", "y", "c"))

    g_w_ada = _wgrad_ada(c_all, lax.dynamic_slice(dmod_all, (0, me * W_ADA_BLK), (N_DEV, W_ADA_BLK)))
    g_b_ada = bada8[0:1]
    g_vec = small_sum[0:8]
    g_conv_w = lax.dynamic_slice(small_sum, (8, me * CONV_BLK), (CONV_W, CONV_BLK))

    vec = lambda *a: jnp.concatenate([row1(t) for t in a], axis=0)
    d_vec, m_vec, v_vec = _adamw(
        "adamw_vectors",
        vec(norm_g, conv_b, conv_ln_g, conv_ln_b, sg_ln_g, sg_ln_b, final_g, b_s), g_vec,
        vec(m_norm_g, m_conv_b, m_conv_ln_g, m_conv_ln_b, m_sg_ln_g, m_sg_ln_b, m_final_g, m_b_s),
        vec(v_norm_g, v_conv_b, v_conv_ln_g, v_conv_ln_b, v_sg_ln_g, v_sg_ln_b, v_final_g, v_b_s))
    flat_ws = lambda a: a.reshape(HEADS * CHUNK, CHUNK)
    upd = {
        "w_ada": _adamw("adamw_w_ada", w_ada[0], g_w_ada, m_w_ada[0], v_w_ada[0], 256),
        "b_ada": _adamw("adamw_b_ada", b_ada, g_b_ada, m_b_ada, v_b_ada),
        "w_in": _adamw("adamw_w_in", w_in[0], g_w_in, m_w_in[0], v_w_in[0], 256),
        "conv_w": _adamw("adamw_conv_w", conv_w.reshape(CONV_W, CONV_BLK), g_conv_w, m_conv_w.reshape(CONV_W, CONV_BLK),
                         v_conv_w.reshape(CONV_W, CONV_BLK)),
        "w_s": _adamw("adamw_w_s", flat_ws(w_s), gws_sum, flat_ws(m_w_s), flat_ws(v_w_s)),
        "w_out": _adamw("adamw_w_out", w_out[0], g_w_out, m_w_out[0], v_w_out[0], 128),
    }
    grads = {"w_ada": g_w_ada, "b_ada": g_b_ada, "w_in": g_w_in, "conv_w": g_conv_w, "w_s": gws_sum, "w_out": g_w_out}
    vec_names = ["norm_g", "conv_b", "conv_ln_g", "conv_ln_b", "sg_ln_g", "sg_ln_b", "final_g", "b_s"]
    shapes = {"w_ada": w_ada.shape, "b_ada": b_ada.shape, "norm_g": norm_g.shape, "w_in": w_in.shape,
              "conv_w": conv_w.shape, "conv_b": conv_b.shape, "conv_ln_g": conv_ln_g.shape, "conv_ln_b": conv_ln_b.shape,
              "sg_ln_g": sg_ln_g.shape, "sg_ln_b": sg_ln_b.shape, "w_s": w_s.shape, "b_s": b_s.shape,
              "w_out": w_out.shape, "final_g": final_g.shape}
    order = ["w_ada", "b_ada", "norm_g", "w_in", "conv_w", "conv_b", "conv_ln_g", "conv_ln_b", "sg_ln_g", "sg_ln_b",
             "w_s", "b_s", "w_out", "final_g"]

    def leaf(kind, name):
        if name in vec_names:
            src = (g_vec, d_vec, m_vec, v_vec)[kind]
            val = src[vec_names.index(name)]
        elif kind == 0:
            val = grads[name]
        else:
            val = upd[name][kind - 1]
        return val.reshape(shapes[name])

    outs = [loss, grad_x.reshape(x.shape)]
    for kind in range(4):
        outs += [leaf(kind, n) for n in order]
    return tuple(outs)
```

```python
import functools

import jax
import jax.numpy as jnp
from jax import lax
from jax.experimental import pallas as pl
from jax.experimental.pallas import tpu as pltpu

F32 = jnp.float32
BF16 = jnp.bfloat16
MESH = pl.DeviceIdType.MESH

D = 1024
D_IN = 6 * D
D_MIX = 2 * D
N_DEV = 8
W_IN_BLK = D_IN // N_DEV
W_OUT_BLK = D_MIX // N_DEV
W_ADA_BLK = 3 * D // N_DEV
CONV_BLK = D // N_DEV
CONV_W = 31
CONV_HALF = CONV_W // 2
CONV_ROWS = 32
HALO = 16
CHUNK = 128
HEADS = 8
HEAD_DIM = 128
EPS = 1e-6
ROW_TILE = 256
LOSS_ROW = 40
SMALL_ROWS = 48
VMEM_LIMIT = 56 * 1024 * 1024

ADAM_LR = 0.001
ADAM_B1 = 0.9
ADAM_B2 = 0.999
ADAM_EPS = 1e-08
ADAM_WD = 0.01
ADAM_STEP = 10

VM = pl.BlockSpec(memory_space=pltpu.VMEM)


def _params(grid_rank=0, **kw):
    sem = ("arbitrary",) * grid_rank if grid_rank else None
    return pltpu.CompilerParams(dimension_semantics=sem, vmem_limit_bytes=VMEM_LIMIT, **kw)


def _sigmoid(t):
    return jax.nn.sigmoid(t)


def _dsilu(t, sig):
    return sig * (1.0 + t * (1.0 - sig))


def _mesh_pos():
    return lax.axis_index("x"), lax.axis_index("y"), lax.axis_index("c")


class _AllGather:
    def __init__(self, bufs, send_sems, recv_sems):
        x, y, c = _mesh_pos()
        self.bufs, self.send_sems, self.recv_sems = bufs, send_sems, recv_sems
        self.me, self.sib = (x, y, c), (x, y, 1 - c)
        self.chips = [(x, 1 - y), (1 - x, y), (1 - x, 1 - y)]
        self.c = c

    def _copy(self, a, k, block, to):
        px, py, pc = block
        ref = self.bufs[a].at[4 * px + 2 * py + pc]
        return pltpu.make_async_remote_copy(
            src_ref=ref, dst_ref=ref, send_sem=self.send_sems.at[7 * a + k], recv_sem=self.recv_sems.at[7 * a + k],
            device_id=to, device_id_type=MESH)

    def _outgoing(self, a, k):
        if k == 0:
            return self._copy(a, 0, self.me, self.sib)
        if k <= 3:
            return self._copy(a, k, self.me, (*self.chips[k - 1], self.c))
        return self._copy(a, k, (*self.chips[k - 4], self.c), self.sib)

    def source(self, k):
        if k == 0:
            return self.sib
        return (*self.chips[(k - 1) % 3], self.c if k <= 3 else 1 - self.c)

    def block_index(self, k):
        px, py, pc = self.source(k)
        return 4 * px + 2 * py + pc

    def send_own(self, a):
        for k in range(4):
            self._outgoing(a, k).start()

    def arrived(self, a, k):
        self._copy(a, k, self.source(k), self.me).wait_recv()

    def pass_on(self, a, j):
        self._outgoing(a, 4 + j).start()

    def sent(self, a):
        for k in range(7):
            self._outgoing(a, k).wait_send()

    def run(self):
        n = range(len(self.bufs))
        for a in n:
            self.send_own(a)
        for j in range(3):
            for a in n:
                self.arrived(a, 1 + j)
                self.pass_on(a, j)
        for k in (0, 4, 5, 6):
            for a in n:
                self.arrived(a, k)
        for a in n:
            self.sent(a)


def _my_block():
    x, y, c = _mesh_pos()
    return 4 * x + 2 * y + c


def _modulation(c_g, w_ref, b_ref, mod_ref, part, land, send_sems, recv_sems):
    x, y, c = _mesh_pos()
    me = 4 * x + 2 * y + c
    w = w_ref[...]
    for b in range(N_DEV):
        cb = c_g[b]
        part[b] = jnp.dot(cb * _sigmoid(cb), w, preferred_element_type=F32, precision=lax.Precision.HIGHEST)
    land[me] = part[me]

    def copy(b):
        return pltpu.make_async_remote_copy(
            src_ref=part.at[b], dst_ref=land.at[me], send_sem=send_sems.at[b], recv_sem=recv_sems.at[me],
            device_id=(b // 4, (b // 2) % 2, b % 2), device_id_type=MESH)

    def arrival(b):
        return pltpu.make_async_remote_copy(
            src_ref=part.at[b], dst_ref=land.at[b], send_sem=send_sems.at[b], recv_sem=recv_sems.at[b],
            device_id=(b // 4, (b // 2) % 2, b % 2), device_id_type=MESH)

    for b in range(N_DEV):
        @pl.when(b != me)
        def _():
            copy(b).start()
    for b in range(N_DEV):
        @pl.when(b != me)
        def _():
            arrival(b).wait_recv()
            copy(b).wait_send()
    for b in range(N_DEV):
        cols = slice(b * W_ADA_BLK, (b + 1) * W_ADA_BLK)
        mod_ref[:, cols] = land[b] + b_ref[:, cols]


def _rms_modulate(x, mod_ref):
    shift = mod_ref[0:1, 0:D]
    scale = mod_ref[0:1, D:2 * D]
    r = lax.rsqrt(jnp.mean(x * x, axis=-1, keepdims=True) + EPS)
    xn = x * r
    return xn, r, shift, scale


FWD_TILE = 512


def _fwd_in_proj(x, c_rep, w_ada, b_ada, norm_g, w_in, w_out, conv_w):
    s = x.shape[0]
    t = FWD_TILE
    n_tiles = s // t
    C_PAY, WIN_PAY, WOUT_PAY, CW_PAY = 0, 1, 2, 3

    def body(x_hbm, c_ref, wada_ref, bada_ref, ng_ref, win_ref, wout_ref, cw_ref,
             z_hbm, ht_ref, win_g, wout_g, cw_g, c_g, mod_ref,
             h, xbuf, zbuf, part, land, ag_send, ag_recv, mod_send, mod_recv, x_sem, z_sem):
        me = _my_block()
        c_g[me] = c_ref[...]
        win_g[me] = win_ref[...].astype(BF16)
        wout_g[me] = wout_ref[...].astype(BF16)
        cw_g[me] = cw_ref[...]
        ag = _AllGather([c_g, win_g, wout_g, cw_g], ag_send, ag_recv)
        for a in (C_PAY, WIN_PAY, WOUT_PAY, CW_PAY):
            ag.send_own(a)

        def x_copy(i):
            return pltpu.make_async_copy(x_hbm.at[pl.ds(i * t, t), :], xbuf.at[i % 2], x_sem.at[i % 2])

        x_copy(0).start()

        for j in range(3):
            ag.arrived(C_PAY, 1 + j)
            ag.pass_on(C_PAY, j)
        for k in (0, 4, 5, 6):
            ag.arrived(C_PAY, k)
        _modulation(c_g, wada_ref, bada_ref, mod_ref, part, land, mod_send, mod_recv)

        for i in range(n_tiles):
            x_copy(i).wait()
            if i + 1 < n_tiles:
                x_copy(i + 1).start()
            xn, _, shift, scale = _rms_modulate(xbuf[i % 2], mod_ref)
            hh = xn * ng_ref[...] * (1.0 + scale) + shift
            h[i * t:(i + 1) * t, :] = hh.astype(BF16)
            ht_ref[:, i * t:(i + 1) * t] = hh.T.astype(BF16)

        def z_copy(slot, row0, col0):
            return pltpu.make_async_copy(zbuf.at[slot], z_hbm.at[pl.ds(row0, t), pl.ds(col0, W_IN_BLK)], z_sem.at[slot])

        done = [0]

        def z_block(blk):
            col0 = pl.multiple_of(blk * W_IN_BLK, 128)
            for i in range(n_tiles):
                slot = done[0] % 2
                if done[0] >= 2:
                    z_copy(slot, 0, 0).wait()
                zbuf[slot] = jnp.dot(h[i * t:(i + 1) * t, :], win_g[blk], preferred_element_type=F32)
                z_copy(slot, i * t, col0).start()
                done[0] += 1

        z_block(me)
        ag.arrived(WIN_PAY, 0)
        z_block(ag.block_index(0))
        for j in (0, 1):
            ag.arrived(WIN_PAY, 1 + j)
            ag.pass_on(WIN_PAY, j)
        for j in (0, 1):
            z_block(ag.block_index(1 + j))
        ag.arrived(WIN_PAY, 4)
        z_block(ag.block_index(4))
        ag.arrived(WIN_PAY, 3)
        ag.pass_on(WIN_PAY, 2)
        ag.arrived(WIN_PAY, 5)
        z_block(ag.block_index(5))
        z_block(ag.block_index(3))
        ag.arrived(WIN_PAY, 6)
        z_block(ag.block_index(6))

        for a in (WOUT_PAY, CW_PAY):
            for j in range(3):
                ag.arrived(a, 1 + j)
                ag.pass_on(a, j)
        for a in (WOUT_PAY, CW_PAY):
            for k in (0, 4, 5, 6):
                ag.arrived(a, k)
        for a in (C_PAY, WIN_PAY, WOUT_PAY, CW_PAY):
            ag.sent(a)
        z_copy(0, 0, 0).wait()
        z_copy(1, 0, 0).wait()

    any_spec = pl.BlockSpec(memory_space=pl.ANY)
    return pl.pallas_call(
        body, name="fwd_in_proj",
        out_shape=(jax.ShapeDtypeStruct((s, D_IN), F32), jax.ShapeDtypeStruct((D, s), BF16),
                   jax.ShapeDtypeStruct((N_DEV,) + w_in.shape, BF16), jax.ShapeDtypeStruct((N_DEV,) + w_out.shape, BF16),
                   jax.ShapeDtypeStruct((N_DEV,) + conv_w.shape, F32), jax.ShapeDtypeStruct((N_DEV,) + c_rep.shape, F32),
                   jax.ShapeDtypeStruct((8, 3 * D), F32)),
        in_specs=[any_spec] + [VM] * 7, out_specs=(any_spec,) + (VM,) * 6,
        scratch_shapes=[pltpu.VMEM((s, D), BF16), pltpu.VMEM((2, t, D), F32), pltpu.VMEM((2, t, W_IN_BLK), F32),
                        pltpu.VMEM((N_DEV, 8, W_ADA_BLK), F32), pltpu.VMEM((N_DEV, 8, W_ADA_BLK), F32),
                        pltpu.SemaphoreType.DMA((28,)), pltpu.SemaphoreType.DMA((28,)),
                        pltpu.SemaphoreType.DMA((N_DEV,)), pltpu.SemaphoreType.DMA((N_DEV,)),
                        pltpu.SemaphoreType.DMA((2,)), pltpu.SemaphoreType.DMA((2,))],
        compiler_params=_params(),
    )(x, c_rep, w_ada, b_ada, norm_g, w_in, w_out, conv_w)


def _halo_specs(t, s, width):
    per = t // HALO
    last = s // HALO - 1
    prev = pl.BlockSpec((HALO, width), lambda i: (jnp.maximum(i * per - 1, 0), 0))
    nxt = pl.BlockSpec((HALO, width), lambda i: (jnp.minimum((i + 1) * per, last), 0))
    return prev, nxt


def _glu(ref):
    return ref[:, 0:D] * _sigmoid(ref[:, D:2 * D])


SUB = 8
CONV_PHASE_ROWS = ROW_TILE + SUB


def _conv_taps(ext, qbuf, cols, tap, t):
    out = None
    for b in range(SUB):
        q = None
        for a in range((CONV_W - b + SUB - 1) // SUB):
            term = ext[SUB * a:SUB * a + t + SUB, cols] * tap(SUB * a + b)
            q = term if q is None else q + term
        qbuf[b] = q
        shifted = qbuf[b, pl.ds(b + 1, t), :]
        out = shifted if out is None else out + shifted
    return out


def _layer_norm_stats(v):
    mu = jnp.mean(v, axis=-1, keepdims=True)
    cen = v - mu
    rstd = lax.rsqrt(jnp.mean(cen * cen, axis=-1, keepdims=True) + EPS)
    return cen * rstd, rstd


def _layer_norm_bwd(dy_hat, hat, rstd):
    m1 = jnp.mean(dy_hat, axis=-1, keepdims=True)
    m2 = jnp.mean(dy_hat * hat, axis=-1, keepdims=True)
    return rstd * (dy_hat - m1 - hat * m2)


def _colsum(v):
    return jnp.sum(v, axis=0, keepdims=True)


def _mix_and_head(z, x, tgt, mod, conv_w_g, conv_b, cln_g, cln_b, sln_g, sln_b, final_g, ws_b, wst_b, bs_full, wout):
    s = x.shape[0]
    t = ROW_TILE
    n_chunks = t // CHUNK
    n_steps = s // t
    prev_spec, next_spec = _halo_specs(t, s, 2 * D)

    def body(z_ref, zp_ref, zn_ref, x_ref, tgt_ref, mod_ref, cw_ref, cb_ref, clg_ref, clb_ref, slg_ref, slb_ref, fg_ref,
             ws_ref, wst_ref, bs_ref, wout_ref,
             dx2_ref, ycatt_ref, dy_ref, dcv_ref, dzr_ref, acc_ref, gws_ref, gbs_ref,
             gext, cv, vs, dvn, ycat, gbs_acc, qbuf):
        i = pl.program_id(0)

        @pl.when(i == 0)
        def _():
            acc_ref[...] = jnp.zeros_like(acc_ref)
            gws_ref[...] = jnp.zeros_like(gws_ref)
            gbs_acc[...] = jnp.zeros_like(gbs_acc)

        gext[0:HALO, :] = jnp.where(i > 0, _glu(zp_ref), 0.0)
        gext[HALO:HALO + t, :] = _glu(z_ref)
        gext[HALO + t:2 * HALO + t, :] = jnp.where(i < n_steps - 1, _glu(zn_ref), 0.0)
        for blk in range(N_DEV):
            cols = slice(blk * CONV_BLK, (blk + 1) * CONV_BLK)
            cv[:, cols] = _conv_taps(gext, qbuf, cols, lambda k, blk=blk: cw_ref[blk, k:k + 1, :], t) + cb_ref[:, cols]
        ln_hat, ln_rstd = _layer_norm_stats(cv[...])
        ln_a = ln_hat * clg_ref[...] + clb_ref[...]
        sig_ln = _sigmoid(ln_a)
        sa = ln_a * sig_ln
        a_gate = z_ref[:, 2 * D:3 * D]
        sig_ag = _sigmoid(a_gate)
        s_gate = a_gate * sig_ag
        ya = sa * s_gate

        v_hat, v_rstd = _layer_norm_stats(z_ref[:, 4 * D:5 * D])
        vn = v_hat * slg_ref[...] + slb_ref[...]
        vnb = vn.astype(BF16)
        for n in range(n_chunks):
            rows = slice(n * CHUNK, (n + 1) * CHUNK)
            for h in range(HEADS):
                cols = slice(h * HEAD_DIM, (h + 1) * HEAD_DIM)
                vs[rows, cols] = jnp.dot(ws_ref[h], vnb[rows, cols], preferred_element_type=F32) + bs_ref[:, cols]
        u = z_ref[:, 3 * D:4 * D]
        b_gate = z_ref[:, 5 * D:6 * D]
        sig_bg = _sigmoid(b_gate)
        s_bg = b_gate * sig_bg
        vsv = vs[...]
        yb = u * vsv * s_bg

        ycat[:, 0:D] = ya.astype(BF16)
        ycat[:, D:2 * D] = yb.astype(BF16)
        ycatt_ref[0:D, :] = ya.T.astype(BF16)
        ycatt_ref[D:2 * D, :] = yb.T.astype(BF16)
        y = jnp.dot(ycat[...], wout_ref[...], preferred_element_type=F32)
        gate = mod_ref[0:1, 2 * D:3 * D]
        x2 = x_ref[...] + gate * y
        r2 = lax.rsqrt(jnp.mean(x2 * x2, axis=-1, keepdims=True) + EPS)
        x2n = x2 * r2
        fg = fg_ref[...]
        diff = x2n * fg - tgt_ref[...]
        acc_ref[7:8, :] += _colsum(diff * diff)
        dout = diff * (1.0 / D)
        acc_ref[0:1, :] += _colsum(dout * x2n)
        dx2n = dout * fg
        dx2 = r2 * (dx2n - x2n * jnp.mean(dx2n * x2n, axis=-1, keepdims=True))
        dx2_ref[...] = dx2
        acc_ref[1:2, :] += _colsum(dx2 * y)
        dyb16 = (dx2 * gate).astype(BF16)
        dy_ref[...] = dyb16
        dycat = lax.dot_general(dyb16, wout_ref[...], (((1,), (1,)), ((), ())), preferred_element_type=F32)
        dya = dycat[:, 0:D]
        dyb = dycat[:, D:2 * D]

        du = dyb * vsv * s_bg
        dvs = dyb * u * s_bg
        dbg = dyb * u * vsv * _dsilu(b_gate, sig_bg)
        dvsb = dvs.astype(BF16)
        gbs = gbs_acc[...]
        for n in range(n_chunks):
            rows = slice(n * CHUNK, (n + 1) * CHUNK)
            gbs = gbs + dvs[rows, :]
            for h in range(HEADS):
                cols = slice(h * HEAD_DIM, (h + 1) * HEAD_DIM)
                gws_ref[h] += lax.dot_general(dvsb[rows, cols], vnb[rows, cols], (((1,), (1,)), ((), ())),
                                              preferred_element_type=F32)
                dvn[rows, cols] = jnp.dot(wst_ref[h], dvsb[rows, cols], preferred_element_type=F32)
        gbs_acc[...] = gbs

        @pl.when(i == n_steps - 1)
        def _():
            for h in range(HEADS):
                gbs_ref[:, h:h + 1] = jnp.sum(gbs_acc[:, h * HEAD_DIM:(h + 1) * HEAD_DIM], axis=1, keepdims=True)

        dvnv = dvn[...]
        acc_ref[5:6, :] += _colsum(dvnv * v_hat)
        acc_ref[6:7, :] += _colsum(dvnv)
        dv = _layer_norm_bwd(dvnv * slg_ref[...], v_hat, v_rstd)

        dsa = dya * s_gate
        dagate = dya * sa * _dsilu(a_gate, sig_ag)
        dln = dsa * _dsilu(ln_a, sig_ln)
        acc_ref[3:4, :] += _colsum(dln * ln_hat)
        acc_ref[4:5, :] += _colsum(dln)
        dcv = _layer_norm_bwd(dln * clg_ref[...], ln_hat, ln_rstd)
        acc_ref[2:3, :] += _colsum(dcv)
        dcv_ref[...] = dcv

        dzr_ref[:, 0:D] = dagate.astype(BF16)
        dzr_ref[:, D:2 * D] = du.astype(BF16)
        dzr_ref[:, 2 * D:3 * D] = dv.astype(BF16)
        dzr_ref[:, 3 * D:4 * D] = dbg.astype(BF16)

    row = lambda w: pl.BlockSpec((t, w), lambda i: (i, 0))
    const = lambda shape: pl.BlockSpec(shape, lambda i: (0,) * len(shape))
    return pl.pallas_call(
        body, name="mix_and_head", grid=(n_steps,),
        out_shape=(jax.ShapeDtypeStruct((s, D), F32),
                   jax.ShapeDtypeStruct((D_MIX, s), BF16),
                   jax.ShapeDtypeStruct((s, D), BF16),
                   jax.ShapeDtypeStruct((s, D), F32),
                   jax.ShapeDtypeStruct((s, 4 * D), BF16),
                   jax.ShapeDtypeStruct((8, D), F32),
                   jax.ShapeDtypeStruct((HEADS, CHUNK, CHUNK), F32),
                   jax.ShapeDtypeStruct((CHUNK, HEADS), F32)),
        in_specs=[row(D_IN), prev_spec, next_spec, row(D), row(D)] + [VM] * 12,
        out_specs=(row(D), pl.BlockSpec((D_MIX, t), lambda i: (0, i)), row(D), row(D), row(4 * D),
                   const((8, D)), const((HEADS, CHUNK, CHUNK)), const((CHUNK, HEADS))),
        scratch_shapes=[pltpu.VMEM((t + 2 * HALO, D), F32), pltpu.VMEM((t, D), F32), pltpu.VMEM((t, D), F32),
                        pltpu.VMEM((t, D), F32), pltpu.VMEM((t, D_MIX), BF16), pltpu.VMEM((CHUNK, D), F32),
                        pltpu.VMEM((SUB, CONV_PHASE_ROWS, CONV_BLK), F32)],
        compiler_params=_params(1),
    )(z, z, z, x, tgt, mod, conv_w_g, conv_b, cln_g, cln_b, sln_g, sln_b, final_g, ws_b, wst_b, bs_full, wout)


def _bwd_in_proj(z, dcv, dz_rest, x, dx2, mod, norm_g, conv_w_g, win_g):
    s = x.shape[0]
    t = ROW_TILE
    n_steps = s // t
    zp_spec, zn_spec = _halo_specs(t, s, 2 * D)
    dp_spec, dn_spec = _halo_specs(t, s, D)

    def body(z_ref, zp_ref, zn_ref, dcv_ref, dcvp_ref, dcvn_ref, dzr_ref, x_ref, dx2_ref, mod_ref, ng_ref, cw_ref, w_ref,
             gx_ref, dz_ref, acc_ref, gcw_ref,
             gext, dext, dg, taps, dpad, qbuf):
        i = pl.program_id(0)

        @pl.when(i == 0)
        def _():
            acc_ref[...] = jnp.zeros_like(acc_ref)
            gcw_ref[...] = jnp.zeros_like(gcw_ref)

        not_first = i > 0
        not_last = i < n_steps - 1
        gext[0:HALO, :] = jnp.where(not_first, _glu(zp_ref), 0.0)
        gext[HALO:HALO + t, :] = _glu(z_ref)
        gext[HALO + t:2 * HALO + t, :] = jnp.where(not_last, _glu(zn_ref), 0.0)
        dext[0:HALO, :] = jnp.where(not_first, dcvp_ref[...], 0.0)
        dext[HALO:HALO + t, :] = dcv_ref[...]
        dext[HALO + t:2 * HALO + t, :] = jnp.where(not_last, dcvn_ref[...], 0.0)

        taps[...] = jnp.zeros_like(taps)
        dpad[0:SUB, :] = jnp.zeros((SUB, D), F32)
        dpad[SUB:SUB + t, :] = dcv_ref[...]
        dpad[SUB + t:2 * SUB + t, :] = jnp.zeros((SUB, D), F32)
        for blk in range(N_DEV):
            cols = slice(blk * CONV_BLK, (blk + 1) * CONV_BLK)
            dg[:, cols] = _conv_taps(dext, qbuf, cols, lambda k, blk=blk: cw_ref[blk, CONV_W - 1 - k:CONV_W - k, :], t)
            for b in range(SUB):
                dshift = dpad[pl.ds(SUB - 1 - b, t + SUB), cols]
                for a in range((CONV_W - b + SUB - 1) // SUB):
                    k = SUB * a + b
                    taps[k:k + 1, :] = _colsum(gext[SUB * a:SUB * a + t + SUB, cols] * dshift)
            gcw_ref[blk] += taps[...]

        a = z_ref[:, 0:D]
        sig = _sigmoid(z_ref[:, D:2 * D])
        dgv = dg[...]
        dz_ref[:, 0:D] = (dgv * sig).astype(BF16)
        dz_ref[:, D:2 * D] = (dgv * a * sig * (1.0 - sig)).astype(BF16)
        dz_ref[:, 2 * D:6 * D] = dzr_ref[...]

        dh = jnp.zeros((t, D), F32)
        for j in range(N_DEV):
            dh = dh + lax.dot_general(dz_ref[:, j * W_IN_BLK:(j + 1) * W_IN_BLK], w_ref[j], (((1,), (1,)), ((), ())),
                                      preferred_element_type=F32)

        xn, r, _, scale = _rms_modulate(x_ref[...], mod_ref)
        ng = ng_ref[...]
        one_scale = 1.0 + scale
        dh_xn = dh * xn
        acc_ref[0:1, :] += _colsum(dh_xn * one_scale)
        acc_ref[1:2, :] += _colsum(dh)
        acc_ref[2:3, :] += _colsum(dh_xn * ng)
        dxn = dh * (ng * one_scale)
        gx_ref[...] = dx2_ref[...] + r * (dxn - xn * jnp.mean(dxn * xn, axis=-1, keepdims=True))

    row = lambda w: pl.BlockSpec((t, w), lambda i: (i, 0))
    const = lambda shape: pl.BlockSpec(shape, lambda i: (0,) * len(shape))
    return pl.pallas_call(
        body, name="bwd_in_proj", grid=(n_steps,),
        out_shape=(jax.ShapeDtypeStruct((s, D), F32), jax.ShapeDtypeStruct((s, D_IN), BF16),
                   jax.ShapeDtypeStruct((8, D), F32), jax.ShapeDtypeStruct((N_DEV, CONV_ROWS, CONV_BLK), F32)),
        in_specs=[pl.BlockSpec((t, 2 * D), lambda i: (i, 0)), zp_spec, zn_spec, row(D), dp_spec, dn_spec, row(4 * D),
                  row(D), row(D), VM, VM, VM, VM],
        out_specs=(row(D), row(D_IN), const((8, D)), const((N_DEV, CONV_ROWS, CONV_BLK))),
        scratch_shapes=[pltpu.VMEM((t + 2 * HALO, D), F32), pltpu.VMEM((t + 2 * HALO, D), F32), pltpu.VMEM((t, D), F32),
                        pltpu.VMEM((CONV_ROWS, CONV_BLK), F32), pltpu.VMEM((t + 2 * SUB, D), F32),
                        pltpu.VMEM((SUB, CONV_PHASE_ROWS, CONV_BLK), F32)],
        compiler_params=_params(1),
    )(z, z, z, dcv, dcv, dcv, dz_rest, x, dx2, mod, norm_g, conv_w_g, win_g)


WGRAD_K = 1024
_OTHER_CHIPS = [(3, True), (3, False), (1, True), (1, False), (2, True), (2, False)]
_OWN_CHIP = [(0, True), (0, False)]
WGRAD_ROLES = ([("out",) + e for e in _OTHER_CHIPS] + [("in",) + e for e in _OTHER_CHIPS]
               + [("out",) + e for e in _OWN_CHIP] + [("in",) + e for e in _OWN_CHIP])


def _wgrad_schedule():
    x, y, c = _mesh_pos()
    chip = [2 * x + y, 2 * x + (1 - y), 2 * (1 - x) + y, 2 * (1 - x) + (1 - y)]
    blk = lambda r, sibling: 2 * chip[r] + ((1 - c) if sibling else c)
    out_blk, in_blk, is_out = [], [], []
    last = {"out": blk(*_OTHER_CHIPS[0]), "in": blk(*_OTHER_CHIPS[0])}
    for kind, r, sibling in WGRAD_ROLES:
        last[kind] = blk(r, sibling)
        out_blk.append(last["out"])
        in_blk.append(last["in"])
        is_out.append(1 if kind == "out" else 0)
    as_vec = lambda v: jnp.stack([jnp.asarray(e, jnp.int32) for e in v])
    return as_vec(out_blk), as_vec(in_blk), as_vec(is_out)


def _wgrad_reduce(ht, dz, ycatt, dy, small, gws, dmod):
    s = dz.shape[0]
    n_kc = s // WGRAD_K
    n_steps = len(WGRAD_ROLES)
    first_in = [k for k, role in enumerate(WGRAD_ROLES) if role[0] == "in"][0]
    blk_in, blk_out = (D, W_IN_BLK), (W_OUT_BLK, D)

    def body(out_blk, in_blk, is_out, ht_ref, dz_ref, yt_ref, dy_ref, small_ref, gws_ref, dmod_ref,
             oin_ref, oout_ref, osmall_ref, ogws_ref, odmod_ref, obada_ref, oloss_ref,
             acc_in, acc_out, p1_in, p1_out, l1_in, l1_out, l2_in, l2_out, small_g, gws_g, dmod_g,
             s1_send, s1_recv, s2_send, s2_recv, ag_send, ag_recv):
        step, kc = pl.program_id(0), pl.program_id(1)
        x, y, c = _mesh_pos()
        sib = (x, y, 1 - c)
        chip_of = [(x, y), (x, 1 - y), (1 - x, y), (1 - x, 1 - y)]
        pay = {"in": (0, acc_in, p1_in, l1_in, l2_in, oin_ref), "out": (1, acc_out, p1_out, l1_out, l2_out, oout_ref)}
        ag = _AllGather([dmod_g, small_g, gws_g], ag_send, ag_recv)
        last_kc = kc == n_kc - 1

        @pl.when((step == 0) & (kc == 0))
        def _():
            me = 4 * x + 2 * y + c
            small_g[me] = small_ref[...]
            gws_g[me] = gws_ref[...]
            dmod_g[me] = dmod_ref[...]
            for a in range(3):
                ag.send_own(a)

        @pl.when((step == 4) & (kc == 0))
        def _():
            for j in range(3):
                for a in range(3):
                    ag.arrived(a, 1 + j)
                    ag.pass_on(a, j)

        def accumulate(acc, prod):
            @pl.when(kc == 0)
            def _():
                acc[...] = prod

            @pl.when(kc != 0)
            def _():
                acc[...] += prod

        @pl.when(is_out[step] == 1)
        def _():
            accumulate(acc_out, jnp.dot(yt_ref[...], dy_ref[...], preferred_element_type=F32))

        @pl.when(is_out[step] == 0)
        def _():
            accumulate(acc_in, jnp.dot(ht_ref[...], dz_ref[...], preferred_element_type=F32))

        def step1(kind, r):
            a, _, p1, l1, _, _ = pay[kind]
            return pltpu.make_async_remote_copy(
                src_ref=p1.at[r], dst_ref=l1.at[r], send_sem=s1_send.at[4 * a + r], recv_sem=s1_recv.at[4 * a + r],
                device_id=sib, device_id_type=MESH)

        def step2(kind, r):
            a, _, _, l1, l2, _ = pay[kind]
            return pltpu.make_async_remote_copy(
                src_ref=l1.at[r], dst_ref=l2.at[r - 1], send_sem=s2_send.at[3 * a + r - 1],
                recv_sem=s2_recv.at[3 * a + r - 1], device_id=(*chip_of[r], c), device_id_type=MESH)

        for k, (kind, r, sibling) in enumerate(WGRAD_ROLES):
            @pl.when((step == k) & last_kc)
            def _(kind=kind, r=r, sibling=sibling):
                _, acc, p1, l1, _, out = pay[kind]
                if sibling:
                    p1[r] = acc[...].astype(BF16)
                    step1(kind, r).start()
                else:
                    step1(kind, r).wait_recv()
                    both = acc[...] + l1[r].astype(F32)
                    if r:
                        l1[r] = both.astype(BF16)
                        step2(kind, r).start()
                    else:
                        out[...] = both

        @pl.when((step == n_steps - 1) & last_kc)
        def _():
            for kind in ("out", "in"):
                _, _, _, _, l2, out = pay[kind]
                total = out[...]
                for r in (1, 2, 3):
                    step2(kind, r).wait_recv()
                    total = total + l2[r - 1].astype(F32)
                out[...] = total
            for kind in ("out", "in"):
                for r in range(4):
                    step1(kind, r).wait_send()
                for r in (1, 2, 3):
                    step2(kind, r).wait_send()
            for k in (0, 4, 5, 6):
                for a in range(3):
                    ag.arrived(a, k)
            for a in range(3):
                ag.sent(a)
            tot_small, tot_gws, tot_dmod = small_g[0], gws_g[0], dmod_g[0]
            for b in range(1, N_DEV):
                tot_small = tot_small + small_g[b]
                tot_gws = tot_gws + gws_g[b]
                tot_dmod = tot_dmod + dmod_g[b]
            osmall_ref[...] = tot_small
            oloss_ref[...] = jnp.full(oloss_ref.shape, (0.5 / D) * jnp.sum(tot_small[LOSS_ROW:LOSS_ROW + 1, :]), F32)
            ogws_ref[...] = tot_gws
            obada_ref[...] = tot_dmod
            for b in range(N_DEV):
                odmod_ref[b:b + 1, :] = dmod_g[b, 0:1, :]

    def kc_of(working, step, kc, hold_first):
        held = jnp.where(step < hold_first, 0, n_kc - 1)
        return jnp.where(working, kc, held)

    out_kc = lambda i, kc, ob, ib, io: kc_of(io[i] == 1, i, kc, 0)
    in_kc = lambda i, kc, ob, ib, io: kc_of(io[i] == 0, i, kc, first_in)
    grid_spec = pltpu.PrefetchScalarGridSpec(
        num_scalar_prefetch=3, grid=(n_steps, n_kc),
        in_specs=[pl.BlockSpec((D, WGRAD_K), lambda i, kc, ob, ib, io: (0, in_kc(i, kc, ob, ib, io))),
                  pl.BlockSpec((WGRAD_K, W_IN_BLK), lambda i, kc, ob, ib, io: (in_kc(i, kc, ob, ib, io), ib[i])),
                  pl.BlockSpec((W_OUT_BLK, WGRAD_K), lambda i, kc, ob, ib, io: (ob[i], out_kc(i, kc, ob, ib, io))),
                  pl.BlockSpec((WGRAD_K, D), lambda i, kc, ob, ib, io: (out_kc(i, kc, ob, ib, io), 0)),
                  VM, VM, VM],
        out_specs=(VM,) * 7,
        scratch_shapes=[pltpu.VMEM(blk_in, F32), pltpu.VMEM(blk_out, F32),
                        pltpu.VMEM((4,) + blk_in, BF16), pltpu.VMEM((4,) + blk_out, BF16),
                        pltpu.VMEM((4,) + blk_in, BF16), pltpu.VMEM((4,) + blk_out, BF16),
                        pltpu.VMEM((3,) + blk_in, BF16), pltpu.VMEM((3,) + blk_out, BF16),
                        pltpu.VMEM((N_DEV,) + small.shape, F32), pltpu.VMEM((N_DEV,) + gws.shape, F32),
                        pltpu.VMEM((N_DEV,) + dmod.shape, F32),
                        pltpu.SemaphoreType.DMA((8,)), pltpu.SemaphoreType.DMA((8,)),
                        pltpu.SemaphoreType.DMA((6,)), pltpu.SemaphoreType.DMA((6,)),
                        pltpu.SemaphoreType.DMA((21,)), pltpu.SemaphoreType.DMA((21,))])
    return pl.pallas_call(
        body, name="wgrad_reduce", grid_spec=grid_spec,
        out_shape=(jax.ShapeDtypeStruct(blk_in, F32), jax.ShapeDtypeStruct(blk_out, F32),
                   jax.ShapeDtypeStruct(small.shape, F32), jax.ShapeDtypeStruct(gws.shape, F32),
                   jax.ShapeDtypeStruct((N_DEV, 3 * D), F32), jax.ShapeDtypeStruct((8, 3 * D), F32),
                   jax.ShapeDtypeStruct((8, 128), F32)),
        compiler_params=_params(2),
    )(*_wgrad_schedule(), ht, dz, ycatt, dy, small, gws, dmod)


def _wgrad_ada(c_all, dmod_cols):
    def body(c_ref, dm_ref, g_ref, act):
        for b in range(N_DEV):
            cb = c_ref[b, 0:1, :]
            act[b:b + 1, :] = cb * _sigmoid(cb)
        g_ref[...] = lax.dot_general(act[...], dm_ref[...], (((0,), (0,)), ((), ())), preferred_element_type=F32,
                                     precision=lax.Precision.HIGHEST)

    return pl.pallas_call(
        body, name="wgrad_ada", out_shape=jax.ShapeDtypeStruct((D, W_ADA_BLK), F32),
        in_specs=[VM, VM], out_specs=VM, scratch_shapes=[pltpu.VMEM((N_DEV, D), F32)],
        compiler_params=_params(),
    )(c_all, dmod_cols)


def _adamw_math(w, g, m, v):
    m = ADAM_B1 * m + (1.0 - ADAM_B1) * g
    v = ADAM_B2 * v + (1.0 - ADAM_B2) * (g * g)
    m_hat = m / (1.0 - ADAM_B1 ** ADAM_STEP)
    v_hat = v / (1.0 - ADAM_B2 ** ADAM_STEP)
    delta = -ADAM_LR * (m_hat / (jnp.sqrt(v_hat) + ADAM_EPS) + ADAM_WD * w)
    return delta, m, v


def _adamw(name, w, g, m, v, row_block=None):
    shape = w.shape

    def body(w_ref, g_ref, m_ref, v_ref, d_ref, nm_ref, nv_ref):
        d_ref[...], nm_ref[...], nv_ref[...] = _adamw_math(w_ref[...], g_ref[...], m_ref[...], v_ref[...])

    out_shape = (jax.ShapeDtypeStruct(shape, F32),) * 3
    if row_block is None:
        return pl.pallas_call(body, name=name, out_shape=out_shape, in_specs=[VM] * 4, out_specs=(VM,) * 3,
                              compiler_params=_params())(w, g, m, v)
    spec = pl.BlockSpec((row_block, shape[1]), lambda i: (i, 0))
    return pl.pallas_call(body, name=name, grid=(shape[0] // row_block,), out_shape=out_shape,
                          in_specs=[spec] * 4, out_specs=(spec,) * 3, compiler_params=_params(1))(w, g, m, v)


def kernel(x, c, w_ada, b_ada, norm_g, w_in, conv_w, conv_b, conv_ln_g, conv_ln_b, sg_ln_g, sg_ln_b, w_s, b_s, w_out, final_g, loss_target, m_w_ada, m_b_ada, m_norm_g, m_w_in, m_conv_w, m_conv_b, m_conv_ln_g, m_conv_ln_b, m_sg_ln_g, m_sg_ln_b, m_w_s, m_b_s, m_w_out, m_final_g, v_w_ada, v_b_ada, v_norm_g, v_w_in, v_conv_w, v_conv_b, v_conv_ln_g, v_conv_ln_b, v_sg_ln_g, v_sg_ln_b, v_w_s, v_b_s, v_w_out, v_final_g):
    me = 4 * lax.axis_index("x") + 2 * lax.axis_index("y") + lax.axis_index("c")
    x2d, tgt2d = x[0], loss_target[0]
    row1 = lambda a: a.reshape(1, D)
    taps = lambda a: jnp.pad(a.reshape(CONV_W, CONV_BLK), ((0, CONV_ROWS - CONV_W), (0, 0)))

    z, ht, win_g, wout_g, cw_g, c_all, mod = _fwd_in_proj(
        x2d, jnp.broadcast_to(c, (8, D)), w_ada[0], b_ada, norm_g, w_in[0], w_out[0], taps(conv_w))
    ws_b = w_s[0].astype(BF16)
    wst_b = jnp.swapaxes(w_s[0], 1, 2).astype(BF16)
    bs_full = jnp.repeat(b_s[0].T, HEAD_DIM, axis=1)

    dx2, ycatt, dy, dcv, dz_rest, acc_a, gws, gbs = _mix_and_head(
        z, x2d, tgt2d, mod, cw_g, conv_b, conv_ln_g, conv_ln_b, sg_ln_g, sg_ln_b, row1(final_g), ws_b, wst_b, bs_full,
        wout_g.reshape(D_MIX, D))
    grad_x, dz, acc_b, gcw = _bwd_in_proj(z, dcv, dz_rest, x2d, dx2, mod, norm_g, cw_g, win_g)

    gbs_row = gbs.T.reshape(1, D)
    small = jnp.concatenate(
        [acc_b[0:1], acc_a[2:7], acc_a[0:1], gbs_row, jnp.transpose(gcw, (1, 0, 2)).reshape(CONV_ROWS, D),
         acc_a[7:8], jnp.zeros((SMALL_ROWS - LOSS_ROW - 1, D), F32)], axis=0)
    dmod_row = jnp.concatenate([acc_b[1:2], acc_b[2:3], acc_a[1:2]], axis=1)
    g_w_in, g_w_out, small_sum, gws_sum, dmod_all, bada8, loss_tile = _wgrad_reduce(
        ht, dz, ycatt, dy, small, gws.reshape(HEADS * CHUNK, CHUNK), jnp.broadcast_to(dmod_row, (8, 3 * D)))
    loss = loss_tile[0, 0]

    g_w_ada = _wgrad_ada(c_all, lax.dynamic_slice(dmod_all, (0, me * W_ADA_BLK), (N_DEV, W_ADA_BLK)))
    g_b_ada = bada8[0:1]
    g_vec = small_sum[0:8]
    g_conv_w = lax.dynamic_slice(small_sum, (8, me * CONV_BLK), (CONV_W, CONV_BLK))

    vec = lambda *a: jnp.concatenate([row1(t) for t in a], axis=0)
    d_vec, m_vec, v_vec = _adamw(
        "adamw_vectors",
        vec(norm_g, conv_b, conv_ln_g, conv_ln_b, sg_ln_g, sg_ln_b, final_g, b_s), g_vec,
        vec(m_norm_g, m_conv_b, m_conv_ln_g, m_conv_ln_b, m_sg_ln_g, m_sg_ln_b, m_final_g, m_b_s),
        vec(v_norm_g, v_conv_b, v_conv_ln_g, v_conv_ln_b, v_sg_ln_g, v_sg_ln_b, v_final_g, v_b_s))
    flat_ws = lambda a: a.reshape(HEADS * CHUNK, CHUNK)
    upd = {
        "w_ada": _adamw("adamw_w_ada", w_ada[0], g_w_ada, m_w_ada[0], v_w_ada[0], 256),
        "b_ada": _adamw("adamw_b_ada", b_ada, g_b_ada, m_b_ada, v_b_ada),
        "w_in": _adamw("adamw_w_in", w_in[0], g_w_in, m_w_in[0], v_w_in[0], 256),
        "conv_w": _adamw("adamw_conv_w", conv_w.reshape(CONV_W, CONV_BLK), g_conv_w, m_conv_w.reshape(CONV_W, CONV_BLK),
                         v_conv_w.reshape(CONV_W, CONV_BLK)),
        "w_s": _adamw("adamw_w_s", flat_ws(w_s), gws_sum, flat_ws(m_w_s), flat_ws(v_w_s)),
        "w_out": _adamw("adamw_w_out", w_out[0], g_w_out, m_w_out[0], v_w_out[0], 128),
    }
    grads = {"w_ada": g_w_ada, "b_ada": g_b_ada, "w_in": g_w_in, "conv_w": g_conv_w, "w_s": gws_sum, "w_out": g_w_out}
    vec_names = ["norm_g", "conv_b", "conv_ln_g", "conv_ln_b", "sg_ln_g", "sg_ln_b", "final_g", "b_s"]
    shapes = {"w_ada": w_ada.shape, "b_ada": b_ada.shape, "norm_g": norm_g.shape, "w_in": w_in.shape,
              "conv_w": conv_w.shape, "conv_b": conv_b.shape, "conv_ln_g": conv_ln_g.shape, "conv_ln_b": conv_ln_b.shape,
              "sg_ln_g": sg_ln_g.shape, "sg_ln_b": sg_ln_b.shape, "w_s": w_s.shape, "b_s": b_s.shape,
              "w_out": w_out.shape, "final_g": final_g.shape}
    order = ["w_ada", "b_ada", "norm_g", "w_in", "conv_w", "conv_b", "conv_ln_g", "conv_ln_b", "sg_ln_g", "sg_ln_b",
             "w_s", "b_s", "w_out", "final_g"]

    def leaf(kind, name):
        if name in vec_names:
            src = (g_vec, d_vec, m_vec, v_vec)[kind]
            val = src[vec_names.index(name)]
        elif kind == 0:
            val = grads[name]
        else:
            val = upd[name][kind - 1]
        return val.reshape(shapes[name])

    outs = [loss, grad_x.reshape(x.shape)]
    for kind in range(4):
        outs += [leaf(kind, n) for n in order]
    return tuple(outs)
```

```python
import functools

import jax
import jax.numpy as jnp
from jax import lax
from jax.experimental import pallas as pl
from jax.experimental.pallas import tpu as pltpu

F32 = jnp.float32
BF16 = jnp.bfloat16
MESH = pl.DeviceIdType.MESH

D = 1024
D_IN = 6 * D
D_MIX = 2 * D
N_DEV = 8
W_IN_BLK = D_IN // N_DEV
W_OUT_BLK = D_MIX // N_DEV
W_ADA_BLK = 3 * D // N_DEV
CONV_BLK = D // N_DEV
CONV_W = 31
CONV_HALF = CONV_W // 2
CONV_ROWS = 32
HALO = 16
CHUNK = 128
HEADS = 8
HEAD_DIM = 128
EPS = 1e-6
ROW_TILE = 256
LOSS_ROW = 40
SMALL_ROWS = 48
VMEM_LIMIT = 56 * 1024 * 1024

ADAM_LR = 0.001
ADAM_B1 = 0.9
ADAM_B2 = 0.999
ADAM_EPS = 1e-08
ADAM_WD = 0.01
ADAM_STEP = 10

VM = pl.BlockSpec(memory_space=pltpu.VMEM)


def _params(grid_rank=0, **kw):
    sem = ("arbitrary",) * grid_rank if grid_rank else None
    return pltpu.CompilerParams(dimension_semantics=sem, vmem_limit_bytes=VMEM_LIMIT, **kw)


def _sigmoid(t):
    return jax.nn.sigmoid(t)


def _dsilu(t, sig):
    return sig * (1.0 + t * (1.0 - sig))


def _mesh_pos():
    return lax.axis_index("x"), lax.axis_index("y"), lax.axis_index("c")


class _AllGather:
    def __init__(self, bufs, send_sems, recv_sems):
        x, y, c = _mesh_pos()
        self.bufs, self.send_sems, self.recv_sems = bufs, send_sems, recv_sems
        self.me, self.sib = (x, y, c), (x, y, 1 - c)
        self.chips = [(x, 1 - y), (1 - x, y), (1 - x, 1 - y)]
        self.c = c

    def _copy(self, a, k, block, to):
        px, py, pc = block
        ref = self.bufs[a].at[4 * px + 2 * py + pc]
        return pltpu.make_async_remote_copy(
            src_ref=ref, dst_ref=ref, send_sem=self.send_sems.at[7 * a + k], recv_sem=self.recv_sems.at[7 * a + k],
            device_id=to, device_id_type=MESH)

    def _outgoing(self, a, k):
        if k == 0:
            return self._copy(a, 0, self.me, self.sib)
        if k <= 3:
            return self._copy(a, k, self.me, (*self.chips[k - 1], self.c))
        return self._copy(a, k, (*self.chips[k - 4], self.c), self.sib)

    def source(self, k):
        if k == 0:
            return self.sib
        return (*self.chips[(k - 1) % 3], self.c if k <= 3 else 1 - self.c)

    def block_index(self, k):
        px, py, pc = self.source(k)
        return 4 * px + 2 * py + pc

    def send_own(self, a):
        for k in range(4):
            self._outgoing(a, k).start()

    def arrived(self, a, k):
        self._copy(a, k, self.source(k), self.me).wait_recv()

    def pass_on(self, a, j):
        self._outgoing(a, 4 + j).start()

    def sent(self, a):
        for k in range(7):
            self._outgoing(a, k).wait_send()

    def run(self):
        n = range(len(self.bufs))
        for a in n:
            self.send_own(a)
        for j in range(3):
            for a in n:
                self.arrived(a, 1 + j)
                self.pass_on(a, j)
        for k in (0, 4, 5, 6):
            for a in n:
                self.arrived(a, k)
        for a in n:
            self.sent(a)


def _my_block():
    x, y, c = _mesh_pos()
    return 4 * x + 2 * y + c


def _modulation(c_g, w_ref, b_ref, mod_ref, part, land, send_sems, recv_sems):
    x, y, c = _mesh_pos()
    me = 4 * x + 2 * y + c
    w = w_ref[...]
    for b in range(N_DEV):
        cb = c_g[b]
        part[b] = jnp.dot(cb * _sigmoid(cb), w, preferred_element_type=F32, precision=lax.Precision.HIGHEST)
    land[me] = part[me]

    def copy(b):
        return pltpu.make_async_remote_copy(
            src_ref=part.at[b], dst_ref=land.at[me], send_sem=send_sems.at[b], recv_sem=recv_sems.at[me],
            device_id=(b // 4, (b // 2) % 2, b % 2), device_id_type=MESH)

    def arrival(b):
        return pltpu.make_async_remote_copy(
            src_ref=part.at[b], dst_ref=land.at[b], send_sem=send_sems.at[b], recv_sem=recv_sems.at[b],
            device_id=(b // 4, (b // 2) % 2, b % 2), device_id_type=MESH)

    for b in range(N_DEV):
        @pl.when(b != me)
        def _():
            copy(b).start()
    for b in range(N_DEV):
        @pl.when(b != me)
        def _():
            arrival(b).wait_recv()
            copy(b).wait_send()
    for b in range(N_DEV):
        cols = slice(b * W_ADA_BLK, (b + 1) * W_ADA_BLK)
        mod_ref[:, cols] = land[b] + b_ref[:, cols]


def _rms_modulate(x, mod_ref):
    shift = mod_ref[0:1, 0:D]
    scale = mod_ref[0:1, D:2 * D]
    r = lax.rsqrt(jnp.mean(x * x, axis=-1, keepdims=True) + EPS)
    xn = x * r
    return xn, r, shift, scale


FWD_TILE = 512
FWD_Z_TILE = 1024


def _fwd_in_proj(x, c_rep, w_ada, b_ada, norm_g, w_in, w_out, conv_w):
    s = x.shape[0]
    t, tz = FWD_TILE, FWD_Z_TILE
    n_tiles = s // t
    C_PAY, WIN_PAY, WOUT_PAY, CW_PAY = 0, 1, 2, 3

    def body(x_hbm, c_ref, wada_ref, bada_ref, ng_ref, win_ref, wout_ref, cw_ref,
             z_hbm, ht_ref, win_g, wout_g, cw_g, c_g, mod_ref,
             h, xbuf, zbuf, part, land, ag_send, ag_recv, mod_send, mod_recv, x_sem, z_sem):
        me = _my_block()
        c_g[me] = c_ref[...]
        ag = _AllGather([c_g, win_g, wout_g, cw_g], ag_send, ag_recv)
        ag.send_own(C_PAY)

        def x_copy(i):
            return pltpu.make_async_copy(x_hbm.at[pl.ds(i * t, t), :], xbuf.at[i % 2], x_sem.at[i % 2])

        x_copy(0).start()
        win_g[me] = win_ref[...].astype(BF16)
        wout_g[me] = wout_ref[...].astype(BF16)
        cw_g[me] = cw_ref[...]

        for j in range(3):
            ag.arrived(C_PAY, 1 + j)
            ag.pass_on(C_PAY, j)
        for k in (0, 4, 5, 6):
            ag.arrived(C_PAY, k)
        _modulation(c_g, wada_ref, bada_ref, mod_ref, part, land, mod_send, mod_recv)
        for a in (WIN_PAY, WOUT_PAY, CW_PAY):
            ag.send_own(a)

        for i in range(n_tiles):
            x_copy(i).wait()
            if i + 1 < n_tiles:
                x_copy(i + 1).start()
            xn, _, shift, scale = _rms_modulate(xbuf[i % 2], mod_ref)
            hh = xn * ng_ref[...] * (1.0 + scale) + shift
            h[i * t:(i + 1) * t, :] = hh.astype(BF16)
            ht_ref[:, i * t:(i + 1) * t] = hh.T.astype(BF16)

        def z_copy(slot, row0, col0):
            return pltpu.make_async_copy(zbuf.at[slot], z_hbm.at[pl.ds(row0, tz), pl.ds(col0, W_IN_BLK)], z_sem.at[slot])

        done = [0]

        def z_block(blk):
            col0 = pl.multiple_of(blk * W_IN_BLK, 128)
            for i in range(s // tz):
                slot = done[0] % 2
                if done[0] >= 2:
                    z_copy(slot, 0, 0).wait()
                zbuf[slot] = jnp.dot(h[i * tz:(i + 1) * tz, :], win_g[blk], preferred_element_type=F32)
                z_copy(slot, i * tz, col0).start()
                done[0] += 1

        z_block(me)
        ag.arrived(WIN_PAY, 0)
        z_block(ag.block_index(0))
        for j in (0, 1):
            ag.arrived(WIN_PAY, 1 + j)
            ag.pass_on(WIN_PAY, j)
        for j in (0, 1):
            z_block(ag.block_index(1 + j))
        ag.arrived(WIN_PAY, 4)
        z_block(ag.block_index(4))
        ag.arrived(WIN_PAY, 3)
        ag.pass_on(WIN_PAY, 2)
        ag.arrived(WIN_PAY, 5)
        z_block(ag.block_index(5))
        z_block(ag.block_index(3))
        ag.arrived(WIN_PAY, 6)
        z_block(ag.block_index(6))

        for a in (WOUT_PAY, CW_PAY):
            for j in range(3):
                ag.arrived(a, 1 + j)
                ag.pass_on(a, j)
        for a in (WOUT_PAY, CW_PAY):
            for k in (0, 4, 5, 6):
                ag.arrived(a, k)
        for a in (C_PAY, WIN_PAY, WOUT_PAY, CW_PAY):
            ag.sent(a)
        z_copy(0, 0, 0).wait()
        z_copy(1, 0, 0).wait()

    any_spec = pl.BlockSpec(memory_space=pl.ANY)
    return pl.pallas_call(
        body, name="fwd_in_proj",
        out_shape=(jax.ShapeDtypeStruct((s, D_IN), F32), jax.ShapeDtypeStruct((D, s), BF16),
                   jax.ShapeDtypeStruct((N_DEV,) + w_in.shape, BF16), jax.ShapeDtypeStruct((N_DEV,) + w_out.shape, BF16),
                   jax.ShapeDtypeStruct((N_DEV,) + conv_w.shape, F32), jax.ShapeDtypeStruct((N_DEV,) + c_rep.shape, F32),
                   jax.ShapeDtypeStruct((8, 3 * D), F32)),
        in_specs=[any_spec] + [VM] * 7, out_specs=(any_spec,) + (VM,) * 6,
        scratch_shapes=[pltpu.VMEM((s, D), BF16), pltpu.VMEM((2, t, D), F32), pltpu.VMEM((2, tz, W_IN_BLK), F32),
                        pltpu.VMEM((N_DEV, 8, W_ADA_BLK), F32), pltpu.VMEM((N_DEV, 8, W_ADA_BLK), F32),
                        pltpu.SemaphoreType.DMA((28,)), pltpu.SemaphoreType.DMA((28,)),
                        pltpu.SemaphoreType.DMA((N_DEV,)), pltpu.SemaphoreType.DMA((N_DEV,)),
                        pltpu.SemaphoreType.DMA((2,)), pltpu.SemaphoreType.DMA((2,))],
        compiler_params=_params(),
    )(x, c_rep, w_ada, b_ada, norm_g, w_in, w_out, conv_w)


def _halo_specs(t, s, width):
    per = t // HALO
    last = s // HALO - 1
    prev = pl.BlockSpec((HALO, width), lambda i: (jnp.maximum(i * per - 1, 0), 0))
    nxt = pl.BlockSpec((HALO, width), lambda i: (jnp.minimum((i + 1) * per, last), 0))
    return prev, nxt


def _glu(ref):
    return ref[:, 0:D] * _sigmoid(ref[:, D:2 * D])


SUB = 8
CONV_PHASE_ROWS = ROW_TILE + SUB


def _conv_taps(ext, qbuf, cols, tap, t):
    out = None
    for b in range(SUB):
        q = None
        for a in range((CONV_W - b + SUB - 1) // SUB):
            term = ext[SUB * a:SUB * a + t + SUB, cols] * tap(SUB * a + b)
            q = term if q is None else q + term
        qbuf[b] = q
        shifted = qbuf[b, pl.ds(b + 1, t), :]
        out = shifted if out is None else out + shifted
    return out


def _layer_norm_stats(v):
    mu = jnp.mean(v, axis=-1, keepdims=True)
    cen = v - mu
    rstd = lax.rsqrt(jnp.mean(cen * cen, axis=-1, keepdims=True) + EPS)
    return cen * rstd, rstd


def _layer_norm_bwd(dy_hat, hat, rstd):
    m1 = jnp.mean(dy_hat, axis=-1, keepdims=True)
    m2 = jnp.mean(dy_hat * hat, axis=-1, keepdims=True)
    return rstd * (dy_hat - m1 - hat * m2)


def _colsum(v):
    return jnp.sum(v, axis=0, keepdims=True)


def _mix_and_head(z, x, tgt, mod, conv_w_g, conv_b, cln_g, cln_b, sln_g, sln_b, final_g, ws_b, wst_b, bs_full, wout):
    s = x.shape[0]
    t = ROW_TILE
    n_chunks = t // CHUNK
    n_steps = s // t
    prev_spec, next_spec = _halo_specs(t, s, 2 * D)

    def body(z_ref, zp_ref, zn_ref, x_ref, tgt_ref, mod_ref, cw_ref, cb_ref, clg_ref, clb_ref, slg_ref, slb_ref, fg_ref,
             ws_ref, wst_ref, bs_ref, wout_ref,
             dx2_ref, ycatt_ref, dy_ref, dcv_ref, dzr_ref, acc_ref, gws_ref, gbs_ref,
             gext, cv, vs, dvn, ycat, gbs_acc, qbuf):
        i = pl.program_id(0)

        @pl.when(i == 0)
        def _():
            acc_ref[...] = jnp.zeros_like(acc_ref)
            gws_ref[...] = jnp.zeros_like(gws_ref)
            gbs_acc[...] = jnp.zeros_like(gbs_acc)

        gext[0:HALO, :] = jnp.where(i > 0, _glu(zp_ref), 0.0)
        gext[HALO:HALO + t, :] = _glu(z_ref)
        gext[HALO + t:2 * HALO + t, :] = jnp.where(i < n_steps - 1, _glu(zn_ref), 0.0)
        for blk in range(N_DEV):
            cols = slice(blk * CONV_BLK, (blk + 1) * CONV_BLK)
            cv[:, cols] = _conv_taps(gext, qbuf, cols, lambda k, blk=blk: cw_ref[blk, k:k + 1, :], t) + cb_ref[:, cols]
        ln_hat, ln_rstd = _layer_norm_stats(cv[...])
        ln_a = ln_hat * clg_ref[...] + clb_ref[...]
        sig_ln = _sigmoid(ln_a)
        sa = ln_a * sig_ln
        a_gate = z_ref[:, 2 * D:3 * D]
        sig_ag = _sigmoid(a_gate)
        s_gate = a_gate * sig_ag
        ya = sa * s_gate

        v_hat, v_rstd = _layer_norm_stats(z_ref[:, 4 * D:5 * D])
        vn = v_hat * slg_ref[...] + slb_ref[...]
        vnb = vn.astype(BF16)
        for n in range(n_chunks):
            rows = slice(n * CHUNK, (n + 1) * CHUNK)
            for h in range(HEADS):
                cols = slice(h * HEAD_DIM, (h + 1) * HEAD_DIM)
                vs[rows, cols] = jnp.dot(ws_ref[h], vnb[rows, cols], preferred_element_type=F32) + bs_ref[:, cols]
        u = z_ref[:, 3 * D:4 * D]
        b_gate = z_ref[:, 5 * D:6 * D]
        sig_bg = _sigmoid(b_gate)
        s_bg = b_gate * sig_bg
        vsv = vs[...]
        yb = u * vsv * s_bg

        ycat[:, 0:D] = ya.astype(BF16)
        ycat[:, D:2 * D] = yb.astype(BF16)
        ycatt_ref[0:D, :] = ya.T.astype(BF16)
        ycatt_ref[D:2 * D, :] = yb.T.astype(BF16)
        y = jnp.dot(ycat[...], wout_ref[...], preferred_element_type=F32)
        gate = mod_ref[0:1, 2 * D:3 * D]
        x2 = x_ref[...] + gate * y
        r2 = lax.rsqrt(jnp.mean(x2 * x2, axis=-1, keepdims=True) + EPS)
        x2n = x2 * r2
        fg = fg_ref[...]
        diff = x2n * fg - tgt_ref[...]
        acc_ref[7:8, :] += _colsum(diff * diff)
        dout = diff * (1.0 / D)
        acc_ref[0:1, :] += _colsum(dout * x2n)
        dx2n = dout * fg
        dx2 = r2 * (dx2n - x2n * jnp.mean(dx2n * x2n, axis=-1, keepdims=True))
        dx2_ref[...] = dx2
        acc_ref[1:2, :] += _colsum(dx2 * y)
        dyb16 = (dx2 * gate).astype(BF16)
        dy_ref[...] = dyb16
        dycat = lax.dot_general(dyb16, wout_ref[...], (((1,), (1,)), ((), ())), preferred_element_type=F32)
        dya = dycat[:, 0:D]
        dyb = dycat[:, D:2 * D]

        du = dyb * vsv * s_bg
        dvs = dyb * u * s_bg
        dbg = dyb * u * vsv * _dsilu(b_gate, sig_bg)
        dvsb = dvs.astype(BF16)
        gbs = gbs_acc[...]
        for n in range(n_chunks):
            rows = slice(n * CHUNK, (n + 1) * CHUNK)
            gbs = gbs + dvs[rows, :]
            for h in range(HEADS):
                cols = slice(h * HEAD_DIM, (h + 1) * HEAD_DIM)
                gws_ref[h] += lax.dot_general(dvsb[rows, cols], vnb[rows, cols], (((1,), (1,)), ((), ())),
                                              preferred_element_type=F32)
                dvn[rows, cols] = jnp.dot(wst_ref[h], dvsb[rows, cols], preferred_element_type=F32)
        gbs_acc[...] = gbs

        @pl.when(i == n_steps - 1)
        def _():
            for h in range(HEADS):
                gbs_ref[:, h:h + 1] = jnp.sum(gbs_acc[:, h * HEAD_DIM:(h + 1) * HEAD_DIM], axis=1, keepdims=True)

        dvnv = dvn[...]
        acc_ref[5:6, :] += _colsum(dvnv * v_hat)
        acc_ref[6:7, :] += _colsum(dvnv)
        dv = _layer_norm_bwd(dvnv * slg_ref[...], v_hat, v_rstd)

        dsa = dya * s_gate
        dagate = dya * sa * _dsilu(a_gate, sig_ag)
        dln = dsa * _dsilu(ln_a, sig_ln)
        acc_ref[3:4, :] += _colsum(dln * ln_hat)
        acc_ref[4:5, :] += _colsum(dln)
        dcv = _layer_norm_bwd(dln * clg_ref[...], ln_hat, ln_rstd)
        acc_ref[2:3, :] += _colsum(dcv)
        dcv_ref[...] = dcv

        dzr_ref[:, 0:D] = dagate.astype(BF16)
        dzr_ref[:, D:2 * D] = du.astype(BF16)
        dzr_ref[:, 2 * D:3 * D] = dv.astype(BF16)
        dzr_ref[:, 3 * D:4 * D] = dbg.astype(BF16)

    row = lambda w: pl.BlockSpec((t, w), lambda i: (i, 0))
    const = lambda shape: pl.BlockSpec(shape, lambda i: (0,) * len(shape))
    return pl.pallas_call(
        body, name="mix_and_head", grid=(n_steps,),
        out_shape=(jax.ShapeDtypeStruct((s, D), F32),
                   jax.ShapeDtypeStruct((D_MIX, s), BF16),
                   jax.ShapeDtypeStruct((s, D), BF16),
                   jax.ShapeDtypeStruct((s, D), F32),
                   jax.ShapeDtypeStruct((s, 4 * D), BF16),
                   jax.ShapeDtypeStruct((8, D), F32),
                   jax.ShapeDtypeStruct((HEADS, CHUNK, CHUNK), F32),
                   jax.ShapeDtypeStruct((CHUNK, HEADS), F32)),
        in_specs=[row(D_IN), prev_spec, next_spec, row(D), row(D)] + [VM] * 12,
        out_specs=(row(D), pl.BlockSpec((D_MIX, t), lambda i: (0, i)), row(D), row(D), row(4 * D),
                   const((8, D)), const((HEADS, CHUNK, CHUNK)), const((CHUNK, HEADS))),
        scratch_shapes=[pltpu.VMEM((t + 2 * HALO, D), F32), pltpu.VMEM((t, D), F32), pltpu.VMEM((t, D), F32),
                        pltpu.VMEM((t, D), F32), pltpu.VMEM((t, D_MIX), BF16), pltpu.VMEM((CHUNK, D), F32),
                        pltpu.VMEM((SUB, CONV_PHASE_ROWS, CONV_BLK), F32)],
        compiler_params=_params(1),
    )(z, z, z, x, tgt, mod, conv_w_g, conv_b, cln_g, cln_b, sln_g, sln_b, final_g, ws_b, wst_b, bs_full, wout)


def _bwd_in_proj(z, dcv, dz_rest, x, dx2, mod, norm_g, conv_w_g, win_g):
    s = x.shape[0]
    t = ROW_TILE
    n_steps = s // t
    zp_spec, zn_spec = _halo_specs(t, s, 2 * D)
    dp_spec, dn_spec = _halo_specs(t, s, D)

    def body(z_ref, zp_ref, zn_ref, dcv_ref, dcvp_ref, dcvn_ref, dzr_ref, x_ref, dx2_ref, mod_ref, ng_ref, cw_ref, w_ref,
             gx_ref, dz_ref, acc_ref, gcw_ref,
             gext, dext, dg, taps, dpad, qbuf):
        i = pl.program_id(0)

        @pl.when(i == 0)
        def _():
            acc_ref[...] = jnp.zeros_like(acc_ref)
            gcw_ref[...] = jnp.zeros_like(gcw_ref)

        not_first = i > 0
        not_last = i < n_steps - 1
        gext[0:HALO, :] = jnp.where(not_first, _glu(zp_ref), 0.0)
        gext[HALO:HALO + t, :] = _glu(z_ref)
        gext[HALO + t:2 * HALO + t, :] = jnp.where(not_last, _glu(zn_ref), 0.0)
        dext[0:HALO, :] = jnp.where(not_first, dcvp_ref[...], 0.0)
        dext[HALO:HALO + t, :] = dcv_ref[...]
        dext[HALO + t:2 * HALO + t, :] = jnp.where(not_last, dcvn_ref[...], 0.0)

        taps[...] = jnp.zeros_like(taps)
        dpad[0:SUB, :] = jnp.zeros((SUB, D), F32)
        dpad[SUB:SUB + t, :] = dcv_ref[...]
        dpad[SUB + t:2 * SUB + t, :] = jnp.zeros((SUB, D), F32)
        for blk in range(N_DEV):
            cols = slice(blk * CONV_BLK, (blk + 1) * CONV_BLK)
            dg[:, cols] = _conv_taps(dext, qbuf, cols, lambda k, blk=blk: cw_ref[blk, CONV_W - 1 - k:CONV_W - k, :], t)
            for b in range(SUB):
                dshift = dpad[pl.ds(SUB - 1 - b, t + SUB), cols]
                for a in range((CONV_W - b + SUB - 1) // SUB):
                    k = SUB * a + b
                    taps[k:k + 1, :] = _colsum(gext[SUB * a:SUB * a + t + SUB, cols] * dshift)
            gcw_ref[blk] += taps[...]

        a = z_ref[:, 0:D]
        sig = _sigmoid(z_ref[:, D:2 * D])
        dgv = dg[...]
        dz_ref[:, 0:D] = (dgv * sig).astype(BF16)
        dz_ref[:, D:2 * D] = (dgv * a * sig * (1.0 - sig)).astype(BF16)
        dz_ref[:, 2 * D:6 * D] = dzr_ref[...]

        dh = jnp.zeros((t, D), F32)
        for j in range(N_DEV):
            dh = dh + lax.dot_general(dz_ref[:, j * W_IN_BLK:(j + 1) * W_IN_BLK], w_ref[j], (((1,), (1,)), ((), ())),
                                      preferred_element_type=F32)

        xn, r, _, scale = _rms_modulate(x_ref[...], mod_ref)
        ng = ng_ref[...]
        one_scale = 1.0 + scale
        dh_xn = dh * xn
        acc_ref[0:1, :] += _colsum(dh_xn * one_scale)
        acc_ref[1:2, :] += _colsum(dh)
        acc_ref[2:3, :] += _colsum(dh_xn * ng)
        dxn = dh * (ng * one_scale)
        gx_ref[...] = dx2_ref[...] + r * (dxn - xn * jnp.mean(dxn * xn, axis=-1, keepdims=True))

    row = lambda w: pl.BlockSpec((t, w), lambda i: (i, 0))
    const = lambda shape: pl.BlockSpec(shape, lambda i: (0,) * len(shape))
    return pl.pallas_call(
        body, name="bwd_in_proj", grid=(n_steps,),
        out_shape=(jax.ShapeDtypeStruct((s, D), F32), jax.ShapeDtypeStruct((s, D_IN), BF16),
                   jax.ShapeDtypeStruct((8, D), F32), jax.ShapeDtypeStruct((N_DEV, CONV_ROWS, CONV_BLK), F32)),
        in_specs=[pl.BlockSpec((t, 2 * D), lambda i: (i, 0)), zp_spec, zn_spec, row(D), dp_spec, dn_spec, row(4 * D),
                  row(D), row(D), VM, VM, VM, VM],
        out_specs=(row(D), row(D_IN), const((8, D)), const((N_DEV, CONV_ROWS, CONV_BLK))),
        scratch_shapes=[pltpu.VMEM((t + 2 * HALO, D), F32), pltpu.VMEM((t + 2 * HALO, D), F32), pltpu.VMEM((t, D), F32),
                        pltpu.VMEM((CONV_ROWS, CONV_BLK), F32), pltpu.VMEM((t + 2 * SUB, D), F32),
                        pltpu.VMEM((SUB, CONV_PHASE_ROWS, CONV_BLK), F32)],
        compiler_params=_params(1),
    )(z, z, z, dcv, dcv, dcv, dz_rest, x, dx2, mod, norm_g, conv_w_g, win_g)


WGRAD_K = 1024
_OTHER_CHIPS = [(3, True), (3, False), (1, True), (1, False), (2, True), (2, False)]
_OWN_CHIP = [(0, True), (0, False)]
WGRAD_ROLES = ([("out",) + e for e in _OTHER_CHIPS] + [("in",) + e for e in _OTHER_CHIPS]
               + [("out",) + e for e in _OWN_CHIP] + [("in",) + e for e in _OWN_CHIP])


def _wgrad_schedule():
    x, y, c = _mesh_pos()
    chip = [2 * x + y, 2 * x + (1 - y), 2 * (1 - x) + y, 2 * (1 - x) + (1 - y)]
    blk = lambda r, sibling: 2 * chip[r] + ((1 - c) if sibling else c)
    out_blk, in_blk, is_out = [], [], []
    last = {"out": blk(*_OTHER_CHIPS[0]), "in": blk(*_OTHER_CHIPS[0])}
    for kind, r, sibling in WGRAD_ROLES:
        last[kind] = blk(r, sibling)
        out_blk.append(last["out"])
        in_blk.append(last["in"])
        is_out.append(1 if kind == "out" else 0)
    as_vec = lambda v: jnp.stack([jnp.asarray(e, jnp.int32) for e in v])
    return as_vec(out_blk), as_vec(in_blk), as_vec(is_out)


def _wgrad_reduce(ht, dz, ycatt, dy, small, gws, dmod):
    s = dz.shape[0]
    n_kc = s // WGRAD_K
    n_steps = len(WGRAD_ROLES)
    first_in = [k for k, role in enumerate(WGRAD_ROLES) if role[0] == "in"][0]
    blk_in, blk_out = (D, W_IN_BLK), (W_OUT_BLK, D)

    def body(out_blk, in_blk, is_out, ht_ref, dz_ref, yt_ref, dy_ref, small_ref, gws_ref, dmod_ref,
             oin_ref, oout_ref, osmall_ref, ogws_ref, odmod_ref, obada_ref, oloss_ref,
             acc_in, acc_out, p1_in, p1_out, l1_in, l1_out, l2_in, l2_out, small_g, gws_g, dmod_g,
             s1_send, s1_recv, s2_send, s2_recv, ag_send, ag_recv):
        step, kc = pl.program_id(0), pl.program_id(1)
        x, y, c = _mesh_pos()
        sib = (x, y, 1 - c)
        chip_of = [(x, y), (x, 1 - y), (1 - x, y), (1 - x, 1 - y)]
        pay = {"in": (0, acc_in, p1_in, l1_in, l2_in, oin_ref), "out": (1, acc_out, p1_out, l1_out, l2_out, oout_ref)}
        ag = _AllGather([dmod_g, small_g, gws_g], ag_send, ag_recv)
        last_kc = kc == n_kc - 1

        @pl.when((step == 0) & (kc == 0))
        def _():
            me = 4 * x + 2 * y + c
            small_g[me] = small_ref[...]
            gws_g[me] = gws_ref[...]
            dmod_g[me] = dmod_ref[...]
            for a in range(3):
                ag.send_own(a)

        @pl.when((step == 4) & (kc == 0))
        def _():
            for j in range(3):
                for a in range(3):
                    ag.arrived(a, 1 + j)
                    ag.pass_on(a, j)

        def accumulate(acc, prod):
            @pl.when(kc == 0)
            def _():
                acc[...] = prod

            @pl.when(kc != 0)
            def _():
                acc[...] += prod

        @pl.when(is_out[step] == 1)
        def _():
            accumulate(acc_out, jnp.dot(yt_ref[...], dy_ref[...], preferred_element_type=F32))

        @pl.when(is_out[step] == 0)
        def _():
            accumulate(acc_in, jnp.dot(ht_ref[...], dz_ref[...], preferred_element_type=F32))

        def step1(kind, r):
            a, _, p1, l1, _, _ = pay[kind]
            return pltpu.make_async_remote_copy(
                src_ref=p1.at[r], dst_ref=l1.at[r], send_sem=s1_send.at[4 * a + r], recv_sem=s1_recv.at[4 * a + r],
                device_id=sib, device_id_type=MESH)

        def step2(kind, r):
            a, _, _, l1, l2, _ = pay[kind]
            return pltpu.make_async_remote_copy(
                src_ref=l1.at[r], dst_ref=l2.at[r - 1], send_sem=s2_send.at[3 * a + r - 1],
                recv_sem=s2_recv.at[3 * a + r - 1], device_id=(*chip_of[r], c), device_id_type=MESH)

        for k, (kind, r, sibling) in enumerate(WGRAD_ROLES):
            @pl.when((step == k) & last_kc)
            def _(kind=kind, r=r, sibling=sibling):
                _, acc, p1, l1, _, out = pay[kind]
                if sibling:
                    p1[r] = acc[...].astype(BF16)
                    step1(kind, r).start()
                else:
                    step1(kind, r).wait_recv()
                    both = acc[...] + l1[r].astype(F32)
                    if r:
                        l1[r] = both.astype(BF16)
                        step2(kind, r).start()
                    else:
                        out[...] = both

        @pl.when((step == n_steps - 1) & last_kc)
        def _():
            for kind in ("out", "in"):
                _, _, _, _, l2, out = pay[kind]
                total = out[...]
                for r in (1, 2, 3):
                    step2(kind, r).wait_recv()
                    total = total + l2[r - 1].astype(F32)
                out[...] = total
            for kind in ("out", "in"):
                for r in range(4):
                    step1(kind, r).wait_send()
                for r in (1, 2, 3):
                    step2(kind, r).wait_send()
            for k in (0, 4, 5, 6):
                for a in range(3):
                    ag.arrived(a, k)
            for a in range(3):
                ag.sent(a)
            tot_small, tot_gws, tot_dmod = small_g[0], gws_g[0], dmod_g[0]
            for b in range(1, N_DEV):
                tot_small = tot_small + small_g[b]
                tot_gws = tot_gws + gws_g[b]
                tot_dmod = tot_dmod + dmod_g[b]
            osmall_ref[...] = tot_small
            oloss_ref[...] = jnp.full(oloss_ref.shape, (0.5 / D) * jnp.sum(tot_small[LOSS_ROW:LOSS_ROW + 1, :]), F32)
            ogws_ref[...] = tot_gws
            obada_ref[...] = tot_dmod
            for b in range(N_DEV):
                odmod_ref[b:b + 1, :] = dmod_g[b, 0:1, :]

    def kc_of(working, step, kc, hold_first):
        held = jnp.where(step < hold_first, 0, n_kc - 1)
        return jnp.where(working, kc, held)

    out_kc = lambda i, kc, ob, ib, io: kc_of(io[i] == 1, i, kc, 0)
    in_kc = lambda i, kc, ob, ib, io: kc_of(io[i] == 0, i, kc, first_in)
    grid_spec = pltpu.PrefetchScalarGridSpec(
        num_scalar_prefetch=3, grid=(n_steps, n_kc),
        in_specs=[pl.BlockSpec((D, WGRAD_K), lambda i, kc, ob, ib, io: (0, in_kc(i, kc, ob, ib, io))),
                  pl.BlockSpec((WGRAD_K, W_IN_BLK), lambda i, kc, ob, ib, io: (in_kc(i, kc, ob, ib, io), ib[i])),
                  pl.BlockSpec((W_OUT_BLK, WGRAD_K), lambda i, kc, ob, ib, io: (ob[i], out_kc(i, kc, ob, ib, io))),
                  pl.BlockSpec((WGRAD_K, D), lambda i, kc, ob, ib, io: (out_kc(i, kc, ob, ib, io), 0)),
                  VM, VM, VM],
        out_specs=(VM,) * 7,
        scratch_shapes=[pltpu.VMEM(blk_in, F32), pltpu.VMEM(blk_out, F32),
                        pltpu.VMEM((4,) + blk_in, BF16), pltpu.VMEM((4,) + blk_out, BF16),
                        pltpu.VMEM((4,) + blk_in, BF16), pltpu.VMEM((4,) + blk_out, BF16),
                        pltpu.VMEM((3,) + blk_in, BF16), pltpu.VMEM((3,) + blk_out, BF16),
                        pltpu.VMEM((N_DEV,) + small.shape, F32), pltpu.VMEM((N_DEV,) + gws.shape, F32),
                        pltpu.VMEM((N_DEV,) + dmod.shape, F32),
                        pltpu.SemaphoreType.DMA((8,)), pltpu.SemaphoreType.DMA((8,)),
                        pltpu.SemaphoreType.DMA((6,)), pltpu.SemaphoreType.DMA((6,)),
                        pltpu.SemaphoreType.DMA((21,)), pltpu.SemaphoreType.DMA((21,))])
    return pl.pallas_call(
        body, name="wgrad_reduce", grid_spec=grid_spec,
        out_shape=(jax.ShapeDtypeStruct(blk_in, F32), jax.ShapeDtypeStruct(blk_out, F32),
                   jax.ShapeDtypeStruct(small.shape, F32), jax.ShapeDtypeStruct(gws.shape, F32),
                   jax.ShapeDtypeStruct((N_DEV, 3 * D), F32), jax.ShapeDtypeStruct((8, 3 * D), F32),
                   jax.ShapeDtypeStruct((8, 128), F32)),
        compiler_params=_params(2),
    )(*_wgrad_schedule(), ht, dz, ycatt, dy, small, gws, dmod)


def _wgrad_ada(c_all, dmod_cols):
    def body(c_ref, dm_ref, g_ref, act):
        for b in range(N_DEV):
            cb = c_ref[b, 0:1, :]
            act[b:b + 1, :] = cb * _sigmoid(cb)
        g_ref[...] = lax.dot_general(act[...], dm_ref[...], (((0,), (0,)), ((), ())), preferred_element_type=F32,
                                     precision=lax.Precision.HIGHEST)

    return pl.pallas_call(
        body, name="wgrad_ada", out_shape=jax.ShapeDtypeStruct((D, W_ADA_BLK), F32),
        in_specs=[VM, VM], out_specs=VM, scratch_shapes=[pltpu.VMEM((N_DEV, D), F32)],
        compiler_params=_params(),
    )(c_all, dmod_cols)


def _adamw_math(w, g, m, v):
    m = ADAM_B1 * m + (1.0 - ADAM_B1) * g
    v = ADAM_B2 * v + (1.0 - ADAM_B2) * (g * g)
    m_hat = m / (1.0 - ADAM_B1 ** ADAM_STEP)
    v_hat = v / (1.0 - ADAM_B2 ** ADAM_STEP)
    delta = -ADAM_LR * (m_hat / (jnp.sqrt(v_hat) + ADAM_EPS) + ADAM_WD * w)
    return delta, m, v


def _adamw(name, w, g, m, v, row_block=None):
    shape = w.shape

    def body(w_ref, g_ref, m_ref, v_ref, d_ref, nm_ref, nv_ref):
        d_ref[...], nm_ref[...], nv_ref[...] = _adamw_math(w_ref[...], g_ref[...], m_ref[...], v_ref[...])

    out_shape = (jax.ShapeDtypeStruct(shape, F32),) * 3
    if row_block is None:
        return pl.pallas_call(body, name=name, out_shape=out_shape, in_specs=[VM] * 4, out_specs=(VM,) * 3,
                              compiler_params=_params())(w, g, m, v)
    spec = pl.BlockSpec((row_block, shape[1]), lambda i: (i, 0))
    return pl.pallas_call(body, name=name, grid=(shape[0] // row_block,), out_shape=out_shape,
                          in_specs=[spec] * 4, out_specs=(spec,) * 3, compiler_params=_params(1))(w, g, m, v)


def kernel(x, c, w_ada, b_ada, norm_g, w_in, conv_w, conv_b, conv_ln_g, conv_ln_b, sg_ln_g, sg_ln_b, w_s, b_s, w_out, final_g, loss_target, m_w_ada, m_b_ada, m_norm_g, m_w_in, m_conv_w, m_conv_b, m_conv_ln_g, m_conv_ln_b, m_sg_ln_g, m_sg_ln_b, m_w_s, m_b_s, m_w_out, m_final_g, v_w_ada, v_b_ada, v_norm_g, v_w_in, v_conv_w, v_conv_b, v_conv_ln_g, v_conv_ln_b, v_sg_ln_g, v_sg_ln_b, v_w_s, v_b_s, v_w_out, v_final_g):
    me = 4 * lax.axis_index("x") + 2 * lax.axis_index("y") + lax.axis_index("c")
    x2d, tgt2d = x[0], loss_target[0]
    row1 = lambda a: a.reshape(1, D)
    taps = lambda a: jnp.pad(a.reshape(CONV_W, CONV_BLK), ((0, CONV_ROWS - CONV_W), (0, 0)))

    z, ht, win_g, wout_g, cw_g, c_all, mod = _fwd_in_proj(
        x2d, jnp.broadcast_to(c, (8, D)), w_ada[0], b_ada, norm_g, w_in[0], w_out[0], taps(conv_w))
    ws_b = w_s[0].astype(BF16)
    wst_b = jnp.swapaxes(w_s[0], 1, 2).astype(BF16)
    bs_full = jnp.repeat(b_s[0].T, HEAD_DIM, axis=1)

    dx2, ycatt, dy, dcv, dz_rest, acc_a, gws, gbs = _mix_and_head(
        z, x2d, tgt2d, mod, cw_g, conv_b, conv_ln_g, conv_ln_b, sg_ln_g, sg_ln_b, row1(final_g), ws_b, wst_b, bs_full,
        wout_g.reshape(D_MIX, D))
    grad_x, dz, acc_b, gcw = _bwd_in_proj(z, dcv, dz_rest, x2d, dx2, mod, norm_g, cw_g, win_g)

    gbs_row = gbs.T.reshape(1, D)
    small = jnp.concatenate(
        [acc_b[0:1], acc_a[2:7], acc_a[0:1], gbs_row, jnp.transpose(gcw, (1, 0, 2)).reshape(CONV_ROWS, D),
         acc_a[7:8], jnp.zeros((SMALL_ROWS - LOSS_ROW - 1, D), F32)], axis=0)
    dmod_row = jnp.concatenate([acc_b[1:2], acc_b[2:3], acc_a[1:2]], axis=1)
    g_w_in, g_w_out, small_sum, gws_sum, dmod_all, bada8, loss_tile = _wgrad_reduce(
        ht, dz, ycatt, dy, small, gws.reshape(HEADS * CHUNK, CHUNK), jnp.broadcast_to(dmod_row, (8, 3 * D)))
    loss = loss_tile[0, 0]

    g_w_ada = _wgrad_ada(c_all, lax.dynamic_slice(dmod_all, (0, me * W_ADA_BLK), (N_DEV, W_ADA_BLK)))
    g_b_ada = bada8[0:1]
    g_vec = small_sum[0:8]
    g_conv_w = lax.dynamic_slice(small_sum, (8, me * CONV_BLK), (CONV_W, CONV_BLK))

    vec = lambda *a: jnp.concatenate([row1(t) for t in a], axis=0)
    d_vec, m_vec, v_vec = _adamw(
        "adamw_vectors",
        vec(norm_g, conv_b, conv_ln_g, conv_ln_b, sg_ln_g, sg_ln_b, final_g, b_s), g_vec,
        vec(m_norm_g, m_conv_b, m_conv_ln_g, m_conv_ln_b, m_sg_ln_g, m_sg_ln_b, m_final_g, m_b_s),
        vec(v_norm_g, v_conv_b, v_conv_ln_g, v_conv_ln_b, v_sg_ln_g, v_sg_ln_b, v_final_g, v_b_s))
    flat_ws = lambda a: a.reshape(HEADS * CHUNK, CHUNK)
    upd = {
        "w_ada": _adamw("adamw_w_ada", w_ada[0], g_w_ada, m_w_ada[0], v_w_ada[0], 256),
        "b_ada": _adamw("adamw_b_ada", b_ada, g_b_ada, m_b_ada, v_b_ada),
        "w_in": _adamw("adamw_w_in", w_in[0], g_w_in, m_w_in[0], v_w_in[0], 256),
        "conv_w": _adamw("adamw_conv_w", conv_w.reshape(CONV_W, CONV_BLK), g_conv_w, m_conv_w.reshape(CONV_W, CONV_BLK),
                         v_conv_w.reshape(CONV_W, CONV_BLK)),
        "w_s": _adamw("adamw_w_s", flat_ws(w_s), gws_sum, flat_ws(m_w_s), flat_ws(v_w_s)),
        "w_out": _adamw("adamw_w_out", w_out[0], g_w_out, m_w_out[0], v_w_out[0], 128),
    }
    grads = {"w_ada": g_w_ada, "b_ada": g_b_ada, "w_in": g_w_in, "conv_w": g_conv_w, "w_s": gws_sum, "w_out": g_w_out}
    vec_names = ["norm_g", "conv_b", "conv_ln_g", "conv_ln_b", "sg_ln_g", "sg_ln_b", "final_g", "b_s"]
    shapes = {"w_ada": w_ada.shape, "b_ada": b_ada.shape, "norm_g": norm_g.shape, "w_in": w_in.shape,
              "conv_w": conv_w.shape, "conv_b": conv_b.shape, "conv_ln_g": conv_ln_g.shape, "conv_ln_b": conv_ln_b.shape,
              "sg_ln_g": sg_ln_g.shape, "sg_ln_b": sg_ln_b.shape, "w_s": w_s.shape, "b_s": b_s.shape,
              "w_out": w_out.shape, "final_g": final_g.shape}
    order = ["w_ada", "b_ada", "norm_g", "w_in", "conv_w", "conv_b", "conv_ln_g", "conv_ln_b", "sg_ln_g", "sg_ln_b",
             "w_s", "b_s", "w_out", "final_g"]

    def leaf(kind, name):
        if name in vec_names:
            src = (g_vec, d_vec, m_vec, v_vec)[kind]
            val = src[vec_names.index(name)]
        elif kind == 0:
            val = grads[name]
        else:
            val = upd[name][kind - 1]
        return val.reshape(shapes[name])

    outs = [loss, grad_x.reshape(x.shape)]
    for kind in range(4):
        outs += [leaf(kind, n) for n in order]
    return tuple(outs)
```

```python
import functools

import jax
import jax.numpy as jnp
from jax import lax
from jax.experimental import pallas as pl
from jax.experimental.pallas import tpu as pltpu

F32 = jnp.float32
BF16 = jnp.bfloat16
MESH = pl.DeviceIdType.MESH

D = 1024
D_IN = 6 * D
D_MIX = 2 * D
N_DEV = 8
W_IN_BLK = D_IN // N_DEV
W_OUT_BLK = D_MIX // N_DEV
W_ADA_BLK = 3 * D // N_DEV
CONV_BLK = D // N_DEV
CONV_W = 31
CONV_HALF = CONV_W // 2
CONV_ROWS = 32
HALO = 16
CHUNK = 128
HEADS = 8
HEAD_DIM = 128
EPS = 1e-6
ROW_TILE = 256
VMEM_LIMIT = 56 * 1024 * 1024

ADAM_LR = 0.001
ADAM_B1 = 0.9
ADAM_B2 = 0.999
ADAM_EPS = 1e-08
ADAM_WD = 0.01
ADAM_STEP = 10

VM = pl.BlockSpec(memory_space=pltpu.VMEM)


def _params(grid_rank=0, **kw):
    sem = ("arbitrary",) * grid_rank if grid_rank else None
    return pltpu.CompilerParams(dimension_semantics=sem, vmem_limit_bytes=VMEM_LIMIT, **kw)


def _sigmoid(t):
    return jax.nn.sigmoid(t)


def _dsilu(t, sig):
    return sig * (1.0 + t * (1.0 - sig))


def _mesh_pos():
    return lax.axis_index("x"), lax.axis_index("y"), lax.axis_index("c")


class _AllGather:
    def __init__(self, bufs, send_sems, recv_sems):
        x, y, c = _mesh_pos()
        self.bufs, self.send_sems, self.recv_sems = bufs, send_sems, recv_sems
        self.me, self.sib = (x, y, c), (x, y, 1 - c)
        self.chips = [(x, 1 - y), (1 - x, y), (1 - x, 1 - y)]
        self.c = c

    def _copy(self, a, k, block, to):
        px, py, pc = block
        ref = self.bufs[a].at[4 * px + 2 * py + pc]
        return pltpu.make_async_remote_copy(
            src_ref=ref, dst_ref=ref, send_sem=self.send_sems.at[7 * a + k], recv_sem=self.recv_sems.at[7 * a + k],
            device_id=to, device_id_type=MESH)

    def _outgoing(self, a, k):
        if k == 0:
            return self._copy(a, 0, self.me, self.sib)
        if k <= 3:
            return self._copy(a, k, self.me, (*self.chips[k - 1], self.c))
        return self._copy(a, k, (*self.chips[k - 4], self.c), self.sib)

    def source(self, k):
        if k == 0:
            return self.sib
        return (*self.chips[(k - 1) % 3], self.c if k <= 3 else 1 - self.c)

    def block_index(self, k):
        px, py, pc = self.source(k)
        return 4 * px + 2 * py + pc

    def send_own(self, a):
        for k in range(4):
            self._outgoing(a, k).start()

    def arrived(self, a, k):
        self._copy(a, k, self.source(k), self.me).wait_recv()

    def pass_on(self, a, j):
        self._outgoing(a, 4 + j).start()

    def sent(self, a):
        for k in range(7):
            self._outgoing(a, k).wait_send()

    def run(self):
        n = range(len(self.bufs))
        for a in n:
            self.send_own(a)
        for j in range(3):
            for a in n:
                self.arrived(a, 1 + j)
                self.pass_on(a, j)
        for k in (0, 4, 5, 6):
            for a in n:
                self.arrived(a, k)
        for a in n:
            self.sent(a)


class _ChipReduce:
    def __init__(self, l1, l2, s1_send, s1_recv, s2_send, s2_recv):
        x, y, c = _mesh_pos()
        self.l1, self.l2 = l1, l2
        self.s1_send, self.s1_recv, self.s2_send, self.s2_recv = s1_send, s1_recv, s2_send, s2_recv
        self.c, self.sib = c, (x, y, 1 - c)
        self.chip_of = [(x, y), (x, 1 - y), (1 - x, y), (1 - x, 1 - y)]
        self.chip = [2 * x + y, 2 * x + (1 - y), 2 * (1 - x) + y, 2 * (1 - x) + (1 - y)]

    def block(self, r, sibling):
        return 2 * self.chip[r] + ((1 - self.c) if sibling else self.c)

    def to_sibling(self, r, src=None):
        return pltpu.make_async_remote_copy(
            src_ref=self.l1.at[r] if src is None else src, dst_ref=self.l1.at[r], send_sem=self.s1_send.at[r],
            recv_sem=self.s1_recv.at[r], device_id=self.sib, device_id_type=MESH)

    def to_chip(self, r):
        return pltpu.make_async_remote_copy(
            src_ref=self.l1.at[r], dst_ref=self.l2.at[r - 1], send_sem=self.s2_send.at[r - 1],
            recv_sem=self.s2_recv.at[r - 1], device_id=(*self.chip_of[r], self.c), device_id_type=MESH)

    def combine(self, r, mine):
        self.to_sibling(r).wait_recv()
        both = mine + self.l1[r].astype(F32)
        if r == 0:
            return both
        self.l1[r] = both.astype(BF16)
        self.to_chip(r).start()
        return None

    def finish(self, own_chip_sum):
        total = own_chip_sum
        for r in (1, 2, 3):
            self.to_chip(r).wait_recv()
            total = total + self.l2[r - 1].astype(F32)
        for r in range(4):
            self.to_sibling(r).wait_send()
        for r in (1, 2, 3):
            self.to_chip(r).wait_send()
        return total


def _my_block():
    x, y, c = _mesh_pos()
    return 4 * x + 2 * y + c


def _modulation(c_g, w_ref, b_ref, mod_ref, part, land, send_sems, recv_sems):
    x, y, c = _mesh_pos()
    me = 4 * x + 2 * y + c
    w = w_ref[...]
    for b in range(N_DEV):
        cb = c_g[b]
        part[b] = jnp.dot(cb * _sigmoid(cb), w, preferred_element_type=F32, precision=lax.Precision.HIGHEST)
    land[me] = part[me]

    def copy(b):
        return pltpu.make_async_remote_copy(
            src_ref=part.at[b], dst_ref=land.at[me], send_sem=send_sems.at[b], recv_sem=recv_sems.at[me],
            device_id=(b // 4, (b // 2) % 2, b % 2), device_id_type=MESH)

    def arrival(b):
        return pltpu.make_async_remote_copy(
            src_ref=part.at[b], dst_ref=land.at[b], send_sem=send_sems.at[b], recv_sem=recv_sems.at[b],
            device_id=(b // 4, (b // 2) % 2, b % 2), device_id_type=MESH)

    for b in range(N_DEV):
        @pl.when(b != me)
        def _():
            copy(b).start()
    for b in range(N_DEV):
        @pl.when(b != me)
        def _():
            arrival(b).wait_recv()
            copy(b).wait_send()
    for b in range(N_DEV):
        cols = slice(b * W_ADA_BLK, (b + 1) * W_ADA_BLK)
        mod_ref[:, cols] = land[b] + b_ref[:, cols]


def _rms_modulate(x, mod_ref):
    shift = mod_ref[0:1, 0:D]
    scale = mod_ref[0:1, D:2 * D]
    r = lax.rsqrt(jnp.mean(x * x, axis=-1, keepdims=True) + EPS)
    xn = x * r
    return xn, r, shift, scale


FWD_TILE = 512
FWD_Z_TILE = 1024


def _fwd_in_proj(x, c_rep, w_ada, b_ada, norm_g, w_in, w_out, conv_w):
    s = x.shape[0]
    t, tz = FWD_TILE, FWD_Z_TILE
    n_tiles = s // t
    C_PAY, WIN_PAY, WOUT_PAY, CW_PAY = 0, 1, 2, 3

    def body(x_hbm, c_ref, wada_ref, bada_ref, ng_ref, win_ref, wout_ref, cw_ref,
             z_hbm, ht_ref, win_g, wout_g, cw_g, c_g, mod_ref,
             h, xbuf, zbuf, part, land, ag_send, ag_recv, mod_send, mod_recv, x_sem, z_sem):
        me = _my_block()
        c_g[me] = c_ref[...]
        ag = _AllGather([c_g, win_g, wout_g, cw_g], ag_send, ag_recv)
        ag.send_own(C_PAY)

        def x_copy(i):
            return pltpu.make_async_copy(x_hbm.at[pl.ds(i * t, t), :], xbuf.at[i % 2], x_sem.at[i % 2])

        x_copy(0).start()
        win_g[me] = win_ref[...].astype(BF16)
        wout_g[me] = wout_ref[...].astype(BF16)
        cw_g[me] = cw_ref[...]

        for j in range(3):
            ag.arrived(C_PAY, 1 + j)
            ag.pass_on(C_PAY, j)
        for k in (0, 4, 5, 6):
            ag.arrived(C_PAY, k)
        _modulation(c_g, wada_ref, bada_ref, mod_ref, part, land, mod_send, mod_recv)
        for a in (WIN_PAY, WOUT_PAY, CW_PAY):
            ag.send_own(a)

        for i in range(n_tiles):
            x_copy(i).wait()
            if i + 1 < n_tiles:
                x_copy(i + 1).start()
            xn, _, shift, scale = _rms_modulate(xbuf[i % 2], mod_ref)
            hh = xn * ng_ref[...] * (1.0 + scale) + shift
            h[i * t:(i + 1) * t, :] = hh.astype(BF16)
            ht_ref[:, i * t:(i + 1) * t] = hh.T.astype(BF16)

        def z_copy(slot, row0, col0):
            return pltpu.make_async_copy(zbuf.at[slot], z_hbm.at[pl.ds(row0, tz), pl.ds(col0, W_IN_BLK)], z_sem.at[slot])

        done = [0]

        def z_block(blk):
            col0 = pl.multiple_of(blk * W_IN_BLK, 128)
            for i in range(s // tz):
                slot = done[0] % 2
                if done[0] >= 2:
                    z_copy(slot, 0, 0).wait()
                zbuf[slot] = jnp.dot(h[i * tz:(i + 1) * tz, :], win_g[blk], preferred_element_type=F32)
                z_copy(slot, i * tz, col0).start()
                done[0] += 1

        z_block(me)
        ag.arrived(WIN_PAY, 0)
        z_block(ag.block_index(0))
        for j in (0, 1):
            ag.arrived(WIN_PAY, 1 + j)
            ag.pass_on(WIN_PAY, j)
        for j in (0, 1):
            z_block(ag.block_index(1 + j))
        ag.arrived(WIN_PAY, 4)
        z_block(ag.block_index(4))
        ag.arrived(WIN_PAY, 3)
        ag.pass_on(WIN_PAY, 2)
        ag.arrived(WIN_PAY, 5)
        z_block(ag.block_index(5))
        z_block(ag.block_index(3))
        ag.arrived(WIN_PAY, 6)
        z_block(ag.block_index(6))

        for a in (WOUT_PAY, CW_PAY):
            for j in range(3):
                ag.arrived(a, 1 + j)
                ag.pass_on(a, j)
        for a in (WOUT_PAY, CW_PAY):
            for k in (0, 4, 5, 6):
                ag.arrived(a, k)
        for a in (C_PAY, WIN_PAY, WOUT_PAY, CW_PAY):
            ag.sent(a)
        z_copy(0, 0, 0).wait()
        z_copy(1, 0, 0).wait()

    any_spec = pl.BlockSpec(memory_space=pl.ANY)
    return pl.pallas_call(
        body, name="fwd_in_proj",
        out_shape=(jax.ShapeDtypeStruct((s, D_IN), F32), jax.ShapeDtypeStruct((D, s), BF16),
                   jax.ShapeDtypeStruct((N_DEV,) + w_in.shape, BF16), jax.ShapeDtypeStruct((N_DEV,) + w_out.shape, BF16),
                   jax.ShapeDtypeStruct((N_DEV,) + conv_w.shape, F32), jax.ShapeDtypeStruct((N_DEV,) + c_rep.shape, F32),
                   jax.ShapeDtypeStruct((8, 3 * D), F32)),
        in_specs=[any_spec] + [VM] * 7, out_specs=(any_spec,) + (VM,) * 6,
        scratch_shapes=[pltpu.VMEM((s, D), BF16), pltpu.VMEM((2, t, D), F32), pltpu.VMEM((2, tz, W_IN_BLK), F32),
                        pltpu.VMEM((N_DEV, 8, W_ADA_BLK), F32), pltpu.VMEM((N_DEV, 8, W_ADA_BLK), F32),
                        pltpu.SemaphoreType.DMA((28,)), pltpu.SemaphoreType.DMA((28,)),
                        pltpu.SemaphoreType.DMA((N_DEV,)), pltpu.SemaphoreType.DMA((N_DEV,)),
                        pltpu.SemaphoreType.DMA((2,)), pltpu.SemaphoreType.DMA((2,))],
        compiler_params=_params(),
    )(x, c_rep, w_ada, b_ada, norm_g, w_in, w_out, conv_w)


def _halo_specs(t, s, width):
    per = t // HALO
    last = s // HALO - 1
    prev = pl.BlockSpec((HALO, width), lambda i: (jnp.maximum(i * per - 1, 0), 0))
    nxt = pl.BlockSpec((HALO, width), lambda i: (jnp.minimum((i + 1) * per, last), 0))
    return prev, nxt


def _glu(ref):
    return ref[:, 0:D] * _sigmoid(ref[:, D:2 * D])


SUB = 8
CONV_PHASE_ROWS = ROW_TILE + SUB


def _conv_taps(ext, qbuf, cols, tap, t):
    out = None
    for b in range(SUB):
        q = None
        for a in range((CONV_W - b + SUB - 1) // SUB):
            term = ext[SUB * a:SUB * a + t + SUB, cols] * tap(SUB * a + b)
            q = term if q is None else q + term
        qbuf[b] = q
        shifted = qbuf[b, pl.ds(b + 1, t), :]
        out = shifted if out is None else out + shifted
    return out


def _layer_norm_stats(v):
    mu = jnp.mean(v, axis=-1, keepdims=True)
    cen = v - mu
    rstd = lax.rsqrt(jnp.mean(cen * cen, axis=-1, keepdims=True) + EPS)
    return cen * rstd, rstd


def _layer_norm_bwd(dy_hat, hat, rstd):
    m1 = jnp.mean(dy_hat, axis=-1, keepdims=True)
    m2 = jnp.mean(dy_hat * hat, axis=-1, keepdims=True)
    return rstd * (dy_hat - m1 - hat * m2)


def _colsum(v):
    return jnp.sum(v, axis=0, keepdims=True)


def _mix_and_head(z, x, tgt, mod, conv_w_g, conv_b, cln_g, cln_b, sln_g, sln_b, final_g, ws_b, wst_b, bs_full, wout):
    s = x.shape[0]
    t = ROW_TILE
    n_chunks = t // CHUNK
    n_steps = s // t
    prev_spec, next_spec = _halo_specs(t, s, 2 * D)

    def body(z_ref, zp_ref, zn_ref, x_ref, tgt_ref, mod_ref, cw_ref, cb_ref, clg_ref, clb_ref, slg_ref, slb_ref, fg_ref,
             ws_ref, wst_ref, bs_ref, wout_ref,
             dx2_ref, ycatt_ref, dy_ref, dcv_ref, dzr_ref, acc_ref, gws_ref, gbs_ref,
             gext, cv, vs, dvn, ycat, gbs_acc, qbuf):
        i = pl.program_id(0)

        @pl.when(i == 0)
        def _():
            acc_ref[...] = jnp.zeros_like(acc_ref)
            gws_ref[...] = jnp.zeros_like(gws_ref)
            gbs_acc[...] = jnp.zeros_like(gbs_acc)

        gext[0:HALO, :] = jnp.where(i > 0, _glu(zp_ref), 0.0)
        gext[HALO:HALO + t, :] = _glu(z_ref)
        gext[HALO + t:2 * HALO + t, :] = jnp.where(i < n_steps - 1, _glu(zn_ref), 0.0)
        for blk in range(N_DEV):
            cols = slice(blk * CONV_BLK, (blk + 1) * CONV_BLK)
            cv[:, cols] = _conv_taps(gext, qbuf, cols, lambda k, blk=blk: cw_ref[blk, k:k + 1, :], t) + cb_ref[:, cols]
        ln_hat, ln_rstd = _layer_norm_stats(cv[...])
        ln_a = ln_hat * clg_ref[...] + clb_ref[...]
        sig_ln = _sigmoid(ln_a)
        sa = ln_a * sig_ln
        a_gate = z_ref[:, 2 * D:3 * D]
        sig_ag = _sigmoid(a_gate)
        s_gate = a_gate * sig_ag
        ya = sa * s_gate

        v_hat, v_rstd = _layer_norm_stats(z_ref[:, 4 * D:5 * D])
        vn = v_hat * slg_ref[...] + slb_ref[...]
        vnb = vn.astype(BF16)
        for n in range(n_chunks):
            rows = slice(n * CHUNK, (n + 1) * CHUNK)
            for h in range(HEADS):
                cols = slice(h * HEAD_DIM, (h + 1) * HEAD_DIM)
                vs[rows, cols] = jnp.dot(ws_ref[h], vnb[rows, cols], preferred_element_type=F32) + bs_ref[:, cols]
        u = z_ref[:, 3 * D:4 * D]
        b_gate = z_ref[:, 5 * D:6 * D]
        sig_bg = _sigmoid(b_gate)
        s_bg = b_gate * sig_bg
        vsv = vs[...]
        yb = u * vsv * s_bg

        ycat[:, 0:D] = ya.astype(BF16)
        ycat[:, D:2 * D] = yb.astype(BF16)
        ycatt_ref[0:D, :] = ya.T.astype(BF16)
        ycatt_ref[D:2 * D, :] = yb.T.astype(BF16)
        y = jnp.dot(ycat[...], wout_ref[...], preferred_element_type=F32)
        gate = mod_ref[0:1, 2 * D:3 * D]
        x2 = x_ref[...] + gate * y
        r2 = lax.rsqrt(jnp.mean(x2 * x2, axis=-1, keepdims=True) + EPS)
        x2n = x2 * r2
        fg = fg_ref[...]
        diff = x2n * fg - tgt_ref[...]
        acc_ref[7:8, :] += _colsum(diff * diff)
        dout = diff * (1.0 / D)
        acc_ref[0:1, :] += _colsum(dout * x2n)
        dx2n = dout * fg
        dx2 = r2 * (dx2n - x2n * jnp.mean(dx2n * x2n, axis=-1, keepdims=True))
        dx2_ref[...] = dx2
        acc_ref[1:2, :] += _colsum(dx2 * y)
        dyb16 = (dx2 * gate).astype(BF16)
        dy_ref[...] = dyb16
        dycat = lax.dot_general(dyb16, wout_ref[...], (((1,), (1,)), ((), ())), preferred_element_type=F32)
        dya = dycat[:, 0:D]
        dyb = dycat[:, D:2 * D]

        du = dyb * vsv * s_bg
        dvs = dyb * u * s_bg
        dbg = dyb * u * vsv * _dsilu(b_gate, sig_bg)
        dvsb = dvs.astype(BF16)
        gbs = gbs_acc[...]
        for n in range(n_chunks):
            rows = slice(n * CHUNK, (n + 1) * CHUNK)
            gbs = gbs + dvs[rows, :]
            for h in range(HEADS):
                cols = slice(h * HEAD_DIM, (h + 1) * HEAD_DIM)
                gws_ref[h] += lax.dot_general(dvsb[rows, cols], vnb[rows, cols], (((1,), (1,)), ((), ())),
                                              preferred_element_type=F32)
                dvn[rows, cols] = jnp.dot(wst_ref[h], dvsb[rows, cols], preferred_element_type=F32)
        gbs_acc[...] = gbs

        @pl.when(i == n_steps - 1)
        def _():
            for h in range(HEADS):
                gbs_ref[:, h:h + 1] = jnp.sum(gbs_acc[:, h * HEAD_DIM:(h + 1) * HEAD_DIM], axis=1, keepdims=True)

        dvnv = dvn[...]
        acc_ref[5:6, :] += _colsum(dvnv * v_hat)
        acc_ref[6:7, :] += _colsum(dvnv)
        dv = _layer_norm_bwd(dvnv * slg_ref[...], v_hat, v_rstd)

        dsa = dya * s_gate
        dagate = dya * sa * _dsilu(a_gate, sig_ag)
        dln = dsa * _dsilu(ln_a, sig_ln)
        acc_ref[3:4, :] += _colsum(dln * ln_hat)
        acc_ref[4:5, :] += _colsum(dln)
        dcv = _layer_norm_bwd(dln * clg_ref[...], ln_hat, ln_rstd)
        acc_ref[2:3, :] += _colsum(dcv)
        dcv_ref[...] = dcv

        dzr_ref[:, 0:D] = dagate.astype(BF16)
        dzr_ref[:, D:2 * D] = du.astype(BF16)
        dzr_ref[:, 2 * D:3 * D] = dv.astype(BF16)
        dzr_ref[:, 3 * D:4 * D] = dbg.astype(BF16)

    row = lambda w: pl.BlockSpec((t, w), lambda i: (i, 0))
    const = lambda shape: pl.BlockSpec(shape, lambda i: (0,) * len(shape))
    return pl.pallas_call(
        body, name="mix_and_head", grid=(n_steps,),
        out_shape=(jax.ShapeDtypeStruct((s, D), F32),
                   jax.ShapeDtypeStruct((D_MIX, s), BF16),
                   jax.ShapeDtypeStruct((s, D), BF16),
                   jax.ShapeDtypeStruct((s, D), F32),
                   jax.ShapeDtypeStruct((s, 4 * D), BF16),
                   jax.ShapeDtypeStruct((8, D), F32),
                   jax.ShapeDtypeStruct((HEADS, CHUNK, CHUNK), F32),
                   jax.ShapeDtypeStruct((CHUNK, HEADS), F32)),
        in_specs=[row(D_IN), prev_spec, next_spec, row(D), row(D)] + [VM] * 12,
        out_specs=(row(D), pl.BlockSpec((D_MIX, t), lambda i: (0, i)), row(D), row(D), row(4 * D),
                   const((8, D)), const((HEADS, CHUNK, CHUNK)), const((CHUNK, HEADS))),
        scratch_shapes=[pltpu.VMEM((t + 2 * HALO, D), F32), pltpu.VMEM((t, D), F32), pltpu.VMEM((t, D), F32),
                        pltpu.VMEM((t, D), F32), pltpu.VMEM((t, D_MIX), BF16), pltpu.VMEM((CHUNK, D), F32),
                        pltpu.VMEM((SUB, CONV_PHASE_ROWS, CONV_BLK), F32)],
        compiler_params=_params(1),
    )(z, z, z, x, tgt, mod, conv_w_g, conv_b, cln_g, cln_b, sln_g, sln_b, final_g, ws_b, wst_b, bs_full, wout)


def _bwd_in_proj(z, dcv, dz_rest, x, dx2, mod, norm_g, conv_w_g, win_g, gout_part):
    s = x.shape[0]
    t = ROW_TILE
    n_steps = s // t
    combine_step = min(2, n_steps - 1)
    zp_spec, zn_spec = _halo_specs(t, s, 2 * D)
    dp_spec, dn_spec = _halo_specs(t, s, D)

    def body(z_ref, zp_ref, zn_ref, dcv_ref, dcvp_ref, dcvn_ref, dzr_ref, x_ref, dx2_ref, mod_ref, ng_ref, cw_ref, w_ref,
             gout_hbm,
             gx_ref, dz_ref, acc_ref, gcw_ref, gwout_ref,
             gext, dext, dg, taps, dpad, qbuf, own, l1, l2, own_sem, s1_send, s1_recv, s2_send, s2_recv):
        i = pl.program_id(0)
        red = _ChipReduce(l1, l2, s1_send, s1_recv, s2_send, s2_recv)

        def fetch_own(r):
            return pltpu.make_async_copy(gout_hbm.at[red.block(r, False)], own.at[r], own_sem.at[r])

        @pl.when(i == 0)
        def _():
            acc_ref[...] = jnp.zeros_like(acc_ref)
            gcw_ref[...] = jnp.zeros_like(gcw_ref)
            for r in (3, 1, 2, 0):
                red.to_sibling(r, gout_hbm.at[red.block(r, True)]).start()
                fetch_own(r).start()

        @pl.when(i == combine_step)
        def _():
            for r in (3, 1, 2):
                fetch_own(r).wait()
                red.combine(r, own[r].astype(F32))

        @pl.when(i == n_steps - 1)
        def _():
            fetch_own(0).wait()
            gwout_ref[...] = red.finish(red.combine(0, own[0].astype(F32)))

        not_first = i > 0
        not_last = i < n_steps - 1
        gext[0:HALO, :] = jnp.where(not_first, _glu(zp_ref), 0.0)
        gext[HALO:HALO + t, :] = _glu(z_ref)
        gext[HALO + t:2 * HALO + t, :] = jnp.where(not_last, _glu(zn_ref), 0.0)
        dext[0:HALO, :] = jnp.where(not_first, dcvp_ref[...], 0.0)
        dext[HALO:HALO + t, :] = dcv_ref[...]
        dext[HALO + t:2 * HALO + t, :] = jnp.where(not_last, dcvn_ref[...], 0.0)

        taps[...] = jnp.zeros_like(taps)
        dpad[0:SUB, :] = jnp.zeros((SUB, D), F32)
        dpad[SUB:SUB + t, :] = dcv_ref[...]
        dpad[SUB + t:2 * SUB + t, :] = jnp.zeros((SUB, D), F32)
        for blk in range(N_DEV):
            cols = slice(blk * CONV_BLK, (blk + 1) * CONV_BLK)
            dg[:, cols] = _conv_taps(dext, qbuf, cols, lambda k, blk=blk: cw_ref[blk, CONV_W - 1 - k:CONV_W - k, :], t)
            for b in range(SUB):
                dshift = dpad[pl.ds(SUB - 1 - b, t + SUB), cols]
                for a in range((CONV_W - b + SUB - 1) // SUB):
                    k = SUB * a + b
                    taps[k:k + 1, :] = _colsum(gext[SUB * a:SUB * a + t + SUB, cols] * dshift)
            gcw_ref[blk] += taps[...]

        a = z_ref[:, 0:D]
        sig = _sigmoid(z_ref[:, D:2 * D])
        dgv = dg[...]
        dz_ref[:, 0:D] = (dgv * sig).astype(BF16)
        dz_ref[:, D:2 * D] = (dgv * a * sig * (1.0 - sig)).astype(BF16)
        dz_ref[:, 2 * D:6 * D] = dzr_ref[...]

        dh = jnp.zeros((t, D), F32)
        for j in range(N_DEV):
            dh = dh + lax.dot_general(dz_ref[:, j * W_IN_BLK:(j + 1) * W_IN_BLK], w_ref[j], (((1,), (1,)), ((), ())),
                                      preferred_element_type=F32)

        xn, r, _, scale = _rms_modulate(x_ref[...], mod_ref)
        ng = ng_ref[...]
        one_scale = 1.0 + scale
        dh_xn = dh * xn
        acc_ref[0:1, :] += _colsum(dh_xn * one_scale)
        acc_ref[1:2, :] += _colsum(dh)
        acc_ref[2:3, :] += _colsum(dh_xn * ng)
        dxn = dh * (ng * one_scale)
        gx_ref[...] = dx2_ref[...] + r * (dxn - xn * jnp.mean(dxn * xn, axis=-1, keepdims=True))

    row = lambda w: pl.BlockSpec((t, w), lambda i: (i, 0))
    const = lambda shape: pl.BlockSpec(shape, lambda i: (0,) * len(shape))
    return pl.pallas_call(
        body, name="bwd_in_proj", grid=(n_steps,),
        out_shape=(jax.ShapeDtypeStruct((s, D), F32), jax.ShapeDtypeStruct((s, D_IN), BF16),
                   jax.ShapeDtypeStruct((8, D), F32), jax.ShapeDtypeStruct((N_DEV, CONV_ROWS, CONV_BLK), F32),
                   jax.ShapeDtypeStruct((W_OUT_BLK, D), F32)),
        in_specs=[pl.BlockSpec((t, 2 * D), lambda i: (i, 0)), zp_spec, zn_spec, row(D), dp_spec, dn_spec, row(4 * D),
                  row(D), row(D), VM, VM, VM, VM, pl.BlockSpec(memory_space=pl.ANY)],
        out_specs=(row(D), row(D_IN), const((8, D)), const((N_DEV, CONV_ROWS, CONV_BLK)), VM),
        scratch_shapes=[pltpu.VMEM((t + 2 * HALO, D), F32), pltpu.VMEM((t + 2 * HALO, D), F32), pltpu.VMEM((t, D), F32),
                        pltpu.VMEM((CONV_ROWS, CONV_BLK), F32), pltpu.VMEM((t + 2 * SUB, D), F32),
                        pltpu.VMEM((SUB, CONV_PHASE_ROWS, CONV_BLK), F32),
                        pltpu.VMEM((4, W_OUT_BLK, D), BF16), pltpu.VMEM((4, W_OUT_BLK, D), BF16),
                        pltpu.VMEM((3, W_OUT_BLK, D), BF16), pltpu.SemaphoreType.DMA((4,)),
                        pltpu.SemaphoreType.DMA((4,)), pltpu.SemaphoreType.DMA((4,)),
                        pltpu.SemaphoreType.DMA((3,)), pltpu.SemaphoreType.DMA((3,))],
        compiler_params=_params(1),
    )(z, z, z, dcv, dcv, dcv, dz_rest, x, dx2, mod, norm_g, conv_w_g, win_g, gout_part)


WGRAD_K = 1024
WGRAD_ROLES = [(3, True), (3, False), (1, True), (1, False), (2, True), (2, False), (0, True), (0, False)]


def _wgrad_schedule():
    x, y, c = _mesh_pos()
    chip = [2 * x + y, 2 * x + (1 - y), 2 * (1 - x) + y, 2 * (1 - x) + (1 - y)]
    blocks = [2 * chip[r] + ((1 - c) if sibling else c) for r, sibling in WGRAD_ROLES]
    return jnp.stack([jnp.asarray(b, jnp.int32) for b in blocks])


def _sum_blocks(gathered):
    total = gathered[0]
    for b in range(1, N_DEV):
        total = total + gathered[b]
    return total


def _wgrad_out(ycatt, dy, rows, gws):
    s = dy.shape[0]

    def body(yt_ref, dy_ref, rows_ref, gws_ref, g_ref, orows_ref, ogws_ref, oloss_ref,
             rows_g, gws_g, ag_send, ag_recv):
        j = pl.program_id(0)
        ag = _AllGather([rows_g, gws_g], ag_send, ag_recv)

        @pl.when(j == 0)
        def _():
            me = _my_block()
            rows_g[me] = rows_ref[...]
            gws_g[me] = gws_ref[...]
            for a in range(2):
                ag.send_own(a)

        g_ref[0] = jnp.dot(yt_ref[...], dy_ref[...], preferred_element_type=F32).astype(BF16)

        @pl.when(j == 3)
        def _():
            for jj in range(3):
                for a in range(2):
                    ag.arrived(a, 1 + jj)
                    ag.pass_on(a, jj)

        @pl.when(j == N_DEV - 1)
        def _():
            for k in (0, 4, 5, 6):
                for a in range(2):
                    ag.arrived(a, k)
            for a in range(2):
                ag.sent(a)
            tot_rows = _sum_blocks(rows_g)
            orows_ref[...] = tot_rows
            ogws_ref[...] = _sum_blocks(gws_g)
            oloss_ref[...] = jnp.full(oloss_ref.shape, (0.5 / D) * jnp.sum(tot_rows[7:8, :]), F32)

    return pl.pallas_call(
        body, name="wgrad_out", grid=(N_DEV,),
        out_shape=(jax.ShapeDtypeStruct((N_DEV, W_OUT_BLK, D), BF16), jax.ShapeDtypeStruct(rows.shape, F32),
                   jax.ShapeDtypeStruct(gws.shape, F32), jax.ShapeDtypeStruct((8, 128), F32)),
        in_specs=[pl.BlockSpec((W_OUT_BLK, s), lambda j: (j, 0)), VM, VM, VM],
        out_specs=(pl.BlockSpec((1, W_OUT_BLK, D), lambda j: (j, 0, 0)), VM, VM, VM),
        scratch_shapes=[pltpu.VMEM((N_DEV,) + rows.shape, F32), pltpu.VMEM((N_DEV,) + gws.shape, F32),
                        pltpu.SemaphoreType.DMA((14,)), pltpu.SemaphoreType.DMA((14,))],
        compiler_params=_params(1),
    )(ycatt, dy, rows, gws)


def _wgrad_reduce(ht, dz, small, dmod):
    s = dz.shape[0]
    n_kc = s // WGRAD_K
    n_steps = len(WGRAD_ROLES)
    blk = (D, W_IN_BLK)

    def body(in_blk, ht_ref, dz_ref, small_ref, dmod_ref,
             oin_ref, osmall_ref, odmod_ref, obada_ref,
             acc, p1, l1, l2, small_g, dmod_g, s1_send, s1_recv, s2_send, s2_recv, ag_send, ag_recv):
        step, kc = pl.program_id(0), pl.program_id(1)
        red = _ChipReduce(l1, l2, s1_send, s1_recv, s2_send, s2_recv)
        ag = _AllGather([dmod_g, small_g], ag_send, ag_recv)
        last_kc = kc == n_kc - 1

        @pl.when((step == 0) & (kc == 0))
        def _():
            me = _my_block()
            small_g[me] = small_ref[...]
            dmod_g[me] = dmod_ref[...]
            for a in range(2):
                ag.send_own(a)

        @pl.when((step == 2) & (kc == 0))
        def _():
            for j in range(3):
                for a in range(2):
                    ag.arrived(a, 1 + j)
                    ag.pass_on(a, j)

        prod = jnp.dot(ht_ref[...], dz_ref[...], preferred_element_type=F32)

        @pl.when(kc == 0)
        def _():
            acc[...] = prod

        @pl.when(kc != 0)
        def _():
            acc[...] += prod

        for k, (r, sibling) in enumerate(WGRAD_ROLES):
            @pl.when((step == k) & last_kc)
            def _(r=r, sibling=sibling):
                if sibling:
                    p1[r] = acc[...].astype(BF16)
                    red.to_sibling(r, p1.at[r]).start()
                else:
                    chip_sum = red.combine(r, acc[...])
                    if r == 0:
                        oin_ref[...] = red.finish(chip_sum)

        @pl.when((step == n_steps - 1) & last_kc)
        def _():
            for k in (0, 4, 5, 6):
                for a in range(2):
                    ag.arrived(a, k)
            for a in range(2):
                ag.sent(a)
            osmall_ref[...] = _sum_blocks(small_g)
            obada_ref[...] = _sum_blocks(dmod_g)
            for b in range(N_DEV):
                odmod_ref[b:b + 1, :] = dmod_g[b, 0:1, :]

    grid_spec = pltpu.PrefetchScalarGridSpec(
        num_scalar_prefetch=1, grid=(n_steps, n_kc),
        in_specs=[pl.BlockSpec((D, WGRAD_K), lambda i, kc, ib: (0, kc)),
                  pl.BlockSpec((WGRAD_K, W_IN_BLK), lambda i, kc, ib: (kc, ib[i])),
                  VM, VM],
        out_specs=(VM,) * 4,
        scratch_shapes=[pltpu.VMEM(blk, F32), pltpu.VMEM((4,) + blk, BF16), pltpu.VMEM((4,) + blk, BF16),
                        pltpu.VMEM((3,) + blk, BF16),
                        pltpu.VMEM((N_DEV,) + small.shape, F32), pltpu.VMEM((N_DEV,) + dmod.shape, F32),
                        pltpu.SemaphoreType.DMA((4,)), pltpu.SemaphoreType.DMA((4,)),
                        pltpu.SemaphoreType.DMA((3,)), pltpu.SemaphoreType.DMA((3,)),
                        pltpu.SemaphoreType.DMA((14,)), pltpu.SemaphoreType.DMA((14,))])
    return pl.pallas_call(
        body, name="wgrad_reduce", grid_spec=grid_spec,
        out_shape=(jax.ShapeDtypeStruct(blk, F32), jax.ShapeDtypeStruct(small.shape, F32),
                   jax.ShapeDtypeStruct((N_DEV, 3 * D), F32), jax.ShapeDtypeStruct((8, 3 * D), F32)),
        compiler_params=_params(2),
    )(_wgrad_schedule(), ht, dz, small, dmod)


def _wgrad_ada(c_all, dmod_cols):
    def body(c_ref, dm_ref, g_ref, act):
        for b in range(N_DEV):
            cb = c_ref[b, 0:1, :]
            act[b:b + 1, :] = cb * _sigmoid(cb)
        g_ref[...] = lax.dot_general(act[...], dm_ref[...], (((0,), (0,)), ((), ())), preferred_element_type=F32,
                                     precision=lax.Precision.HIGHEST)

    return pl.pallas_call(
        body, name="wgrad_ada", out_shape=jax.ShapeDtypeStruct((D, W_ADA_BLK), F32),
        in_specs=[VM, VM], out_specs=VM, scratch_shapes=[pltpu.VMEM((N_DEV, D), F32)],
        compiler_params=_params(),
    )(c_all, dmod_cols)


def _adamw_math(w, g, m, v):
    m = ADAM_B1 * m + (1.0 - ADAM_B1) * g
    v = ADAM_B2 * v + (1.0 - ADAM_B2) * (g * g)
    m_hat = m / (1.0 - ADAM_B1 ** ADAM_STEP)
    v_hat = v / (1.0 - ADAM_B2 ** ADAM_STEP)
    delta = -ADAM_LR * (m_hat / (jnp.sqrt(v_hat) + ADAM_EPS) + ADAM_WD * w)
    return delta, m, v


def _adamw(name, w, g, m, v, row_block=None):
    shape = w.shape

    def body(w_ref, g_ref, m_ref, v_ref, d_ref, nm_ref, nv_ref):
        d_ref[...], nm_ref[...], nv_ref[...] = _adamw_math(w_ref[...], g_ref[...], m_ref[...], v_ref[...])

    out_shape = (jax.ShapeDtypeStruct(shape, F32),) * 3
    if row_block is None:
        return pl.pallas_call(body, name=name, out_shape=out_shape, in_specs=[VM] * 4, out_specs=(VM,) * 3,
                              compiler_params=_params())(w, g, m, v)
    spec = pl.BlockSpec((row_block, shape[1]), lambda i: (i, 0))
    return pl.pallas_call(body, name=name, grid=(shape[0] // row_block,), out_shape=out_shape,
                          in_specs=[spec] * 4, out_specs=(spec,) * 3, compiler_params=_params(1))(w, g, m, v)


def kernel(x, c, w_ada, b_ada, norm_g, w_in, conv_w, conv_b, conv_ln_g, conv_ln_b, sg_ln_g, sg_ln_b, w_s, b_s, w_out, final_g, loss_target, m_w_ada, m_b_ada, m_norm_g, m_w_in, m_conv_w, m_conv_b, m_conv_ln_g, m_conv_ln_b, m_sg_ln_g, m_sg_ln_b, m_w_s, m_b_s, m_w_out, m_final_g, v_w_ada, v_b_ada, v_norm_g, v_w_in, v_conv_w, v_conv_b, v_conv_ln_g, v_conv_ln_b, v_sg_ln_g, v_sg_ln_b, v_w_s, v_b_s, v_w_out, v_final_g):
    me = 4 * lax.axis_index("x") + 2 * lax.axis_index("y") + lax.axis_index("c")
    x2d, tgt2d = x[0], loss_target[0]
    row1 = lambda a: a.reshape(1, D)
    taps = lambda a: jnp.pad(a.reshape(CONV_W, CONV_BLK), ((0, CONV_ROWS - CONV_W), (0, 0)))

    z, ht, win_g, wout_g, cw_g, c_all, mod = _fwd_in_proj(
        x2d, jnp.broadcast_to(c, (8, D)), w_ada[0], b_ada, norm_g, w_in[0], w_out[0], taps(conv_w))
    ws_b = w_s[0].astype(BF16)
    wst_b = jnp.swapaxes(w_s[0], 1, 2).astype(BF16)
    bs_full = jnp.repeat(b_s[0].T, HEAD_DIM, axis=1)

    dx2, ycatt, dy, dcv, dz_rest, acc_a, gws, gbs = _mix_and_head(
        z, x2d, tgt2d, mod, cw_g, conv_b, conv_ln_g, conv_ln_b, sg_ln_g, sg_ln_b, row1(final_g), ws_b, wst_b, bs_full,
        wout_g.reshape(D_MIX, D))
    rows_a = jnp.concatenate([acc_a[0:1], gbs.T.reshape(1, D), acc_a[2:8]], axis=0)
    gout_part, rows_sum, gws_sum, loss_tile = _wgrad_out(ycatt, dy, rows_a, gws.reshape(HEADS * CHUNK, CHUNK))
    loss = loss_tile[0, 0]
    grad_x, dz, acc_b, gcw, g_w_out = _bwd_in_proj(z, dcv, dz_rest, x2d, dx2, mod, norm_g, cw_g, win_g, gout_part)

    small = jnp.concatenate([acc_b[0:1], jnp.zeros((7, D), F32), jnp.transpose(gcw, (1, 0, 2)).reshape(CONV_ROWS, D)],
                            axis=0)
    dmod_row = jnp.concatenate([acc_b[1:2], acc_b[2:3], acc_a[1:2]], axis=1)
    g_w_in, small_sum, dmod_all, bada8 = _wgrad_reduce(ht, dz, small, jnp.broadcast_to(dmod_row, (8, 3 * D)))

    g_w_ada = _wgrad_ada(c_all, lax.dynamic_slice(dmod_all, (0, me * W_ADA_BLK), (N_DEV, W_ADA_BLK)))
    g_b_ada = bada8[0:1]
    g_vec = jnp.concatenate([small_sum[0:1], rows_sum[2:7], rows_sum[0:2]], axis=0)
    g_conv_w = lax.dynamic_slice(small_sum, (8, me * CONV_BLK), (CONV_W, CONV_BLK))

    vec = lambda *a: jnp.concatenate([row1(t) for t in a], axis=0)
    d_vec, m_vec, v_vec = _adamw(
        "adamw_vectors",
        vec(norm_g, conv_b, conv_ln_g, conv_ln_b, sg_ln_g, sg_ln_b, final_g, b_s), g_vec,
        vec(m_norm_g, m_conv_b, m_conv_ln_g, m_conv_ln_b, m_sg_ln_g, m_sg_ln_b, m_final_g, m_b_s),
        vec(v_norm_g, v_conv_b, v_conv_ln_g, v_conv_ln_b, v_sg_ln_g, v_sg_ln_b, v_final_g, v_b_s))
    flat_ws = lambda a: a.reshape(HEADS * CHUNK, CHUNK)
    upd = {
        "w_ada": _adamw("adamw_w_ada", w_ada[0], g_w_ada, m_w_ada[0], v_w_ada[0], 256),
        "b_ada": _adamw("adamw_b_ada", b_ada, g_b_ada, m_b_ada, v_b_ada),
        "w_in": _adamw("adamw_w_in", w_in[0], g_w_in, m_w_in[0], v_w_in[0], 256),
        "conv_w": _adamw("adamw_conv_w", conv_w.reshape(CONV_W, CONV_BLK), g_conv_w, m_conv_w.reshape(CONV_W, CONV_BLK),
                         v_conv_w.reshape(CONV_W, CONV_BLK)),
        "w_s": _adamw("adamw_w_s", flat_ws(w_s), gws_sum, flat_ws(m_w_s), flat_ws(v_w_s)),
        "w_out": _adamw("adamw_w_out", w_out[0], g_w_out, m_w_out[0], v_w_out[0], 128),
    }
    grads = {"w_ada": g_w_ada, "b_ada": g_b_ada, "w_in": g_w_in, "conv_w": g_conv_w, "w_s": gws_sum, "w_out": g_w_out}
    vec_names = ["norm_g", "conv_b", "conv_ln_g", "conv_ln_b", "sg_ln_g", "sg_ln_b", "final_g", "b_s"]
    shapes = {"w_ada": w_ada.shape, "b_ada": b_ada.shape, "norm_g": norm_g.shape, "w_in": w_in.shape,
              "conv_w": conv_w.shape, "conv_b": conv_b.shape, "conv_ln_g": conv_ln_g.shape, "conv_ln_b": conv_ln_b.shape,
              "sg_ln_g": sg_ln_g.shape, "sg_ln_b": sg_ln_b.shape, "w_s": w_s.shape, "b_s": b_s.shape,
              "w_out": w_out.shape, "final_g": final_g.shape}
    order = ["w_ada", "b_ada", "norm_g", "w_in", "conv_w", "conv_b", "conv_ln_g", "conv_ln_b", "sg_ln_g", "sg_ln_b",
             "w_s", "b_s", "w_out", "final_g"]

    def leaf(kind, name):
        if name in vec_names:
            src = (g_vec, d_vec, m_vec, v_vec)[kind]
            val = src[vec_names.index(name)]
        elif kind == 0:
            val = grads[name]
        else:
            val = upd[name][kind - 1]
        return val.reshape(shapes[name])

    outs = [loss, grad_x.reshape(x.shape)]
    for kind in range(4):
        outs += [leaf(kind, n) for n in order]
    return tuple(outs)
```

```python
import functools

import jax
import jax.numpy as jnp
from jax import lax
from jax.experimental import pallas as pl
from jax.experimental.pallas import tpu as pltpu

F32 = jnp.float32
BF16 = jnp.bfloat16
MESH = pl.DeviceIdType.MESH

D = 1024
D_IN = 6 * D
D_MIX = 2 * D
N_DEV = 8
W_IN_BLK = D_IN // N_DEV
W_OUT_BLK = D_MIX // N_DEV
W_ADA_BLK = 3 * D // N_DEV
CONV_BLK = D // N_DEV
CONV_W = 31
CONV_HALF = CONV_W // 2
CONV_ROWS = 32
HALO = 16
CHUNK = 128
HEADS = 8
HEAD_DIM = 128
EPS = 1e-6
ROW_TILE = 256
VMEM_LIMIT = 56 * 1024 * 1024

ADAM_LR = 0.001
ADAM_B1 = 0.9
ADAM_B2 = 0.999
ADAM_EPS = 1e-08
ADAM_WD = 0.01
ADAM_STEP = 10

VM = pl.BlockSpec(memory_space=pltpu.VMEM)


def _params(grid_rank=0, **kw):
    sem = ("arbitrary",) * grid_rank if grid_rank else None
    return pltpu.CompilerParams(dimension_semantics=sem, vmem_limit_bytes=VMEM_LIMIT, **kw)


def _sigmoid(t):
    return jax.nn.sigmoid(t)


def _dsilu(t, sig):
    return sig * (1.0 + t * (1.0 - sig))


def _mesh_pos():
    return lax.axis_index("x"), lax.axis_index("y"), lax.axis_index("c")


class _AllGather:
    def __init__(self, bufs, send_sems, recv_sems):
        x, y, c = _mesh_pos()
        self.bufs, self.send_sems, self.recv_sems = bufs, send_sems, recv_sems
        self.me, self.sib = (x, y, c), (x, y, 1 - c)
        self.chips = [(x, 1 - y), (1 - x, y), (1 - x, 1 - y)]
        self.c = c

    def _copy(self, a, k, block, to):
        px, py, pc = block
        ref = self.bufs[a].at[4 * px + 2 * py + pc]
        return pltpu.make_async_remote_copy(
            src_ref=ref, dst_ref=ref, send_sem=self.send_sems.at[7 * a + k], recv_sem=self.recv_sems.at[7 * a + k],
            device_id=to, device_id_type=MESH)

    def _outgoing(self, a, k):
        if k == 0:
            return self._copy(a, 0, self.me, self.sib)
        if k <= 3:
            return self._copy(a, k, self.me, (*self.chips[k - 1], self.c))
        return self._copy(a, k, (*self.chips[k - 4], self.c), self.sib)

    def source(self, k):
        if k == 0:
            return self.sib
        return (*self.chips[(k - 1) % 3], self.c if k <= 3 else 1 - self.c)

    def block_index(self, k):
        px, py, pc = self.source(k)
        return 4 * px + 2 * py + pc

    def send_own(self, a):
        for k in range(4):
            self._outgoing(a, k).start()

    def arrived(self, a, k):
        self._copy(a, k, self.source(k), self.me).wait_recv()

    def pass_on(self, a, j):
        self._outgoing(a, 4 + j).start()

    def sent(self, a):
        for k in range(7):
            self._outgoing(a, k).wait_send()

    def run(self):
        n = range(len(self.bufs))
        for a in n:
            self.send_own(a)
        for j in range(3):
            for a in n:
                self.arrived(a, 1 + j)
                self.pass_on(a, j)
        for k in (0, 4, 5, 6):
            for a in n:
                self.arrived(a, k)
        for a in n:
            self.sent(a)


class _ChipReduce:
    def __init__(self, l1, l2, s1_send, s1_recv, s2_send, s2_recv):
        x, y, c = _mesh_pos()
        self.l1, self.l2 = l1, l2
        self.s1_send, self.s1_recv, self.s2_send, self.s2_recv = s1_send, s1_recv, s2_send, s2_recv
        self.c, self.sib = c, (x, y, 1 - c)
        self.chip_of = [(x, y), (x, 1 - y), (1 - x, y), (1 - x, 1 - y)]
        self.chip = [2 * x + y, 2 * x + (1 - y), 2 * (1 - x) + y, 2 * (1 - x) + (1 - y)]

    def block(self, r, sibling):
        return 2 * self.chip[r] + ((1 - self.c) if sibling else self.c)

    def to_sibling(self, r, src=None):
        return pltpu.make_async_remote_copy(
            src_ref=self.l1.at[r] if src is None else src, dst_ref=self.l1.at[r], send_sem=self.s1_send.at[r],
            recv_sem=self.s1_recv.at[r], device_id=self.sib, device_id_type=MESH)

    def to_chip(self, r):
        return pltpu.make_async_remote_copy(
            src_ref=self.l1.at[r], dst_ref=self.l2.at[r - 1], send_sem=self.s2_send.at[r - 1],
            recv_sem=self.s2_recv.at[r - 1], device_id=(*self.chip_of[r], self.c), device_id_type=MESH)

    def combine(self, r, mine):
        self.to_sibling(r).wait_recv()
        both = mine + self.l1[r].astype(F32)
        if r == 0:
            return both
        self.l1[r] = both.astype(BF16)
        self.to_chip(r).start()
        return None

    def finish(self, own_chip_sum):
        total = own_chip_sum
        for r in (1, 2, 3):
            self.to_chip(r).wait_recv()
            total = total + self.l2[r - 1].astype(F32)
        for r in range(4):
            self.to_sibling(r).wait_send()
        for r in (1, 2, 3):
            self.to_chip(r).wait_send()
        return total


def _my_block():
    x, y, c = _mesh_pos()
    return 4 * x + 2 * y + c


def _modulation(c_g, w_ref, b_ref, mod_ref, part, land, send_sems, recv_sems):
    x, y, c = _mesh_pos()
    me = 4 * x + 2 * y + c
    w = w_ref[...]
    for b in range(N_DEV):
        cb = c_g[b]
        part[b] = jnp.dot(cb * _sigmoid(cb), w, preferred_element_type=F32, precision=lax.Precision.HIGHEST)
    land[me] = part[me]

    def copy(b):
        return pltpu.make_async_remote_copy(
            src_ref=part.at[b], dst_ref=land.at[me], send_sem=send_sems.at[b], recv_sem=recv_sems.at[me],
            device_id=(b // 4, (b // 2) % 2, b % 2), device_id_type=MESH)

    def arrival(b):
        return pltpu.make_async_remote_copy(
            src_ref=part.at[b], dst_ref=land.at[b], send_sem=send_sems.at[b], recv_sem=recv_sems.at[b],
            device_id=(b // 4, (b // 2) % 2, b % 2), device_id_type=MESH)

    for b in range(N_DEV):
        @pl.when(b != me)
        def _():
            copy(b).start()
    for b in range(N_DEV):
        @pl.when(b != me)
        def _():
            arrival(b).wait_recv()
            copy(b).wait_send()
    for b in range(N_DEV):
        cols = slice(b * W_ADA_BLK, (b + 1) * W_ADA_BLK)
        mod_ref[:, cols] = land[b] + b_ref[:, cols]


def _rms_modulate(x, mod_ref):
    shift = mod_ref[0:1, 0:D]
    scale = mod_ref[0:1, D:2 * D]
    r = lax.rsqrt(jnp.mean(x * x, axis=-1, keepdims=True) + EPS)
    xn = x * r
    return xn, r, shift, scale


FWD_TILE = 512
FWD_Z_TILE = 1024


def _fwd_in_proj(x, c_rep, w_ada, b_ada, norm_g, w_in, w_out, conv_w):
    s = x.shape[0]
    t, tz = FWD_TILE, FWD_Z_TILE
    n_tiles = s // t
    C_PAY, WIN_PAY, WOUT_PAY, CW_PAY = 0, 1, 2, 3

    def body(x_hbm, c_ref, wada_ref, bada_ref, ng_ref, win_ref, wout_ref, cw_ref,
             z_hbm, ht_ref, win_g, wout_g, cw_g, c_g, mod_ref,
             h, xbuf, zbuf, part, land, ag_send, ag_recv, mod_send, mod_recv, x_sem, z_sem):
        me = _my_block()
        c_g[me] = c_ref[...]
        ag = _AllGather([c_g, win_g, wout_g, cw_g], ag_send, ag_recv)
        ag.send_own(C_PAY)

        def x_copy(i):
            return pltpu.make_async_copy(x_hbm.at[pl.ds(i * t, t), :], xbuf.at[i % 2], x_sem.at[i % 2])

        x_copy(0).start()
        win_g[me] = win_ref[...].astype(BF16)
        wout_g[me] = wout_ref[...].astype(BF16)
        cw_g[me] = cw_ref[...]

        for j in range(3):
            ag.arrived(C_PAY, 1 + j)
            ag.pass_on(C_PAY, j)
        for k in (0, 4, 5, 6):
            ag.arrived(C_PAY, k)
        _modulation(c_g, wada_ref, bada_ref, mod_ref, part, land, mod_send, mod_recv)
        for a in (WIN_PAY, WOUT_PAY, CW_PAY):
            ag.send_own(a)

        for i in range(n_tiles):
            x_copy(i).wait()
            if i + 1 < n_tiles:
                x_copy(i + 1).start()
            xn, _, shift, scale = _rms_modulate(xbuf[i % 2], mod_ref)
            hh = xn * ng_ref[...] * (1.0 + scale) + shift
            h[i * t:(i + 1) * t, :] = hh.astype(BF16)
            ht_ref[:, i * t:(i + 1) * t] = hh.T.astype(BF16)

        def z_copy(slot, row0, col0):
            return pltpu.make_async_copy(zbuf.at[slot], z_hbm.at[pl.ds(row0, tz), pl.ds(col0, W_IN_BLK)], z_sem.at[slot])

        done = [0]

        def z_block(blk):
            col0 = pl.multiple_of(blk * W_IN_BLK, 128)
            for i in range(s // tz):
                slot = done[0] % 2
                if done[0] >= 2:
                    z_copy(slot, 0, 0).wait()
                zbuf[slot] = jnp.dot(h[i * tz:(i + 1) * tz, :], win_g[blk], preferred_element_type=F32)
                z_copy(slot, i * tz, col0).start()
                done[0] += 1

        z_block(me)
        ag.arrived(WIN_PAY, 0)
        z_block(ag.block_index(0))
        for j in (0, 1):
            ag.arrived(WIN_PAY, 1 + j)
            ag.pass_on(WIN_PAY, j)
        for j in (0, 1):
            z_block(ag.block_index(1 + j))
        ag.arrived(WIN_PAY, 4)
        z_block(ag.block_index(4))
        ag.arrived(WIN_PAY, 3)
        ag.pass_on(WIN_PAY, 2)
        ag.arrived(WIN_PAY, 5)
        z_block(ag.block_index(5))
        z_block(ag.block_index(3))
        ag.arrived(WIN_PAY, 6)
        z_block(ag.block_index(6))

        for a in (WOUT_PAY, CW_PAY):
            for j in range(3):
                ag.arrived(a, 1 + j)
                ag.pass_on(a, j)
        for a in (WOUT_PAY, CW_PAY):
            for k in (0, 4, 5, 6):
                ag.arrived(a, k)
        for a in (C_PAY, WIN_PAY, WOUT_PAY, CW_PAY):
            ag.sent(a)
        z_copy(0, 0, 0).wait()
        z_copy(1, 0, 0).wait()

    any_spec = pl.BlockSpec(memory_space=pl.ANY)
    return pl.pallas_call(
        body, name="fwd_in_proj",
        out_shape=(jax.ShapeDtypeStruct((s, D_IN), F32), jax.ShapeDtypeStruct((D, s), BF16),
                   jax.ShapeDtypeStruct((N_DEV,) + w_in.shape, BF16), jax.ShapeDtypeStruct((N_DEV,) + w_out.shape, BF16),
                   jax.ShapeDtypeStruct((N_DEV,) + conv_w.shape, F32), jax.ShapeDtypeStruct((N_DEV,) + c_rep.shape, F32),
                   jax.ShapeDtypeStruct((8, 3 * D), F32)),
        in_specs=[any_spec] + [VM] * 7, out_specs=(any_spec,) + (VM,) * 6,
        scratch_shapes=[pltpu.VMEM((s, D), BF16), pltpu.VMEM((2, t, D), F32), pltpu.VMEM((2, tz, W_IN_BLK), F32),
                        pltpu.VMEM((N_DEV, 8, W_ADA_BLK), F32), pltpu.VMEM((N_DEV, 8, W_ADA_BLK), F32),
                        pltpu.SemaphoreType.DMA((28,)), pltpu.SemaphoreType.DMA((28,)),
                        pltpu.SemaphoreType.DMA((N_DEV,)), pltpu.SemaphoreType.DMA((N_DEV,)),
                        pltpu.SemaphoreType.DMA((2,)), pltpu.SemaphoreType.DMA((2,))],
        compiler_params=_params(),
    )(x, c_rep, w_ada, b_ada, norm_g, w_in, w_out, conv_w)


def _halo_specs(t, s, width):
    per = t // HALO
    last = s // HALO - 1
    prev = pl.BlockSpec((HALO, width), lambda i: (jnp.maximum(i * per - 1, 0), 0))
    nxt = pl.BlockSpec((HALO, width), lambda i: (jnp.minimum((i + 1) * per, last), 0))
    return prev, nxt


def _glu(ref):
    return ref[:, 0:D] * _sigmoid(ref[:, D:2 * D])


SUB = 8
CONV_PHASE_ROWS = ROW_TILE + SUB


def _conv_taps(ext, qbuf, cols, tap, t):
    out = None
    for b in range(SUB):
        q = None
        for a in range((CONV_W - b + SUB - 1) // SUB):
            term = ext[SUB * a:SUB * a + t + SUB, cols] * tap(SUB * a + b)
            q = term if q is None else q + term
        qbuf[b] = q
        shifted = qbuf[b, pl.ds(b + 1, t), :]
        out = shifted if out is None else out + shifted
    return out


def _layer_norm_stats(v):
    mu = jnp.mean(v, axis=-1, keepdims=True)
    cen = v - mu
    rstd = lax.rsqrt(jnp.mean(cen * cen, axis=-1, keepdims=True) + EPS)
    return cen * rstd, rstd


def _layer_norm_bwd(dy_hat, hat, rstd):
    m1 = jnp.mean(dy_hat, axis=-1, keepdims=True)
    m2 = jnp.mean(dy_hat * hat, axis=-1, keepdims=True)
    return rstd * (dy_hat - m1 - hat * m2)


def _colsum(v):
    return jnp.sum(v, axis=0, keepdims=True)


def _mix_and_head(z, x, tgt, mod, conv_w_g, conv_b, cln_g, cln_b, sln_g, sln_b, final_g, ws_b, wst_b, bs_full, wout):
    s = x.shape[0]
    t = ROW_TILE
    n_chunks = t // CHUNK
    n_steps = s // t
    prev_spec, next_spec = _halo_specs(t, s, 2 * D)

    def body(z_ref, zp_ref, zn_ref, x_ref, tgt_ref, mod_ref, cw_ref, cb_ref, clg_ref, clb_ref, slg_ref, slb_ref, fg_ref,
             ws_ref, wst_ref, bs_ref, wout_ref,
             dx2_ref, ycatt_ref, dy_ref, dcv_ref, dzr_ref, acc_ref, gws_ref, gbs_ref,
             gext, cv, vs, dvn, ycat, gbs_acc, qbuf):
        i = pl.program_id(0)

        @pl.when(i == 0)
        def _():
            acc_ref[...] = jnp.zeros_like(acc_ref)
            gws_ref[...] = jnp.zeros_like(gws_ref)
            gbs_acc[...] = jnp.zeros_like(gbs_acc)

        gext[0:HALO, :] = jnp.where(i > 0, _glu(zp_ref), 0.0)
        gext[HALO:HALO + t, :] = _glu(z_ref)
        gext[HALO + t:2 * HALO + t, :] = jnp.where(i < n_steps - 1, _glu(zn_ref), 0.0)
        for blk in range(N_DEV):
            cols = slice(blk * CONV_BLK, (blk + 1) * CONV_BLK)
            cv[:, cols] = _conv_taps(gext, qbuf, cols, lambda k, blk=blk: cw_ref[blk, k:k + 1, :], t) + cb_ref[:, cols]
        ln_hat, ln_rstd = _layer_norm_stats(cv[...])
        ln_a = ln_hat * clg_ref[...] + clb_ref[...]
        sig_ln = _sigmoid(ln_a)
        sa = ln_a * sig_ln
        a_gate = z_ref[:, 2 * D:3 * D]
        sig_ag = _sigmoid(a_gate)
        s_gate = a_gate * sig_ag
        ya = sa * s_gate

        v_hat, v_rstd = _layer_norm_stats(z_ref[:, 4 * D:5 * D])
        vn = v_hat * slg_ref[...] + slb_ref[...]
        vnb = vn.astype(BF16)
        for n in range(n_chunks):
            rows = slice(n * CHUNK, (n + 1) * CHUNK)
            for h in range(HEADS):
                cols = slice(h * HEAD_DIM, (h + 1) * HEAD_DIM)
                vs[rows, cols] = jnp.dot(ws_ref[h], vnb[rows, cols], preferred_element_type=F32) + bs_ref[:, cols]
        u = z_ref[:, 3 * D:4 * D]
        b_gate = z_ref[:, 5 * D:6 * D]
        sig_bg = _sigmoid(b_gate)
        s_bg = b_gate * sig_bg
        vsv = vs[...]
        yb = u * vsv * s_bg

        ycat[:, 0:D] = ya.astype(BF16)
        ycat[:, D:2 * D] = yb.astype(BF16)
        ycatt_ref[0:D, :] = ya.T.astype(BF16)
        ycatt_ref[D:2 * D, :] = yb.T.astype(BF16)
        y = jnp.dot(ycat[...], wout_ref[...], preferred_element_type=F32)
        gate = mod_ref[0:1, 2 * D:3 * D]
        x2 = x_ref[...] + gate * y
        r2 = lax.rsqrt(jnp.mean(x2 * x2, axis=-1, keepdims=True) + EPS)
        x2n = x2 * r2
        fg = fg_ref[...]
        diff = x2n * fg - tgt_ref[...]
        acc_ref[7:8, :] += _colsum(diff * diff)
        dout = diff * (1.0 / D)
        acc_ref[0:1, :] += _colsum(dout * x2n)
        dx2n = dout * fg
        dx2 = r2 * (dx2n - x2n * jnp.mean(dx2n * x2n, axis=-1, keepdims=True))
        dx2_ref[...] = dx2
        acc_ref[1:2, :] += _colsum(dx2 * y)
        dyb16 = (dx2 * gate).astype(BF16)
        dy_ref[...] = dyb16
        dycat = lax.dot_general(dyb16, wout_ref[...], (((1,), (1,)), ((), ())), preferred_element_type=F32)
        dya = dycat[:, 0:D]
        dyb = dycat[:, D:2 * D]

        du = dyb * vsv * s_bg
        dvs = dyb * u * s_bg
        dbg = dyb * u * vsv * _dsilu(b_gate, sig_bg)
        dvsb = dvs.astype(BF16)
        gbs = gbs_acc[...]
        for n in range(n_chunks):
            rows = slice(n * CHUNK, (n + 1) * CHUNK)
            gbs = gbs + dvs[rows, :]
            for h in range(HEADS):
                cols = slice(h * HEAD_DIM, (h + 1) * HEAD_DIM)
                gws_ref[h] += lax.dot_general(dvsb[rows, cols], vnb[rows, cols], (((1,), (1,)), ((), ())),
                                              preferred_element_type=F32)
                dvn[rows, cols] = jnp.dot(wst_ref[h], dvsb[rows, cols], preferred_element_type=F32)
        gbs_acc[...] = gbs

        @pl.when(i == n_steps - 1)
        def _():
            for h in range(HEADS):
                gbs_ref[:, h:h + 1] = jnp.sum(gbs_acc[:, h * HEAD_DIM:(h + 1) * HEAD_DIM], axis=1, keepdims=True)

        dvnv = dvn[...]
        acc_ref[5:6, :] += _colsum(dvnv * v_hat)
        acc_ref[6:7, :] += _colsum(dvnv)
        dv = _layer_norm_bwd(dvnv * slg_ref[...], v_hat, v_rstd)

        dsa = dya * s_gate
        dagate = dya * sa * _dsilu(a_gate, sig_ag)
        dln = dsa * _dsilu(ln_a, sig_ln)
        acc_ref[3:4, :] += _colsum(dln * ln_hat)
        acc_ref[4:5, :] += _colsum(dln)
        dcv = _layer_norm_bwd(dln * clg_ref[...], ln_hat, ln_rstd)
        acc_ref[2:3, :] += _colsum(dcv)
        dcv_ref[...] = dcv

        dzr_ref[:, 0:D] = dagate.astype(BF16)
        dzr_ref[:, D:2 * D] = du.astype(BF16)
        dzr_ref[:, 2 * D:3 * D] = dv.astype(BF16)
        dzr_ref[:, 3 * D:4 * D] = dbg.astype(BF16)

    row = lambda w: pl.BlockSpec((t, w), lambda i: (i, 0))
    const = lambda shape: pl.BlockSpec(shape, lambda i: (0,) * len(shape))
    return pl.pallas_call(
        body, name="mix_and_head", grid=(n_steps,),
        out_shape=(jax.ShapeDtypeStruct((s, D), F32),
                   jax.ShapeDtypeStruct((D_MIX, s), BF16),
                   jax.ShapeDtypeStruct((s, D), BF16),
                   jax.ShapeDtypeStruct((s, D), F32),
                   jax.ShapeDtypeStruct((s, 4 * D), BF16),
                   jax.ShapeDtypeStruct((8, D), F32),
                   jax.ShapeDtypeStruct((HEADS, CHUNK, CHUNK), F32),
                   jax.ShapeDtypeStruct((CHUNK, HEADS), F32)),
        in_specs=[row(D_IN), prev_spec, next_spec, row(D), row(D)] + [VM] * 12,
        out_specs=(row(D), pl.BlockSpec((D_MIX, t), lambda i: (0, i)), row(D), row(D), row(4 * D),
                   const((8, D)), const((HEADS, CHUNK, CHUNK)), const((CHUNK, HEADS))),
        scratch_shapes=[pltpu.VMEM((t + 2 * HALO, D), F32), pltpu.VMEM((t, D), F32), pltpu.VMEM((t, D), F32),
                        pltpu.VMEM((t, D), F32), pltpu.VMEM((t, D_MIX), BF16), pltpu.VMEM((CHUNK, D), F32),
                        pltpu.VMEM((SUB, CONV_PHASE_ROWS, CONV_BLK), F32)],
        compiler_params=_params(1),
    )(z, z, z, x, tgt, mod, conv_w_g, conv_b, cln_g, cln_b, sln_g, sln_b, final_g, ws_b, wst_b, bs_full, wout)


def _bwd_in_proj(z, dcv, dz_rest, x, dx2, mod, norm_g, conv_w_g, win_g):
    s = x.shape[0]
    t = ROW_TILE
    n_steps = s // t
    zp_spec, zn_spec = _halo_specs(t, s, 2 * D)
    dp_spec, dn_spec = _halo_specs(t, s, D)

    def body(z_ref, zp_ref, zn_ref, dcv_ref, dcvp_ref, dcvn_ref, dzr_ref, x_ref, dx2_ref, mod_ref, ng_ref, cw_ref, w_ref,
             gx_ref, dz_ref, acc_ref, gcw_ref,
             gext, dext, dg, taps, dpad, qbuf):
        i = pl.program_id(0)

        @pl.when(i == 0)
        def _():
            acc_ref[...] = jnp.zeros_like(acc_ref)
            gcw_ref[...] = jnp.zeros_like(gcw_ref)

        not_first = i > 0
        not_last = i < n_steps - 1
        gext[0:HALO, :] = jnp.where(not_first, _glu(zp_ref), 0.0)
        gext[HALO:HALO + t, :] = _glu(z_ref)
        gext[HALO + t:2 * HALO + t, :] = jnp.where(not_last, _glu(zn_ref), 0.0)
        dext[0:HALO, :] = jnp.where(not_first, dcvp_ref[...], 0.0)
        dext[HALO:HALO + t, :] = dcv_ref[...]
        dext[HALO + t:2 * HALO + t, :] = jnp.where(not_last, dcvn_ref[...], 0.0)

        taps[...] = jnp.zeros_like(taps)
        dpad[0:SUB, :] = jnp.zeros((SUB, D), F32)
        dpad[SUB:SUB + t, :] = dcv_ref[...]
        dpad[SUB + t:2 * SUB + t, :] = jnp.zeros((SUB, D), F32)
        for blk in range(N_DEV):
            cols = slice(blk * CONV_BLK, (blk + 1) * CONV_BLK)
            dg[:, cols] = _conv_taps(dext, qbuf, cols, lambda k, blk=blk: cw_ref[blk, CONV_W - 1 - k:CONV_W - k, :], t)
            for b in range(SUB):
                dshift = dpad[pl.ds(SUB - 1 - b, t + SUB), cols]
                for a in range((CONV_W - b + SUB - 1) // SUB):
                    k = SUB * a + b
                    taps[k:k + 1, :] = _colsum(gext[SUB * a:SUB * a + t + SUB, cols] * dshift)
            gcw_ref[blk] += taps[...]

        a = z_ref[:, 0:D]
        sig = _sigmoid(z_ref[:, D:2 * D])
        dgv = dg[...]
        dz_ref[:, 0:D] = (dgv * sig).astype(BF16)
        dz_ref[:, D:2 * D] = (dgv * a * sig * (1.0 - sig)).astype(BF16)
        dz_ref[:, 2 * D:6 * D] = dzr_ref[...]

        dh = jnp.zeros((t, D), F32)
        for j in range(N_DEV):
            dh = dh + lax.dot_general(dz_ref[:, j * W_IN_BLK:(j + 1) * W_IN_BLK], w_ref[j], (((1,), (1,)), ((), ())),
                                      preferred_element_type=F32)

        xn, r, _, scale = _rms_modulate(x_ref[...], mod_ref)
        ng = ng_ref[...]
        one_scale = 1.0 + scale
        dh_xn = dh * xn
        acc_ref[0:1, :] += _colsum(dh_xn * one_scale)
        acc_ref[1:2, :] += _colsum(dh)
        acc_ref[2:3, :] += _colsum(dh_xn * ng)
        dxn = dh * (ng * one_scale)
        gx_ref[...] = dx2_ref[...] + r * (dxn - xn * jnp.mean(dxn * xn, axis=-1, keepdims=True))

    row = lambda w: pl.BlockSpec((t, w), lambda i: (i, 0))
    const = lambda shape: pl.BlockSpec(shape, lambda i: (0,) * len(shape))
    return pl.pallas_call(
        body, name="bwd_in_proj", grid=(n_steps,),
        out_shape=(jax.ShapeDtypeStruct((s, D), F32), jax.ShapeDtypeStruct((s, D_IN), BF16),
                   jax.ShapeDtypeStruct((8, D), F32), jax.ShapeDtypeStruct((N_DEV, CONV_ROWS, CONV_BLK), F32)),
        in_specs=[pl.BlockSpec((t, 2 * D), lambda i: (i, 0)), zp_spec, zn_spec, row(D), dp_spec, dn_spec, row(4 * D),
                  row(D), row(D), VM, VM, VM, VM],
        out_specs=(row(D), row(D_IN), const((8, D)), const((N_DEV, CONV_ROWS, CONV_BLK))),
        scratch_shapes=[pltpu.VMEM((t + 2 * HALO, D), F32), pltpu.VMEM((t + 2 * HALO, D), F32), pltpu.VMEM((t, D), F32),
                        pltpu.VMEM((CONV_ROWS, CONV_BLK), F32), pltpu.VMEM((t + 2 * SUB, D), F32),
                        pltpu.VMEM((SUB, CONV_PHASE_ROWS, CONV_BLK), F32)],
        compiler_params=_params(1),
    )(z, z, z, dcv, dcv, dcv, dz_rest, x, dx2, mod, norm_g, conv_w_g, win_g)


WGRAD_K = 1024
_OTHER_CHIPS = [(3, True), (3, False), (1, True), (1, False), (2, True), (2, False)]
_OWN_CHIP = [(0, True), (0, False)]
WGRAD_ROLES = ([("out",) + e for e in _OTHER_CHIPS] + [("in",) + e for e in _OTHER_CHIPS]
               + [("out",) + e for e in _OWN_CHIP] + [("in",) + e for e in _OWN_CHIP])


def _wgrad_schedule():
    x, y, c = _mesh_pos()
    chip = [2 * x + y, 2 * x + (1 - y), 2 * (1 - x) + y, 2 * (1 - x) + (1 - y)]
    blk = lambda r, sibling: 2 * chip[r] + ((1 - c) if sibling else c)
    out_blk, in_blk, is_out = [], [], []
    last = {"out": blk(*_OTHER_CHIPS[0]), "in": blk(*_OTHER_CHIPS[0])}
    for kind, r, sibling in WGRAD_ROLES:
        last[kind] = blk(r, sibling)
        out_blk.append(last["out"])
        in_blk.append(last["in"])
        is_out.append(1 if kind == "out" else 0)
    as_vec = lambda v: jnp.stack([jnp.asarray(e, jnp.int32) for e in v])
    return as_vec(out_blk), as_vec(in_blk), as_vec(is_out)


def _sum_blocks(gathered):
    total = gathered[0]
    for b in range(1, N_DEV):
        total = total + gathered[b]
    return total


SMALL_ROWS = 48
ROW_NORM_G, ROW_FINAL_G, ROW_B_S, ROW_CONV_B, ROW_CLN_G, ROW_CLN_B, ROW_SLN_G, ROW_SLN_B, ROW_LOSS = range(32, 41)


def _wgrad_reduce(ht, dz, ycatt, dy, small, gws, dmod):
    s = dz.shape[0]
    n_kc = s // WGRAD_K
    n_steps = len(WGRAD_ROLES)
    first_in = [k for k, role in enumerate(WGRAD_ROLES) if role[0] == "in"][0]
    blk_in, blk_out = (D, W_IN_BLK), (W_OUT_BLK, D)

    def body(out_blk, in_blk, is_out, ht_ref, dz_ref, yt_ref, dy_ref, small_ref, gws_ref, dmod_ref,
             oin_ref, oout_ref, osmall_ref, ogws_ref, odmod_ref, obada_ref, oloss_ref,
             acc_in, acc_out, p1_in, p1_out, l1_in, l1_out, l2_in, l2_out, small_g, gws_g, dmod_g,
             s1_send, s1_recv, s2_send, s2_recv, t1_send, t1_recv, t2_send, t2_recv, ag_send, ag_recv):
        step, kc = pl.program_id(0), pl.program_id(1)
        pay = {"in": (acc_in, p1_in, oin_ref, _ChipReduce(l1_in, l2_in, s1_send, s1_recv, s2_send, s2_recv)),
               "out": (acc_out, p1_out, oout_ref, _ChipReduce(l1_out, l2_out, t1_send, t1_recv, t2_send, t2_recv))}
        ag = _AllGather([dmod_g, small_g, gws_g], ag_send, ag_recv)
        last_kc = kc == n_kc - 1

        @pl.when((step == 0) & (kc == 0))
        def _():
            me = _my_block()
            small_g[me] = small_ref[...]
            gws_g[me] = gws_ref[...].astype(BF16)
            dmod_g[me] = dmod_ref[...]
            for a in range(3):
                ag.send_own(a)

        @pl.when((step == 4) & (kc == 0))
        def _():
            for j in range(3):
                for a in range(3):
                    ag.arrived(a, 1 + j)
                    ag.pass_on(a, j)

        def accumulate(acc, prod):
            @pl.when(kc == 0)
            def _():
                acc[...] = prod

            @pl.when(kc != 0)
            def _():
                acc[...] += prod

        @pl.when(is_out[step] == 1)
        def _():
            accumulate(acc_out, jnp.dot(yt_ref[...], dy_ref[...], preferred_element_type=F32))

        @pl.when(is_out[step] == 0)
        def _():
            accumulate(acc_in, jnp.dot(ht_ref[...], dz_ref[...], preferred_element_type=F32))

        for k, (kind, r, sibling) in enumerate(WGRAD_ROLES):
            @pl.when((step == k) & last_kc)
            def _(kind=kind, r=r, sibling=sibling):
                acc, p1, out, red = pay[kind]
                if sibling:
                    p1[r] = acc[...].astype(BF16)
                    red.to_sibling(r, p1.at[r]).start()
                else:
                    chip_sum = red.combine(r, acc[...])
                    if r == 0:
                        out[...] = chip_sum

        @pl.when((step == n_steps - 1) & last_kc)
        def _():
            for kind in ("out", "in"):
                _, _, out, red = pay[kind]
                out[...] = red.finish(out[...])
            for k in (0, 4, 5, 6):
                for a in range(3):
                    ag.arrived(a, k)
            for a in range(3):
                ag.sent(a)
            tot_small = _sum_blocks(small_g)
            osmall_ref[...] = tot_small
            oloss_ref[...] = jnp.full(oloss_ref.shape, (0.5 / D) * jnp.sum(tot_small[ROW_LOSS:ROW_LOSS + 1, :]), F32)
            tot_gws = gws_g[0].astype(F32)
            for b in range(1, N_DEV):
                tot_gws = tot_gws + gws_g[b].astype(F32)
            ogws_ref[...] = tot_gws
            obada_ref[...] = _sum_blocks(dmod_g)
            for b in range(N_DEV):
                odmod_ref[b:b + 1, :] = dmod_g[b, 0:1, :]

    def kc_of(working, step, kc, hold_first):
        held = jnp.where(step < hold_first, 0, n_kc - 1)
        return jnp.where(working, kc, held)

    out_kc = lambda i, kc, ob, ib, io: kc_of(io[i] == 1, i, kc, 0)
    in_kc = lambda i, kc, ob, ib, io: kc_of(io[i] == 0, i, kc, first_in)
    sems = lambda n: [pltpu.SemaphoreType.DMA((n,)), pltpu.SemaphoreType.DMA((n,))]
    grid_spec = pltpu.PrefetchScalarGridSpec(
        num_scalar_prefetch=3, grid=(n_steps, n_kc),
        in_specs=[pl.BlockSpec((D, WGRAD_K), lambda i, kc, ob, ib, io: (0, in_kc(i, kc, ob, ib, io))),
                  pl.BlockSpec((WGRAD_K, W_IN_BLK), lambda i, kc, ob, ib, io: (in_kc(i, kc, ob, ib, io), ib[i])),
                  pl.BlockSpec((W_OUT_BLK, WGRAD_K), lambda i, kc, ob, ib, io: (ob[i], out_kc(i, kc, ob, ib, io))),
                  pl.BlockSpec((WGRAD_K, D), lambda i, kc, ob, ib, io: (out_kc(i, kc, ob, ib, io), 0)),
                  VM, VM, VM],
        out_specs=(VM,) * 7,
        scratch_shapes=[pltpu.VMEM(blk_in, F32), pltpu.VMEM(blk_out, F32),
                        pltpu.VMEM((4,) + blk_in, BF16), pltpu.VMEM((4,) + blk_out, BF16),
                        pltpu.VMEM((4,) + blk_in, BF16), pltpu.VMEM((4,) + blk_out, BF16),
                        pltpu.VMEM((3,) + blk_in, BF16), pltpu.VMEM((3,) + blk_out, BF16),
                        pltpu.VMEM((N_DEV,) + small.shape, F32), pltpu.VMEM((N_DEV,) + gws.shape, BF16),
                        pltpu.VMEM((N_DEV,) + dmod.shape, F32)]
        + sems(4) + sems(3) + sems(4) + sems(3) + sems(21))
    return pl.pallas_call(
        body, name="wgrad_reduce", grid_spec=grid_spec,
        out_shape=(jax.ShapeDtypeStruct(blk_in, F32), jax.ShapeDtypeStruct(blk_out, F32),
                   jax.ShapeDtypeStruct(small.shape, F32), jax.ShapeDtypeStruct(gws.shape, F32),
                   jax.ShapeDtypeStruct((N_DEV, 3 * D), F32), jax.ShapeDtypeStruct((8, 3 * D), F32),
                   jax.ShapeDtypeStruct((8, 128), F32)),
        compiler_params=_params(2),
    )(*_wgrad_schedule(), ht, dz, ycatt, dy, small, gws, dmod)


def _adamw_math(w, g, m, v):
    m = ADAM_B1 * m + (1.0 - ADAM_B1) * g
    v = ADAM_B2 * v + (1.0 - ADAM_B2) * (g * g)
    m_hat = m / (1.0 - ADAM_B1 ** ADAM_STEP)
    v_hat = v / (1.0 - ADAM_B2 ** ADAM_STEP)
    delta = -ADAM_LR * (m_hat / (jnp.sqrt(v_hat) + ADAM_EPS) + ADAM_WD * w)
    return delta, m, v


VECTORS = ["norm_g", "conv_b", "conv_ln_g", "conv_ln_b", "sg_ln_g", "sg_ln_b", "final_g", "b_s"]
VECTOR_ROWS = [ROW_NORM_G, ROW_CONV_B, ROW_CLN_G, ROW_CLN_B, ROW_SLN_G, ROW_SLN_B, ROW_FINAL_G, ROW_B_S]
ADAM_STEPS = 4


def _adamw_all(me, g_w_in, g_w_out, g_w_s, small_sum, dmod_all, bada8, c_all, matrices, b_ada, conv_w, vectors):
    mat_shapes = [(D, W_IN_BLK), (D, W_ADA_BLK), (W_OUT_BLK, D), (HEADS * CHUNK, CHUNK)]
    mat_blocks = [(sh[0] // ADAM_STEPS, sh[1]) for sh in mat_shapes]
    n_small = 2 + len(VECTORS)

    def body(me_ref, gin_ref, gout_ref, gws_ref, taps_ref, small_ref, dmc_ref, bada_ref, c_ref, *refs):
        params = refs[:3 * (4 + n_small)]
        outs = refs[3 * (4 + n_small):-2]
        act, gada = refs[-2:]
        mat_out, gada_out, small_out = outs[:12], outs[12], outs[13:]
        i = pl.program_id(0)

        @pl.when(i == 0)
        def _():
            for b in range(N_DEV):
                cb = c_ref[b, 0:1, :]
                act[b:b + 1, :] = cb * _sigmoid(cb)
            gada[...] = lax.dot_general(act[...], dmc_ref[...], (((0,), (0,)), ((), ())), preferred_element_type=F32,
                                        precision=lax.Precision.HIGHEST)
            small_grads = [bada_ref[0:1, :], taps_ref[0:CONV_W, :]] + [small_ref[r:r + 1, :] for r in VECTOR_ROWS]
            for k, g in enumerate(small_grads):
                w_ref, m_ref, v_ref = params[3 * (4 + k):3 * (5 + k)]
                o = small_out[4 * k:4 * k + 4]
                o[0][...] = g
                o[1][...], o[2][...], o[3][...] = _adamw_math(w_ref[...], g, m_ref[...], v_ref[...])

        rows = pl.ds(pl.multiple_of(i * mat_blocks[1][0], mat_blocks[1][0]), mat_blocks[1][0])
        g_ada = gada[rows, :]
        gada_out[...] = g_ada
        for k, g in enumerate([gin_ref[...], g_ada, gout_ref[...], gws_ref[...]]):
            w_ref, m_ref, v_ref = params[3 * k:3 * k + 3]
            o = mat_out[3 * k:3 * k + 3]
            o[0][...], o[1][...], o[2][...] = _adamw_math(w_ref[...], g, m_ref[...], v_ref[...])

    rows_of = lambda blk: pl.BlockSpec(blk, lambda i, me_ref: (i, 0))
    mat_specs = [rows_of(b) for b in mat_blocks]
    grid_spec = pltpu.PrefetchScalarGridSpec(
        num_scalar_prefetch=1, grid=(ADAM_STEPS,),
        in_specs=[mat_specs[0], mat_specs[2], mat_specs[3],
                  pl.BlockSpec((CONV_ROWS, CONV_BLK), lambda i, me_ref: (0, me_ref[0])), VM,
                  pl.BlockSpec((N_DEV, W_ADA_BLK), lambda i, me_ref: (0, me_ref[0])), VM, VM]
        + [s for s in mat_specs for _ in range(3)] + [VM] * (3 * n_small),
        out_specs=tuple([s for s in mat_specs for _ in range(3)] + [mat_specs[1]] + [VM] * (4 * n_small)),
        scratch_shapes=[pltpu.VMEM((N_DEV, D), F32), pltpu.VMEM((D, W_ADA_BLK), F32)])
    small_shapes = [b_ada[0].shape, conv_w[0].shape] + [(1, D)] * len(VECTORS)
    out_shape = tuple([jax.ShapeDtypeStruct(sh, F32) for sh in mat_shapes for _ in range(3)]
                      + [jax.ShapeDtypeStruct(mat_shapes[1], F32)]
                      + [jax.ShapeDtypeStruct(sh, F32) for sh in small_shapes for _ in range(4)])
    flat = [a for group in matrices for a in group] + list(b_ada) + list(conv_w) + [a for group in vectors for a in group]
    return pl.pallas_call(body, name="adamw_all", grid_spec=grid_spec, out_shape=out_shape,
                          compiler_params=_params(1))(
        me, g_w_in, g_w_out, g_w_s, small_sum, small_sum, dmod_all, bada8, c_all, *flat)


def kernel(x, c, w_ada, b_ada, norm_g, w_in, conv_w, conv_b, conv_ln_g, conv_ln_b, sg_ln_g, sg_ln_b, w_s, b_s, w_out, final_g, loss_target, m_w_ada, m_b_ada, m_norm_g, m_w_in, m_conv_w, m_conv_b, m_conv_ln_g, m_conv_ln_b, m_sg_ln_g, m_sg_ln_b, m_w_s, m_b_s, m_w_out, m_final_g, v_w_ada, v_b_ada, v_norm_g, v_w_in, v_conv_w, v_conv_b, v_conv_ln_g, v_conv_ln_b, v_sg_ln_g, v_sg_ln_b, v_w_s, v_b_s, v_w_out, v_final_g):
    me = 4 * lax.axis_index("x") + 2 * lax.axis_index("y") + lax.axis_index("c")
    x2d, tgt2d = x[0], loss_target[0]
    row1 = lambda a: a.reshape(1, D)
    taps = lambda a: jnp.pad(a.reshape(CONV_W, CONV_BLK), ((0, CONV_ROWS - CONV_W), (0, 0)))

    z, ht, win_g, wout_g, cw_g, c_all, mod = _fwd_in_proj(
        x2d, jnp.broadcast_to(c, (8, D)), w_ada[0], b_ada, norm_g, w_in[0], w_out[0], taps(conv_w))
    ws_b = w_s[0].astype(BF16)
    wst_b = jnp.swapaxes(w_s[0], 1, 2).astype(BF16)
    bs_full = jnp.repeat(b_s[0].T, HEAD_DIM, axis=1)

    dx2, ycatt, dy, dcv, dz_rest, acc_a, gws, gbs = _mix_and_head(
        z, x2d, tgt2d, mod, cw_g, conv_b, conv_ln_g, conv_ln_b, sg_ln_g, sg_ln_b, row1(final_g), ws_b, wst_b, bs_full,
        wout_g.reshape(D_MIX, D))
    grad_x, dz, acc_b, gcw = _bwd_in_proj(z, dcv, dz_rest, x2d, dx2, mod, norm_g, cw_g, win_g)

    small = jnp.concatenate(
        [jnp.transpose(gcw, (1, 0, 2)).reshape(CONV_ROWS, D), acc_b[0:1], acc_a[0:1], gbs.T.reshape(1, D), acc_a[2:8],
         jnp.zeros((SMALL_ROWS - ROW_LOSS - 1, D), F32)], axis=0)
    dmod_row = jnp.concatenate([acc_b[1:2], acc_b[2:3], acc_a[1:2]], axis=1)
    g_w_in, g_w_out, small_sum, g_w_s, dmod_all, bada8, loss_tile = _wgrad_reduce(
        ht, dz, ycatt, dy, small, gws.reshape(HEADS * CHUNK, CHUNK), jnp.broadcast_to(dmod_row, (8, 3 * D)))

    given = dict(w_ada=(w_ada, m_w_ada, v_w_ada), b_ada=(b_ada, m_b_ada, v_b_ada), norm_g=(norm_g, m_norm_g, v_norm_g),
                 w_in=(w_in, m_w_in, v_w_in), conv_w=(conv_w, m_conv_w, v_conv_w), conv_b=(conv_b, m_conv_b, v_conv_b),
                 conv_ln_g=(conv_ln_g, m_conv_ln_g, v_conv_ln_g), conv_ln_b=(conv_ln_b, m_conv_ln_b, v_conv_ln_b),
                 sg_ln_g=(sg_ln_g, m_sg_ln_g, v_sg_ln_g), sg_ln_b=(sg_ln_b, m_sg_ln_b, v_sg_ln_b),
                 w_s=(w_s, m_w_s, v_w_s), b_s=(b_s, m_b_s, v_b_s), w_out=(w_out, m_w_out, v_w_out),
                 final_g=(final_g, m_final_g, v_final_g))
    as2d = lambda name, shape: tuple(a.reshape(shape) for a in given[name])
    res = _adamw_all(
        jnp.reshape(me, (1,)).astype(jnp.int32), g_w_in, g_w_out, g_w_s, small_sum, dmod_all, bada8, c_all,
        [as2d("w_in", (D, W_IN_BLK)), as2d("w_ada", (D, W_ADA_BLK)), as2d("w_out", (W_OUT_BLK, D)),
         as2d("w_s", (HEADS * CHUNK, CHUNK))],
        given["b_ada"], as2d("conv_w", (CONV_W, CONV_BLK)), [as2d(n, (1, D)) for n in VECTORS])
    out = {}
    for k, name in enumerate(["w_in", "w_ada", "w_out", "w_s"]):
        out[name] = [None] + list(res[3 * k:3 * k + 3])
    out["w_in"][0], out["w_ada"][0], out["w_out"][0], out["w_s"][0] = g_w_in, res[12], g_w_out, g_w_s
    for k, name in enumerate(["b_ada", "conv_w"] + VECTORS):
        out[name] = list(res[13 + 4 * k:17 + 4 * k])
    order = ["w_ada", "b_ada", "norm_g", "w_in", "conv_w", "conv_b", "conv_ln_g", "conv_ln_b", "sg_ln_g", "sg_ln_b",
             "w_s", "b_s", "w_out", "final_g"]
    outs = [loss_tile[0, 0], grad_x.reshape(x.shape)]
    for kind in range(4):
        outs += [out[n][kind].reshape(given[n][0].shape) for n in order]
    return tuple(outs)
```

```python
import functools

import jax
import jax.numpy as jnp
from jax import lax
from jax.experimental import pallas as pl
from jax.experimental.pallas import tpu as pltpu

F32 = jnp.float32
BF16 = jnp.bfloat16
MESH = pl.DeviceIdType.MESH

D = 1024
D_IN = 6 * D
D_MIX = 2 * D
N_DEV = 8
W_IN_BLK = D_IN // N_DEV
W_OUT_BLK = D_MIX // N_DEV
W_ADA_BLK = 3 * D // N_DEV
CONV_BLK = D // N_DEV
CONV_W = 31
CONV_HALF = CONV_W // 2
CONV_ROWS = 32
HALO = 16
CHUNK = 128
HEADS = 8
HEAD_DIM = 128
EPS = 1e-6
ROW_TILE = 256
MIX_TILE = 128
VMEM_LIMIT = 56 * 1024 * 1024

ADAM_LR = 0.001
ADAM_B1 = 0.9
ADAM_B2 = 0.999
ADAM_EPS = 1e-08
ADAM_WD = 0.01
ADAM_STEP = 10

VM = pl.BlockSpec(memory_space=pltpu.VMEM)


def _params(grid_rank=0, **kw):
    sem = ("arbitrary",) * grid_rank if grid_rank else None
    return pltpu.CompilerParams(dimension_semantics=sem, vmem_limit_bytes=VMEM_LIMIT, **kw)


def _sigmoid(t):
    return jax.nn.sigmoid(t)


def _dsilu(t, sig):
    return sig * (1.0 + t * (1.0 - sig))


def _mesh_pos():
    return lax.axis_index("x"), lax.axis_index("y"), lax.axis_index("c")


class _AllGather:
    def __init__(self, bufs, send_sems, recv_sems):
        x, y, c = _mesh_pos()
        self.bufs, self.send_sems, self.recv_sems = bufs, send_sems, recv_sems
        self.me, self.sib = (x, y, c), (x, y, 1 - c)
        self.chips = [(x, 1 - y), (1 - x, y), (1 - x, 1 - y)]
        self.c = c

    def _copy(self, a, k, block, to):
        px, py, pc = block
        ref = self.bufs[a].at[4 * px + 2 * py + pc]
        return pltpu.make_async_remote_copy(
            src_ref=ref, dst_ref=ref, send_sem=self.send_sems.at[7 * a + k], recv_sem=self.recv_sems.at[7 * a + k],
            device_id=to, device_id_type=MESH)

    def _outgoing(self, a, k):
        if k == 0:
            return self._copy(a, 0, self.me, self.sib)
        if k <= 3:
            return self._copy(a, k, self.me, (*self.chips[k - 1], self.c))
        return self._copy(a, k, (*self.chips[k - 4], self.c), self.sib)

    def source(self, k):
        if k == 0:
            return self.sib
        return (*self.chips[(k - 1) % 3], self.c if k <= 3 else 1 - self.c)

    def block_index(self, k):
        px, py, pc = self.source(k)
        return 4 * px + 2 * py + pc

    def send_own(self, a):
        for k in range(4):
            self._outgoing(a, k).start()

    def arrived(self, a, k):
        self._copy(a, k, self.source(k), self.me).wait_recv()

    def pass_on(self, a, j):
        self._outgoing(a, 4 + j).start()

    def sent(self, a):
        for k in range(7):
            self._outgoing(a, k).wait_send()

    def run(self):
        n = range(len(self.bufs))
        for a in n:
            self.send_own(a)
        for j in range(3):
            for a in n:
                self.arrived(a, 1 + j)
                self.pass_on(a, j)
        for k in (0, 4, 5, 6):
            for a in n:
                self.arrived(a, k)
        for a in n:
            self.sent(a)


class _ChipReduce:
    def __init__(self, l1, l2, s1_send, s1_recv, s2_send, s2_recv):
        x, y, c = _mesh_pos()
        self.l1, self.l2 = l1, l2
        self.s1_send, self.s1_recv, self.s2_send, self.s2_recv = s1_send, s1_recv, s2_send, s2_recv
        self.c, self.sib = c, (x, y, 1 - c)
        self.chip_of = [(x, y), (x, 1 - y), (1 - x, y), (1 - x, 1 - y)]
        self.chip = [2 * x + y, 2 * x + (1 - y), 2 * (1 - x) + y, 2 * (1 - x) + (1 - y)]

    def block(self, r, sibling):
        return 2 * self.chip[r] + ((1 - self.c) if sibling else self.c)

    def to_sibling(self, r, src=None):
        return pltpu.make_async_remote_copy(
            src_ref=self.l1.at[r] if src is None else src, dst_ref=self.l1.at[r], send_sem=self.s1_send.at[r],
            recv_sem=self.s1_recv.at[r], device_id=self.sib, device_id_type=MESH)

    def to_chip(self, r):
        return pltpu.make_async_remote_copy(
            src_ref=self.l1.at[r], dst_ref=self.l2.at[r - 1], send_sem=self.s2_send.at[r - 1],
            recv_sem=self.s2_recv.at[r - 1], device_id=(*self.chip_of[r], self.c), device_id_type=MESH)

    def combine(self, r, mine):
        self.to_sibling(r).wait_recv()
        both = mine + self.l1[r].astype(F32)
        if r == 0:
            return both
        self.l1[r] = both.astype(BF16)
        self.to_chip(r).start()
        return None

    def finish(self, own_chip_sum):
        total = own_chip_sum
        for r in (1, 2, 3):
            self.to_chip(r).wait_recv()
            total = total + self.l2[r - 1].astype(F32)
        for r in range(4):
            self.to_sibling(r).wait_send()
        for r in (1, 2, 3):
            self.to_chip(r).wait_send()
        return total


def _my_block():
    x, y, c = _mesh_pos()
    return 4 * x + 2 * y + c


def _modulation(c_g, w_ref, b_ref, mod_ref, part, land, send_sems, recv_sems):
    x, y, c = _mesh_pos()
    me = 4 * x + 2 * y + c
    w = w_ref[...]
    for b in range(N_DEV):
        cb = c_g[b]
        part[b] = jnp.dot(cb * _sigmoid(cb), w, preferred_element_type=F32, precision=lax.Precision.HIGHEST)
    land[me] = part[me]

    def copy(b):
        return pltpu.make_async_remote_copy(
            src_ref=part.at[b], dst_ref=land.at[me], send_sem=send_sems.at[b], recv_sem=recv_sems.at[me],
            device_id=(b // 4, (b // 2) % 2, b % 2), device_id_type=MESH)

    def arrival(b):
        return pltpu.make_async_remote_copy(
            src_ref=part.at[b], dst_ref=land.at[b], send_sem=send_sems.at[b], recv_sem=recv_sems.at[b],
            device_id=(b // 4, (b // 2) % 2, b % 2), device_id_type=MESH)

    for b in range(N_DEV):
        @pl.when(b != me)
        def _():
            copy(b).start()
    for b in range(N_DEV):
        @pl.when(b != me)
        def _():
            arrival(b).wait_recv()
            copy(b).wait_send()
    for b in range(N_DEV):
        cols = slice(b * W_ADA_BLK, (b + 1) * W_ADA_BLK)
        mod_ref[:, cols] = land[b] + b_ref[:, cols]


def _rms_modulate(x, mod_ref):
    shift = mod_ref[0:1, 0:D]
    scale = mod_ref[0:1, D:2 * D]
    r = lax.rsqrt(jnp.mean(x * x, axis=-1, keepdims=True) + EPS)
    xn = x * r
    return xn, r, shift, scale


FWD_TILE = 512
FWD_Z_TILE = 1024


def _fwd_in_proj(x, c_rep, w_ada, b_ada, norm_g, w_in, w_out, conv_w):
    s = x.shape[0]
    t, tz = FWD_TILE, FWD_Z_TILE
    n_tiles = s // t
    C_PAY, WIN_PAY, WOUT_PAY, CW_PAY = 0, 1, 2, 3

    def body(x_hbm, c_ref, wada_ref, bada_ref, ng_ref, win_ref, wout_ref, cw_ref,
             z_hbm, ht_ref, win_g, wout_g, cw_g, c_g, mod_ref,
             h, xbuf, zbuf, part, land, ag_send, ag_recv, mod_send, mod_recv, x_sem, z_sem):
        me = _my_block()
        c_g[me] = c_ref[...]
        ag = _AllGather([c_g, win_g, wout_g, cw_g], ag_send, ag_recv)
        ag.send_own(C_PAY)

        def x_copy(i):
            return pltpu.make_async_copy(x_hbm.at[pl.ds(i * t, t), :], xbuf.at[i % 2], x_sem.at[i % 2])

        x_copy(0).start()
        win_g[me] = win_ref[...].astype(BF16)
        wout_g[me] = wout_ref[...].astype(BF16)
        cw_g[me] = cw_ref[...]

        for j in range(3):
            ag.arrived(C_PAY, 1 + j)
            ag.pass_on(C_PAY, j)
        for k in (0, 4, 5, 6):
            ag.arrived(C_PAY, k)
        _modulation(c_g, wada_ref, bada_ref, mod_ref, part, land, mod_send, mod_recv)
        for a in (WIN_PAY, WOUT_PAY, CW_PAY):
            ag.send_own(a)

        for i in range(n_tiles):
            x_copy(i).wait()
            if i + 1 < n_tiles:
                x_copy(i + 1).start()
            xn, _, shift, scale = _rms_modulate(xbuf[i % 2], mod_ref)
            hh = xn * ng_ref[...] * (1.0 + scale) + shift
            h[i * t:(i + 1) * t, :] = hh.astype(BF16)
            ht_ref[:, i * t:(i + 1) * t] = hh.T.astype(BF16)

        def z_copy(slot, row0, col0):
            return pltpu.make_async_copy(zbuf.at[slot], z_hbm.at[pl.ds(row0, tz), pl.ds(col0, W_IN_BLK)], z_sem.at[slot])

        done = [0]

        def z_block(blk):
            col0 = pl.multiple_of(blk * W_IN_BLK, 128)
            for i in range(s // tz):
                slot = done[0] % 2
                if done[0] >= 2:
                    z_copy(slot, 0, 0).wait()
                zbuf[slot] = jnp.dot(h[i * tz:(i + 1) * tz, :], win_g[blk], preferred_element_type=F32)
                z_copy(slot, i * tz, col0).start()
                done[0] += 1

        z_block(me)
        ag.arrived(WIN_PAY, 0)
        z_block(ag.block_index(0))
        for j in (0, 1):
            ag.arrived(WIN_PAY, 1 + j)
            ag.pass_on(WIN_PAY, j)
        for j in (0, 1):
            z_block(ag.block_index(1 + j))
        ag.arrived(WIN_PAY, 4)
        z_block(ag.block_index(4))
        ag.arrived(WIN_PAY, 3)
        ag.pass_on(WIN_PAY, 2)
        ag.arrived(WIN_PAY, 5)
        z_block(ag.block_index(5))
        z_block(ag.block_index(3))
        ag.arrived(WIN_PAY, 6)
        z_block(ag.block_index(6))

        for a in (WOUT_PAY, CW_PAY):
            for j in range(3):
                ag.arrived(a, 1 + j)
                ag.pass_on(a, j)
        for a in (WOUT_PAY, CW_PAY):
            for k in (0, 4, 5, 6):
                ag.arrived(a, k)
        for a in (C_PAY, WIN_PAY, WOUT_PAY, CW_PAY):
            ag.sent(a)
        z_copy(0, 0, 0).wait()
        z_copy(1, 0, 0).wait()

    any_spec = pl.BlockSpec(memory_space=pl.ANY)
    return pl.pallas_call(
        body, name="fwd_in_proj",
        out_shape=(jax.ShapeDtypeStruct((s, D_IN), F32), jax.ShapeDtypeStruct((D, s), BF16),
                   jax.ShapeDtypeStruct((N_DEV,) + w_in.shape, BF16), jax.ShapeDtypeStruct((N_DEV,) + w_out.shape, BF16),
                   jax.ShapeDtypeStruct((N_DEV,) + conv_w.shape, F32), jax.ShapeDtypeStruct((N_DEV,) + c_rep.shape, F32),
                   jax.ShapeDtypeStruct((8, 3 * D), F32)),
        in_specs=[any_spec] + [VM] * 7, out_specs=(any_spec,) + (VM,) * 6,
        scratch_shapes=[pltpu.VMEM((s, D), BF16), pltpu.VMEM((2, t, D), F32), pltpu.VMEM((2, tz, W_IN_BLK), F32),
                        pltpu.VMEM((N_DEV, 8, W_ADA_BLK), F32), pltpu.VMEM((N_DEV, 8, W_ADA_BLK), F32),
                        pltpu.SemaphoreType.DMA((28,)), pltpu.SemaphoreType.DMA((28,)),
                        pltpu.SemaphoreType.DMA((N_DEV,)), pltpu.SemaphoreType.DMA((N_DEV,)),
                        pltpu.SemaphoreType.DMA((2,)), pltpu.SemaphoreType.DMA((2,))],
        compiler_params=_params(),
    )(x, c_rep, w_ada, b_ada, norm_g, w_in, w_out, conv_w)


def _halo_specs(t, s, width):
    per = t // HALO
    last = s // HALO - 1
    prev = pl.BlockSpec((HALO, width), lambda i: (jnp.maximum(i * per - 1, 0), 0))
    nxt = pl.BlockSpec((HALO, width), lambda i: (jnp.minimum((i + 1) * per, last), 0))
    return prev, nxt


def _glu(ref):
    return ref[:, 0:D] * _sigmoid(ref[:, D:2 * D])


SUB = 8
CONV_PHASE_ROWS = ROW_TILE + SUB


def _conv_taps(ext, qbuf, cols, tap, t):
    out = None
    for b in range(SUB):
        q = None
        for a in range((CONV_W - b + SUB - 1) // SUB):
            term = ext[SUB * a:SUB * a + t + SUB, cols] * tap(SUB * a + b)
            q = term if q is None else q + term
        qbuf[b] = q
        shifted = qbuf[b, pl.ds(b + 1, t), :]
        out = shifted if out is None else out + shifted
    return out


def _layer_norm_stats(v):
    mu = jnp.mean(v, axis=-1, keepdims=True)
    cen = v - mu
    rstd = lax.rsqrt(jnp.mean(cen * cen, axis=-1, keepdims=True) + EPS)
    return cen * rstd, rstd


def _layer_norm_bwd(dy_hat, hat, rstd):
    m1 = jnp.mean(dy_hat, axis=-1, keepdims=True)
    m2 = jnp.mean(dy_hat * hat, axis=-1, keepdims=True)
    return rstd * (dy_hat - m1 - hat * m2)


def _colsum(v):
    return jnp.sum(v, axis=0, keepdims=True)


def _mix_and_head(z, x, tgt, mod, conv_w_g, conv_b, cln_g, cln_b, sln_g, sln_b, final_g, ws_b, wst_b, bs_full, wout):
    s = x.shape[0]
    t = MIX_TILE
    n_chunks = t // CHUNK
    n_steps = s // t
    assert n_steps % 2 == 0
    prev_spec, next_spec = _halo_specs(t, s, 2 * D)

    def body(z_ref, zp_ref, zn_ref, x_ref, tgt_ref, mod_ref, cw_ref, cb_ref, clg_ref, clb_ref, slg_ref, slb_ref, fg_ref,
             ws_ref, wst_ref, bs_ref, wout_ref,
             dx2_ref, dcv_ref, dzr_ref, acc_ref, gws_ref, gbs_ref, gout_ref,
             gext, cv, vs, dvn, ycat, dyp, gwo, gbs_acc, qbuf):
        i = pl.program_id(0)
        half = pl.ds(pl.multiple_of((i % 2) * t, t), t)

        @pl.when(i == 0)
        def _():
            acc_ref[...] = jnp.zeros_like(acc_ref)
            gws_ref[...] = jnp.zeros_like(gws_ref)
            gbs_acc[...] = jnp.zeros_like(gbs_acc)

        gext[0:HALO, :] = jnp.where(i > 0, _glu(zp_ref), 0.0)
        gext[HALO:HALO + t, :] = _glu(z_ref)
        gext[HALO + t:2 * HALO + t, :] = jnp.where(i < n_steps - 1, _glu(zn_ref), 0.0)
        for blk in range(N_DEV):
            cols = slice(blk * CONV_BLK, (blk + 1) * CONV_BLK)
            cv[:, cols] = _conv_taps(gext, qbuf, cols, lambda k, blk=blk: cw_ref[blk, k:k + 1, :], t) + cb_ref[:, cols]
        ln_hat, ln_rstd = _layer_norm_stats(cv[...])
        ln_a = ln_hat * clg_ref[...] + clb_ref[...]
        sig_ln = _sigmoid(ln_a)
        sa = ln_a * sig_ln
        a_gate = z_ref[:, 2 * D:3 * D]
        sig_ag = _sigmoid(a_gate)
        s_gate = a_gate * sig_ag
        ya = sa * s_gate

        v_hat, v_rstd = _layer_norm_stats(z_ref[:, 4 * D:5 * D])
        vn = v_hat * slg_ref[...] + slb_ref[...]
        vnb = vn.astype(BF16)
        for n in range(n_chunks):
            rows = slice(n * CHUNK, (n + 1) * CHUNK)
            for h in range(HEADS):
                cols = slice(h * HEAD_DIM, (h + 1) * HEAD_DIM)
                vs[rows, cols] = jnp.dot(ws_ref[h], vnb[rows, cols], preferred_element_type=F32) + bs_ref[:, cols]
        u = z_ref[:, 3 * D:4 * D]
        b_gate = z_ref[:, 5 * D:6 * D]
        sig_bg = _sigmoid(b_gate)
        s_bg = b_gate * sig_bg
        vsv = vs[...]
        yb = u * vsv * s_bg

        ycat[half, 0:D] = ya.astype(BF16)
        ycat[half, D:2 * D] = yb.astype(BF16)
        y = jnp.dot(ycat[half, :], wout_ref[...], preferred_element_type=F32)
        gate = mod_ref[0:1, 2 * D:3 * D]
        x2 = x_ref[...] + gate * y
        r2 = lax.rsqrt(jnp.mean(x2 * x2, axis=-1, keepdims=True) + EPS)
        x2n = x2 * r2
        fg = fg_ref[...]
        diff = x2n * fg - tgt_ref[...]
        acc_ref[7:8, :] += _colsum(diff * diff)
        dout = diff * (1.0 / D)
        acc_ref[0:1, :] += _colsum(dout * x2n)
        dx2n = dout * fg
        dx2 = r2 * (dx2n - x2n * jnp.mean(dx2n * x2n, axis=-1, keepdims=True))
        dx2_ref[...] = dx2
        acc_ref[1:2, :] += _colsum(dx2 * y)
        dyb16 = (dx2 * gate).astype(BF16)
        dyp[half, :] = dyb16
        dycat = lax.dot_general(dyb16, wout_ref[...], (((1,), (1,)), ((), ())), preferred_element_type=F32)

        for j in range(N_DEV):
            shard = slice(j * W_OUT_BLK, (j + 1) * W_OUT_BLK)

            @pl.when(i == 1)
            def _(shard=shard):
                gwo[shard, :] = lax.dot_general(ycat[:, shard], dyp[...], (((0,), (0,)), ((), ())),
                                                preferred_element_type=F32)

            @pl.when((i % 2 == 1) & (i > 1))
            def _(shard=shard):
                gwo[shard, :] += lax.dot_general(ycat[:, shard], dyp[...], (((0,), (0,)), ((), ())),
                                                 preferred_element_type=F32)

            @pl.when(i == n_steps - 1)
            def _(j=j, shard=shard):
                gout_ref[j] = gwo[shard, :].astype(BF16)
        dya = dycat[:, 0:D]
        dyb = dycat[:, D:2 * D]

        du = dyb * vsv * s_bg
        dvs = dyb * u * s_bg
        dbg = dyb * u * vsv * _dsilu(b_gate, sig_bg)
        dvsb = dvs.astype(BF16)
        gbs = gbs_acc[...]
        for n in range(n_chunks):
            rows = slice(n * CHUNK, (n + 1) * CHUNK)
            gbs = gbs + dvs[rows, :]
            for h in range(HEADS):
                cols = slice(h * HEAD_DIM, (h + 1) * HEAD_DIM)
                gws_ref[h] += lax.dot_general(dvsb[rows, cols], vnb[rows, cols], (((1,), (1,)), ((), ())),
                                              preferred_element_type=F32)
                dvn[rows, cols] = jnp.dot(wst_ref[h], dvsb[rows, cols], preferred_element_type=F32)
        gbs_acc[...] = gbs

        @pl.when(i == n_steps - 1)
        def _():
            for h in range(HEADS):
                gbs_ref[:, h:h + 1] = jnp.sum(gbs_acc[:, h * HEAD_DIM:(h + 1) * HEAD_DIM], axis=1, keepdims=True)

        dvnv = dvn[...]
        acc_ref[5:6, :] += _colsum(dvnv * v_hat)
        acc_ref[6:7, :] += _colsum(dvnv)
        dv = _layer_norm_bwd(dvnv * slg_ref[...], v_hat, v_rstd)

        dsa = dya * s_gate
        dagate = dya * sa * _dsilu(a_gate, sig_ag)
        dln = dsa * _dsilu(ln_a, sig_ln)
        acc_ref[3:4, :] += _colsum(dln * ln_hat)
        acc_ref[4:5, :] += _colsum(dln)
        dcv = _layer_norm_bwd(dln * clg_ref[...], ln_hat, ln_rstd)
        acc_ref[2:3, :] += _colsum(dcv)
        dcv_ref[...] = dcv

        dzr_ref[:, 0:D] = dagate.astype(BF16)
        dzr_ref[:, D:2 * D] = du.astype(BF16)
        dzr_ref[:, 2 * D:3 * D] = dv.astype(BF16)
        dzr_ref[:, 3 * D:4 * D] = dbg.astype(BF16)

    row = lambda w: pl.BlockSpec((t, w), lambda i: (i, 0))
    const = lambda shape: pl.BlockSpec(shape, lambda i: (0,) * len(shape))
    return pl.pallas_call(
        body, name="mix_and_head", grid=(n_steps,),
        out_shape=(jax.ShapeDtypeStruct((s, D), F32),
                   jax.ShapeDtypeStruct((s, D), F32),
                   jax.ShapeDtypeStruct((s, 4 * D), BF16),
                   jax.ShapeDtypeStruct((8, D), F32),
                   jax.ShapeDtypeStruct((HEADS, CHUNK, CHUNK), F32),
                   jax.ShapeDtypeStruct((CHUNK, HEADS), F32),
                   jax.ShapeDtypeStruct((N_DEV, W_OUT_BLK, D), BF16)),
        in_specs=[row(D_IN), prev_spec, next_spec, row(D), row(D)] + [VM] * 12,
        out_specs=(row(D), row(D), row(4 * D),
                   const((8, D)), const((HEADS, CHUNK, CHUNK)), const((CHUNK, HEADS)), VM),
        scratch_shapes=[pltpu.VMEM((t + 2 * HALO, D), F32), pltpu.VMEM((t, D), F32), pltpu.VMEM((t, D), F32),
                        pltpu.VMEM((t, D), F32), pltpu.VMEM((2 * t, D_MIX), BF16), pltpu.VMEM((2 * t, D), BF16),
                        pltpu.VMEM((D_MIX, D), F32), pltpu.VMEM((CHUNK, D), F32),
                        pltpu.VMEM((SUB, t + SUB, CONV_BLK), F32)],
        compiler_params=_params(1),
    )(z, z, z, x, tgt, mod, conv_w_g, conv_b, cln_g, cln_b, sln_g, sln_b, final_g, ws_b, wst_b, bs_full, wout)


def _bwd_in_proj(z, dcv, dz_rest, x, dx2, mod, norm_g, conv_w_g, win_g):
    s = x.shape[0]
    t = ROW_TILE
    n_steps = s // t
    zp_spec, zn_spec = _halo_specs(t, s, 2 * D)
    dp_spec, dn_spec = _halo_specs(t, s, D)

    def body(z_ref, zp_ref, zn_ref, dcv_ref, dcvp_ref, dcvn_ref, dzr_ref, x_ref, dx2_ref, mod_ref, ng_ref, cw_ref, w_ref,
             gx_ref, dz_ref, acc_ref, gcw_ref,
             gext, dext, dg, taps, dpad, qbuf):
        i = pl.program_id(0)

        @pl.when(i == 0)
        def _():
            acc_ref[...] = jnp.zeros_like(acc_ref)
            gcw_ref[...] = jnp.zeros_like(gcw_ref)

        not_first = i > 0
        not_last = i < n_steps - 1
        gext[0:HALO, :] = jnp.where(not_first, _glu(zp_ref), 0.0)
        gext[HALO:HALO + t, :] = _glu(z_ref)
        gext[HALO + t:2 * HALO + t, :] = jnp.where(not_last, _glu(zn_ref), 0.0)
        dext[0:HALO, :] = jnp.where(not_first, dcvp_ref[...], 0.0)
        dext[HALO:HALO + t, :] = dcv_ref[...]
        dext[HALO + t:2 * HALO + t, :] = jnp.where(not_last, dcvn_ref[...], 0.0)

        taps[...] = jnp.zeros_like(taps)
        dpad[0:SUB, :] = jnp.zeros((SUB, D), F32)
        dpad[SUB:SUB + t, :] = dcv_ref[...]
        dpad[SUB + t:2 * SUB + t, :] = jnp.zeros((SUB, D), F32)
        for blk in range(N_DEV):
            cols = slice(blk * CONV_BLK, (blk + 1) * CONV_BLK)
            dg[:, cols] = _conv_taps(dext, qbuf, cols, lambda k, blk=blk: cw_ref[blk, CONV_W - 1 - k:CONV_W - k, :], t)
            for b in range(SUB):
                dshift = dpad[pl.ds(SUB - 1 - b, t + SUB), cols]
                for a in range((CONV_W - b + SUB - 1) // SUB):
                    k = SUB * a + b
                    taps[k:k + 1, :] = _colsum(gext[SUB * a:SUB * a + t + SUB, cols] * dshift)
            gcw_ref[blk] += taps[...]

        a = z_ref[:, 0:D]
        sig = _sigmoid(z_ref[:, D:2 * D])
        dgv = dg[...]
        dz_ref[:, 0:D] = (dgv * sig).astype(BF16)
        dz_ref[:, D:2 * D] = (dgv * a * sig * (1.0 - sig)).astype(BF16)
        dz_ref[:, 2 * D:6 * D] = dzr_ref[...]

        dh = jnp.zeros((t, D), F32)
        for j in range(N_DEV):
            dh = dh + lax.dot_general(dz_ref[:, j * W_IN_BLK:(j + 1) * W_IN_BLK], w_ref[j], (((1,), (1,)), ((), ())),
                                      preferred_element_type=F32)

        xn, r, _, scale = _rms_modulate(x_ref[...], mod_ref)
        ng = ng_ref[...]
        one_scale = 1.0 + scale
        dh_xn = dh * xn
        acc_ref[0:1, :] += _colsum(dh_xn * one_scale)
        acc_ref[1:2, :] += _colsum(dh)
        acc_ref[2:3, :] += _colsum(dh_xn * ng)
        dxn = dh * (ng * one_scale)
        gx_ref[...] = dx2_ref[...] + r * (dxn - xn * jnp.mean(dxn * xn, axis=-1, keepdims=True))

    row = lambda w: pl.BlockSpec((t, w), lambda i: (i, 0))
    const = lambda shape: pl.BlockSpec(shape, lambda i: (0,) * len(shape))
    return pl.pallas_call(
        body, name="bwd_in_proj", grid=(n_steps,),
        out_shape=(jax.ShapeDtypeStruct((s, D), F32), jax.ShapeDtypeStruct((s, D_IN), BF16),
                   jax.ShapeDtypeStruct((8, D), F32), jax.ShapeDtypeStruct((N_DEV, CONV_ROWS, CONV_BLK), F32)),
        in_specs=[pl.BlockSpec((t, 2 * D), lambda i: (i, 0)), zp_spec, zn_spec, row(D), dp_spec, dn_spec, row(4 * D),
                  row(D), row(D), VM, VM, VM, VM],
        out_specs=(row(D), row(D_IN), const((8, D)), const((N_DEV, CONV_ROWS, CONV_BLK))),
        scratch_shapes=[pltpu.VMEM((t + 2 * HALO, D), F32), pltpu.VMEM((t + 2 * HALO, D), F32), pltpu.VMEM((t, D), F32),
                        pltpu.VMEM((CONV_ROWS, CONV_BLK), F32), pltpu.VMEM((t + 2 * SUB, D), F32),
                        pltpu.VMEM((SUB, CONV_PHASE_ROWS, CONV_BLK), F32)],
        compiler_params=_params(1),
    )(z, z, z, dcv, dcv, dcv, dz_rest, x, dx2, mod, norm_g, conv_w_g, win_g)


WGRAD_K = 1024
WGRAD_ROLES = [(3, True), (3, False), (1, True), (1, False), (2, True), (2, False), (0, True), (0, False)]


def _wgrad_schedule():
    x, y, c = _mesh_pos()
    chip = [2 * x + y, 2 * x + (1 - y), 2 * (1 - x) + y, 2 * (1 - x) + (1 - y)]
    blocks = [2 * chip[r] + ((1 - c) if sibling else c) for r, sibling in WGRAD_ROLES]
    return jnp.stack([jnp.asarray(b, jnp.int32) for b in blocks])


def _sum_blocks(gathered):
    total = gathered[0]
    for b in range(1, N_DEV):
        total = total + gathered[b]
    return total


SMALL_ROWS = 48
ROW_NORM_G, ROW_FINAL_G, ROW_B_S, ROW_CONV_B, ROW_CLN_G, ROW_CLN_B, ROW_SLN_G, ROW_SLN_B, ROW_LOSS = range(32, 41)


def _wgrad_reduce(ht, dz, gout, small, gws, dmod):
    s = dz.shape[0]
    n_kc = s // WGRAD_K
    n_steps = len(WGRAD_ROLES)
    blk_in, blk_out = (D, W_IN_BLK), (W_OUT_BLK, D)

    def body(in_blk, ht_ref, dz_ref, gout_ref, small_ref, gws_ref, dmod_ref,
             oin_ref, oout_ref, osmall_ref, ogws_ref, odmod_ref, obada_ref, oloss_ref,
             acc, p1, l1_in, l1_out, l2_in, l2_out, small_g, gws_g, dmod_g,
             s1_send, s1_recv, s2_send, s2_recv, t1_send, t1_recv, t2_send, t2_recv, ag_send, ag_recv):
        step, kc = pl.program_id(0), pl.program_id(1)
        red_in = _ChipReduce(l1_in, l2_in, s1_send, s1_recv, s2_send, s2_recv)
        red_out = _ChipReduce(l1_out, l2_out, t1_send, t1_recv, t2_send, t2_recv)
        ag = _AllGather([dmod_g, small_g, gws_g], ag_send, ag_recv)
        last_kc = kc == n_kc - 1

        @pl.when((step == 0) & (kc == 0))
        def _():
            me = _my_block()
            small_g[me] = small_ref[...]
            gws_g[me] = gws_ref[...].astype(BF16)
            dmod_g[me] = dmod_ref[...]
            for a in range(3):
                ag.send_own(a)
            for r in (3, 1, 2, 0):
                red_out.to_sibling(r, gout_ref.at[red_out.block(r, True)]).start()

        @pl.when((step == 1) & (kc == 0))
        def _():
            for r in (3, 1, 2):
                red_out.combine(r, gout_ref[red_out.block(r, False)].astype(F32))

        @pl.when((step == 4) & (kc == 0))
        def _():
            for j in range(3):
                for a in range(3):
                    ag.arrived(a, 1 + j)
                    ag.pass_on(a, j)

        prod = jnp.dot(ht_ref[...], dz_ref[...], preferred_element_type=F32)

        @pl.when(kc == 0)
        def _():
            acc[...] = prod

        @pl.when(kc != 0)
        def _():
            acc[...] += prod

        for k, (r, sibling) in enumerate(WGRAD_ROLES):
            @pl.when((step == k) & last_kc)
            def _(r=r, sibling=sibling):
                if sibling:
                    p1[r] = acc[...].astype(BF16)
                    red_in.to_sibling(r, p1.at[r]).start()
                else:
                    chip_sum = red_in.combine(r, acc[...])
                    if r == 0:
                        oin_ref[...] = red_in.finish(chip_sum)

        @pl.when((step == n_steps - 1) & last_kc)
        def _():
            oout_ref[...] = red_out.finish(red_out.combine(0, gout_ref[red_out.block(0, False)].astype(F32)))
            for k in (0, 4, 5, 6):
                for a in range(3):
                    ag.arrived(a, k)
            for a in range(3):
                ag.sent(a)
            tot_small = _sum_blocks(small_g)
            osmall_ref[...] = tot_small
            oloss_ref[...] = jnp.full(oloss_ref.shape, (0.5 / D) * jnp.sum(tot_small[ROW_LOSS:ROW_LOSS + 1, :]), F32)
            tot_gws = gws_g[0].astype(F32)
            for b in range(1, N_DEV):
                tot_gws = tot_gws + gws_g[b].astype(F32)
            ogws_ref[...] = tot_gws
            obada_ref[...] = _sum_blocks(dmod_g)
            for b in range(N_DEV):
                odmod_ref[b:b + 1, :] = dmod_g[b, 0:1, :]

    sems = lambda n: [pltpu.SemaphoreType.DMA((n,)), pltpu.SemaphoreType.DMA((n,))]
    grid_spec = pltpu.PrefetchScalarGridSpec(
        num_scalar_prefetch=1, grid=(n_steps, n_kc),
        in_specs=[pl.BlockSpec((D, WGRAD_K), lambda i, kc, ib: (0, kc)),
                  pl.BlockSpec((WGRAD_K, W_IN_BLK), lambda i, kc, ib: (kc, ib[i])),
                  VM, VM, VM, VM],
        out_specs=(VM,) * 7,
        scratch_shapes=[pltpu.VMEM(blk_in, F32), pltpu.VMEM((4,) + blk_in, BF16),
                        pltpu.VMEM((4,) + blk_in, BF16), pltpu.VMEM((4,) + blk_out, BF16),
                        pltpu.VMEM((3,) + blk_in, BF16), pltpu.VMEM((3,) + blk_out, BF16),
                        pltpu.VMEM((N_DEV,) + small.shape, F32), pltpu.VMEM((N_DEV,) + gws.shape, BF16),
                        pltpu.VMEM((N_DEV,) + dmod.shape, F32)]
        + sems(4) + sems(3) + sems(4) + sems(3) + sems(21))
    return pl.pallas_call(
        body, name="wgrad_reduce", grid_spec=grid_spec,
        out_shape=(jax.ShapeDtypeStruct(blk_in, F32), jax.ShapeDtypeStruct(blk_out, F32),
                   jax.ShapeDtypeStruct(small.shape, F32), jax.ShapeDtypeStruct(gws.shape, F32),
                   jax.ShapeDtypeStruct((N_DEV, 3 * D), F32), jax.ShapeDtypeStruct((8, 3 * D), F32),
                   jax.ShapeDtypeStruct((8, 128), F32)),
        compiler_params=_params(2),
    )(_wgrad_schedule(), ht, dz, gout, small, gws, dmod)


def _adamw_math(w, g, m, v):
    m = ADAM_B1 * m + (1.0 - ADAM_B1) * g
    v = ADAM_B2 * v + (1.0 - ADAM_B2) * (g * g)
    m_hat = m / (1.0 - ADAM_B1 ** ADAM_STEP)
    v_hat = v / (1.0 - ADAM_B2 ** ADAM_STEP)
    delta = -ADAM_LR * (m_hat / (jnp.sqrt(v_hat) + ADAM_EPS) + ADAM_WD * w)
    return delta, m, v


VECTORS = ["norm_g", "conv_b", "conv_ln_g", "conv_ln_b", "sg_ln_g", "sg_ln_b", "final_g", "b_s"]
VECTOR_ROWS = [ROW_NORM_G, ROW_CONV_B, ROW_CLN_G, ROW_CLN_B, ROW_SLN_G, ROW_SLN_B, ROW_FINAL_G, ROW_B_S]
ADAM_STEPS = 4


def _adamw_all(me, g_w_in, g_w_out, g_w_s, small_sum, dmod_all, bada8, c_all, matrices, b_ada, conv_w, vectors):
    mat_shapes = [(D, W_IN_BLK), (D, W_ADA_BLK), (W_OUT_BLK, D), (HEADS * CHUNK, CHUNK)]
    mat_blocks = [(sh[0] // ADAM_STEPS, sh[1]) for sh in mat_shapes]
    n_small = 2 + len(VECTORS)

    def body(me_ref, gin_ref, gout_ref, gws_ref, taps_ref, small_ref, dmc_ref, bada_ref, c_ref, *refs):
        params = refs[:3 * (4 + n_small)]
        outs = refs[3 * (4 + n_small):-2]
        act, gada = refs[-2:]
        mat_out, gada_out, small_out = outs[:12], outs[12], outs[13:]
        i = pl.program_id(0)

        @pl.when(i == 0)
        def _():
            for b in range(N_DEV):
                cb = c_ref[b, 0:1, :]
                act[b:b + 1, :] = cb * _sigmoid(cb)
            gada[...] = lax.dot_general(act[...], dmc_ref[...], (((0,), (0,)), ((), ())), preferred_element_type=F32,
                                        precision=lax.Precision.HIGHEST)
            small_grads = [bada_ref[0:1, :], taps_ref[0:CONV_W, :]] + [small_ref[r:r + 1, :] for r in VECTOR_ROWS]
            for k, g in enumerate(small_grads):
                w_ref, m_ref, v_ref = params[3 * (4 + k):3 * (5 + k)]
                o = small_out[4 * k:4 * k + 4]
                o[0][...] = g
                o[1][...], o[2][...], o[3][...] = _adamw_math(w_ref[...], g, m_ref[...], v_ref[...])

        rows = pl.ds(pl.multiple_of(i * mat_blocks[1][0], mat_blocks[1][0]), mat_blocks[1][0])
        g_ada = gada[rows, :]
        gada_out[...] = g_ada
        for k, g in enumerate([gin_ref[...], g_ada, gout_ref[...], gws_ref[...]]):
            w_ref, m_ref, v_ref = params[3 * k:3 * k + 3]
            o = mat_out[3 * k:3 * k + 3]
            o[0][...], o[1][...], o[2][...] = _adamw_math(w_ref[...], g, m_ref[...], v_ref[...])

    rows_of = lambda blk: pl.BlockSpec(blk, lambda i, me_ref: (i, 0))
    mat_specs = [rows_of(b) for b in mat_blocks]
    grid_spec = pltpu.PrefetchScalarGridSpec(
        num_scalar_prefetch=1, grid=(ADAM_STEPS,),
        in_specs=[mat_specs[0], mat_specs[2], mat_specs[3],
                  pl.BlockSpec((CONV_ROWS, CONV_BLK), lambda i, me_ref: (0, me_ref[0])), VM,
                  pl.BlockSpec((N_DEV, W_ADA_BLK), lambda i, me_ref: (0, me_ref[0])), VM, VM]
        + [s for s in mat_specs for _ in range(3)] + [VM] * (3 * n_small),
        out_specs=tuple([s for s in mat_specs for _ in range(3)] + [mat_specs[1]] + [VM] * (4 * n_small)),
        scratch_shapes=[pltpu.VMEM((N_DEV, D), F32), pltpu.VMEM((D, W_ADA_BLK), F32)])
    small_shapes = [b_ada[0].shape, conv_w[0].shape] + [(1, D)] * len(VECTORS)
    out_shape = tuple([jax.ShapeDtypeStruct(sh, F32) for sh in mat_shapes for _ in range(3)]
                      + [jax.ShapeDtypeStruct(mat_shapes[1], F32)]
                      + [jax.ShapeDtypeStruct(sh, F32) for sh in small_shapes for _ in range(4)])
    flat = [a for group in matrices for a in group] + list(b_ada) + list(conv_w) + [a for group in vectors for a in group]
    return pl.pallas_call(body, name="adamw_all", grid_spec=grid_spec, out_shape=out_shape,
                          compiler_params=_params(1))(
        me, g_w_in, g_w_out, g_w_s, small_sum, small_sum, dmod_all, bada8, c_all, *flat)


def kernel(x, c, w_ada, b_ada, norm_g, w_in, conv_w, conv_b, conv_ln_g, conv_ln_b, sg_ln_g, sg_ln_b, w_s, b_s, w_out, final_g, loss_target, m_w_ada, m_b_ada, m_norm_g, m_w_in, m_conv_w, m_conv_b, m_conv_ln_g, m_conv_ln_b, m_sg_ln_g, m_sg_ln_b, m_w_s, m_b_s, m_w_out, m_final_g, v_w_ada, v_b_ada, v_norm_g, v_w_in, v_conv_w, v_conv_b, v_conv_ln_g, v_conv_ln_b, v_sg_ln_g, v_sg_ln_b, v_w_s, v_b_s, v_w_out, v_final_g):
    me = 4 * lax.axis_index("x") + 2 * lax.axis_index("y") + lax.axis_index("c")
    x2d, tgt2d = x[0], loss_target[0]
    row1 = lambda a: a.reshape(1, D)
    taps = lambda a: jnp.pad(a.reshape(CONV_W, CONV_BLK), ((0, CONV_ROWS - CONV_W), (0, 0)))

    z, ht, win_g, wout_g, cw_g, c_all, mod = _fwd_in_proj(
        x2d, jnp.broadcast_to(c, (8, D)), w_ada[0], b_ada, norm_g, w_in[0], w_out[0], taps(conv_w))
    ws_b = w_s[0].astype(BF16)
    wst_b = jnp.swapaxes(w_s[0], 1, 2).astype(BF16)
    bs_full = jnp.repeat(b_s[0].T, HEAD_DIM, axis=1)

    dx2, dcv, dz_rest, acc_a, gws, gbs, gout_part = _mix_and_head(
        z, x2d, tgt2d, mod, cw_g, conv_b, conv_ln_g, conv_ln_b, sg_ln_g, sg_ln_b, row1(final_g), ws_b, wst_b, bs_full,
        wout_g.reshape(D_MIX, D))
    grad_x, dz, acc_b, gcw = _bwd_in_proj(z, dcv, dz_rest, x2d, dx2, mod, norm_g, cw_g, win_g)

    small = jnp.concatenate(
        [jnp.transpose(gcw, (1, 0, 2)).reshape(CONV_ROWS, D), acc_b[0:1], acc_a[0:1], gbs.T.reshape(1, D), acc_a[2:8],
         jnp.zeros((SMALL_ROWS - ROW_LOSS - 1, D), F32)], axis=0)
    dmod_row = jnp.concatenate([acc_b[1:2], acc_b[2:3], acc_a[1:2]], axis=1)
    g_w_in, g_w_out, small_sum, g_w_s, dmod_all, bada8, loss_tile = _wgrad_reduce(
        ht, dz, gout_part, small, gws.reshape(HEADS * CHUNK, CHUNK), jnp.broadcast_to(dmod_row, (8, 3 * D)))

    given = dict(w_ada=(w_ada, m_w_ada, v_w_ada), b_ada=(b_ada, m_b_ada, v_b_ada), norm_g=(norm_g, m_norm_g, v_norm_g),
                 w_in=(w_in, m_w_in, v_w_in), conv_w=(conv_w, m_conv_w, v_conv_w), conv_b=(conv_b, m_conv_b, v_conv_b),
                 conv_ln_g=(conv_ln_g, m_conv_ln_g, v_conv_ln_g), conv_ln_b=(conv_ln_b, m_conv_ln_b, v_conv_ln_b),
                 sg_ln_g=(sg_ln_g, m_sg_ln_g, v_sg_ln_g), sg_ln_b=(sg_ln_b, m_sg_ln_b, v_sg_ln_b),
                 w_s=(w_s, m_w_s, v_w_s), b_s=(b_s, m_b_s, v_b_s), w_out=(w_out, m_w_out, v_w_out),
                 final_g=(final_g, m_final_g, v_final_g))
    as2d = lambda name, shape: tuple(a.reshape(shape) for a in given[name])
    res = _adamw_all(
        jnp.reshape(me, (1,)).astype(jnp.int32), g_w_in, g_w_out, g_w_s, small_sum, dmod_all, bada8, c_all,
        [as2d("w_in", (D, W_IN_BLK)), as2d("w_ada", (D, W_ADA_BLK)), as2d("w_out", (W_OUT_BLK, D)),
         as2d("w_s", (HEADS * CHUNK, CHUNK))],
        given["b_ada"], as2d("conv_w", (CONV_W, CONV_BLK)), [as2d(n, (1, D)) for n in VECTORS])
    out = {}
    for k, name in enumerate(["w_in", "w_ada", "w_out", "w_s"]):
        out[name] = [None] + list(res[3 * k:3 * k + 3])
    out["w_in"][0], out["w_ada"][0], out["w_out"][0], out["w_s"][0] = g_w_in, res[12], g_w_out, g_w_s
    for k, name in enumerate(["b_ada", "conv_w"] + VECTORS):
        out[name] = list(res[13 + 4 * k:17 + 4 * k])
    order = ["w_ada", "b_ada", "norm_g", "w_in", "conv_w", "conv_b", "conv_ln_g", "conv_ln_b", "sg_ln_g", "sg_ln_b",
             "w_s", "b_s", "w_out", "final_g"]
    outs = [loss_tile[0, 0], grad_x.reshape(x.shape)]
    for kind in range(4):
        outs += [out[n][kind].reshape(given[n][0].shape) for n in order]
    return tuple(outs)
```

```python
import functools

import jax
import jax.numpy as jnp
from jax import lax
from jax.experimental import pallas as pl
from jax.experimental.pallas import tpu as pltpu

F32 = jnp.float32
BF16 = jnp.bfloat16
MESH = pl.DeviceIdType.MESH

D = 1024
D_IN = 6 * D
D_MIX = 2 * D
N_DEV = 8
W_IN_BLK = D_IN // N_DEV
W_OUT_BLK = D_MIX // N_DEV
W_ADA_BLK = 3 * D // N_DEV
CONV_BLK = D // N_DEV
CONV_W = 31
CONV_HALF = CONV_W // 2
CONV_ROWS = 32
HALO = 16
CHUNK = 128
HEADS = 8
HEAD_DIM = 128
EPS = 1e-6
ROW_TILE = 256
VMEM_LIMIT = 56 * 1024 * 1024

ADAM_LR = 0.001
ADAM_B1 = 0.9
ADAM_B2 = 0.999
ADAM_EPS = 1e-08
ADAM_WD = 0.01
ADAM_STEP = 10

VM = pl.BlockSpec(memory_space=pltpu.VMEM)


def _params(grid_rank=0, **kw):
    sem = ("arbitrary",) * grid_rank if grid_rank else None
    return pltpu.CompilerParams(dimension_semantics=sem, vmem_limit_bytes=VMEM_LIMIT, **kw)


def _sigmoid(t):
    return jax.nn.sigmoid(t)


def _dsilu(t, sig):
    return sig * (1.0 + t * (1.0 - sig))


def _mesh_pos():
    return lax.axis_index("x"), lax.axis_index("y"), lax.axis_index("c")


def _neighbours():
    x, y, c = _mesh_pos()
    nb1 = (x + c * (1 - 2 * x), y + (1 - c) * (1 - 2 * y))
    nb2 = (x + (1 - c) * (1 - 2 * x), y + c * (1 - 2 * y))
    return (x, y, c), (x, y, 1 - c), [nb1, nb2, (1 - x, 1 - y)], c


def _sibling_slot(j):
    return j if j == 2 else 1 - j


class _AllGather:
    def __init__(self, bufs, send_sems, recv_sems):
        self.bufs, self.send_sems, self.recv_sems = bufs, send_sems, recv_sems
        self.me, self.sib, self.chips, self.c = _neighbours()

    def _copy(self, a, k, block, to):
        px, py, pc = block
        ref = self.bufs[a].at[4 * px + 2 * py + pc]
        return pltpu.make_async_remote_copy(
            src_ref=ref, dst_ref=ref, send_sem=self.send_sems.at[7 * a + k], recv_sem=self.recv_sems.at[7 * a + k],
            device_id=to, device_id_type=MESH)

    def _outgoing(self, a, k):
        if k == 0:
            return self._copy(a, 0, self.me, self.sib)
        if k <= 2:
            return self._copy(a, k, self.me, (*self.chips[k - 1], self.c))
        if k == 3:
            return self._copy(a, 3, (*self.chips[0], self.c), (*self.chips[1], self.c))
        return self._copy(a, k, (*self.chips[_sibling_slot(k - 4)], self.c), self.sib)

    def source(self, k):
        if k == 0:
            return self.sib
        return (*self.chips[(k - 1) % 3], self.c if k <= 3 else 1 - self.c)

    def block_index(self, k):
        px, py, pc = self.source(k)
        return 4 * px + 2 * py + pc

    def send_own(self, a):
        for k in range(3):
            self._outgoing(a, k).start()

    def arrived(self, a, k):
        self._copy(a, k, self.source(k), self.me).wait_recv()

    def relay(self, a):
        self._outgoing(a, 3).start()

    def pass_on(self, a, j):
        self._outgoing(a, 4 + _sibling_slot(j)).start()

    def other_chips(self, a):
        self.arrived(a, 1)
        self.relay(a)
        self.pass_on(a, 0)
        for j in (1, 2):
            self.arrived(a, 1 + j)
            self.pass_on(a, j)

    def from_sibling(self, a):
        for k in (0, 4, 5, 6):
            self.arrived(a, k)

    def sent(self, a):
        for k in range(7):
            self._outgoing(a, k).wait_send()


def _shard_index(chip_of, c, r, sibling):
    if sibling and r:
        r = 1 + _sibling_slot(r - 1)
    cx, cy = chip_of[r]
    return 4 * cx + 2 * cy + ((1 - c) if sibling else c)


class _ChipReduce:
    def __init__(self, l1, l2, s1_send, s1_recv, s2_send, s2_recv):
        self.l1, self.l2 = l1, l2
        self.s1_send, self.s1_recv, self.s2_send, self.s2_recv = s1_send, s1_recv, s2_send, s2_recv
        me, self.sib, chips, self.c = _neighbours()
        self.chip_of = [me[:2]] + chips

    def block(self, r, sibling):
        return _shard_index(self.chip_of, self.c, r, sibling)

    def to_sibling(self, r, src=None):
        return pltpu.make_async_remote_copy(
            src_ref=self.l1.at[r] if src is None else src, dst_ref=self.l1.at[r], send_sem=self.s1_send.at[r],
            recv_sem=self.s1_recv.at[r], device_id=self.sib, device_id_type=MESH)

    def to_chip(self, r):
        target = self.chip_of[1] if r == 3 else self.chip_of[r]
        return pltpu.make_async_remote_copy(
            src_ref=self.l1.at[r], dst_ref=self.l2.at[r - 1], send_sem=self.s2_send.at[r - 1],
            recv_sem=self.s2_recv.at[r - 1], device_id=(*target, self.c), device_id_type=MESH)

    def combine(self, r, mine):
        self.to_sibling(r).wait_recv()
        both = mine + self.l1[r].astype(F32)
        if r == 0:
            return both
        if r == 2:
            self.to_chip(3).wait_recv()
            both = both + self.l2[2].astype(F32)
        self.l1[r] = both.astype(BF16)
        self.to_chip(r).start()
        return None

    def finish(self, own_chip_sum):
        total = own_chip_sum
        for r in (1, 2):
            self.to_chip(r).wait_recv()
            total = total + self.l2[r - 1].astype(F32)
        for r in range(4):
            self.to_sibling(r).wait_send()
        for r in (1, 2, 3):
            self.to_chip(r).wait_send()
        return total


def _my_block():
    x, y, c = _mesh_pos()
    return 4 * x + 2 * y + c


def _modulation(c_g, w_ref, b_ref, mod_ref, part, land, send_sems, recv_sems):
    x, y, c = _mesh_pos()
    me = 4 * x + 2 * y + c
    w = w_ref[...]
    for b in range(N_DEV):
        cb = c_g[b]
        part[b] = jnp.dot(cb * _sigmoid(cb), w, preferred_element_type=F32, precision=lax.Precision.HIGHEST)
    land[me] = part[me]

    def copy(b):
        return pltpu.make_async_remote_copy(
            src_ref=part.at[b], dst_ref=land.at[me], send_sem=send_sems.at[b], recv_sem=recv_sems.at[me],
            device_id=(b // 4, (b // 2) % 2, b % 2), device_id_type=MESH)

    def arrival(b):
        return pltpu.make_async_remote_copy(
            src_ref=part.at[b], dst_ref=land.at[b], send_sem=send_sems.at[b], recv_sem=recv_sems.at[b],
            device_id=(b // 4, (b // 2) % 2, b % 2), device_id_type=MESH)

    for b in range(N_DEV):
        @pl.when(b != me)
        def _():
            copy(b).start()
    for b in range(N_DEV):
        @pl.when(b != me)
        def _():
            arrival(b).wait_recv()
            copy(b).wait_send()
    for b in range(N_DEV):
        cols = slice(b * W_ADA_BLK, (b + 1) * W_ADA_BLK)
        mod_ref[:, cols] = land[b] + b_ref[:, cols]


def _rms_modulate(x, mod_ref):
    shift = mod_ref[0:1, 0:D]
    scale = mod_ref[0:1, D:2 * D]
    r = lax.rsqrt(jnp.mean(x * x, axis=-1, keepdims=True) + EPS)
    xn = x * r
    return xn, r, shift, scale


FWD_TILE = 512
FWD_Z_TILE = 1024


def _fwd_in_proj(x, c_rep, w_ada, b_ada, norm_g, w_in, w_out, conv_w):
    s = x.shape[0]
    t, tz = FWD_TILE, FWD_Z_TILE
    n_tiles = s // t
    C_PAY, WIN_PAY, WOUT_PAY, CW_PAY = 0, 1, 2, 3

    def body(x_hbm, c_ref, wada_ref, bada_ref, ng_ref, win_ref, wout_ref, cw_ref,
             z_hbm, ht_ref, win_g, wout_g, cw_g, c_g, mod_ref,
             h, xbuf, zbuf, part, land, ag_send, ag_recv, mod_send, mod_recv, x_sem, z_sem):
        me = _my_block()
        c_g[me] = c_ref[...]
        ag = _AllGather([c_g, win_g, wout_g, cw_g], ag_send, ag_recv)
        ag.send_own(C_PAY)

        def x_copy(i):
            return pltpu.make_async_copy(x_hbm.at[pl.ds(i * t, t), :], xbuf.at[i % 2], x_sem.at[i % 2])

        x_copy(0).start()
        win_g[me] = win_ref[...].astype(BF16)
        wout_g[me] = wout_ref[...].astype(BF16)
        cw_g[me] = cw_ref[...]

        ag.other_chips(C_PAY)
        ag.from_sibling(C_PAY)
        _modulation(c_g, wada_ref, bada_ref, mod_ref, part, land, mod_send, mod_recv)
        for a in (WIN_PAY, WOUT_PAY, CW_PAY):
            ag.send_own(a)

        for i in range(n_tiles):
            x_copy(i).wait()
            if i + 1 < n_tiles:
                x_copy(i + 1).start()
            xn, _, shift, scale = _rms_modulate(xbuf[i % 2], mod_ref)
            hh = xn * ng_ref[...] * (1.0 + scale) + shift
            h[i * t:(i + 1) * t, :] = hh.astype(BF16)
            ht_ref[:, i * t:(i + 1) * t] = hh.T.astype(BF16)

        def z_copy(slot, row0, col0):
            return pltpu.make_async_copy(zbuf.at[slot], z_hbm.at[pl.ds(row0, tz), pl.ds(col0, W_IN_BLK)], z_sem.at[slot])

        done = [0]

        def z_block(blk):
            col0 = pl.multiple_of(blk * W_IN_BLK, 128)
            for i in range(s // tz):
                slot = done[0] % 2
                if done[0] >= 2:
                    z_copy(slot, 0, 0).wait()
                zbuf[slot] = jnp.dot(h[i * tz:(i + 1) * tz, :], win_g[blk], preferred_element_type=F32)
                z_copy(slot, i * tz, col0).start()
                done[0] += 1

        z_block(me)
        ag.arrived(WIN_PAY, 0)
        z_block(ag.block_index(0))
        ag.arrived(WIN_PAY, 1)
        ag.relay(WIN_PAY)
        ag.pass_on(WIN_PAY, 0)
        ag.arrived(WIN_PAY, 2)
        ag.pass_on(WIN_PAY, 1)
        for k in (1, 2, 4, 5):
            if k >= 4:
                ag.arrived(WIN_PAY, k)
            z_block(ag.block_index(k))
        ag.arrived(WIN_PAY, 3)
        ag.pass_on(WIN_PAY, 2)
        z_block(ag.block_index(3))
        ag.arrived(WIN_PAY, 6)
        z_block(ag.block_index(6))

        for a in (WOUT_PAY, CW_PAY):
            ag.other_chips(a)
        for a in (WOUT_PAY, CW_PAY):
            ag.from_sibling(a)
        for a in (C_PAY, WIN_PAY, WOUT_PAY, CW_PAY):
            ag.sent(a)
        z_copy(0, 0, 0).wait()
        z_copy(1, 0, 0).wait()

    any_spec = pl.BlockSpec(memory_space=pl.ANY)
    return pl.pallas_call(
        body, name="fwd_in_proj",
        out_shape=(jax.ShapeDtypeStruct((s, D_IN), F32), jax.ShapeDtypeStruct((D, s), BF16),
                   jax.ShapeDtypeStruct((N_DEV,) + w_in.shape, BF16), jax.ShapeDtypeStruct((N_DEV,) + w_out.shape, BF16),
                   jax.ShapeDtypeStruct((N_DEV,) + conv_w.shape, F32), jax.ShapeDtypeStruct((N_DEV,) + c_rep.shape, F32),
                   jax.ShapeDtypeStruct((8, 3 * D), F32)),
        in_specs=[any_spec] + [VM] * 7, out_specs=(any_spec,) + (VM,) * 6,
        scratch_shapes=[pltpu.VMEM((s, D), BF16), pltpu.VMEM((2, t, D), F32), pltpu.VMEM((2, tz, W_IN_BLK), F32),
                        pltpu.VMEM((N_DEV, 8, W_ADA_BLK), F32), pltpu.VMEM((N_DEV, 8, W_ADA_BLK), F32),
                        pltpu.SemaphoreType.DMA((28,)), pltpu.SemaphoreType.DMA((28,)),
                        pltpu.SemaphoreType.DMA((N_DEV,)), pltpu.SemaphoreType.DMA((N_DEV,)),
                        pltpu.SemaphoreType.DMA((2,)), pltpu.SemaphoreType.DMA((2,))],
        compiler_params=_params(),
    )(x, c_rep, w_ada, b_ada, norm_g, w_in, w_out, conv_w)


def _halo_specs(t, s, width):
    per = t // HALO
    last = s // HALO - 1
    prev = pl.BlockSpec((HALO, width), lambda i: (jnp.maximum(i * per - 1, 0), 0))
    nxt = pl.BlockSpec((HALO, width), lambda i: (jnp.minimum((i + 1) * per, last), 0))
    return prev, nxt


def _glu(ref):
    return ref[:, 0:D] * _sigmoid(ref[:, D:2 * D])


SUB = 8
CONV_PHASE_ROWS = ROW_TILE + SUB


def _conv_taps(ext, qbuf, cols, tap, t):
    out = None
    for b in range(SUB):
        q = None
        for a in range((CONV_W - b + SUB - 1) // SUB):
            term = ext[SUB * a:SUB * a + t + SUB, cols] * tap(SUB * a + b)
            q = term if q is None else q + term
        qbuf[b] = q
        shifted = qbuf[b, pl.ds(b + 1, t), :]
        out = shifted if out is None else out + shifted
    return out


def _layer_norm_stats(v):
    mu = jnp.mean(v, axis=-1, keepdims=True)
    cen = v - mu
    rstd = lax.rsqrt(jnp.mean(cen * cen, axis=-1, keepdims=True) + EPS)
    return cen * rstd, rstd


def _layer_norm_bwd(dy_hat, hat, rstd):
    m1 = jnp.mean(dy_hat, axis=-1, keepdims=True)
    m2 = jnp.mean(dy_hat * hat, axis=-1, keepdims=True)
    return rstd * (dy_hat - m1 - hat * m2)


def _colsum(v):
    return jnp.sum(v, axis=0, keepdims=True)


def _mix_and_head(z, x, tgt, mod, conv_w_g, conv_b, cln_g, cln_b, sln_g, sln_b, final_g, ws_b, wst_b, bs_full, wout):
    s = x.shape[0]
    t = ROW_TILE
    n_chunks = t // CHUNK
    n_steps = s // t
    prev_spec, next_spec = _halo_specs(t, s, 2 * D)

    def body(z_ref, zp_ref, zn_ref, x_ref, tgt_ref, mod_ref, cw_ref, cb_ref, clg_ref, clb_ref, slg_ref, slb_ref, fg_ref,
             ws_ref, wst_ref, bs_ref, wout_ref,
             dx2_ref, ycatt_ref, dy_ref, dcv_ref, dzr_ref, acc_ref, gws_ref, gbs_ref,
             gext, cv, vs, dvn, ycat, gbs_acc, qbuf):
        i = pl.program_id(0)

        @pl.when(i == 0)
        def _():
            acc_ref[...] = jnp.zeros_like(acc_ref)
            gws_ref[...] = jnp.zeros_like(gws_ref)
            gbs_acc[...] = jnp.zeros_like(gbs_acc)

        gext[0:HALO, :] = jnp.where(i > 0, _glu(zp_ref), 0.0)
        gext[HALO:HALO + t, :] = _glu(z_ref)
        gext[HALO + t:2 * HALO + t, :] = jnp.where(i < n_steps - 1, _glu(zn_ref), 0.0)
        for blk in range(N_DEV):
            cols = slice(blk * CONV_BLK, (blk + 1) * CONV_BLK)
            cv[:, cols] = _conv_taps(gext, qbuf, cols, lambda k, blk=blk: cw_ref[blk, k:k + 1, :], t) + cb_ref[:, cols]
        ln_hat, ln_rstd = _layer_norm_stats(cv[...])
        ln_a = ln_hat * clg_ref[...] + clb_ref[...]
        sig_ln = _sigmoid(ln_a)
        sa = ln_a * sig_ln
        a_gate = z_ref[:, 2 * D:3 * D]
        sig_ag = _sigmoid(a_gate)
        s_gate = a_gate * sig_ag
        ya = sa * s_gate

        v_hat, v_rstd = _layer_norm_stats(z_ref[:, 4 * D:5 * D])
        vn = v_hat * slg_ref[...] + slb_ref[...]
        vnb = vn.astype(BF16)
        for n in range(n_chunks):
            rows = slice(n * CHUNK, (n + 1) * CHUNK)
            for h in range(HEADS):
                cols = slice(h * HEAD_DIM, (h + 1) * HEAD_DIM)
                vs[rows, cols] = jnp.dot(ws_ref[h], vnb[rows, cols], preferred_element_type=F32) + bs_ref[:, cols]
        u = z_ref[:, 3 * D:4 * D]
        b_gate = z_ref[:, 5 * D:6 * D]
        sig_bg = _sigmoid(b_gate)
        s_bg = b_gate * sig_bg
        vsv = vs[...]
        yb = u * vsv * s_bg

        ycat[:, 0:D] = ya.astype(BF16)
        ycat[:, D:2 * D] = yb.astype(BF16)
        ycatt_ref[0:D, :] = ya.T.astype(BF16)
        ycatt_ref[D:2 * D, :] = yb.T.astype(BF16)
        y = jnp.dot(ycat[...], wout_ref[...], preferred_element_type=F32)
        gate = mod_ref[0:1, 2 * D:3 * D]
        x2 = x_ref[...] + gate * y
        r2 = lax.rsqrt(jnp.mean(x2 * x2, axis=-1, keepdims=True) + EPS)
        x2n = x2 * r2
        fg = fg_ref[...]
        diff = x2n * fg - tgt_ref[...]
        acc_ref[7:8, :] += _colsum(diff * diff)
        dout = diff * (1.0 / D)
        acc_ref[0:1, :] += _colsum(dout * x2n)
        dx2n = dout * fg
        dx2 = r2 * (dx2n - x2n * jnp.mean(dx2n * x2n, axis=-1, keepdims=True))
        dx2_ref[...] = dx2
        acc_ref[1:2, :] += _colsum(dx2 * y)
        dyb16 = (dx2 * gate).astype(BF16)
        dy_ref[...] = dyb16
        dycat = lax.dot_general(dyb16, wout_ref[...], (((1,), (1,)), ((), ())), preferred_element_type=F32)
        dya = dycat[:, 0:D]
        dyb = dycat[:, D:2 * D]

        du = dyb * vsv * s_bg
        dvs = dyb * u * s_bg
        dbg = dyb * u * vsv * _dsilu(b_gate, sig_bg)
        dvsb = dvs.astype(BF16)
        gbs = gbs_acc[...]
        for n in range(n_chunks):
            rows = slice(n * CHUNK, (n + 1) * CHUNK)
            gbs = gbs + dvs[rows, :]
            for h in range(HEADS):
                cols = slice(h * HEAD_DIM, (h + 1) * HEAD_DIM)
                gws_ref[h] += lax.dot_general(dvsb[rows, cols], vnb[rows, cols], (((1,), (1,)), ((), ())),
                                              preferred_element_type=F32)
                dvn[rows, cols] = jnp.dot(wst_ref[h], dvsb[rows, cols], preferred_element_type=F32)
        gbs_acc[...] = gbs

        @pl.when(i == n_steps - 1)
        def _():
            for h in range(HEADS):
                gbs_ref[:, h:h + 1] = jnp.sum(gbs_acc[:, h * HEAD_DIM:(h + 1) * HEAD_DIM], axis=1, keepdims=True)

        dvnv = dvn[...]
        acc_ref[5:6, :] += _colsum(dvnv * v_hat)
        acc_ref[6:7, :] += _colsum(dvnv)
        dv = _layer_norm_bwd(dvnv * slg_ref[...], v_hat, v_rstd)

        dsa = dya * s_gate
        dagate = dya * sa * _dsilu(a_gate, sig_ag)
        dln = dsa * _dsilu(ln_a, sig_ln)
        acc_ref[3:4, :] += _colsum(dln * ln_hat)
        acc_ref[4:5, :] += _colsum(dln)
        dcv = _layer_norm_bwd(dln * clg_ref[...], ln_hat, ln_rstd)
        acc_ref[2:3, :] += _colsum(dcv)
        dcv_ref[...] = dcv

        dzr_ref[:, 0:D] = dagate.astype(BF16)
        dzr_ref[:, D:2 * D] = du.astype(BF16)
        dzr_ref[:, 2 * D:3 * D] = dv.astype(BF16)
        dzr_ref[:, 3 * D:4 * D] = dbg.astype(BF16)

    row = lambda w: pl.BlockSpec((t, w), lambda i: (i, 0))
    const = lambda shape: pl.BlockSpec(shape, lambda i: (0,) * len(shape))
    return pl.pallas_call(
        body, name="mix_and_head", grid=(n_steps,),
        out_shape=(jax.ShapeDtypeStruct((s, D), F32),
                   jax.ShapeDtypeStruct((D_MIX, s), BF16),
                   jax.ShapeDtypeStruct((s, D), BF16),
                   jax.ShapeDtypeStruct((s, D), F32),
                   jax.ShapeDtypeStruct((s, 4 * D), BF16),
                   jax.ShapeDtypeStruct((8, D), F32),
                   jax.ShapeDtypeStruct((HEADS, CHUNK, CHUNK), F32),
                   jax.ShapeDtypeStruct((CHUNK, HEADS), F32)),
        in_specs=[row(D_IN), prev_spec, next_spec, row(D), row(D)] + [VM] * 12,
        out_specs=(row(D), pl.BlockSpec((D_MIX, t), lambda i: (0, i)), row(D), row(D), row(4 * D),
                   const((8, D)), const((HEADS, CHUNK, CHUNK)), const((CHUNK, HEADS))),
        scratch_shapes=[pltpu.VMEM((t + 2 * HALO, D), F32), pltpu.VMEM((t, D), F32), pltpu.VMEM((t, D), F32),
                        pltpu.VMEM((t, D), F32), pltpu.VMEM((t, D_MIX), BF16), pltpu.VMEM((CHUNK, D), F32),
                        pltpu.VMEM((SUB, CONV_PHASE_ROWS, CONV_BLK), F32)],
        compiler_params=_params(1),
    )(z, z, z, x, tgt, mod, conv_w_g, conv_b, cln_g, cln_b, sln_g, sln_b, final_g, ws_b, wst_b, bs_full, wout)


def _bwd_in_proj(z, dcv, dz_rest, x, dx2, mod, norm_g, conv_w_g, win_g):
    s = x.shape[0]
    t = ROW_TILE
    n_steps = s // t
    zp_spec, zn_spec = _halo_specs(t, s, 2 * D)
    dp_spec, dn_spec = _halo_specs(t, s, D)

    def body(z_ref, zp_ref, zn_ref, dcv_ref, dcvp_ref, dcvn_ref, dzr_ref, x_ref, dx2_ref, mod_ref, ng_ref, cw_ref, w_ref,
             gx_ref, dz_ref, acc_ref, gcw_ref,
             gext, dext, dg, taps, dpad, qbuf):
        i = pl.program_id(0)

        @pl.when(i == 0)
        def _():
            acc_ref[...] = jnp.zeros_like(acc_ref)
            gcw_ref[...] = jnp.zeros_like(gcw_ref)

        not_first = i > 0
        not_last = i < n_steps - 1
        gext[0:HALO, :] = jnp.where(not_first, _glu(zp_ref), 0.0)
        gext[HALO:HALO + t, :] = _glu(z_ref)
        gext[HALO + t:2 * HALO + t, :] = jnp.where(not_last, _glu(zn_ref), 0.0)
        dext[0:HALO, :] = jnp.where(not_first, dcvp_ref[...], 0.0)
        dext[HALO:HALO + t, :] = dcv_ref[...]
        dext[HALO + t:2 * HALO + t, :] = jnp.where(not_last, dcvn_ref[...], 0.0)

        taps[...] = jnp.zeros_like(taps)
        dpad[0:SUB, :] = jnp.zeros((SUB, D), F32)
        dpad[SUB:SUB + t, :] = dcv_ref[...]
        dpad[SUB + t:2 * SUB + t, :] = jnp.zeros((SUB, D), F32)
        for blk in range(N_DEV):
            cols = slice(blk * CONV_BLK, (blk + 1) * CONV_BLK)
            dg[:, cols] = _conv_taps(dext, qbuf, cols, lambda k, blk=blk: cw_ref[blk, CONV_W - 1 - k:CONV_W - k, :], t)
            for b in range(SUB):
                dshift = dpad[pl.ds(SUB - 1 - b, t + SUB), cols]
                for a in range((CONV_W - b + SUB - 1) // SUB):
                    k = SUB * a + b
                    taps[k:k + 1, :] = _colsum(gext[SUB * a:SUB * a + t + SUB, cols] * dshift)
            gcw_ref[blk] += taps[...]

        a = z_ref[:, 0:D]
        sig = _sigmoid(z_ref[:, D:2 * D])
        dgv = dg[...]
        dz_ref[:, 0:D] = (dgv * sig).astype(BF16)
        dz_ref[:, D:2 * D] = (dgv * a * sig * (1.0 - sig)).astype(BF16)
        dz_ref[:, 2 * D:6 * D] = dzr_ref[...]

        dh = jnp.zeros((t, D), F32)
        for j in range(N_DEV):
            dh = dh + lax.dot_general(dz_ref[:, j * W_IN_BLK:(j + 1) * W_IN_BLK], w_ref[j], (((1,), (1,)), ((), ())),
                                      preferred_element_type=F32)

        xn, r, _, scale = _rms_modulate(x_ref[...], mod_ref)
        ng = ng_ref[...]
        one_scale = 1.0 + scale
        dh_xn = dh * xn
        acc_ref[0:1, :] += _colsum(dh_xn * one_scale)
        acc_ref[1:2, :] += _colsum(dh)
        acc_ref[2:3, :] += _colsum(dh_xn * ng)
        dxn = dh * (ng * one_scale)
        gx_ref[...] = dx2_ref[...] + r * (dxn - xn * jnp.mean(dxn * xn, axis=-1, keepdims=True))

    row = lambda w: pl.BlockSpec((t, w), lambda i: (i, 0))
    const = lambda shape: pl.BlockSpec(shape, lambda i: (0,) * len(shape))
    return pl.pallas_call(
        body, name="bwd_in_proj", grid=(n_steps,),
        out_shape=(jax.ShapeDtypeStruct((s, D), F32), jax.ShapeDtypeStruct((s, D_IN), BF16),
                   jax.ShapeDtypeStruct((8, D), F32), jax.ShapeDtypeStruct((N_DEV, CONV_ROWS, CONV_BLK), F32)),
        in_specs=[pl.BlockSpec((t, 2 * D), lambda i: (i, 0)), zp_spec, zn_spec, row(D), dp_spec, dn_spec, row(4 * D),
                  row(D), row(D), VM, VM, VM, VM],
        out_specs=(row(D), row(D_IN), const((8, D)), const((N_DEV, CONV_ROWS, CONV_BLK))),
        scratch_shapes=[pltpu.VMEM((t + 2 * HALO, D), F32), pltpu.VMEM((t + 2 * HALO, D), F32), pltpu.VMEM((t, D), F32),
                        pltpu.VMEM((CONV_ROWS, CONV_BLK), F32), pltpu.VMEM((t + 2 * SUB, D), F32),
                        pltpu.VMEM((SUB, CONV_PHASE_ROWS, CONV_BLK), F32)],
        compiler_params=_params(1),
    )(z, z, z, dcv, dcv, dcv, dz_rest, x, dx2, mod, norm_g, conv_w_g, win_g)


WGRAD_K = 1024
_OTHER_CHIPS = [(3, True), (3, False), (1, True), (1, False), (2, True), (2, False)]
_OWN_CHIP = [(0, True), (0, False)]
WGRAD_ROLES = ([("out",) + e for e in _OTHER_CHIPS] + [("in",) + e for e in _OTHER_CHIPS]
               + [("out",) + e for e in _OWN_CHIP] + [("in",) + e for e in _OWN_CHIP])


def _wgrad_schedule():
    me, _, chips, c = _neighbours()
    chip_of = [me[:2]] + chips
    blk = lambda r, sibling: _shard_index(chip_of, c, r, sibling)
    out_blk, in_blk, is_out = [], [], []
    last = {"out": blk(*_OTHER_CHIPS[0]), "in": blk(*_OTHER_CHIPS[0])}
    for kind, r, sibling in WGRAD_ROLES:
        last[kind] = blk(r, sibling)
        out_blk.append(last["out"])
        in_blk.append(last["in"])
        is_out.append(1 if kind == "out" else 0)
    as_vec = lambda v: jnp.stack([jnp.asarray(e, jnp.int32) for e in v])
    return as_vec(out_blk), as_vec(in_blk), as_vec(is_out)


def _sum_blocks(gathered):
    total = gathered[0]
    for b in range(1, N_DEV):
        total = total + gathered[b]
    return total


SMALL_ROWS = 48
ROW_NORM_G, ROW_FINAL_G, ROW_B_S, ROW_CONV_B, ROW_CLN_G, ROW_CLN_B, ROW_SLN_G, ROW_SLN_B, ROW_LOSS = range(32, 41)


def _wgrad_reduce(ht, dz, ycatt, dy, small, gws, dmod):
    s = dz.shape[0]
    n_kc = s // WGRAD_K
    n_steps = len(WGRAD_ROLES)
    first_in = [k for k, role in enumerate(WGRAD_ROLES) if role[0] == "in"][0]
    blk_in, blk_out = (D, W_IN_BLK), (W_OUT_BLK, D)

    def body(out_blk, in_blk, is_out, ht_ref, dz_ref, yt_ref, dy_ref, small_ref, gws_ref, dmod_ref,
             oin_ref, oout_ref, osmall_ref, ogws_ref, odmod_ref, obada_ref, oloss_ref,
             acc_in, acc_out, p1_in, p1_out, l1_in, l1_out, l2_in, l2_out, small_g, gws_g, dmod_g,
             s1_send, s1_recv, s2_send, s2_recv, t1_send, t1_recv, t2_send, t2_recv, ag_send, ag_recv):
        step, kc = pl.program_id(0), pl.program_id(1)
        pay = {"in": (acc_in, p1_in, oin_ref, _ChipReduce(l1_in, l2_in, s1_send, s1_recv, s2_send, s2_recv)),
               "out": (acc_out, p1_out, oout_ref, _ChipReduce(l1_out, l2_out, t1_send, t1_recv, t2_send, t2_recv))}
        ag = _AllGather([dmod_g, small_g, gws_g], ag_send, ag_recv)
        last_kc = kc == n_kc - 1

        @pl.when((step == 0) & (kc == 0))
        def _():
            me = _my_block()
            small_g[me] = small_ref[...]
            gws_g[me] = gws_ref[...].astype(BF16)
            dmod_g[me] = dmod_ref[...]
            for a in range(3):
                ag.send_own(a)

        @pl.when((step == 4) & (kc == 0))
        def _():
            for a in range(3):
                ag.other_chips(a)

        def accumulate(acc, prod):
            @pl.when(kc == 0)
            def _():
                acc[...] = prod

            @pl.when(kc != 0)
            def _():
                acc[...] += prod

        @pl.when(is_out[step] == 1)
        def _():
            accumulate(acc_out, jnp.dot(yt_ref[...], dy_ref[...], preferred_element_type=F32))

        @pl.when(is_out[step] == 0)
        def _():
            accumulate(acc_in, jnp.dot(ht_ref[...], dz_ref[...], preferred_element_type=F32))

        for k, (kind, r, sibling) in enumerate(WGRAD_ROLES):
            @pl.when((step == k) & last_kc)
            def _(kind=kind, r=r, sibling=sibling):
                acc, p1, out, red = pay[kind]
                if sibling:
                    p1[r] = acc[...].astype(BF16)
                    red.to_sibling(r, p1.at[r]).start()
                else:
                    chip_sum = red.combine(r, acc[...])
                    if r == 0:
                        out[...] = chip_sum

        @pl.when((step == n_steps - 1) & last_kc)
        def _():
            for kind in ("out", "in"):
                _, _, out, red = pay[kind]
                out[...] = red.finish(out[...])
            for a in range(3):
                ag.from_sibling(a)
            for a in range(3):
                ag.sent(a)
            tot_small = _sum_blocks(small_g)
            osmall_ref[...] = tot_small
            oloss_ref[...] = jnp.full(oloss_ref.shape, (0.5 / D) * jnp.sum(tot_small[ROW_LOSS:ROW_LOSS + 1, :]), F32)
            tot_gws = gws_g[0].astype(F32)
            for b in range(1, N_DEV):
                tot_gws = tot_gws + gws_g[b].astype(F32)
            ogws_ref[...] = tot_gws
            obada_ref[...] = _sum_blocks(dmod_g)
            for b in range(N_DEV):
                odmod_ref[b:b + 1, :] = dmod_g[b, 0:1, :]

    def kc_of(working, step, kc, hold_first):
        held = jnp.where(step < hold_first, 0, n_kc - 1)
        return jnp.where(working, kc, held)

    out_kc = lambda i, kc, ob, ib, io: kc_of(io[i] == 1, i, kc, 0)
    in_kc = lambda i, kc, ob, ib, io: kc_of(io[i] == 0, i, kc, first_in)
    sems = lambda n: [pltpu.SemaphoreType.DMA((n,)), pltpu.SemaphoreType.DMA((n,))]
    grid_spec = pltpu.PrefetchScalarGridSpec(
        num_scalar_prefetch=3, grid=(n_steps, n_kc),
        in_specs=[pl.BlockSpec((D, WGRAD_K), lambda i, kc, ob, ib, io: (0, in_kc(i, kc, ob, ib, io))),
                  pl.BlockSpec((WGRAD_K, W_IN_BLK), lambda i, kc, ob, ib, io: (in_kc(i, kc, ob, ib, io), ib[i])),
                  pl.BlockSpec((W_OUT_BLK, WGRAD_K), lambda i, kc, ob, ib, io: (ob[i], out_kc(i, kc, ob, ib, io))),
                  pl.BlockSpec((WGRAD_K, D), lambda i, kc, ob, ib, io: (out_kc(i, kc, ob, ib, io), 0)),
                  VM, VM, VM],
        out_specs=(VM,) * 7,
        scratch_shapes=[pltpu.VMEM(blk_in, F32), pltpu.VMEM(blk_out, F32),
                        pltpu.VMEM((4,) + blk_in, BF16), pltpu.VMEM((4,) + blk_out, BF16),
                        pltpu.VMEM((4,) + blk_in, BF16), pltpu.VMEM((4,) + blk_out, BF16),
                        pltpu.VMEM((3,) + blk_in, BF16), pltpu.VMEM((3,) + blk_out, BF16),
                        pltpu.VMEM((N_DEV,) + small.shape, F32), pltpu.VMEM((N_DEV,) + gws.shape, BF16),
                        pltpu.VMEM((N_DEV,) + dmod.shape, F32)]
        + sems(4) + sems(3) + sems(4) + sems(3) + sems(21))
    return pl.pallas_call(
        body, name="wgrad_reduce", grid_spec=grid_spec,
        out_shape=(jax.ShapeDtypeStruct(blk_in, F32), jax.ShapeDtypeStruct(blk_out, F32),
                   jax.ShapeDtypeStruct(small.shape, F32), jax.ShapeDtypeStruct(gws.shape, F32),
                   jax.ShapeDtypeStruct((N_DEV, 3 * D), F32), jax.ShapeDtypeStruct((8, 3 * D), F32),
                   jax.ShapeDtypeStruct((8, 128), F32)),
        compiler_params=_params(2),
    )(*_wgrad_schedule(), ht, dz, ycatt, dy, small, gws, dmod)


def _adamw_math(w, g, m, v):
    m = ADAM_B1 * m + (1.0 - ADAM_B1) * g
    v = ADAM_B2 * v + (1.0 - ADAM_B2) * (g * g)
    m_hat = m / (1.0 - ADAM_B1 ** ADAM_STEP)
    v_hat = v / (1.0 - ADAM_B2 ** ADAM_STEP)
    delta = -ADAM_LR * (m_hat / (jnp.sqrt(v_hat) + ADAM_EPS) + ADAM_WD * w)
    return delta, m, v


VECTORS = ["norm_g", "conv_b", "conv_ln_g", "conv_ln_b", "sg_ln_g", "sg_ln_b", "final_g", "b_s"]
VECTOR_ROWS = [ROW_NORM_G, ROW_CONV_B, ROW_CLN_G, ROW_CLN_B, ROW_SLN_G, ROW_SLN_B, ROW_FINAL_G, ROW_B_S]
ADAM_STEPS = 4


def _adamw_all(me, g_w_in, g_w_out, g_w_s, small_sum, dmod_all, bada8, c_all, matrices, b_ada, conv_w, vectors):
    mat_shapes = [(D, W_IN_BLK), (D, W_ADA_BLK), (W_OUT_BLK, D), (HEADS * CHUNK, CHUNK)]
    mat_blocks = [(sh[0] // ADAM_STEPS, sh[1]) for sh in mat_shapes]
    n_small = 2 + len(VECTORS)

    def body(me_ref, gin_ref, gout_ref, gws_ref, taps_ref, small_ref, dmc_ref, bada_ref, c_ref, *refs):
        params = refs[:3 * (4 + n_small)]
        outs = refs[3 * (4 + n_small):-2]
        act, gada = refs[-2:]
        mat_out, gada_out, small_out = outs[:12], outs[12], outs[13:]
        i = pl.program_id(0)

        @pl.when(i == 0)
        def _():
            for b in range(N_DEV):
                cb = c_ref[b, 0:1, :]
                act[b:b + 1, :] = cb * _sigmoid(cb)
            gada[...] = lax.dot_general(act[...], dmc_ref[...], (((0,), (0,)), ((), ())), preferred_element_type=F32,
                                        precision=lax.Precision.HIGHEST)
            small_grads = [bada_ref[0:1, :], taps_ref[0:CONV_W, :]] + [small_ref[r:r + 1, :] for r in VECTOR_ROWS]
            for k, g in enumerate(small_grads):
                w_ref, m_ref, v_ref = params[3 * (4 + k):3 * (5 + k)]
                o = small_out[4 * k:4 * k + 4]
                o[0][...] = g
                o[1][...], o[2][...], o[3][...] = _adamw_math(w_ref[...], g, m_ref[...], v_ref[...])

        rows = pl.ds(pl.multiple_of(i * mat_blocks[1][0], mat_blocks[1][0]), mat_blocks[1][0])
        g_ada = gada[rows, :]
        gada_out[...] = g_ada
        for k, g in enumerate([gin_ref[...], g_ada, gout_ref[...], gws_ref[...]]):
            w_ref, m_ref, v_ref = params[3 * k:3 * k + 3]
            o = mat_out[3 * k:3 * k + 3]
            o[0][...], o[1][...], o[2][...] = _adamw_math(w_ref[...], g, m_ref[...], v_ref[...])

    rows_of = lambda blk: pl.BlockSpec(blk, lambda i, me_ref: (i, 0))
    mat_specs = [rows_of(b) for b in mat_blocks]
    grid_spec = pltpu.PrefetchScalarGridSpec(
        num_scalar_prefetch=1, grid=(ADAM_STEPS,),
        in_specs=[mat_specs[0], mat_specs[2], mat_specs[3],
                  pl.BlockSpec((CONV_ROWS, CONV_BLK), lambda i, me_ref: (0, me_ref[0])), VM,
                  pl.BlockSpec((N_DEV, W_ADA_BLK), lambda i, me_ref: (0, me_ref[0])), VM, VM]
        + [s for s in mat_specs for _ in range(3)] + [VM] * (3 * n_small),
        out_specs=tuple([s for s in mat_specs for _ in range(3)] + [mat_specs[1]] + [VM] * (4 * n_small)),
        scratch_shapes=[pltpu.VMEM((N_DEV, D), F32), pltpu.VMEM((D, W_ADA_BLK), F32)])
    small_shapes = [b_ada[0].shape, conv_w[0].shape] + [(1, D)] * len(VECTORS)
    out_shape = tuple([jax.ShapeDtypeStruct(sh, F32) for sh in mat_shapes for _ in range(3)]
                      + [jax.ShapeDtypeStruct(mat_shapes[1], F32)]
                      + [jax.ShapeDtypeStruct(sh, F32) for sh in small_shapes for _ in range(4)])
    flat = [a for group in matrices for a in group] + list(b_ada) + list(conv_w) + [a for group in vectors for a in group]
    return pl.pallas_call(body, name="adamw_all", grid_spec=grid_spec, out_shape=out_shape,
                          compiler_params=_params(1))(
        me, g_w_in, g_w_out, g_w_s, small_sum, small_sum, dmod_all, bada8, c_all, *flat)


def kernel(x, c, w_ada, b_ada, norm_g, w_in, conv_w, conv_b, conv_ln_g, conv_ln_b, sg_ln_g, sg_ln_b, w_s, b_s, w_out, final_g, loss_target, m_w_ada, m_b_ada, m_norm_g, m_w_in, m_conv_w, m_conv_b, m_conv_ln_g, m_conv_ln_b, m_sg_ln_g, m_sg_ln_b, m_w_s, m_b_s, m_w_out, m_final_g, v_w_ada, v_b_ada, v_norm_g, v_w_in, v_conv_w, v_conv_b, v_conv_ln_g, v_conv_ln_b, v_sg_ln_g, v_sg_ln_b, v_w_s, v_b_s, v_w_out, v_final_g):
    me = 4 * lax.axis_index("x") + 2 * lax.axis_index("y") + lax.axis_index("c")
    x2d, tgt2d = x[0], loss_target[0]
    row1 = lambda a: a.reshape(1, D)
    taps = lambda a: jnp.pad(a.reshape(CONV_W, CONV_BLK), ((0, CONV_ROWS - CONV_W), (0, 0)))

    z, ht, win_g, wout_g, cw_g, c_all, mod = _fwd_in_proj(
        x2d, jnp.broadcast_to(c, (8, D)), w_ada[0], b_ada, norm_g, w_in[0], w_out[0], taps(conv_w))
    ws_b = w_s[0].astype(BF16)
    wst_b = jnp.swapaxes(w_s[0], 1, 2).astype(BF16)
    bs_full = jnp.repeat(b_s[0].T, HEAD_DIM, axis=1)

    dx2, ycatt, dy, dcv, dz_rest, acc_a, gws, gbs = _mix_and_head(
        z, x2d, tgt2d, mod, cw_g, conv_b, conv_ln_g, conv_ln_b, sg_ln_g, sg_ln_b, row1(final_g), ws_b, wst_b, bs_full,
        wout_g.reshape(D_MIX, D))
    grad_x, dz, acc_b, gcw = _bwd_in_proj(z, dcv, dz_rest, x2d, dx2, mod, norm_g, cw_g, win_g)

    small = jnp.concatenate(
        [jnp.transpose(gcw, (1, 0, 2)).reshape(CONV_ROWS, D), acc_b[0:1], acc_a[0:1], gbs.T.reshape(1, D), acc_a[2:8],
         jnp.zeros((SMALL_ROWS - ROW_LOSS - 1, D), F32)], axis=0)
    dmod_row = jnp.concatenate([acc_b[1:2], acc_b[2:3], acc_a[1:2]], axis=1)
    g_w_in, g_w_out, small_sum, g_w_s, dmod_all, bada8, loss_tile = _wgrad_reduce(
        ht, dz, ycatt, dy, small, gws.reshape(HEADS * CHUNK, CHUNK), jnp.broadcast_to(dmod_row, (8, 3 * D)))

    given = dict(w_ada=(w_ada, m_w_ada, v_w_ada), b_ada=(b_ada, m_b_ada, v_b_ada), norm_g=(norm_g, m_norm_g, v_norm_g),
                 w_in=(w_in, m_w_in, v_w_in), conv_w=(conv_w, m_conv_w, v_conv_w), conv_b=(conv_b, m_conv_b, v_conv_b),
                 conv_ln_g=(conv_ln_g, m_conv_ln_g, v_conv_ln_g), conv_ln_b=(conv_ln_b, m_conv_ln_b, v_conv_ln_b),
                 sg_ln_g=(sg_ln_g, m_sg_ln_g, v_sg_ln_g), sg_ln_b=(sg_ln_b, m_sg_ln_b, v_sg_ln_b),
                 w_s=(w_s, m_w_s, v_w_s), b_s=(b_s, m_b_s, v_b_s), w_out=(w_out, m_w_out, v_w_out),
                 final_g=(final_g, m_final_g, v_final_g))
    as2d = lambda name, shape: tuple(a.reshape(shape) for a in given[name])
    res = _adamw_all(
        jnp.reshape(me, (1,)).astype(jnp.int32), g_w_in, g_w_out, g_w_s, small_sum, dmod_all, bada8, c_all,
        [as2d("w_in", (D, W_IN_BLK)), as2d("w_ada", (D, W_ADA_BLK)), as2d("w_out", (W_OUT_BLK, D)),
         as2d("w_s", (HEADS * CHUNK, CHUNK))],
        given["b_ada"], as2d("conv_w", (CONV_W, CONV_BLK)), [as2d(n, (1, D)) for n in VECTORS])
    out = {}
    for k, name in enumerate(["w_in", "w_ada", "w_out", "w_s"]):
        out[name] = [None] + list(res[3 * k:3 * k + 3])
    out["w_in"][0], out["w_ada"][0], out["w_out"][0], out["w_s"][0] = g_w_in, res[12], g_w_out, g_w_s
    for k, name in enumerate(["b_ada", "conv_w"] + VECTORS):
        out[name] = list(res[13 + 4 * k:17 + 4 * k])
    order = ["w_ada", "b_ada", "norm_g", "w_in", "conv_w", "conv_b", "conv_ln_g", "conv_ln_b", "sg_ln_g", "sg_ln_b",
             "w_s", "b_s", "w_out", "final_g"]
    outs = [loss_tile[0, 0], grad_x.reshape(x.shape)]
    for kind in range(4):
        outs += [out[n][kind].reshape(given[n][0].shape) for n in order]
    return tuple(outs)
```

```python
import functools

import jax
import jax.numpy as jnp
from jax import lax
from jax.experimental import pallas as pl
from jax.experimental.pallas import tpu as pltpu

F32 = jnp.float32
BF16 = jnp.bfloat16
MESH = pl.DeviceIdType.MESH

D = 1024
D_IN = 6 * D
D_MIX = 2 * D
N_DEV = 8
W_IN_BLK = D_IN // N_DEV
W_OUT_BLK = D_MIX // N_DEV
W_ADA_BLK = 3 * D // N_DEV
CONV_BLK = D // N_DEV
CONV_W = 31
CONV_HALF = CONV_W // 2
CONV_ROWS = 32
HALO = 16
CHUNK = 128
HEADS = 8
HEAD_DIM = 128
EPS = 1e-6
ROW_TILE = 256
VMEM_LIMIT = 56 * 1024 * 1024

ADAM_LR = 0.001
ADAM_B1 = 0.9
ADAM_B2 = 0.999
ADAM_EPS = 1e-08
ADAM_WD = 0.01
ADAM_STEP = 10

VM = pl.BlockSpec(memory_space=pltpu.VMEM)


def _params(grid_rank=0, vmem=VMEM_LIMIT):
    sem = ("arbitrary",) * grid_rank if grid_rank else None
    return pltpu.CompilerParams(dimension_semantics=sem, vmem_limit_bytes=vmem)


def _sigmoid(t):
    return jax.nn.sigmoid(t)


def _dsilu(t, sig):
    return sig * (1.0 + t * (1.0 - sig))


def _mesh_pos():
    return lax.axis_index("x"), lax.axis_index("y"), lax.axis_index("c")


def _neighbours():
    x, y, c = _mesh_pos()
    nb1 = (x + c * (1 - 2 * x), y + (1 - c) * (1 - 2 * y))
    nb2 = (x + (1 - c) * (1 - 2 * x), y + c * (1 - 2 * y))
    return (x, y, c), (x, y, 1 - c), [nb1, nb2, (1 - x, 1 - y)], c


def _sibling_slot(j):
    return j if j == 2 else 1 - j


class _AllGather:
    def __init__(self, bufs, send_sems, recv_sems):
        self.bufs, self.send_sems, self.recv_sems = bufs, send_sems, recv_sems
        self.me, self.sib, self.chips, self.c = _neighbours()

    def _copy(self, a, k, block, to):
        px, py, pc = block
        ref = self.bufs[a].at[4 * px + 2 * py + pc]
        return pltpu.make_async_remote_copy(
            src_ref=ref, dst_ref=ref, send_sem=self.send_sems.at[7 * a + k], recv_sem=self.recv_sems.at[7 * a + k],
            device_id=to, device_id_type=MESH)

    def _outgoing(self, a, k):
        if k == 0:
            return self._copy(a, 0, self.me, self.sib)
        if k <= 2:
            return self._copy(a, k, self.me, (*self.chips[k - 1], self.c))
        if k == 3:
            return self._copy(a, 3, (*self.chips[0], self.c), (*self.chips[1], self.c))
        return self._copy(a, k, (*self.chips[_sibling_slot(k - 4)], self.c), self.sib)

    def source(self, k):
        if k == 0:
            return self.sib
        return (*self.chips[(k - 1) % 3], self.c if k <= 3 else 1 - self.c)

    def block_index(self, k):
        px, py, pc = self.source(k)
        return 4 * px + 2 * py + pc

    def send_own(self, a):
        for k in range(3):
            self._outgoing(a, k).start()

    def arrived(self, a, k):
        self._copy(a, k, self.source(k), self.me).wait_recv()

    def relay(self, a):
        self._outgoing(a, 3).start()

    def pass_on(self, a, j):
        self._outgoing(a, 4 + _sibling_slot(j)).start()

    def other_chips(self, a):
        self.arrived(a, 1)
        self.relay(a)
        self.pass_on(a, 0)
        for j in (1, 2):
            self.arrived(a, 1 + j)
            self.pass_on(a, j)

    def from_sibling(self, a):
        for k in (0, 4, 5, 6):
            self.arrived(a, k)

    def sent(self, a):
        for k in range(7):
            self._outgoing(a, k).wait_send()


def _shard_index(chip_of, c, r, sibling):
    if sibling and r:
        r = 1 + _sibling_slot(r - 1)
    cx, cy = chip_of[r]
    return 4 * cx + 2 * cy + ((1 - c) if sibling else c)


class _ChipReduce:
    def __init__(self, l1, l2, s1_send, s1_recv, s2_send, s2_recv):
        self.l1, self.l2 = l1, l2
        self.s1_send, self.s1_recv, self.s2_send, self.s2_recv = s1_send, s1_recv, s2_send, s2_recv
        me, self.sib, chips, self.c = _neighbours()
        self.chip_of = [me[:2]] + chips

    def block(self, r, sibling):
        return _shard_index(self.chip_of, self.c, r, sibling)

    def to_sibling(self, r, src=None):
        return pltpu.make_async_remote_copy(
            src_ref=self.l1.at[r] if src is None else src, dst_ref=self.l1.at[r], send_sem=self.s1_send.at[r],
            recv_sem=self.s1_recv.at[r], device_id=self.sib, device_id_type=MESH)

    def to_chip(self, r):
        target = self.chip_of[1] if r == 3 else self.chip_of[r]
        return pltpu.make_async_remote_copy(
            src_ref=self.l1.at[r], dst_ref=self.l2.at[r - 1], send_sem=self.s2_send.at[r - 1],
            recv_sem=self.s2_recv.at[r - 1], device_id=(*target, self.c), device_id_type=MESH)

    def combine(self, r, mine):
        self.to_sibling(r).wait_recv()
        both = mine + self.l1[r].astype(F32)
        if r == 0:
            return both
        if r == 2:
            self.to_chip(3).wait_recv()
            both = both + self.l2[2].astype(F32)
        self.l1[r] = both.astype(BF16)
        self.to_chip(r).start()
        return None

    def finish(self, own_chip_sum):
        total = own_chip_sum
        for r in (1, 2):
            self.to_chip(r).wait_recv()
            total = total + self.l2[r - 1].astype(F32)
        for r in range(4):
            self.to_sibling(r).wait_send()
        for r in (1, 2, 3):
            self.to_chip(r).wait_send()
        return total


def _my_block():
    x, y, c = _mesh_pos()
    return 4 * x + 2 * y + c


def _modulation(c_g, w_ref, b_ref, mod_ref, part, land, send_sems, recv_sems):
    x, y, c = _mesh_pos()
    me = 4 * x + 2 * y + c
    w = w_ref[...]
    for b in range(N_DEV):
        cb = c_g[b]
        part[b] = jnp.dot(cb * _sigmoid(cb), w, preferred_element_type=F32, precision=lax.Precision.HIGHEST)
    land[me] = part[me]

    def copy(b):
        return pltpu.make_async_remote_copy(
            src_ref=part.at[b], dst_ref=land.at[me], send_sem=send_sems.at[b], recv_sem=recv_sems.at[me],
            device_id=(b // 4, (b // 2) % 2, b % 2), device_id_type=MESH)

    def arrival(b):
        return pltpu.make_async_remote_copy(
            src_ref=part.at[b], dst_ref=land.at[b], send_sem=send_sems.at[b], recv_sem=recv_sems.at[b],
            device_id=(b // 4, (b // 2) % 2, b % 2), device_id_type=MESH)

    for b in range(N_DEV):
        @pl.when(b != me)
        def _():
            copy(b).start()
    for b in range(N_DEV):
        @pl.when(b != me)
        def _():
            arrival(b).wait_recv()
            copy(b).wait_send()
    for b in range(N_DEV):
        cols = slice(b * W_ADA_BLK, (b + 1) * W_ADA_BLK)
        mod_ref[:, cols] = land[b] + b_ref[:, cols]


def _rms_modulate(x, mod_ref):
    shift = mod_ref[0:1, 0:D]
    scale = mod_ref[0:1, D:2 * D]
    r = lax.rsqrt(jnp.mean(x * x, axis=-1, keepdims=True) + EPS)
    xn = x * r
    return xn, r, shift, scale


FWD_TILE = 512
FWD_Z_TILE = 1024


def _fwd_in_proj(x, c_rep, w_ada, b_ada, norm_g, w_in, w_out, conv_w):
    s = x.shape[0]
    t, tz = FWD_TILE, FWD_Z_TILE
    n_tiles = s // t
    C_PAY, WIN_PAY, WOUT_PAY, CW_PAY = 0, 1, 2, 3

    def body(x_hbm, c_ref, wada_ref, bada_ref, ng_ref, win_ref, wout_ref, cw_ref,
             z_hbm, ht_ref, win_g, wout_g, cw_g, c_g, mod_ref,
             h, xbuf, zbuf, part, land, ag_send, ag_recv, mod_send, mod_recv, x_sem, z_sem):
        me = _my_block()
        c_g[me] = c_ref[...]
        ag = _AllGather([c_g, win_g, wout_g, cw_g], ag_send, ag_recv)
        ag.send_own(C_PAY)

        def x_copy(i):
            return pltpu.make_async_copy(x_hbm.at[pl.ds(i * t, t), :], xbuf.at[i % 2], x_sem.at[i % 2])

        x_copy(0).start()
        win_g[me] = win_ref[...].astype(BF16)
        wout_g[me] = wout_ref[...].astype(BF16)
        cw_g[me] = cw_ref[...]

        ag.other_chips(C_PAY)
        ag.from_sibling(C_PAY)
        _modulation(c_g, wada_ref, bada_ref, mod_ref, part, land, mod_send, mod_recv)
        for a in (WIN_PAY, WOUT_PAY, CW_PAY):
            ag.send_own(a)

        for i in range(n_tiles):
            x_copy(i).wait()
            if i + 1 < n_tiles:
                x_copy(i + 1).start()
            xn, _, shift, scale = _rms_modulate(xbuf[i % 2], mod_ref)
            hh = xn * ng_ref[...] * (1.0 + scale) + shift
            h[i * t:(i + 1) * t, :] = hh.astype(BF16)
            ht_ref[(i * t) // WGRAD_K, :, (i * t) % WGRAD_K:(i * t) % WGRAD_K + t] = hh.T.astype(BF16)

        def z_copy(slot, row0, col0):
            return pltpu.make_async_copy(zbuf.at[slot], z_hbm.at[pl.ds(row0, tz), pl.ds(col0, W_IN_BLK)], z_sem.at[slot])

        done = [0]

        def z_block(blk):
            col0 = pl.multiple_of(blk * W_IN_BLK, 128)
            for i in range(s // tz):
                slot = done[0] % 2
                if done[0] >= 2:
                    z_copy(slot, 0, 0).wait()
                zbuf[slot] = jnp.dot(h[i * tz:(i + 1) * tz, :], win_g[blk], preferred_element_type=F32)
                z_copy(slot, i * tz, col0).start()
                done[0] += 1

        z_block(me)
        ag.arrived(WIN_PAY, 0)
        z_block(ag.block_index(0))
        ag.arrived(WIN_PAY, 1)
        ag.relay(WIN_PAY)
        ag.pass_on(WIN_PAY, 0)
        ag.arrived(WIN_PAY, 2)
        ag.pass_on(WIN_PAY, 1)
        for k in (1, 2, 4, 5):
            if k >= 4:
                ag.arrived(WIN_PAY, k)
            z_block(ag.block_index(k))
        ag.arrived(WIN_PAY, 3)
        ag.pass_on(WIN_PAY, 2)
        z_block(ag.block_index(3))
        ag.arrived(WIN_PAY, 6)
        z_block(ag.block_index(6))

        for a in (WOUT_PAY, CW_PAY):
            ag.other_chips(a)
        for a in (WOUT_PAY, CW_PAY):
            ag.from_sibling(a)
        for a in (C_PAY, WIN_PAY, WOUT_PAY, CW_PAY):
            ag.sent(a)
        z_copy(0, 0, 0).wait()
        z_copy(1, 0, 0).wait()

    any_spec = pl.BlockSpec(memory_space=pl.ANY)
    return pl.pallas_call(
        body, name="fwd_in_proj",
        out_shape=(jax.ShapeDtypeStruct((s, D_IN), F32), jax.ShapeDtypeStruct((s // WGRAD_K, D, WGRAD_K), BF16),
                   jax.ShapeDtypeStruct((N_DEV,) + w_in.shape, BF16), jax.ShapeDtypeStruct((N_DEV,) + w_out.shape, BF16),
                   jax.ShapeDtypeStruct((N_DEV,) + conv_w.shape, F32), jax.ShapeDtypeStruct((N_DEV,) + c_rep.shape, F32),
                   jax.ShapeDtypeStruct((8, 3 * D), F32)),
        in_specs=[any_spec] + [VM] * 7, out_specs=(any_spec,) + (VM,) * 6,
        scratch_shapes=[pltpu.VMEM((s, D), BF16), pltpu.VMEM((2, t, D), F32), pltpu.VMEM((2, tz, W_IN_BLK), F32),
                        pltpu.VMEM((N_DEV, 8, W_ADA_BLK), F32), pltpu.VMEM((N_DEV, 8, W_ADA_BLK), F32),
                        pltpu.SemaphoreType.DMA((28,)), pltpu.SemaphoreType.DMA((28,)),
                        pltpu.SemaphoreType.DMA((N_DEV,)), pltpu.SemaphoreType.DMA((N_DEV,)),
                        pltpu.SemaphoreType.DMA((2,)), pltpu.SemaphoreType.DMA((2,))],
        compiler_params=_params(),
    )(x, c_rep, w_ada, b_ada, norm_g, w_in, w_out, conv_w)


def _halo_specs(t, s, width):
    per = t // HALO
    last = s // HALO - 1
    prev = pl.BlockSpec((HALO, width), lambda i: (jnp.maximum(i * per - 1, 0), 0))
    nxt = pl.BlockSpec((HALO, width), lambda i: (jnp.minimum((i + 1) * per, last), 0))
    return prev, nxt


def _glu(ref):
    return ref[:, 0:D] * _sigmoid(ref[:, D:2 * D])


SUB = 8
CONV_PHASE_ROWS = ROW_TILE + SUB


def _conv_taps(ext, qbuf, cols, tap, t):
    out = None
    for b in range(SUB):
        q = None
        for a in range((CONV_W - b + SUB - 1) // SUB):
            term = ext[SUB * a:SUB * a + t + SUB, cols] * tap(SUB * a + b)
            q = term if q is None else q + term
        qbuf[b] = q
        shifted = qbuf[b, pl.ds(b + 1, t), :]
        out = shifted if out is None else out + shifted
    return out


def _layer_norm_stats(v):
    mu = jnp.mean(v, axis=-1, keepdims=True)
    cen = v - mu
    rstd = lax.rsqrt(jnp.mean(cen * cen, axis=-1, keepdims=True) + EPS)
    return cen * rstd, rstd


def _layer_norm_bwd(dy_hat, hat, rstd):
    m1 = jnp.mean(dy_hat, axis=-1, keepdims=True)
    m2 = jnp.mean(dy_hat * hat, axis=-1, keepdims=True)
    return rstd * (dy_hat - m1 - hat * m2)


def _colsum(v):
    return jnp.sum(v, axis=0, keepdims=True)


def _mix_and_head(z, x, tgt, mod, conv_w_g, conv_b, cln_g, cln_b, sln_g, sln_b, final_g, ws_b, wst_b, bs_full, wout):
    s = x.shape[0]
    t = ROW_TILE
    n_chunks = t // CHUNK
    n_steps = s // t
    prev_spec, next_spec = _halo_specs(t, s, 2 * D)

    def body(z_ref, zp_ref, zn_ref, x_ref, tgt_ref, mod_ref, cw_ref, cb_ref, clg_ref, clb_ref, slg_ref, slb_ref, fg_ref,
             ws_ref, wst_ref, bs_ref, wout_ref,
             dx2_ref, ycatt_ref, dy_ref, dcv_ref, dzr_ref, acc_ref, gws_ref, gbs_ref,
             gext, cv, vs, dvn, ycat, gbs_acc, qbuf):
        i = pl.program_id(0)

        @pl.when(i == 0)
        def _():
            acc_ref[...] = jnp.zeros_like(acc_ref)
            gws_ref[...] = jnp.zeros_like(gws_ref)
            gbs_acc[...] = jnp.zeros_like(gbs_acc)

        gext[0:HALO, :] = jnp.where(i > 0, _glu(zp_ref), 0.0)
        gext[HALO:HALO + t, :] = _glu(z_ref)
        gext[HALO + t:2 * HALO + t, :] = jnp.where(i < n_steps - 1, _glu(zn_ref), 0.0)
        for blk in range(N_DEV):
            cols = slice(blk * CONV_BLK, (blk + 1) * CONV_BLK)
            cv[:, cols] = _conv_taps(gext, qbuf, cols, lambda k, blk=blk: cw_ref[blk, k:k + 1, :], t) + cb_ref[:, cols]
        ln_hat, ln_rstd = _layer_norm_stats(cv[...])
        ln_a = ln_hat * clg_ref[...] + clb_ref[...]
        sig_ln = _sigmoid(ln_a)
        sa = ln_a * sig_ln
        a_gate = z_ref[:, 2 * D:3 * D]
        sig_ag = _sigmoid(a_gate)
        s_gate = a_gate * sig_ag
        ya = sa * s_gate

        v_hat, v_rstd = _layer_norm_stats(z_ref[:, 4 * D:5 * D])
        vn = v_hat * slg_ref[...] + slb_ref[...]
        vnb = vn.astype(BF16)
        for n in range(n_chunks):
            rows = slice(n * CHUNK, (n + 1) * CHUNK)
            for h in range(HEADS):
                cols = slice(h * HEAD_DIM, (h + 1) * HEAD_DIM)
                vs[rows, cols] = jnp.dot(ws_ref[h], vnb[rows, cols], preferred_element_type=F32) + bs_ref[:, cols]
        u = z_ref[:, 3 * D:4 * D]
        b_gate = z_ref[:, 5 * D:6 * D]
        sig_bg = _sigmoid(b_gate)
        s_bg = b_gate * sig_bg
        vsv = vs[...]
        yb = u * vsv * s_bg

        ycat[:, 0:D] = ya.astype(BF16)
        ycat[:, D:2 * D] = yb.astype(BF16)
        ycatt_ref[0:D, :] = ya.T.astype(BF16)
        ycatt_ref[D:2 * D, :] = yb.T.astype(BF16)
        y = jnp.dot(ycat[...], wout_ref[...], preferred_element_type=F32)
        gate = mod_ref[0:1, 2 * D:3 * D]
        x2 = x_ref[...] + gate * y
        r2 = lax.rsqrt(jnp.mean(x2 * x2, axis=-1, keepdims=True) + EPS)
        x2n = x2 * r2
        fg = fg_ref[...]
        diff = x2n * fg - tgt_ref[...]
        acc_ref[7:8, :] += _colsum(diff * diff)
        dout = diff * (1.0 / D)
        acc_ref[0:1, :] += _colsum(dout * x2n)
        dx2n = dout * fg
        dx2 = r2 * (dx2n - x2n * jnp.mean(dx2n * x2n, axis=-1, keepdims=True))
        dx2_ref[...] = dx2
        acc_ref[1:2, :] += _colsum(dx2 * y)
        dyb16 = (dx2 * gate).astype(BF16)
        dy_ref[...] = dyb16
        dycat = lax.dot_general(dyb16, wout_ref[...], (((1,), (1,)), ((), ())), preferred_element_type=F32)
        dya = dycat[:, 0:D]
        dyb = dycat[:, D:2 * D]

        du = dyb * vsv * s_bg
        dvs = dyb * u * s_bg
        dbg = dyb * u * vsv * _dsilu(b_gate, sig_bg)
        dvsb = dvs.astype(BF16)
        gbs = gbs_acc[...]
        for n in range(n_chunks):
            rows = slice(n * CHUNK, (n + 1) * CHUNK)
            gbs = gbs + dvs[rows, :]
            for h in range(HEADS):
                cols = slice(h * HEAD_DIM, (h + 1) * HEAD_DIM)
                gws_ref[h] += lax.dot_general(dvsb[rows, cols], vnb[rows, cols], (((1,), (1,)), ((), ())),
                                              preferred_element_type=F32)
                dvn[rows, cols] = jnp.dot(wst_ref[h], dvsb[rows, cols], preferred_element_type=F32)
        gbs_acc[...] = gbs

        @pl.when(i == n_steps - 1)
        def _():
            for h in range(HEADS):
                gbs_ref[:, h:h + 1] = jnp.sum(gbs_acc[:, h * HEAD_DIM:(h + 1) * HEAD_DIM], axis=1, keepdims=True)

        dvnv = dvn[...]
        acc_ref[5:6, :] += _colsum(dvnv * v_hat)
        acc_ref[6:7, :] += _colsum(dvnv)
        dv = _layer_norm_bwd(dvnv * slg_ref[...], v_hat, v_rstd)

        dsa = dya * s_gate
        dagate = dya * sa * _dsilu(a_gate, sig_ag)
        dln = dsa * _dsilu(ln_a, sig_ln)
        acc_ref[3:4, :] += _colsum(dln * ln_hat)
        acc_ref[4:5, :] += _colsum(dln)
        dcv = _layer_norm_bwd(dln * clg_ref[...], ln_hat, ln_rstd)
        acc_ref[2:3, :] += _colsum(dcv)
        dcv_ref[...] = dcv

        dzr_ref[:, 0:D] = dagate.astype(BF16)
        dzr_ref[:, D:2 * D] = du.astype(BF16)
        dzr_ref[:, 2 * D:3 * D] = dv.astype(BF16)
        dzr_ref[:, 3 * D:4 * D] = dbg.astype(BF16)

    row = lambda w: pl.BlockSpec((t, w), lambda i: (i, 0))
    const = lambda shape: pl.BlockSpec(shape, lambda i: (0,) * len(shape))
    return pl.pallas_call(
        body, name="mix_and_head", grid=(n_steps,),
        out_shape=(jax.ShapeDtypeStruct((s, D), F32),
                   jax.ShapeDtypeStruct((D_MIX, s), BF16),
                   jax.ShapeDtypeStruct((s, D), BF16),
                   jax.ShapeDtypeStruct((s, D), F32),
                   jax.ShapeDtypeStruct((s, 4 * D), BF16),
                   jax.ShapeDtypeStruct((8, D), F32),
                   jax.ShapeDtypeStruct((HEADS, CHUNK, CHUNK), F32),
                   jax.ShapeDtypeStruct((CHUNK, HEADS), F32)),
        in_specs=[row(D_IN), prev_spec, next_spec, row(D), row(D)] + [VM] * 12,
        out_specs=(row(D), pl.BlockSpec((D_MIX, t), lambda i: (0, i)), row(D), row(D), row(4 * D),
                   const((8, D)), const((HEADS, CHUNK, CHUNK)), const((CHUNK, HEADS))),
        scratch_shapes=[pltpu.VMEM((t + 2 * HALO, D), F32), pltpu.VMEM((t, D), F32), pltpu.VMEM((t, D), F32),
                        pltpu.VMEM((t, D), F32), pltpu.VMEM((t, D_MIX), BF16), pltpu.VMEM((CHUNK, D), F32),
                        pltpu.VMEM((SUB, CONV_PHASE_ROWS, CONV_BLK), F32)],
        compiler_params=_params(1),
    )(z, z, z, x, tgt, mod, conv_w_g, conv_b, cln_g, cln_b, sln_g, sln_b, final_g, ws_b, wst_b, bs_full, wout)


def _bwd_in_proj(z, dcv, dz_rest, x, dx2, mod, norm_g, conv_w_g, win_g):
    s = x.shape[0]
    t = ROW_TILE
    n_steps = s // t
    zp_spec, zn_spec = _halo_specs(t, s, 2 * D)
    dp_spec, dn_spec = _halo_specs(t, s, D)

    def body(z_ref, zp_ref, zn_ref, dcv_ref, dcvp_ref, dcvn_ref, dzr_ref, x_ref, dx2_ref, mod_ref, ng_ref, cw_ref, w_ref,
             gx_ref, dz_ref, acc_ref, gcw_ref,
             gext, dext, dg, taps, dpad, qbuf):
        i = pl.program_id(0)

        @pl.when(i == 0)
        def _():
            acc_ref[...] = jnp.zeros_like(acc_ref)
            gcw_ref[...] = jnp.zeros_like(gcw_ref)

        not_first = i > 0
        not_last = i < n_steps - 1
        gext[0:HALO, :] = jnp.where(not_first, _glu(zp_ref), 0.0)
        gext[HALO:HALO + t, :] = _glu(z_ref)
        gext[HALO + t:2 * HALO + t, :] = jnp.where(not_last, _glu(zn_ref), 0.0)
        dext[0:HALO, :] = jnp.where(not_first, dcvp_ref[...], 0.0)
        dext[HALO:HALO + t, :] = dcv_ref[...]
        dext[HALO + t:2 * HALO + t, :] = jnp.where(not_last, dcvn_ref[...], 0.0)

        taps[...] = jnp.zeros_like(taps)
        dpad[0:SUB, :] = jnp.zeros((SUB, D), F32)
        dpad[SUB:SUB + t, :] = dcv_ref[...]
        dpad[SUB + t:2 * SUB + t, :] = jnp.zeros((SUB, D), F32)
        for blk in range(N_DEV):
            cols = slice(blk * CONV_BLK, (blk + 1) * CONV_BLK)
            dg[:, cols] = _conv_taps(dext, qbuf, cols, lambda k, blk=blk: cw_ref[blk, CONV_W - 1 - k:CONV_W - k, :], t)
            for b in range(SUB):
                dshift = dpad[pl.ds(SUB - 1 - b, t + SUB), cols]
                for a in range((CONV_W - b + SUB - 1) // SUB):
                    k = SUB * a + b
                    taps[k:k + 1, :] = _colsum(gext[SUB * a:SUB * a + t + SUB, cols] * dshift)
            gcw_ref[blk] += taps[...]

        a = z_ref[:, 0:D]
        sig = _sigmoid(z_ref[:, D:2 * D])
        dgv = dg[...]
        dz_ref[:, 0:D] = (dgv * sig).astype(BF16)
        dz_ref[:, D:2 * D] = (dgv * a * sig * (1.0 - sig)).astype(BF16)
        dz_ref[:, 2 * D:6 * D] = dzr_ref[...]

        dh = jnp.zeros((t, D), F32)
        for j in range(N_DEV):
            dh = dh + lax.dot_general(dz_ref[:, j * W_IN_BLK:(j + 1) * W_IN_BLK], w_ref[j], (((1,), (1,)), ((), ())),
                                      preferred_element_type=F32)

        xn, r, _, scale = _rms_modulate(x_ref[...], mod_ref)
        ng = ng_ref[...]
        one_scale = 1.0 + scale
        dh_xn = dh * xn
        acc_ref[0:1, :] += _colsum(dh_xn * one_scale)
        acc_ref[1:2, :] += _colsum(dh)
        acc_ref[2:3, :] += _colsum(dh_xn * ng)
        dxn = dh * (ng * one_scale)
        gx_ref[...] = dx2_ref[...] + r * (dxn - xn * jnp.mean(dxn * xn, axis=-1, keepdims=True))

    row = lambda w: pl.BlockSpec((t, w), lambda i: (i, 0))
    const = lambda shape: pl.BlockSpec(shape, lambda i: (0,) * len(shape))
    return pl.pallas_call(
        body, name="bwd_in_proj", grid=(n_steps,),
        out_shape=(jax.ShapeDtypeStruct((s, D), F32), jax.ShapeDtypeStruct((s, D_IN), BF16),
                   jax.ShapeDtypeStruct((8, D), F32), jax.ShapeDtypeStruct((N_DEV, CONV_ROWS, CONV_BLK), F32)),
        in_specs=[pl.BlockSpec((t, 2 * D), lambda i: (i, 0)), zp_spec, zn_spec, row(D), dp_spec, dn_spec, row(4 * D),
                  row(D), row(D), VM, VM, VM, VM],
        out_specs=(row(D), row(D_IN), const((8, D)), const((N_DEV, CONV_ROWS, CONV_BLK))),
        scratch_shapes=[pltpu.VMEM((t + 2 * HALO, D), F32), pltpu.VMEM((t + 2 * HALO, D), F32), pltpu.VMEM((t, D), F32),
                        pltpu.VMEM((CONV_ROWS, CONV_BLK), F32), pltpu.VMEM((t + 2 * SUB, D), F32),
                        pltpu.VMEM((SUB, CONV_PHASE_ROWS, CONV_BLK), F32)],
        compiler_params=_params(1),
    )(z, z, z, dcv, dcv, dcv, dz_rest, x, dx2, mod, norm_g, conv_w_g, win_g)


WGRAD_K = 1024
WGRAD_VMEM_LIMIT = 60 * 1024 * 1024
_OTHER_CHIPS = [(3, True), (3, False), (1, True), (1, False), (2, True), (2, False)]
_OWN_CHIP = [(0, True), (0, False)]
WGRAD_ROLES = ([("out",) + e for e in _OTHER_CHIPS] + [("in",) + e for e in _OTHER_CHIPS]
               + [("out",) + e for e in _OWN_CHIP] + [("in",) + e for e in _OWN_CHIP])


def _wgrad_schedule():
    me, _, chips, c = _neighbours()
    chip_of = [me[:2]] + chips
    blk = lambda r, sibling: _shard_index(chip_of, c, r, sibling)
    out_blk, in_blk, is_out = [], [], []
    last = {"out": blk(*_OTHER_CHIPS[0]), "in": blk(*_OTHER_CHIPS[0])}
    for kind, r, sibling in WGRAD_ROLES:
        last[kind] = blk(r, sibling)
        out_blk.append(last["out"])
        in_blk.append(last["in"])
        is_out.append(1 if kind == "out" else 0)
    as_vec = lambda v: jnp.stack([jnp.asarray(e, jnp.int32) for e in v])
    return as_vec(out_blk), as_vec(in_blk), as_vec(is_out)


def _sum_blocks(gathered):
    total = gathered[0]
    for b in range(1, N_DEV):
        total = total + gathered[b]
    return total


SMALL_ROWS = 48
ROW_NORM_G, ROW_FINAL_G, ROW_B_S, ROW_CONV_B, ROW_CLN_G, ROW_CLN_B, ROW_SLN_G, ROW_SLN_B, ROW_LOSS = range(32, 41)


def _wgrad_reduce(ht, dz, ycatt, dy, small, gws, dmod):
    s = dz.shape[0]
    n_kc = s // WGRAD_K
    n_steps = len(WGRAD_ROLES)
    first_in = [k for k, role in enumerate(WGRAD_ROLES) if role[0] == "in"][0]
    blk_in, blk_out = (D, W_IN_BLK), (W_OUT_BLK, D)

    def body(out_blk, in_blk, is_out, ht_ref, dz_ref, yt_ref, dy_ref, small_ref, gws_ref, dmod_ref,
             oin_ref, oout_ref, osmall_ref, ogws_ref, odmod_ref, obada_ref, oloss_ref,
             acc_in, acc_out, p1_in, p1_out, l1_in, l1_out, l2_in, l2_out, small_g, gws_g, dmod_g,
             s1_send, s1_recv, s2_send, s2_recv, t1_send, t1_recv, t2_send, t2_recv, ag_send, ag_recv):
        step, kc = pl.program_id(0), pl.program_id(1)
        pay = {"in": (acc_in, p1_in, oin_ref, _ChipReduce(l1_in, l2_in, s1_send, s1_recv, s2_send, s2_recv)),
               "out": (acc_out, p1_out, oout_ref, _ChipReduce(l1_out, l2_out, t1_send, t1_recv, t2_send, t2_recv))}
        ag = _AllGather([dmod_g, small_g, gws_g], ag_send, ag_recv)
        last_kc = kc == n_kc - 1

        @pl.when((step == 0) & (kc == 0))
        def _():
            me = _my_block()
            small_g[me] = small_ref[...]
            gws_g[me] = gws_ref[...].astype(BF16)
            dmod_g[me] = dmod_ref[...]
            for a in range(3):
                ag.send_own(a)

        @pl.when((step == 4) & (kc == 0))
        def _():
            for a in range(3):
                ag.other_chips(a)

        def accumulate(acc, prod):
            @pl.when(kc == 0)
            def _():
                acc[...] = prod

            @pl.when(kc != 0)
            def _():
                acc[...] += prod

        @pl.when(is_out[step] == 1)
        def _():
            accumulate(acc_out, jnp.dot(yt_ref[...], dy_ref[...], preferred_element_type=F32))

        @pl.when(is_out[step] == 0)
        def _():
            accumulate(acc_in, jnp.dot(ht_ref[kc], dz_ref[...], preferred_element_type=F32))

        for k, (kind, r, sibling) in enumerate(WGRAD_ROLES):
            @pl.when((step == k) & last_kc)
            def _(kind=kind, r=r, sibling=sibling):
                acc, p1, out, red = pay[kind]
                if sibling:
                    p1[r] = acc[...].astype(BF16)
                    red.to_sibling(r, p1.at[r]).start()
                else:
                    chip_sum = red.combine(r, acc[...])
                    if r == 0:
                        out[...] = chip_sum

        @pl.when((step == n_steps - 1) & last_kc)
        def _():
            for kind in ("out", "in"):
                _, _, out, red = pay[kind]
                out[...] = red.finish(out[...])
            for a in range(3):
                ag.from_sibling(a)
            for a in range(3):
                ag.sent(a)
            tot_small = _sum_blocks(small_g)
            osmall_ref[...] = tot_small
            oloss_ref[...] = jnp.full(oloss_ref.shape, (0.5 / D) * jnp.sum(tot_small[ROW_LOSS:ROW_LOSS + 1, :]), F32)
            tot_gws = gws_g[0].astype(F32)
            for b in range(1, N_DEV):
                tot_gws = tot_gws + gws_g[b].astype(F32)
            ogws_ref[...] = tot_gws
            obada_ref[...] = _sum_blocks(dmod_g)
            for b in range(N_DEV):
                odmod_ref[b:b + 1, :] = dmod_g[b, 0:1, :]

    def kc_of(working, step, kc, hold_first):
        held = jnp.where(step < hold_first, 0, n_kc - 1)
        return jnp.where(working, kc, held)

    out_kc = lambda i, kc, ob, ib, io: kc_of(io[i] == 1, i, kc, 0)
    in_kc = lambda i, kc, ob, ib, io: kc_of(io[i] == 0, i, kc, first_in)
    sems = lambda n: [pltpu.SemaphoreType.DMA((n,)), pltpu.SemaphoreType.DMA((n,))]
    grid_spec = pltpu.PrefetchScalarGridSpec(
        num_scalar_prefetch=3, grid=(n_steps, n_kc),
        in_specs=[VM,
                  pl.BlockSpec((WGRAD_K, W_IN_BLK), lambda i, kc, ob, ib, io: (in_kc(i, kc, ob, ib, io), ib[i])),
                  pl.BlockSpec((W_OUT_BLK, WGRAD_K), lambda i, kc, ob, ib, io: (ob[i], out_kc(i, kc, ob, ib, io))),
                  pl.BlockSpec((WGRAD_K, D), lambda i, kc, ob, ib, io: (out_kc(i, kc, ob, ib, io), 0)),
                  VM, VM, VM],
        out_specs=(VM,) * 7,
        scratch_shapes=[pltpu.VMEM(blk_in, F32), pltpu.VMEM(blk_out, F32),
                        pltpu.VMEM((4,) + blk_in, BF16), pltpu.VMEM((4,) + blk_out, BF16),
                        pltpu.VMEM((4,) + blk_in, BF16), pltpu.VMEM((4,) + blk_out, BF16),
                        pltpu.VMEM((3,) + blk_in, BF16), pltpu.VMEM((3,) + blk_out, BF16),
                        pltpu.VMEM((N_DEV,) + small.shape, F32), pltpu.VMEM((N_DEV,) + gws.shape, BF16),
                        pltpu.VMEM((N_DEV,) + dmod.shape, F32)]
        + sems(4) + sems(3) + sems(4) + sems(3) + sems(21))
    return pl.pallas_call(
        body, name="wgrad_reduce", grid_spec=grid_spec,
        out_shape=(jax.ShapeDtypeStruct(blk_in, F32), jax.ShapeDtypeStruct(blk_out, F32),
                   jax.ShapeDtypeStruct(small.shape, F32), jax.ShapeDtypeStruct(gws.shape, F32),
                   jax.ShapeDtypeStruct((N_DEV, 3 * D), F32), jax.ShapeDtypeStruct((8, 3 * D), F32),
                   jax.ShapeDtypeStruct((8, 128), F32)),
        compiler_params=_params(2, vmem=WGRAD_VMEM_LIMIT),
    )(*_wgrad_schedule(), ht, dz, ycatt, dy, small, gws, dmod)


def _adamw_math(w, g, m, v):
    m = ADAM_B1 * m + (1.0 - ADAM_B1) * g
    v = ADAM_B2 * v + (1.0 - ADAM_B2) * (g * g)
    m_hat = m / (1.0 - ADAM_B1 ** ADAM_STEP)
    v_hat = v / (1.0 - ADAM_B2 ** ADAM_STEP)
    delta = -ADAM_LR * (m_hat / (jnp.sqrt(v_hat) + ADAM_EPS) + ADAM_WD * w)
    return delta, m, v


VECTORS = ["norm_g", "conv_b", "conv_ln_g", "conv_ln_b", "sg_ln_g", "sg_ln_b", "final_g", "b_s"]
VECTOR_ROWS = [ROW_NORM_G, ROW_CONV_B, ROW_CLN_G, ROW_CLN_B, ROW_SLN_G, ROW_SLN_B, ROW_FINAL_G, ROW_B_S]
ADAM_STEPS = 4


def _adamw_all(me, g_w_in, g_w_out, g_w_s, small_sum, dmod_all, bada8, c_all, matrices, b_ada, conv_w, vectors):
    mat_shapes = [(D, W_IN_BLK), (D, W_ADA_BLK), (W_OUT_BLK, D), (HEADS * CHUNK, CHUNK)]
    mat_blocks = [(sh[0] // ADAM_STEPS, sh[1]) for sh in mat_shapes]
    n_small = 2 + len(VECTORS)

    def body(me_ref, gin_ref, gout_ref, gws_ref, taps_ref, small_ref, dmc_ref, bada_ref, c_ref, *refs):
        params = refs[:3 * (4 + n_small)]
        outs = refs[3 * (4 + n_small):-2]
        act, gada = refs[-2:]
        mat_out, gada_out, small_out = outs[:12], outs[12], outs[13:]
        i = pl.program_id(0)

        @pl.when(i == 0)
        def _():
            for b in range(N_DEV):
                cb = c_ref[b, 0:1, :]
                act[b:b + 1, :] = cb * _sigmoid(cb)
            gada[...] = lax.dot_general(act[...], dmc_ref[...], (((0,), (0,)), ((), ())), preferred_element_type=F32,
                                        precision=lax.Precision.HIGHEST)
            small_grads = [bada_ref[0:1, :], taps_ref[0:CONV_W, :]] + [small_ref[r:r + 1, :] for r in VECTOR_ROWS]
            for k, g in enumerate(small_grads):
                w_ref, m_ref, v_ref = params[3 * (4 + k):3 * (5 + k)]
                o = small_out[4 * k:4 * k + 4]
                o[0][...] = g
                o[1][...], o[2][...], o[3][...] = _adamw_math(w_ref[...], g, m_ref[...], v_ref[...])

        rows = pl.ds(pl.multiple_of(i * mat_blocks[1][0], mat_blocks[1][0]), mat_blocks[1][0])
        g_ada = gada[rows, :]
        gada_out[...] = g_ada
        for k, g in enumerate([gin_ref[...], g_ada, gout_ref[...], gws_ref[...]]):
            w_ref, m_ref, v_ref = params[3 * k:3 * k + 3]
            o = mat_out[3 * k:3 * k + 3]
            o[0][...], o[1][...], o[2][...] = _adamw_math(w_ref[...], g, m_ref[...], v_ref[...])

    rows_of = lambda blk: pl.BlockSpec(blk, lambda i, me_ref: (i, 0))
    mat_specs = [rows_of(b) for b in mat_blocks]
    grid_spec = pltpu.PrefetchScalarGridSpec(
        num_scalar_prefetch=1, grid=(ADAM_STEPS,),
        in_specs=[mat_specs[0], mat_specs[2], mat_specs[3],
                  pl.BlockSpec((CONV_ROWS, CONV_BLK), lambda i, me_ref: (0, me_ref[0])), VM,
                  pl.BlockSpec((N_DEV, W_ADA_BLK), lambda i, me_ref: (0, me_ref[0])), VM, VM]
        + [s for s in mat_specs for _ in range(3)] + [VM] * (3 * n_small),
        out_specs=tuple([s for s in mat_specs for _ in range(3)] + [mat_specs[1]] + [VM] * (4 * n_small)),
        scratch_shapes=[pltpu.VMEM((N_DEV, D), F32), pltpu.VMEM((D, W_ADA_BLK), F32)])
    small_shapes = [b_ada[0].shape, conv_w[0].shape] + [(1, D)] * len(VECTORS)
    out_shape = tuple([jax.ShapeDtypeStruct(sh, F32) for sh in mat_shapes for _ in range(3)]
                      + [jax.ShapeDtypeStruct(mat_shapes[1], F32)]
                      + [jax.ShapeDtypeStruct(sh, F32) for sh in small_shapes for _ in range(4)])
    flat = [a for group in matrices for a in group] + list(b_ada) + list(conv_w) + [a for group in vectors for a in group]
    return pl.pallas_call(body, name="adamw_all", grid_spec=grid_spec, out_shape=out_shape,
                          compiler_params=_params(1))(
        me, g_w_in, g_w_out, g_w_s, small_sum, small_sum, dmod_all, bada8, c_all, *flat)


def kernel(x, c, w_ada, b_ada, norm_g, w_in, conv_w, conv_b, conv_ln_g, conv_ln_b, sg_ln_g, sg_ln_b, w_s, b_s, w_out, final_g, loss_target, m_w_ada, m_b_ada, m_norm_g, m_w_in, m_conv_w, m_conv_b, m_conv_ln_g, m_conv_ln_b, m_sg_ln_g, m_sg_ln_b, m_w_s, m_b_s, m_w_out, m_final_g, v_w_ada, v_b_ada, v_norm_g, v_w_in, v_conv_w, v_conv_b, v_conv_ln_g, v_conv_ln_b, v_sg_ln_g, v_sg_ln_b, v_w_s, v_b_s, v_w_out, v_final_g):
    me = 4 * lax.axis_index("x") + 2 * lax.axis_index("y") + lax.axis_index("c")
    x2d, tgt2d = x[0], loss_target[0]
    row1 = lambda a: a.reshape(1, D)
    taps = lambda a: jnp.pad(a.reshape(CONV_W, CONV_BLK), ((0, CONV_ROWS - CONV_W), (0, 0)))

    z, ht, win_g, wout_g, cw_g, c_all, mod = _fwd_in_proj(
        x2d, jnp.broadcast_to(c, (8, D)), w_ada[0], b_ada, norm_g, w_in[0], w_out[0], taps(conv_w))
    ws_b = w_s[0].astype(BF16)
    wst_b = jnp.swapaxes(w_s[0], 1, 2).astype(BF16)
    bs_full = jnp.repeat(b_s[0].T, HEAD_DIM, axis=1)

    dx2, ycatt, dy, dcv, dz_rest, acc_a, gws, gbs = _mix_and_head(
        z, x2d, tgt2d, mod, cw_g, conv_b, conv_ln_g, conv_ln_b, sg_ln_g, sg_ln_b, row1(final_g), ws_b, wst_b, bs_full,
        wout_g.reshape(D_MIX, D))
    grad_x, dz, acc_b, gcw = _bwd_in_proj(z, dcv, dz_rest, x2d, dx2, mod, norm_g, cw_g, win_g)

    small = jnp.concatenate(
        [jnp.transpose(gcw, (1, 0, 2)).reshape(CONV_ROWS, D), acc_b[0:1], acc_a[0:1], gbs.T.reshape(1, D), acc_a[2:8],
         jnp.zeros((SMALL_ROWS - ROW_LOSS - 1, D), F32)], axis=0)
    dmod_row = jnp.concatenate([acc_b[1:2], acc_b[2:3], acc_a[1:2]], axis=1)
    g_w_in, g_w_out, small_sum, g_w_s, dmod_all, bada8, loss_tile = _wgrad_reduce(
        ht, dz, ycatt, dy, small, gws.reshape(HEADS * CHUNK, CHUNK), jnp.broadcast_to(dmod_row, (8, 3 * D)))

    given = dict(w_ada=(w_ada, m_w_ada, v_w_ada), b_ada=(b_ada, m_b_ada, v_b_ada), norm_g=(norm_g, m_norm_g, v_norm_g),
                 w_in=(w_in, m_w_in, v_w_in), conv_w=(conv_w, m_conv_w, v_conv_w), conv_b=(conv_b, m_conv_b, v_conv_b),
                 conv_ln_g=(conv_ln_g, m_conv_ln_g, v_conv_ln_g), conv_ln_b=(conv_ln_b, m_conv_ln_b, v_conv_ln_b),
                 sg_ln_g=(sg_ln_g, m_sg_ln_g, v_sg_ln_g), sg_ln_b=(sg_ln_b, m_sg_ln_b, v_sg_ln_b),
                 w_s=(w_s, m_w_s, v_w_s), b_s=(b_s, m_b_s, v_b_s), w_out=(w_out, m_w_out, v_w_out),
                 final_g=(final_g, m_final_g, v_final_g))
    as2d = lambda name, shape: tuple(a.reshape(shape) for a in given[name])
    res = _adamw_all(
        jnp.reshape(me, (1,)).astype(jnp.int32), g_w_in, g_w_out, g_w_s, small_sum, dmod_all, bada8, c_all,
        [as2d("w_in", (D, W_IN_BLK)), as2d("w_ada", (D, W_ADA_BLK)), as2d("w_out", (W_OUT_BLK, D)),
         as2d("w_s", (HEADS * CHUNK, CHUNK))],
        given["b_ada"], as2d("conv_w", (CONV_W, CONV_BLK)), [as2d(n, (1, D)) for n in VECTORS])
    out = {}
    for k, name in enumerate(["w_in", "w_ada", "w_out", "w_s"]):
        out[name] = [None] + list(res[3 * k:3 * k + 3])
    out["w_in"][0], out["w_ada"][0], out["w_out"][0], out["w_s"][0] = g_w_in, res[12], g_w_out, g_w_s
    for k, name in enumerate(["b_ada", "conv_w"] + VECTORS):
        out[name] = list(res[13 + 4 * k:17 + 4 * k])
    order = ["w_ada", "b_ada", "norm_g", "w_in", "conv_w", "conv_b", "conv_ln_g", "conv_ln_b", "sg_ln_g", "sg_ln_b",
             "w_s", "b_s", "w_out", "final_g"]
    outs = [loss_tile[0, 0], grad_x.reshape(x.shape)]
    for kind in range(4):
        outs += [out[n][kind].reshape(given[n][0].shape) for n in order]
    return tuple(outs)
```

```python
import functools

import jax
import jax.numpy as jnp
from jax import lax
from jax.experimental import pallas as pl
from jax.experimental.pallas import tpu as pltpu

F32 = jnp.float32
BF16 = jnp.bfloat16
MESH = pl.DeviceIdType.MESH

D = 1024
D_IN = 6 * D
D_MIX = 2 * D
N_DEV = 8
W_IN_BLK = D_IN // N_DEV
W_OUT_BLK = D_MIX // N_DEV
W_ADA_BLK = 3 * D // N_DEV
CONV_BLK = D // N_DEV
CONV_W = 31
CONV_HALF = CONV_W // 2
CONV_ROWS = 32
HALO = 16
CHUNK = 128
HEADS = 8
HEAD_DIM = 128
EPS = 1e-6
ROW_TILE = 256
VMEM_LIMIT = 56 * 1024 * 1024

ADAM_LR = 0.001
ADAM_B1 = 0.9
ADAM_B2 = 0.999
ADAM_EPS = 1e-08
ADAM_WD = 0.01
ADAM_STEP = 10

VM = pl.BlockSpec(memory_space=pltpu.VMEM)


def _params(grid_rank=0, vmem=VMEM_LIMIT):
    sem = ("arbitrary",) * grid_rank if grid_rank else None
    return pltpu.CompilerParams(dimension_semantics=sem, vmem_limit_bytes=vmem)


def _sigmoid(t):
    return jax.nn.sigmoid(t)


def _dsilu(t, sig):
    return sig * (1.0 + t * (1.0 - sig))


def _mesh_pos():
    return lax.axis_index("x"), lax.axis_index("y"), lax.axis_index("c")


def _neighbours():
    x, y, c = _mesh_pos()
    nb1 = (x + c * (1 - 2 * x), y + (1 - c) * (1 - 2 * y))
    nb2 = (x + (1 - c) * (1 - 2 * x), y + c * (1 - 2 * y))
    return (x, y, c), (x, y, 1 - c), [nb1, nb2, (1 - x, 1 - y)], c


def _sibling_slot(j):
    return j if j == 2 else 1 - j


class _AllGather:
    def __init__(self, bufs, send_sems, recv_sems):
        self.bufs, self.send_sems, self.recv_sems = bufs, send_sems, recv_sems
        self.me, self.sib, self.chips, self.c = _neighbours()

    def _copy(self, a, k, block, to):
        px, py, pc = block
        ref = self.bufs[a].at[4 * px + 2 * py + pc]
        return pltpu.make_async_remote_copy(
            src_ref=ref, dst_ref=ref, send_sem=self.send_sems.at[7 * a + k], recv_sem=self.recv_sems.at[7 * a + k],
            device_id=to, device_id_type=MESH)

    def _outgoing(self, a, k):
        if k == 0:
            return self._copy(a, 0, self.me, self.sib)
        if k <= 2:
            return self._copy(a, k, self.me, (*self.chips[k - 1], self.c))
        if k == 3:
            return self._copy(a, 3, (*self.chips[0], self.c), (*self.chips[1], self.c))
        return self._copy(a, k, (*self.chips[_sibling_slot(k - 4)], self.c), self.sib)

    def source(self, k):
        if k == 0:
            return self.sib
        return (*self.chips[(k - 1) % 3], self.c if k <= 3 else 1 - self.c)

    def block_index(self, k):
        px, py, pc = self.source(k)
        return 4 * px + 2 * py + pc

    def send_own(self, a):
        for k in range(3):
            self._outgoing(a, k).start()

    def arrived(self, a, k):
        self._copy(a, k, self.source(k), self.me).wait_recv()

    def relay(self, a):
        self._outgoing(a, 3).start()

    def pass_on(self, a, j):
        self._outgoing(a, 4 + _sibling_slot(j)).start()

    def other_chips(self, a):
        self.arrived(a, 1)
        self.relay(a)
        self.pass_on(a, 0)
        for j in (1, 2):
            self.arrived(a, 1 + j)
            self.pass_on(a, j)

    def from_sibling(self, a):
        for k in (0, 4, 5, 6):
            self.arrived(a, k)

    def sent(self, a):
        for k in range(7):
            self._outgoing(a, k).wait_send()


def _shard_index(chip_of, c, r, sibling):
    if sibling and r:
        r = 1 + _sibling_slot(r - 1)
    cx, cy = chip_of[r]
    return 4 * cx + 2 * cy + ((1 - c) if sibling else c)


class _ChipReduce:
    def __init__(self, l1, l2, s1_send, s1_recv, s2_send, s2_recv):
        self.l1, self.l2 = l1, l2
        self.s1_send, self.s1_recv, self.s2_send, self.s2_recv = s1_send, s1_recv, s2_send, s2_recv
        me, self.sib, chips, self.c = _neighbours()
        self.chip_of = [me[:2]] + chips

    def block(self, r, sibling):
        return _shard_index(self.chip_of, self.c, r, sibling)

    def to_sibling(self, r, src=None):
        return pltpu.make_async_remote_copy(
            src_ref=self.l1.at[r] if src is None else src, dst_ref=self.l1.at[r], send_sem=self.s1_send.at[r],
            recv_sem=self.s1_recv.at[r], device_id=self.sib, device_id_type=MESH)

    def to_chip(self, r):
        target = self.chip_of[1] if r == 3 else self.chip_of[r]
        return pltpu.make_async_remote_copy(
            src_ref=self.l1.at[r], dst_ref=self.l2.at[r - 1], send_sem=self.s2_send.at[r - 1],
            recv_sem=self.s2_recv.at[r - 1], device_id=(*target, self.c), device_id_type=MESH)

    def combine(self, r, mine):
        self.to_sibling(r).wait_recv()
        both = mine + self.l1[r].astype(F32)
        if r == 0:
            return both
        if r == 2:
            self.to_chip(3).wait_recv()
            both = both + self.l2[2].astype(F32)
        self.l1[r] = both.astype(BF16)
        self.to_chip(r).start()
        return None

    def finish(self, own_chip_sum):
        total = own_chip_sum
        for r in (1, 2):
            self.to_chip(r).wait_recv()
            total = total + self.l2[r - 1].astype(F32)
        for r in range(4):
            self.to_sibling(r).wait_send()
        for r in (1, 2, 3):
            self.to_chip(r).wait_send()
        return total


def _my_block():
    x, y, c = _mesh_pos()
    return 4 * x + 2 * y + c


def _modulation(c_g, w_ref, b_ref, mod_ref, part, land, send_sems, recv_sems):
    x, y, c = _mesh_pos()
    me = 4 * x + 2 * y + c
    w = w_ref[...]
    for b in range(N_DEV):
        cb = c_g[b]
        part[b] = jnp.dot(cb * _sigmoid(cb), w, preferred_element_type=F32, precision=lax.Precision.HIGHEST)
    land[me] = part[me]

    def copy(b):
        return pltpu.make_async_remote_copy(
            src_ref=part.at[b], dst_ref=land.at[me], send_sem=send_sems.at[b], recv_sem=recv_sems.at[me],
            device_id=(b // 4, (b // 2) % 2, b % 2), device_id_type=MESH)

    def arrival(b):
        return pltpu.make_async_remote_copy(
            src_ref=part.at[b], dst_ref=land.at[b], send_sem=send_sems.at[b], recv_sem=recv_sems.at[b],
            device_id=(b // 4, (b // 2) % 2, b % 2), device_id_type=MESH)

    for b in range(N_DEV):
        @pl.when(b != me)
        def _():
            copy(b).start()
    for b in range(N_DEV):
        @pl.when(b != me)
        def _():
            arrival(b).wait_recv()
            copy(b).wait_send()
    for b in range(N_DEV):
        cols = slice(b * W_ADA_BLK, (b + 1) * W_ADA_BLK)
        mod_ref[:, cols] = land[b] + b_ref[:, cols]


def _rms_modulate(x, mod_ref):
    shift = mod_ref[0:1, 0:D]
    scale = mod_ref[0:1, D:2 * D]
    r = lax.rsqrt(jnp.mean(x * x, axis=-1, keepdims=True) + EPS)
    xn = x * r
    return xn, r, shift, scale


FWD_TILE = 512
FWD_Z_TILE = 1024


def _fwd_in_proj(x, c_rep, w_ada, b_ada, norm_g, w_in, w_out, conv_w):
    s = x.shape[0]
    t, tz = FWD_TILE, FWD_Z_TILE
    n_tiles = s // t
    C_PAY, WIN_PAY, WOUT_PAY, CW_PAY = 0, 1, 2, 3

    def body(x_hbm, c_ref, wada_ref, bada_ref, ng_ref, win_ref, wout_ref, cw_ref,
             z_hbm, ht_ref, win_g, wout_g, cw_g, c_g, mod_ref,
             h, xbuf, zbuf, part, land, ag_send, ag_recv, mod_send, mod_recv, x_sem, z_sem):
        me = _my_block()
        c_g[me] = c_ref[...]
        ag = _AllGather([c_g, win_g, wout_g, cw_g], ag_send, ag_recv)
        ag.send_own(C_PAY)

        def x_copy(i):
            return pltpu.make_async_copy(x_hbm.at[pl.ds(i * t, t), :], xbuf.at[i % 2], x_sem.at[i % 2])

        x_copy(0).start()
        win_g[me] = win_ref[...].astype(BF16)
        wout_g[me] = wout_ref[...].astype(BF16)
        cw_g[me] = cw_ref[...]

        ag.other_chips(C_PAY)
        ag.from_sibling(C_PAY)
        _modulation(c_g, wada_ref, bada_ref, mod_ref, part, land, mod_send, mod_recv)
        for a in (WIN_PAY, WOUT_PAY, CW_PAY):
            ag.send_own(a)

        for i in range(n_tiles):
            x_copy(i).wait()
            if i + 1 < n_tiles:
                x_copy(i + 1).start()
            xn, _, shift, scale = _rms_modulate(xbuf[i % 2], mod_ref)
            hh = xn * ng_ref[...] * (1.0 + scale) + shift
            h[i * t:(i + 1) * t, :] = hh.astype(BF16)
            ht_ref[(i * t) // WGRAD_K, :, (i * t) % WGRAD_K:(i * t) % WGRAD_K + t] = hh.T.astype(BF16)

        def z_copy(slot, row0, col0):
            return pltpu.make_async_copy(zbuf.at[slot], z_hbm.at[pl.ds(row0, tz), pl.ds(col0, W_IN_BLK)], z_sem.at[slot])

        done = [0]

        def z_block(blk):
            col0 = pl.multiple_of(blk * W_IN_BLK, 128)
            for i in range(s // tz):
                slot = done[0] % 2
                if done[0] >= 2:
                    z_copy(slot, 0, 0).wait()
                zbuf[slot] = jnp.dot(h[i * tz:(i + 1) * tz, :], win_g[blk], preferred_element_type=F32)
                z_copy(slot, i * tz, col0).start()
                done[0] += 1

        z_block(me)
        ag.arrived(WIN_PAY, 0)
        z_block(ag.block_index(0))
        ag.arrived(WIN_PAY, 1)
        ag.relay(WIN_PAY)
        ag.pass_on(WIN_PAY, 0)
        ag.arrived(WIN_PAY, 2)
        ag.pass_on(WIN_PAY, 1)
        for k in (1, 2, 4, 5):
            if k >= 4:
                ag.arrived(WIN_PAY, k)
            z_block(ag.block_index(k))
        ag.arrived(WIN_PAY, 3)
        ag.pass_on(WIN_PAY, 2)
        z_block(ag.block_index(3))
        ag.arrived(WIN_PAY, 6)
        z_block(ag.block_index(6))

        for a in (WOUT_PAY, CW_PAY):
            ag.other_chips(a)
        for a in (WOUT_PAY, CW_PAY):
            ag.from_sibling(a)
        for a in (C_PAY, WIN_PAY, WOUT_PAY, CW_PAY):
            ag.sent(a)
        z_copy(0, 0, 0).wait()
        z_copy(1, 0, 0).wait()

    any_spec = pl.BlockSpec(memory_space=pl.ANY)
    return pl.pallas_call(
        body, name="fwd_in_proj",
        out_shape=(jax.ShapeDtypeStruct((s, D_IN), F32), jax.ShapeDtypeStruct((s // WGRAD_K, D, WGRAD_K), BF16),
                   jax.ShapeDtypeStruct((N_DEV,) + w_in.shape, BF16), jax.ShapeDtypeStruct((N_DEV,) + w_out.shape, BF16),
                   jax.ShapeDtypeStruct((N_DEV,) + conv_w.shape, F32), jax.ShapeDtypeStruct((N_DEV,) + c_rep.shape, F32),
                   jax.ShapeDtypeStruct((8, 3 * D), F32)),
        in_specs=[any_spec] + [VM] * 7, out_specs=(any_spec,) + (VM,) * 6,
        scratch_shapes=[pltpu.VMEM((s, D), BF16), pltpu.VMEM((2, t, D), F32), pltpu.VMEM((2, tz, W_IN_BLK), F32),
                        pltpu.VMEM((N_DEV, 8, W_ADA_BLK), F32), pltpu.VMEM((N_DEV, 8, W_ADA_BLK), F32),
                        pltpu.SemaphoreType.DMA((28,)), pltpu.SemaphoreType.DMA((28,)),
                        pltpu.SemaphoreType.DMA((N_DEV,)), pltpu.SemaphoreType.DMA((N_DEV,)),
                        pltpu.SemaphoreType.DMA((2,)), pltpu.SemaphoreType.DMA((2,))],
        compiler_params=_params(),
    )(x, c_rep, w_ada, b_ada, norm_g, w_in, w_out, conv_w)


def _halo_specs(t, s, width):
    per = t // HALO
    last = s // HALO - 1
    prev = pl.BlockSpec((HALO, width), lambda i: (jnp.maximum(i * per - 1, 0), 0))
    nxt = pl.BlockSpec((HALO, width), lambda i: (jnp.minimum((i + 1) * per, last), 0))
    return prev, nxt


def _glu(ref):
    return ref[:, 0:D] * _sigmoid(ref[:, D:2 * D])


SUB = 8
CONV_PHASE_ROWS = ROW_TILE + SUB


def _conv_taps(ext, qbuf, cols, tap, t):
    out = None
    for b in range(SUB):
        q = None
        for a in range((CONV_W - b + SUB - 1) // SUB):
            term = ext[SUB * a:SUB * a + t + SUB, cols] * tap(SUB * a + b)
            q = term if q is None else q + term
        qbuf[b] = q
        shifted = qbuf[b, pl.ds(b + 1, t), :]
        out = shifted if out is None else out + shifted
    return out


def _layer_norm_stats(v):
    mu = jnp.mean(v, axis=-1, keepdims=True)
    cen = v - mu
    rstd = lax.rsqrt(jnp.mean(cen * cen, axis=-1, keepdims=True) + EPS)
    return cen * rstd, rstd


def _layer_norm_bwd(dy_hat, hat, rstd):
    m1 = jnp.mean(dy_hat, axis=-1, keepdims=True)
    m2 = jnp.mean(dy_hat * hat, axis=-1, keepdims=True)
    return rstd * (dy_hat - m1 - hat * m2)


def _colsum(v):
    return jnp.sum(v, axis=0, keepdims=True)


def _mix_and_head(z, x, tgt, mod, conv_w_g, conv_b, cln_g, cln_b, sln_g, sln_b, final_g, ws_b, wst_b, bs_full, wout):
    s = x.shape[0]
    t = ROW_TILE
    n_chunks = t // CHUNK
    n_steps = s // t
    prev_spec, next_spec = _halo_specs(t, s, 2 * D)

    def body(z_ref, zp_ref, zn_ref, x_ref, tgt_ref, mod_ref, cw_ref, cb_ref, clg_ref, clb_ref, slg_ref, slb_ref, fg_ref,
             ws_ref, wst_ref, bs_ref, wout_ref,
             dx2_ref, ycatt_ref, dy_ref, dcv_ref, dzr_ref, acc_ref, gws_ref, gbs_ref,
             gext, cv, vs, dvn, ycat, gbs_acc, qbuf):
        i = pl.program_id(0)

        @pl.when(i == 0)
        def _():
            acc_ref[...] = jnp.zeros_like(acc_ref)
            gws_ref[...] = jnp.zeros_like(gws_ref)
            gbs_acc[...] = jnp.zeros_like(gbs_acc)

        gext[0:HALO, :] = jnp.where(i > 0, _glu(zp_ref), 0.0)
        gext[HALO:HALO + t, :] = _glu(z_ref)
        gext[HALO + t:2 * HALO + t, :] = jnp.where(i < n_steps - 1, _glu(zn_ref), 0.0)
        for blk in range(N_DEV):
            cols = slice(blk * CONV_BLK, (blk + 1) * CONV_BLK)
            cv[:, cols] = _conv_taps(gext, qbuf, cols, lambda k, blk=blk: cw_ref[blk, k:k + 1, :], t) + cb_ref[:, cols]
        ln_hat, ln_rstd = _layer_norm_stats(cv[...])
        ln_a = ln_hat * clg_ref[...] + clb_ref[...]
        sig_ln = _sigmoid(ln_a)
        sa = ln_a * sig_ln
        a_gate = z_ref[:, 2 * D:3 * D]
        sig_ag = _sigmoid(a_gate)
        s_gate = a_gate * sig_ag
        ya = sa * s_gate

        v_hat, v_rstd = _layer_norm_stats(z_ref[:, 4 * D:5 * D])
        vn = v_hat * slg_ref[...] + slb_ref[...]
        vnb = vn.astype(BF16)
        for n in range(n_chunks):
            rows = slice(n * CHUNK, (n + 1) * CHUNK)
            for h in range(HEADS):
                cols = slice(h * HEAD_DIM, (h + 1) * HEAD_DIM)
                vs[rows, cols] = jnp.dot(ws_ref[h], vnb[rows, cols], preferred_element_type=F32) + bs_ref[:, cols]
        u = z_ref[:, 3 * D:4 * D]
        b_gate = z_ref[:, 5 * D:6 * D]
        sig_bg = _sigmoid(b_gate)
        s_bg = b_gate * sig_bg
        vsv = vs[...]
        yb = u * vsv * s_bg

        ycat[:, 0:D] = ya.astype(BF16)
        ycat[:, D:2 * D] = yb.astype(BF16)
        ycatt_ref[0:D, :] = ya.T.astype(BF16)
        ycatt_ref[D:2 * D, :] = yb.T.astype(BF16)
        y = jnp.dot(ycat[...], wout_ref[...], preferred_element_type=F32)
        gate = mod_ref[0:1, 2 * D:3 * D]
        x2 = x_ref[...] + gate * y
        r2 = lax.rsqrt(jnp.mean(x2 * x2, axis=-1, keepdims=True) + EPS)
        x2n = x2 * r2
        fg = fg_ref[...]
        diff = x2n * fg - tgt_ref[...]
        acc_ref[7:8, :] += _colsum(diff * diff)
        dout = diff * (1.0 / D)
        acc_ref[0:1, :] += _colsum(dout * x2n)
        dx2n = dout * fg
        dx2 = r2 * (dx2n - x2n * jnp.mean(dx2n * x2n, axis=-1, keepdims=True))
        dx2_ref[...] = dx2
        acc_ref[1:2, :] += _colsum(dx2 * y)
        dyb16 = (dx2 * gate).astype(BF16)
        dy_ref[...] = dyb16
        dycat = lax.dot_general(dyb16, wout_ref[...], (((1,), (1,)), ((), ())), preferred_element_type=F32)
        dya = dycat[:, 0:D]
        dyb = dycat[:, D:2 * D]

        du = dyb * vsv * s_bg
        dvs = dyb * u * s_bg
        dbg = dyb * u * vsv * _dsilu(b_gate, sig_bg)
        dvsb = dvs.astype(BF16)
        gbs = gbs_acc[...]
        for n in range(n_chunks):
            rows = slice(n * CHUNK, (n + 1) * CHUNK)
            gbs = gbs + dvs[rows, :]
            for h in range(HEADS):
                cols = slice(h * HEAD_DIM, (h + 1) * HEAD_DIM)
                gws_ref[h] += lax.dot_general(dvsb[rows, cols], vnb[rows, cols], (((1,), (1,)), ((), ())),
                                              preferred_element_type=F32)
                dvn[rows, cols] = jnp.dot(wst_ref[h], dvsb[rows, cols], preferred_element_type=F32)
        gbs_acc[...] = gbs

        @pl.when(i == n_steps - 1)
        def _():
            for h in range(HEADS):
                gbs_ref[:, h:h + 1] = jnp.sum(gbs_acc[:, h * HEAD_DIM:(h + 1) * HEAD_DIM], axis=1, keepdims=True)

        dvnv = dvn[...]
        acc_ref[5:6, :] += _colsum(dvnv * v_hat)
        acc_ref[6:7, :] += _colsum(dvnv)
        dv = _layer_norm_bwd(dvnv * slg_ref[...], v_hat, v_rstd)

        dsa = dya * s_gate
        dagate = dya * sa * _dsilu(a_gate, sig_ag)
        dln = dsa * _dsilu(ln_a, sig_ln)
        acc_ref[3:4, :] += _colsum(dln * ln_hat)
        acc_ref[4:5, :] += _colsum(dln)
        dcv = _layer_norm_bwd(dln * clg_ref[...], ln_hat, ln_rstd)
        acc_ref[2:3, :] += _colsum(dcv)
        dcv_ref[...] = dcv

        dzr_ref[:, 0:D] = dagate.astype(BF16)
        dzr_ref[:, D:2 * D] = du.astype(BF16)
        dzr_ref[:, 2 * D:3 * D] = dv.astype(BF16)
        dzr_ref[:, 3 * D:4 * D] = dbg.astype(BF16)

    row = lambda w: pl.BlockSpec((t, w), lambda i: (i, 0))
    const = lambda shape: pl.BlockSpec(shape, lambda i: (0,) * len(shape))
    return pl.pallas_call(
        body, name="mix_and_head", grid=(n_steps,),
        out_shape=(jax.ShapeDtypeStruct((s, D), F32),
                   jax.ShapeDtypeStruct((D_MIX, s), BF16),
                   jax.ShapeDtypeStruct((s, D), BF16),
                   jax.ShapeDtypeStruct((s, D), F32),
                   jax.ShapeDtypeStruct((s, 4 * D), BF16),
                   jax.ShapeDtypeStruct((8, D), F32),
                   jax.ShapeDtypeStruct((HEADS, CHUNK, CHUNK), F32),
                   jax.ShapeDtypeStruct((CHUNK, HEADS), F32)),
        in_specs=[row(D_IN), prev_spec, next_spec, row(D), row(D)] + [VM] * 12,
        out_specs=(row(D), pl.BlockSpec((D_MIX, t), lambda i: (0, i)), row(D), row(D), row(4 * D),
                   const((8, D)), const((HEADS, CHUNK, CHUNK)), const((CHUNK, HEADS))),
        scratch_shapes=[pltpu.VMEM((t + 2 * HALO, D), F32), pltpu.VMEM((t, D), F32), pltpu.VMEM((t, D), F32),
                        pltpu.VMEM((t, D), F32), pltpu.VMEM((t, D_MIX), BF16), pltpu.VMEM((CHUNK, D), F32),
                        pltpu.VMEM((SUB, CONV_PHASE_ROWS, CONV_BLK), F32)],
        compiler_params=_params(1),
    )(z, z, z, x, tgt, mod, conv_w_g, conv_b, cln_g, cln_b, sln_g, sln_b, final_g, ws_b, wst_b, bs_full, wout)


def _bwd_in_proj(z, dcv, dz_rest, x, dx2, mod, norm_g, conv_w_g, win_g):
    s = x.shape[0]
    t = ROW_TILE
    n_steps = s // t
    zp_spec, zn_spec = _halo_specs(t, s, 2 * D)
    dp_spec, dn_spec = _halo_specs(t, s, D)

    def body(z_ref, zp_ref, zn_ref, dcv_ref, dcvp_ref, dcvn_ref, dzr_ref, x_ref, dx2_ref, mod_ref, ng_ref, cw_ref, w_ref,
             gx_ref, dz_ref, acc_ref, gcw_ref,
             gext, dext, dg, taps, dpad, qbuf):
        i = pl.program_id(0)

        @pl.when(i == 0)
        def _():
            acc_ref[...] = jnp.zeros_like(acc_ref)
            gcw_ref[...] = jnp.zeros_like(gcw_ref)

        not_first = i > 0
        not_last = i < n_steps - 1
        gext[0:HALO, :] = jnp.where(not_first, _glu(zp_ref), 0.0)
        gext[HALO:HALO + t, :] = _glu(z_ref)
        gext[HALO + t:2 * HALO + t, :] = jnp.where(not_last, _glu(zn_ref), 0.0)
        dext[0:HALO, :] = jnp.where(not_first, dcvp_ref[...], 0.0)
        dext[HALO:HALO + t, :] = dcv_ref[...]
        dext[HALO + t:2 * HALO + t, :] = jnp.where(not_last, dcvn_ref[...], 0.0)

        taps[...] = jnp.zeros_like(taps)
        dpad[0:SUB, :] = jnp.zeros((SUB, D), F32)
        dpad[SUB:SUB + t, :] = dcv_ref[...]
        dpad[SUB + t:2 * SUB + t, :] = jnp.zeros((SUB, D), F32)
        for blk in range(N_DEV):
            cols = slice(blk * CONV_BLK, (blk + 1) * CONV_BLK)
            dg[:, cols] = _conv_taps(dext, qbuf, cols, lambda k, blk=blk: cw_ref[blk, CONV_W - 1 - k:CONV_W - k, :], t)
            for b in range(SUB):
                dshift = dpad[pl.ds(SUB - 1 - b, t + SUB), cols]
                for a in range((CONV_W - b + SUB - 1) // SUB):
                    k = SUB * a + b
                    taps[k:k + 1, :] = _colsum(gext[SUB * a:SUB * a + t + SUB, cols] * dshift)
            gcw_ref[blk] += taps[...]

        a = z_ref[:, 0:D]
        sig = _sigmoid(z_ref[:, D:2 * D])
        dgv = dg[...]
        dz_ref[:, 0:D] = (dgv * sig).astype(BF16)
        dz_ref[:, D:2 * D] = (dgv * a * sig * (1.0 - sig)).astype(BF16)
        dz_ref[:, 2 * D:6 * D] = dzr_ref[...]

        dh = jnp.zeros((t, D), F32)
        for j in range(N_DEV):
            dh = dh + lax.dot_general(dz_ref[:, j * W_IN_BLK:(j + 1) * W_IN_BLK], w_ref[j], (((1,), (1,)), ((), ())),
                                      preferred_element_type=F32)

        xn, r, _, scale = _rms_modulate(x_ref[...], mod_ref)
        ng = ng_ref[...]
        one_scale = 1.0 + scale
        dh_xn = dh * xn
        acc_ref[0:1, :] += _colsum(dh_xn * one_scale)
        acc_ref[1:2, :] += _colsum(dh)
        acc_ref[2:3, :] += _colsum(dh_xn * ng)
        dxn = dh * (ng * one_scale)
        gx_ref[...] = dx2_ref[...] + r * (dxn - xn * jnp.mean(dxn * xn, axis=-1, keepdims=True))

    row = lambda w: pl.BlockSpec((t, w), lambda i: (i, 0))
    const = lambda shape: pl.BlockSpec(shape, lambda i: (0,) * len(shape))
    return pl.pallas_call(
        body, name="bwd_in_proj", grid=(n_steps,),
        out_shape=(jax.ShapeDtypeStruct((s, D), F32), jax.ShapeDtypeStruct((s, D_IN), BF16),
                   jax.ShapeDtypeStruct((8, D), F32), jax.ShapeDtypeStruct((N_DEV, CONV_ROWS, CONV_BLK), F32)),
        in_specs=[pl.BlockSpec((t, 2 * D), lambda i: (i, 0)), zp_spec, zn_spec, row(D), dp_spec, dn_spec, row(4 * D),
                  row(D), row(D), VM, VM, VM, VM],
        out_specs=(row(D), row(D_IN), const((8, D)), const((N_DEV, CONV_ROWS, CONV_BLK))),
        scratch_shapes=[pltpu.VMEM((t + 2 * HALO, D), F32), pltpu.VMEM((t + 2 * HALO, D), F32), pltpu.VMEM((t, D), F32),
                        pltpu.VMEM((CONV_ROWS, CONV_BLK), F32), pltpu.VMEM((t + 2 * SUB, D), F32),
                        pltpu.VMEM((SUB, CONV_PHASE_ROWS, CONV_BLK), F32)],
        compiler_params=_params(1),
    )(z, z, z, dcv, dcv, dcv, dz_rest, x, dx2, mod, norm_g, conv_w_g, win_g)


WGRAD_K = 1024
WGRAD_VMEM_LIMIT = 62 * 1024 * 1024
_OTHER_CHIPS = [(3, True), (3, False), (1, True), (1, False), (2, True), (2, False)]
_OWN_CHIP = [(0, True), (0, False)]
WGRAD_ROLES = ([("out",) + e for e in _OTHER_CHIPS] + [("in",) + e for e in _OTHER_CHIPS]
               + [("out",) + e for e in _OWN_CHIP] + [("in",) + e for e in _OWN_CHIP])


def _wgrad_schedule():
    me, _, chips, c = _neighbours()
    chip_of = [me[:2]] + chips
    blk = lambda r, sibling: _shard_index(chip_of, c, r, sibling)
    out_blk, in_blk, is_out = [], [], []
    last = {"out": blk(*_OTHER_CHIPS[0]), "in": blk(*_OTHER_CHIPS[0])}
    for kind, r, sibling in WGRAD_ROLES:
        last[kind] = blk(r, sibling)
        out_blk.append(last["out"])
        in_blk.append(last["in"])
        is_out.append(1 if kind == "out" else 0)
    as_vec = lambda v: jnp.stack([jnp.asarray(e, jnp.int32) for e in v])
    return as_vec(out_blk), as_vec(in_blk), as_vec(is_out)


def _sum_blocks(gathered):
    total = gathered[0]
    for b in range(1, N_DEV):
        total = total + gathered[b]
    return total


SMALL_ROWS = 48
ROW_NORM_G, ROW_FINAL_G, ROW_B_S, ROW_CONV_B, ROW_CLN_G, ROW_CLN_B, ROW_SLN_G, ROW_SLN_B, ROW_LOSS = range(32, 41)


def _wgrad_reduce(ht, dz, ycatt, dy, small, gws, dmod):
    s = dz.shape[0]
    n_kc = s // WGRAD_K
    n_steps = len(WGRAD_ROLES)
    first_in = [k for k, role in enumerate(WGRAD_ROLES) if role[0] == "in"][0]
    blk_in, blk_out = (D, W_IN_BLK), (W_OUT_BLK, D)

    def body(out_blk, in_blk, is_out, ht_ref, dz_ref, yt_ref, dy_ref, small_ref, gws_ref, dmod_ref,
             oin_ref, oout_ref, osmall_ref, ogws_ref, odmod_ref, obada_ref, oloss_ref,
             acc_in, acc_out, p1_in, p1_out, l1_in, l1_out, l2_in, l2_out, small_g, gws_g, dmod_g,
             s1_send, s1_recv, s2_send, s2_recv, t1_send, t1_recv, t2_send, t2_recv, ag_send, ag_recv):
        step, kc = pl.program_id(0), pl.program_id(1)
        pay = {"in": (acc_in, p1_in, oin_ref, _ChipReduce(l1_in, l2_in, s1_send, s1_recv, s2_send, s2_recv)),
               "out": (acc_out, p1_out, oout_ref, _ChipReduce(l1_out, l2_out, t1_send, t1_recv, t2_send, t2_recv))}
        ag = _AllGather([dmod_g, small_g, gws_g], ag_send, ag_recv)
        last_kc = kc == n_kc - 1

        @pl.when((step == 0) & (kc == 0))
        def _():
            me = _my_block()
            small_g[me] = small_ref[...]
            gws_g[me] = gws_ref[...].astype(BF16)
            dmod_g[me] = dmod_ref[...]
            for a in range(3):
                ag.send_own(a)

        @pl.when((step == 4) & (kc == 0))
        def _():
            for a in range(3):
                ag.other_chips(a)

        def accumulate(acc, prod):
            @pl.when(kc == 0)
            def _():
                acc[...] = prod

            @pl.when(kc != 0)
            def _():
                acc[...] += prod

        @pl.when(is_out[step] == 1)
        def _():
            accumulate(acc_out, jnp.dot(yt_ref[...], dy_ref[kc], preferred_element_type=F32))

        @pl.when(is_out[step] == 0)
        def _():
            accumulate(acc_in, jnp.dot(ht_ref[kc], dz_ref[...], preferred_element_type=F32))

        for k, (kind, r, sibling) in enumerate(WGRAD_ROLES):
            @pl.when((step == k) & last_kc)
            def _(kind=kind, r=r, sibling=sibling):
                acc, p1, out, red = pay[kind]
                if sibling:
                    p1[r] = acc[...].astype(BF16)
                    red.to_sibling(r, p1.at[r]).start()
                else:
                    chip_sum = red.combine(r, acc[...])
                    if r == 0:
                        out[...] = chip_sum

        @pl.when((step == n_steps - 1) & last_kc)
        def _():
            for kind in ("out", "in"):
                _, _, out, red = pay[kind]
                out[...] = red.finish(out[...])
            for a in range(3):
                ag.from_sibling(a)
            for a in range(3):
                ag.sent(a)
            tot_small = _sum_blocks(small_g)
            osmall_ref[...] = tot_small
            oloss_ref[...] = jnp.full(oloss_ref.shape, (0.5 / D) * jnp.sum(tot_small[ROW_LOSS:ROW_LOSS + 1, :]), F32)
            tot_gws = gws_g[0].astype(F32)
            for b in range(1, N_DEV):
                tot_gws = tot_gws + gws_g[b].astype(F32)
            ogws_ref[...] = tot_gws
            obada_ref[...] = _sum_blocks(dmod_g)
            for b in range(N_DEV):
                odmod_ref[b:b + 1, :] = dmod_g[b, 0:1, :]

    def kc_of(working, step, kc, hold_first):
        held = jnp.where(step < hold_first, 0, n_kc - 1)
        return jnp.where(working, kc, held)

    out_kc = lambda i, kc, ob, ib, io: kc_of(io[i] == 1, i, kc, 0)
    in_kc = lambda i, kc, ob, ib, io: kc_of(io[i] == 0, i, kc, first_in)
    sems = lambda n: [pltpu.SemaphoreType.DMA((n,)), pltpu.SemaphoreType.DMA((n,))]
    grid_spec = pltpu.PrefetchScalarGridSpec(
        num_scalar_prefetch=3, grid=(n_steps, n_kc),
        in_specs=[VM,
                  pl.BlockSpec((WGRAD_K, W_IN_BLK), lambda i, kc, ob, ib, io: (in_kc(i, kc, ob, ib, io), ib[i])),
                  pl.BlockSpec((W_OUT_BLK, WGRAD_K), lambda i, kc, ob, ib, io: (ob[i], out_kc(i, kc, ob, ib, io))),
                  VM,
                  VM, VM, VM],
        out_specs=(VM,) * 7,
        scratch_shapes=[pltpu.VMEM(blk_in, F32), pltpu.VMEM(blk_out, F32),
                        pltpu.VMEM((4,) + blk_in, BF16), pltpu.VMEM((4,) + blk_out, BF16),
                        pltpu.VMEM((4,) + blk_in, BF16), pltpu.VMEM((4,) + blk_out, BF16),
                        pltpu.VMEM((3,) + blk_in, BF16), pltpu.VMEM((3,) + blk_out, BF16),
                        pltpu.VMEM((N_DEV,) + small.shape, F32), pltpu.VMEM((N_DEV,) + gws.shape, BF16),
                        pltpu.VMEM((N_DEV,) + dmod.shape, F32)]
        + sems(4) + sems(3) + sems(4) + sems(3) + sems(21))
    return pl.pallas_call(
        body, name="wgrad_reduce", grid_spec=grid_spec,
        out_shape=(jax.ShapeDtypeStruct(blk_in, F32), jax.ShapeDtypeStruct(blk_out, F32),
                   jax.ShapeDtypeStruct(small.shape, F32), jax.ShapeDtypeStruct(gws.shape, F32),
                   jax.ShapeDtypeStruct((N_DEV, 3 * D), F32), jax.ShapeDtypeStruct((8, 3 * D), F32),
                   jax.ShapeDtypeStruct((8, 128), F32)),
        compiler_params=_params(2, vmem=WGRAD_VMEM_LIMIT),
    )(*_wgrad_schedule(), ht, dz, ycatt, dy.reshape(n_kc, WGRAD_K, D), small, gws, dmod)


def _adamw_math(w, g, m, v):
    m = ADAM_B1 * m + (1.0 - ADAM_B1) * g
    v = ADAM_B2 * v + (1.0 - ADAM_B2) * (g * g)
    m_hat = m / (1.0 - ADAM_B1 ** ADAM_STEP)
    v_hat = v / (1.0 - ADAM_B2 ** ADAM_STEP)
    delta = -ADAM_LR * (m_hat / (jnp.sqrt(v_hat) + ADAM_EPS) + ADAM_WD * w)
    return delta, m, v


VECTORS = ["norm_g", "conv_b", "conv_ln_g", "conv_ln_b", "sg_ln_g", "sg_ln_b", "final_g", "b_s"]
VECTOR_ROWS = [ROW_NORM_G, ROW_CONV_B, ROW_CLN_G, ROW_CLN_B, ROW_SLN_G, ROW_SLN_B, ROW_FINAL_G, ROW_B_S]
ADAM_STEPS = 4


def _adamw_all(me, g_w_in, g_w_out, g_w_s, small_sum, dmod_all, bada8, c_all, matrices, b_ada, conv_w, vectors):
    mat_shapes = [(D, W_IN_BLK), (D, W_ADA_BLK), (W_OUT_BLK, D), (HEADS * CHUNK, CHUNK)]
    mat_blocks = [(sh[0] // ADAM_STEPS, sh[1]) for sh in mat_shapes]
    n_small = 2 + len(VECTORS)

    def body(me_ref, gin_ref, gout_ref, gws_ref, taps_ref, small_ref, dmc_ref, bada_ref, c_ref, *refs):
        params = refs[:3 * (4 + n_small)]
        outs = refs[3 * (4 + n_small):-2]
        act, gada = refs[-2:]
        mat_out, gada_out, small_out = outs[:12], outs[12], outs[13:]
        i = pl.program_id(0)

        @pl.when(i == 0)
        def _():
            for b in range(N_DEV):
                cb = c_ref[b, 0:1, :]
                act[b:b + 1, :] = cb * _sigmoid(cb)
            gada[...] = lax.dot_general(act[...], dmc_ref[...], (((0,), (0,)), ((), ())), preferred_element_type=F32,
                                        precision=lax.Precision.HIGHEST)
            small_grads = [bada_ref[0:1, :], taps_ref[0:CONV_W, :]] + [small_ref[r:r + 1, :] for r in VECTOR_ROWS]
            for k, g in enumerate(small_grads):
                w_ref, m_ref, v_ref = params[3 * (4 + k):3 * (5 + k)]
                o = small_out[4 * k:4 * k + 4]
                o[0][...] = g
                o[1][...], o[2][...], o[3][...] = _adamw_math(w_ref[...], g, m_ref[...], v_ref[...])

        rows = pl.ds(pl.multiple_of(i * mat_blocks[1][0], mat_blocks[1][0]), mat_blocks[1][0])
        g_ada = gada[rows, :]
        gada_out[...] = g_ada
        for k, g in enumerate([gin_ref[...], g_ada, gout_ref[...], gws_ref[...]]):
            w_ref, m_ref, v_ref = params[3 * k:3 * k + 3]
            o = mat_out[3 * k:3 * k + 3]
            o[0][...], o[1][...], o[2][...] = _adamw_math(w_ref[...], g, m_ref[...], v_ref[...])

    rows_of = lambda blk: pl.BlockSpec(blk, lambda i, me_ref: (i, 0))
    mat_specs = [rows_of(b) for b in mat_blocks]
    grid_spec = pltpu.PrefetchScalarGridSpec(
        num_scalar_prefetch=1, grid=(ADAM_STEPS,),
        in_specs=[mat_specs[0], mat_specs[2], mat_specs[3],
                  pl.BlockSpec((CONV_ROWS, CONV_BLK), lambda i, me_ref: (0, me_ref[0])), VM,
                  pl.BlockSpec((N_DEV, W_ADA_BLK), lambda i, me_ref: (0, me_ref[0])), VM, VM]
        + [s for s in mat_specs for _ in range(3)] + [VM] * (3 * n_small),
        out_specs=tuple([s for s in mat_specs for _ in range(3)] + [mat_specs[1]] + [VM] * (4 * n_small)),
        scratch_shapes=[pltpu.VMEM((N_DEV, D), F32), pltpu.VMEM((D, W_ADA_BLK), F32)])
    small_shapes = [b_ada[0].shape, conv_w[0].shape] + [(1, D)] * len(VECTORS)
    out_shape = tuple([jax.ShapeDtypeStruct(sh, F32) for sh in mat_shapes for _ in range(3)]
                      + [jax.ShapeDtypeStruct(mat_shapes[1], F32)]
                      + [jax.ShapeDtypeStruct(sh, F32) for sh in small_shapes for _ in range(4)])
    flat = [a for group in matrices for a in group] + list(b_ada) + list(conv_w) + [a for group in vectors for a in group]
    return pl.pallas_call(body, name="adamw_all", grid_spec=grid_spec, out_shape=out_shape,
                          compiler_params=_params(1))(
        me, g_w_in, g_w_out, g_w_s, small_sum, small_sum, dmod_all, bada8, c_all, *flat)


def kernel(x, c, w_ada, b_ada, norm_g, w_in, conv_w, conv_b, conv_ln_g, conv_ln_b, sg_ln_g, sg_ln_b, w_s, b_s, w_out, final_g, loss_target, m_w_ada, m_b_ada, m_norm_g, m_w_in, m_conv_w, m_conv_b, m_conv_ln_g, m_conv_ln_b, m_sg_ln_g, m_sg_ln_b, m_w_s, m_b_s, m_w_out, m_final_g, v_w_ada, v_b_ada, v_norm_g, v_w_in, v_conv_w, v_conv_b, v_conv_ln_g, v_conv_ln_b, v_sg_ln_g, v_sg_ln_b, v_w_s, v_b_s, v_w_out, v_final_g):
    me = 4 * lax.axis_index("x") + 2 * lax.axis_index("y") + lax.axis_index("c")
    x2d, tgt2d = x[0], loss_target[0]
    row1 = lambda a: a.reshape(1, D)
    taps = lambda a: jnp.pad(a.reshape(CONV_W, CONV_BLK), ((0, CONV_ROWS - CONV_W), (0, 0)))

    z, ht, win_g, wout_g, cw_g, c_all, mod = _fwd_in_proj(
        x2d, jnp.broadcast_to(c, (8, D)), w_ada[0], b_ada, norm_g, w_in[0], w_out[0], taps(conv_w))
    ws_b = w_s[0].astype(BF16)
    wst_b = jnp.swapaxes(w_s[0], 1, 2).astype(BF16)
    bs_full = jnp.repeat(b_s[0].T, HEAD_DIM, axis=1)

    dx2, ycatt, dy, dcv, dz_rest, acc_a, gws, gbs = _mix_and_head(
        z, x2d, tgt2d, mod, cw_g, conv_b, conv_ln_g, conv_ln_b, sg_ln_g, sg_ln_b, row1(final_g), ws_b, wst_b, bs_full,
        wout_g.reshape(D_MIX, D))
    grad_x, dz, acc_b, gcw = _bwd_in_proj(z, dcv, dz_rest, x2d, dx2, mod, norm_g, cw_g, win_g)

    small = jnp.concatenate(
        [jnp.transpose(gcw, (1, 0, 2)).reshape(CONV_ROWS, D), acc_b[0:1], acc_a[0:1], gbs.T.reshape(1, D), acc_a[2:8],
         jnp.zeros((SMALL_ROWS - ROW_LOSS - 1, D), F32)], axis=0)
    dmod_row = jnp.concatenate([acc_b[1:2], acc_b[2:3], acc_a[1:2]], axis=1)
    g_w_in, g_w_out, small_sum, g_w_s, dmod_all, bada8, loss_tile = _wgrad_reduce(
        ht, dz, ycatt, dy, small, gws.reshape(HEADS * CHUNK, CHUNK), jnp.broadcast_to(dmod_row, (8, 3 * D)))

    given = dict(w_ada=(w_ada, m_w_ada, v_w_ada), b_ada=(b_ada, m_b_ada, v_b_ada), norm_g=(norm_g, m_norm_g, v_norm_g),
                 w_in=(w_in, m_w_in, v_w_in), conv_w=(conv_w, m_conv_w, v_conv_w), conv_b=(conv_b, m_conv_b, v_conv_b),
                 conv_ln_g=(conv_ln_g, m_conv_ln_g, v_conv_ln_g), conv_ln_b=(conv_ln_b, m_conv_ln_b, v_conv_ln_b),
                 sg_ln_g=(sg_ln_g, m_sg_ln_g, v_sg_ln_g), sg_ln_b=(sg_ln_b, m_sg_ln_b, v_sg_ln_b),
                 w_s=(w_s, m_w_s, v_w_s), b_s=(b_s, m_b_s, v_b_s), w_out=(w_out, m_w_out, v_w_out),
                 final_g=(final_g, m_final_g, v_final_g))
    as2d = lambda name, shape: tuple(a.reshape(shape) for a in given[name])
    res = _adamw_all(
        jnp.reshape(me, (1,)).astype(jnp.int32), g_w_in, g_w_out, g_w_s, small_sum, dmod_all, bada8, c_all,
        [as2d("w_in", (D, W_IN_BLK)), as2d("w_ada", (D, W_ADA_BLK)), as2d("w_out", (W_OUT_BLK, D)),
         as2d("w_s", (HEADS * CHUNK, CHUNK))],
        given["b_ada"], as2d("conv_w", (CONV_W, CONV_BLK)), [as2d(n, (1, D)) for n in VECTORS])
    out = {}
    for k, name in enumerate(["w_in", "w_ada", "w_out", "w_s"]):
        out[name] = [None] + list(res[3 * k:3 * k + 3])
    out["w_in"][0], out["w_ada"][0], out["w_out"][0], out["w_s"][0] = g_w_in, res[12], g_w_out, g_w_s
    for k, name in enumerate(["b_ada", "conv_w"] + VECTORS):
        out[name] = list(res[13 + 4 * k:17 + 4 * k])
    order = ["w_ada", "b_ada", "norm_g", "w_in", "conv_w", "conv_b", "conv_ln_g", "conv_ln_b", "sg_ln_g", "sg_ln_b",
             "w_s", "b_s", "w_out", "final_g"]
    outs = [loss_tile[0, 0], grad_x.reshape(x.shape)]
    for kind in range(4):
        outs += [out[n][kind].reshape(given[n][0].shape) for n in order]
    return tuple(outs)
```

```python
import functools

import jax
import jax.numpy as jnp
from jax import lax
from jax.experimental import pallas as pl
from jax.experimental.pallas import tpu as pltpu

F32 = jnp.float32
BF16 = jnp.bfloat16
MESH = pl.DeviceIdType.MESH

D = 1024
D_IN = 6 * D
D_MIX = 2 * D
N_DEV = 8
W_IN_BLK = D_IN // N_DEV
W_OUT_BLK = D_MIX // N_DEV
W_ADA_BLK = 3 * D // N_DEV
CONV_BLK = D // N_DEV
CONV_W = 31
CONV_HALF = CONV_W // 2
CONV_ROWS = 32
HALO = 16
CHUNK = 128
HEADS = 8
HEAD_DIM = 128
EPS = 1e-6
ROW_TILE = 256
VMEM_LIMIT = 56 * 1024 * 1024

ADAM_LR = 0.001
ADAM_B1 = 0.9
ADAM_B2 = 0.999
ADAM_EPS = 1e-08
ADAM_WD = 0.01
ADAM_STEP = 10

VM = pl.BlockSpec(memory_space=pltpu.VMEM)


def _params(grid_rank=0, vmem=VMEM_LIMIT):
    sem = ("arbitrary",) * grid_rank if grid_rank else None
    return pltpu.CompilerParams(dimension_semantics=sem, vmem_limit_bytes=vmem)


def _sigmoid(t):
    return jax.nn.sigmoid(t)


def _dsilu(t, sig):
    return sig * (1.0 + t * (1.0 - sig))


def _mesh_pos():
    return lax.axis_index("x"), lax.axis_index("y"), lax.axis_index("c")


def _neighbours():
    x, y, c = _mesh_pos()
    nb1 = (x + c * (1 - 2 * x), y + (1 - c) * (1 - 2 * y))
    nb2 = (x + (1 - c) * (1 - 2 * x), y + c * (1 - 2 * y))
    return (x, y, c), (x, y, 1 - c), [nb1, nb2, (1 - x, 1 - y)], c


def _sibling_slot(j):
    return j if j == 2 else 1 - j


class _AllGather:
    def __init__(self, bufs, send_sems, recv_sems):
        self.bufs, self.send_sems, self.recv_sems = bufs, send_sems, recv_sems
        self.me, self.sib, self.chips, self.c = _neighbours()

    def _copy(self, a, k, block, to):
        px, py, pc = block
        ref = self.bufs[a].at[4 * px + 2 * py + pc]
        return pltpu.make_async_remote_copy(
            src_ref=ref, dst_ref=ref, send_sem=self.send_sems.at[7 * a + k], recv_sem=self.recv_sems.at[7 * a + k],
            device_id=to, device_id_type=MESH)

    def _outgoing(self, a, k):
        if k == 0:
            return self._copy(a, 0, self.me, self.sib)
        if k <= 2:
            return self._copy(a, k, self.me, (*self.chips[k - 1], self.c))
        if k == 3:
            return self._copy(a, 3, (*self.chips[0], self.c), (*self.chips[1], self.c))
        return self._copy(a, k, (*self.chips[_sibling_slot(k - 4)], self.c), self.sib)

    def source(self, k):
        if k == 0:
            return self.sib
        return (*self.chips[(k - 1) % 3], self.c if k <= 3 else 1 - self.c)

    def block_index(self, k):
        px, py, pc = self.source(k)
        return 4 * px + 2 * py + pc

    def send_own(self, a):
        for k in range(3):
            self._outgoing(a, k).start()

    def arrived(self, a, k):
        self._copy(a, k, self.source(k), self.me).wait_recv()

    def relay(self, a):
        self._outgoing(a, 3).start()

    def pass_on(self, a, j):
        self._outgoing(a, 4 + _sibling_slot(j)).start()

    def other_chips(self, a):
        self.arrived(a, 1)
        self.relay(a)
        self.pass_on(a, 0)
        for j in (1, 2):
            self.arrived(a, 1 + j)
            self.pass_on(a, j)

    def from_sibling(self, a):
        for k in (0, 4, 5, 6):
            self.arrived(a, k)

    def sent(self, a):
        for k in range(7):
            self._outgoing(a, k).wait_send()


def _shard_index(chip_of, c, r, sibling):
    if sibling and r:
        r = 1 + _sibling_slot(r - 1)
    cx, cy = chip_of[r]
    return 4 * cx + 2 * cy + ((1 - c) if sibling else c)


class _ChipReduce:
    def __init__(self, l1, l2, s1_send, s1_recv, s2_send, s2_recv):
        self.l1, self.l2 = l1, l2
        self.s1_send, self.s1_recv, self.s2_send, self.s2_recv = s1_send, s1_recv, s2_send, s2_recv
        me, self.sib, chips, self.c = _neighbours()
        self.chip_of = [me[:2]] + chips

    def block(self, r, sibling):
        return _shard_index(self.chip_of, self.c, r, sibling)

    def to_sibling(self, r, src=None):
        return pltpu.make_async_remote_copy(
            src_ref=self.l1.at[r] if src is None else src, dst_ref=self.l1.at[r], send_sem=self.s1_send.at[r],
            recv_sem=self.s1_recv.at[r], device_id=self.sib, device_id_type=MESH)

    ORDER = (3, 1, 2, 0)

    def send_to_sibling(self, r, stage, value):
        k = self.ORDER.index(r)
        if k:
            self.to_sibling(self.ORDER[k - 1], stage).wait_send()
        stage[...] = value.astype(BF16)
        self.to_sibling(r, stage).start()

    def to_chip(self, r):
        target = self.chip_of[1] if r == 3 else self.chip_of[r]
        return pltpu.make_async_remote_copy(
            src_ref=self.l1.at[r], dst_ref=self.l2.at[r - 1], send_sem=self.s2_send.at[r - 1],
            recv_sem=self.s2_recv.at[r - 1], device_id=(*target, self.c), device_id_type=MESH)

    def combine(self, r, mine):
        self.to_sibling(r).wait_recv()
        both = mine + self.l1[r].astype(F32)
        if r == 0:
            return both
        if r == 2:
            self.to_chip(3).wait_recv()
            both = both + self.l2[2].astype(F32)
        self.l1[r] = both.astype(BF16)
        self.to_chip(r).start()
        return None

    def finish(self, own_chip_sum):
        total = own_chip_sum
        for r in (1, 2):
            self.to_chip(r).wait_recv()
            total = total + self.l2[r - 1].astype(F32)
        self.to_sibling(self.ORDER[-1]).wait_send()
        for r in (1, 2, 3):
            self.to_chip(r).wait_send()
        return total


def _my_block():
    x, y, c = _mesh_pos()
    return 4 * x + 2 * y + c


def _modulation(c_g, w_ref, b_ref, mod_ref, part, land, send_sems, recv_sems):
    x, y, c = _mesh_pos()
    me = 4 * x + 2 * y + c
    w = w_ref[...]
    for b in range(N_DEV):
        cb = c_g[b]
        part[b] = jnp.dot(cb * _sigmoid(cb), w, preferred_element_type=F32, precision=lax.Precision.HIGHEST)
    land[me] = part[me]

    def copy(b):
        return pltpu.make_async_remote_copy(
            src_ref=part.at[b], dst_ref=land.at[me], send_sem=send_sems.at[b], recv_sem=recv_sems.at[me],
            device_id=(b // 4, (b // 2) % 2, b % 2), device_id_type=MESH)

    def arrival(b):
        return pltpu.make_async_remote_copy(
            src_ref=part.at[b], dst_ref=land.at[b], send_sem=send_sems.at[b], recv_sem=recv_sems.at[b],
            device_id=(b // 4, (b // 2) % 2, b % 2), device_id_type=MESH)

    for b in range(N_DEV):
        @pl.when(b != me)
        def _():
            copy(b).start()
    for b in range(N_DEV):
        @pl.when(b != me)
        def _():
            arrival(b).wait_recv()
            copy(b).wait_send()
    for b in range(N_DEV):
        cols = slice(b * W_ADA_BLK, (b + 1) * W_ADA_BLK)
        mod_ref[:, cols] = land[b] + b_ref[:, cols]


def _rms_modulate(x, mod_ref):
    shift = mod_ref[0:1, 0:D]
    scale = mod_ref[0:1, D:2 * D]
    r = lax.rsqrt(jnp.mean(x * x, axis=-1, keepdims=True) + EPS)
    xn = x * r
    return xn, r, shift, scale


FWD_TILE = 512
FWD_Z_TILE = 1024


def _fwd_in_proj(x, c_rep, w_ada, b_ada, norm_g, w_in, w_out, conv_w):
    s = x.shape[0]
    t, tz = FWD_TILE, FWD_Z_TILE
    n_tiles = s // t
    C_PAY, WIN_PAY, WOUT_PAY, CW_PAY = 0, 1, 2, 3

    def body(x_hbm, c_ref, wada_ref, bada_ref, ng_ref, win_ref, wout_ref, cw_ref,
             z_hbm, ht_ref, win_g, wout_g, cw_g, c_g, mod_ref,
             h, xbuf, zbuf, part, land, ag_send, ag_recv, mod_send, mod_recv, x_sem, z_sem):
        me = _my_block()
        c_g[me] = c_ref[...]
        ag = _AllGather([c_g, win_g, wout_g, cw_g], ag_send, ag_recv)
        ag.send_own(C_PAY)

        def x_copy(i):
            return pltpu.make_async_copy(x_hbm.at[pl.ds(i * t, t), :], xbuf.at[i % 2], x_sem.at[i % 2])

        x_copy(0).start()
        win_g[me] = win_ref[...].astype(BF16)
        wout_g[me] = wout_ref[...].astype(BF16)
        cw_g[me] = cw_ref[...]

        ag.other_chips(C_PAY)
        ag.from_sibling(C_PAY)
        _modulation(c_g, wada_ref, bada_ref, mod_ref, part, land, mod_send, mod_recv)
        for a in (WIN_PAY, WOUT_PAY, CW_PAY):
            ag.send_own(a)

        for i in range(n_tiles):
            x_copy(i).wait()
            if i + 1 < n_tiles:
                x_copy(i + 1).start()
            xn, _, shift, scale = _rms_modulate(xbuf[i % 2], mod_ref)
            hh = xn * ng_ref[...] * (1.0 + scale) + shift
            h[i * t:(i + 1) * t, :] = hh.astype(BF16)
            ht_ref[(i * t) // WGRAD_K, :, (i * t) % WGRAD_K:(i * t) % WGRAD_K + t] = hh.T.astype(BF16)

        def z_copy(slot, row0, col0):
            return pltpu.make_async_copy(zbuf.at[slot], z_hbm.at[pl.ds(row0, tz), pl.ds(col0, W_IN_BLK)], z_sem.at[slot])

        done = [0]

        def z_block(blk):
            col0 = pl.multiple_of(blk * W_IN_BLK, 128)
            for i in range(s // tz):
                slot = done[0] % 2
                if done[0] >= 2:
                    z_copy(slot, 0, 0).wait()
                zbuf[slot] = jnp.dot(h[i * tz:(i + 1) * tz, :], win_g[blk], preferred_element_type=F32)
                z_copy(slot, i * tz, col0).start()
                done[0] += 1

        z_block(me)
        ag.arrived(WIN_PAY, 0)
        z_block(ag.block_index(0))
        ag.arrived(WIN_PAY, 1)
        ag.relay(WIN_PAY)
        ag.pass_on(WIN_PAY, 0)
        ag.arrived(WIN_PAY, 2)
        ag.pass_on(WIN_PAY, 1)
        for k in (1, 2, 4, 5):
            if k >= 4:
                ag.arrived(WIN_PAY, k)
            z_block(ag.block_index(k))
        ag.arrived(WIN_PAY, 3)
        ag.pass_on(WIN_PAY, 2)
        z_block(ag.block_index(3))
        ag.arrived(WIN_PAY, 6)
        z_block(ag.block_index(6))

        for a in (WOUT_PAY, CW_PAY):
            ag.other_chips(a)
        for a in (WOUT_PAY, CW_PAY):
            ag.from_sibling(a)
        for a in (C_PAY, WIN_PAY, WOUT_PAY, CW_PAY):
            ag.sent(a)
        z_copy(0, 0, 0).wait()
        z_copy(1, 0, 0).wait()

    any_spec = pl.BlockSpec(memory_space=pl.ANY)
    return pl.pallas_call(
        body, name="fwd_in_proj",
        out_shape=(jax.ShapeDtypeStruct((s, D_IN), F32), jax.ShapeDtypeStruct((s // WGRAD_K, D, WGRAD_K), BF16),
                   jax.ShapeDtypeStruct((N_DEV,) + w_in.shape, BF16), jax.ShapeDtypeStruct((N_DEV,) + w_out.shape, BF16),
                   jax.ShapeDtypeStruct((N_DEV,) + conv_w.shape, F32), jax.ShapeDtypeStruct((N_DEV,) + c_rep.shape, F32),
                   jax.ShapeDtypeStruct((8, 3 * D), F32)),
        in_specs=[any_spec] + [VM] * 7, out_specs=(any_spec,) + (VM,) * 6,
        scratch_shapes=[pltpu.VMEM((s, D), BF16), pltpu.VMEM((2, t, D), F32), pltpu.VMEM((2, tz, W_IN_BLK), F32),
                        pltpu.VMEM((N_DEV, 8, W_ADA_BLK), F32), pltpu.VMEM((N_DEV, 8, W_ADA_BLK), F32),
                        pltpu.SemaphoreType.DMA((28,)), pltpu.SemaphoreType.DMA((28,)),
                        pltpu.SemaphoreType.DMA((N_DEV,)), pltpu.SemaphoreType.DMA((N_DEV,)),
                        pltpu.SemaphoreType.DMA((2,)), pltpu.SemaphoreType.DMA((2,))],
        compiler_params=_params(),
    )(x, c_rep, w_ada, b_ada, norm_g, w_in, w_out, conv_w)


def _halo_specs(t, s, width):
    per = t // HALO
    last = s // HALO - 1
    prev = pl.BlockSpec((HALO, width), lambda i: (jnp.maximum(i * per - 1, 0), 0))
    nxt = pl.BlockSpec((HALO, width), lambda i: (jnp.minimum((i + 1) * per, last), 0))
    return prev, nxt


def _glu(ref):
    return ref[:, 0:D] * _sigmoid(ref[:, D:2 * D])


SUB = 8
CONV_PHASE_ROWS = ROW_TILE + SUB


def _conv_taps(ext, qbuf, cols, tap, t):
    out = None
    for b in range(SUB):
        q = None
        for a in range((CONV_W - b + SUB - 1) // SUB):
            term = ext[SUB * a:SUB * a + t + SUB, cols] * tap(SUB * a + b)
            q = term if q is None else q + term
        qbuf[b] = q
        shifted = qbuf[b, pl.ds(b + 1, t), :]
        out = shifted if out is None else out + shifted
    return out


def _layer_norm_stats(v):
    mu = jnp.mean(v, axis=-1, keepdims=True)
    cen = v - mu
    rstd = lax.rsqrt(jnp.mean(cen * cen, axis=-1, keepdims=True) + EPS)
    return cen * rstd, rstd


def _layer_norm_bwd(dy_hat, hat, rstd):
    m1 = jnp.mean(dy_hat, axis=-1, keepdims=True)
    m2 = jnp.mean(dy_hat * hat, axis=-1, keepdims=True)
    return rstd * (dy_hat - m1 - hat * m2)


def _colsum(v):
    return jnp.sum(v, axis=0, keepdims=True)


def _mix_and_head(z, x, tgt, mod, conv_w_g, conv_b, cln_g, cln_b, sln_g, sln_b, final_g, ws_b, wst_b, bs_full, wout):
    s = x.shape[0]
    t = ROW_TILE
    n_chunks = t // CHUNK
    n_steps = s // t
    prev_spec, next_spec = _halo_specs(t, s, 2 * D)

    def body(z_ref, zp_ref, zn_ref, x_ref, tgt_ref, mod_ref, cw_ref, cb_ref, clg_ref, clb_ref, slg_ref, slb_ref, fg_ref,
             ws_ref, wst_ref, bs_ref, wout_ref,
             dx2_ref, ycatt_ref, dy_ref, dcv_ref, dzr_ref, acc_ref, gws_ref, gbs_ref,
             gext, cv, vs, dvn, ycat, gbs_acc, qbuf):
        i = pl.program_id(0)

        @pl.when(i == 0)
        def _():
            acc_ref[...] = jnp.zeros_like(acc_ref)
            gws_ref[...] = jnp.zeros_like(gws_ref)
            gbs_acc[...] = jnp.zeros_like(gbs_acc)

        gext[0:HALO, :] = jnp.where(i > 0, _glu(zp_ref), 0.0)
        gext[HALO:HALO + t, :] = _glu(z_ref)
        gext[HALO + t:2 * HALO + t, :] = jnp.where(i < n_steps - 1, _glu(zn_ref), 0.0)
        for blk in range(N_DEV):
            cols = slice(blk * CONV_BLK, (blk + 1) * CONV_BLK)
            cv[:, cols] = _conv_taps(gext, qbuf, cols, lambda k, blk=blk: cw_ref[blk, k:k + 1, :], t) + cb_ref[:, cols]
        ln_hat, ln_rstd = _layer_norm_stats(cv[...])
        ln_a = ln_hat * clg_ref[...] + clb_ref[...]
        sig_ln = _sigmoid(ln_a)
        sa = ln_a * sig_ln
        a_gate = z_ref[:, 2 * D:3 * D]
        sig_ag = _sigmoid(a_gate)
        s_gate = a_gate * sig_ag
        ya = sa * s_gate

        v_hat, v_rstd = _layer_norm_stats(z_ref[:, 4 * D:5 * D])
        vn = v_hat * slg_ref[...] + slb_ref[...]
        vnb = vn.astype(BF16)
        for n in range(n_chunks):
            rows = slice(n * CHUNK, (n + 1) * CHUNK)
            for h in range(HEADS):
                cols = slice(h * HEAD_DIM, (h + 1) * HEAD_DIM)
                vs[rows, cols] = jnp.dot(ws_ref[h], vnb[rows, cols], preferred_element_type=F32) + bs_ref[:, cols]
        u = z_ref[:, 3 * D:4 * D]
        b_gate = z_ref[:, 5 * D:6 * D]
        sig_bg = _sigmoid(b_gate)
        s_bg = b_gate * sig_bg
        vsv = vs[...]
        yb = u * vsv * s_bg

        ycat[:, 0:D] = ya.astype(BF16)
        ycat[:, D:2 * D] = yb.astype(BF16)
        ycatt_ref[0:D, :] = ya.T.astype(BF16)
        ycatt_ref[D:2 * D, :] = yb.T.astype(BF16)
        y = jnp.dot(ycat[...], wout_ref[...], preferred_element_type=F32)
        gate = mod_ref[0:1, 2 * D:3 * D]
        x2 = x_ref[...] + gate * y
        r2 = lax.rsqrt(jnp.mean(x2 * x2, axis=-1, keepdims=True) + EPS)
        x2n = x2 * r2
        fg = fg_ref[...]
        diff = x2n * fg - tgt_ref[...]
        acc_ref[7:8, :] += _colsum(diff * diff)
        dout = diff * (1.0 / D)
        acc_ref[0:1, :] += _colsum(dout * x2n)
        dx2n = dout * fg
        dx2 = r2 * (dx2n - x2n * jnp.mean(dx2n * x2n, axis=-1, keepdims=True))
        dx2_ref[...] = dx2
        acc_ref[1:2, :] += _colsum(dx2 * y)
        dyb16 = (dx2 * gate).astype(BF16)
        dy_ref[...] = dyb16
        dycat = lax.dot_general(dyb16, wout_ref[...], (((1,), (1,)), ((), ())), preferred_element_type=F32)
        dya = dycat[:, 0:D]
        dyb = dycat[:, D:2 * D]

        du = dyb * vsv * s_bg
        dvs = dyb * u * s_bg
        dbg = dyb * u * vsv * _dsilu(b_gate, sig_bg)
        dvsb = dvs.astype(BF16)
        gbs = gbs_acc[...]
        for n in range(n_chunks):
            rows = slice(n * CHUNK, (n + 1) * CHUNK)
            gbs = gbs + dvs[rows, :]
            for h in range(HEADS):
                cols = slice(h * HEAD_DIM, (h + 1) * HEAD_DIM)
                gws_ref[h] += lax.dot_general(dvsb[rows, cols], vnb[rows, cols], (((1,), (1,)), ((), ())),
                                              preferred_element_type=F32)
                dvn[rows, cols] = jnp.dot(wst_ref[h], dvsb[rows, cols], preferred_element_type=F32)
        gbs_acc[...] = gbs

        @pl.when(i == n_steps - 1)
        def _():
            for h in range(HEADS):
                gbs_ref[:, h:h + 1] = jnp.sum(gbs_acc[:, h * HEAD_DIM:(h + 1) * HEAD_DIM], axis=1, keepdims=True)

        dvnv = dvn[...]
        acc_ref[5:6, :] += _colsum(dvnv * v_hat)
        acc_ref[6:7, :] += _colsum(dvnv)
        dv = _layer_norm_bwd(dvnv * slg_ref[...], v_hat, v_rstd)

        dsa = dya * s_gate
        dagate = dya * sa * _dsilu(a_gate, sig_ag)
        dln = dsa * _dsilu(ln_a, sig_ln)
        acc_ref[3:4, :] += _colsum(dln * ln_hat)
        acc_ref[4:5, :] += _colsum(dln)
        dcv = _layer_norm_bwd(dln * clg_ref[...], ln_hat, ln_rstd)
        acc_ref[2:3, :] += _colsum(dcv)
        dcv_ref[...] = dcv

        dzr_ref[:, 0:D] = dagate.astype(BF16)
        dzr_ref[:, D:2 * D] = du.astype(BF16)
        dzr_ref[:, 2 * D:3 * D] = dv.astype(BF16)
        dzr_ref[:, 3 * D:4 * D] = dbg.astype(BF16)

    row = lambda w: pl.BlockSpec((t, w), lambda i: (i, 0))
    const = lambda shape: pl.BlockSpec(shape, lambda i: (0,) * len(shape))
    return pl.pallas_call(
        body, name="mix_and_head", grid=(n_steps,),
        out_shape=(jax.ShapeDtypeStruct((s, D), F32),
                   jax.ShapeDtypeStruct((D_MIX, s), BF16),
                   jax.ShapeDtypeStruct((s, D), BF16),
                   jax.ShapeDtypeStruct((s, D), F32),
                   jax.ShapeDtypeStruct((s, 4 * D), BF16),
                   jax.ShapeDtypeStruct((8, D), F32),
                   jax.ShapeDtypeStruct((HEADS, CHUNK, CHUNK), F32),
                   jax.ShapeDtypeStruct((CHUNK, HEADS), F32)),
        in_specs=[row(D_IN), prev_spec, next_spec, row(D), row(D)] + [VM] * 12,
        out_specs=(row(D), pl.BlockSpec((D_MIX, t), lambda i: (0, i)), row(D), row(D), row(4 * D),
                   const((8, D)), const((HEADS, CHUNK, CHUNK)), const((CHUNK, HEADS))),
        scratch_shapes=[pltpu.VMEM((t + 2 * HALO, D), F32), pltpu.VMEM((t, D), F32), pltpu.VMEM((t, D), F32),
                        pltpu.VMEM((t, D), F32), pltpu.VMEM((t, D_MIX), BF16), pltpu.VMEM((CHUNK, D), F32),
                        pltpu.VMEM((SUB, CONV_PHASE_ROWS, CONV_BLK), F32)],
        compiler_params=_params(1),
    )(z, z, z, x, tgt, mod, conv_w_g, conv_b, cln_g, cln_b, sln_g, sln_b, final_g, ws_b, wst_b, bs_full, wout)


def _bwd_in_proj(z, dcv, dz_rest, x, dx2, mod, norm_g, conv_w_g, win_g):
    s = x.shape[0]
    t = ROW_TILE
    n_steps = s // t
    zp_spec, zn_spec = _halo_specs(t, s, 2 * D)
    dp_spec, dn_spec = _halo_specs(t, s, D)

    def body(z_ref, zp_ref, zn_ref, dcv_ref, dcvp_ref, dcvn_ref, dzr_ref, x_ref, dx2_ref, mod_ref, ng_ref, cw_ref, w_ref,
             gx_ref, dz_ref, acc_ref, gcw_ref,
             gext, dext, dg, taps, dpad, qbuf):
        i = pl.program_id(0)

        @pl.when(i == 0)
        def _():
            acc_ref[...] = jnp.zeros_like(acc_ref)
            gcw_ref[...] = jnp.zeros_like(gcw_ref)

        not_first = i > 0
        not_last = i < n_steps - 1
        gext[0:HALO, :] = jnp.where(not_first, _glu(zp_ref), 0.0)
        gext[HALO:HALO + t, :] = _glu(z_ref)
        gext[HALO + t:2 * HALO + t, :] = jnp.where(not_last, _glu(zn_ref), 0.0)
        dext[0:HALO, :] = jnp.where(not_first, dcvp_ref[...], 0.0)
        dext[HALO:HALO + t, :] = dcv_ref[...]
        dext[HALO + t:2 * HALO + t, :] = jnp.where(not_last, dcvn_ref[...], 0.0)

        taps[...] = jnp.zeros_like(taps)
        dpad[0:SUB, :] = jnp.zeros((SUB, D), F32)
        dpad[SUB:SUB + t, :] = dcv_ref[...]
        dpad[SUB + t:2 * SUB + t, :] = jnp.zeros((SUB, D), F32)
        for blk in range(N_DEV):
            cols = slice(blk * CONV_BLK, (blk + 1) * CONV_BLK)
            dg[:, cols] = _conv_taps(dext, qbuf, cols, lambda k, blk=blk: cw_ref[blk, CONV_W - 1 - k:CONV_W - k, :], t)
            for b in range(SUB):
                dshift = dpad[pl.ds(SUB - 1 - b, t + SUB), cols]
                for a in range((CONV_W - b + SUB - 1) // SUB):
                    k = SUB * a + b
                    taps[k:k + 1, :] = _colsum(gext[SUB * a:SUB * a + t + SUB, cols] * dshift)
            gcw_ref[blk] += taps[...]

        a = z_ref[:, 0:D]
        sig = _sigmoid(z_ref[:, D:2 * D])
        dgv = dg[...]
        dz_ref[:, 0:D] = (dgv * sig).astype(BF16)
        dz_ref[:, D:2 * D] = (dgv * a * sig * (1.0 - sig)).astype(BF16)
        dz_ref[:, 2 * D:6 * D] = dzr_ref[...]

        dh = jnp.zeros((t, D), F32)
        for j in range(N_DEV):
            dh = dh + lax.dot_general(dz_ref[:, j * W_IN_BLK:(j + 1) * W_IN_BLK], w_ref[j], (((1,), (1,)), ((), ())),
                                      preferred_element_type=F32)

        xn, r, _, scale = _rms_modulate(x_ref[...], mod_ref)
        ng = ng_ref[...]
        one_scale = 1.0 + scale
        dh_xn = dh * xn
        acc_ref[0:1, :] += _colsum(dh_xn * one_scale)
        acc_ref[1:2, :] += _colsum(dh)
        acc_ref[2:3, :] += _colsum(dh_xn * ng)
        dxn = dh * (ng * one_scale)
        gx_ref[...] = dx2_ref[...] + r * (dxn - xn * jnp.mean(dxn * xn, axis=-1, keepdims=True))

    row = lambda w: pl.BlockSpec((t, w), lambda i: (i, 0))
    const = lambda shape: pl.BlockSpec(shape, lambda i: (0,) * len(shape))
    return pl.pallas_call(
        body, name="bwd_in_proj", grid=(n_steps,),
        out_shape=(jax.ShapeDtypeStruct((s, D), F32), jax.ShapeDtypeStruct((s, D_IN), BF16),
                   jax.ShapeDtypeStruct((8, D), F32), jax.ShapeDtypeStruct((N_DEV, CONV_ROWS, CONV_BLK), F32)),
        in_specs=[pl.BlockSpec((t, 2 * D), lambda i: (i, 0)), zp_spec, zn_spec, row(D), dp_spec, dn_spec, row(4 * D),
                  row(D), row(D), VM, VM, VM, VM],
        out_specs=(row(D), row(D_IN), const((8, D)), const((N_DEV, CONV_ROWS, CONV_BLK))),
        scratch_shapes=[pltpu.VMEM((t + 2 * HALO, D), F32), pltpu.VMEM((t + 2 * HALO, D), F32), pltpu.VMEM((t, D), F32),
                        pltpu.VMEM((CONV_ROWS, CONV_BLK), F32), pltpu.VMEM((t + 2 * SUB, D), F32),
                        pltpu.VMEM((SUB, CONV_PHASE_ROWS, CONV_BLK), F32)],
        compiler_params=_params(1),
    )(z, z, z, dcv, dcv, dcv, dz_rest, x, dx2, mod, norm_g, conv_w_g, win_g)


WGRAD_K = 2048
WGRAD_VMEM_LIMIT = 62 * 1024 * 1024
_OTHER_CHIPS = [(3, True), (3, False), (1, True), (1, False), (2, True), (2, False)]
_OWN_CHIP = [(0, True), (0, False)]
WGRAD_ROLES = ([("out",) + e for e in _OTHER_CHIPS] + [("in",) + e for e in _OTHER_CHIPS]
               + [("out",) + e for e in _OWN_CHIP] + [("in",) + e for e in _OWN_CHIP])


def _wgrad_schedule():
    me, _, chips, c = _neighbours()
    chip_of = [me[:2]] + chips
    blk = lambda r, sibling: _shard_index(chip_of, c, r, sibling)
    out_blk, in_blk, is_out = [], [], []
    last = {"out": blk(*_OTHER_CHIPS[0]), "in": blk(*_OTHER_CHIPS[0])}
    for kind, r, sibling in WGRAD_ROLES:
        last[kind] = blk(r, sibling)
        out_blk.append(last["out"])
        in_blk.append(last["in"])
        is_out.append(1 if kind == "out" else 0)
    as_vec = lambda v: jnp.stack([jnp.asarray(e, jnp.int32) for e in v])
    return as_vec(out_blk), as_vec(in_blk), as_vec(is_out)


def _sum_blocks(gathered):
    total = gathered[0]
    for b in range(1, N_DEV):
        total = total + gathered[b]
    return total


SMALL_ROWS = 48
ROW_NORM_G, ROW_FINAL_G, ROW_B_S, ROW_CONV_B, ROW_CLN_G, ROW_CLN_B, ROW_SLN_G, ROW_SLN_B, ROW_LOSS = range(32, 41)


def _wgrad_reduce(ht, dz, ycatt, dy, small, gws, dmod):
    s = dz.shape[0]
    n_kc = s // WGRAD_K
    n_steps = len(WGRAD_ROLES)
    first_in = [k for k, role in enumerate(WGRAD_ROLES) if role[0] == "in"][0]
    blk_in, blk_out = (D, W_IN_BLK), (W_OUT_BLK, D)

    def body(out_blk, in_blk, is_out, ht_ref, dz_ref, yt_ref, dy_ref, small_ref, gws_ref, dmod_ref,
             oin_ref, oout_ref, osmall_ref, ogws_ref, odmod_ref, obada_ref, oloss_ref,
             acc_in, acc_out, p1_in, p1_out, l1_in, l1_out, l2_in, l2_out, small_g, gws_g, dmod_g,
             s1_send, s1_recv, s2_send, s2_recv, t1_send, t1_recv, t2_send, t2_recv, ag_send, ag_recv):
        step, kc = pl.program_id(0), pl.program_id(1)
        pay = {"in": (acc_in, p1_in, oin_ref, _ChipReduce(l1_in, l2_in, s1_send, s1_recv, s2_send, s2_recv)),
               "out": (acc_out, p1_out, oout_ref, _ChipReduce(l1_out, l2_out, t1_send, t1_recv, t2_send, t2_recv))}
        ag = _AllGather([dmod_g, small_g, gws_g], ag_send, ag_recv)
        last_kc = kc == n_kc - 1

        @pl.when((step == 0) & (kc == 0))
        def _():
            me = _my_block()
            small_g[me] = small_ref[...]
            gws_g[me] = gws_ref[...].astype(BF16)
            dmod_g[me] = dmod_ref[...]
            for a in range(3):
                ag.send_own(a)

        @pl.when((step == 4) & (kc == 0))
        def _():
            for a in range(3):
                ag.other_chips(a)

        def accumulate(acc, prod):
            @pl.when(kc == 0)
            def _():
                acc[...] = prod

            @pl.when(kc != 0)
            def _():
                acc[...] += prod

        @pl.when(is_out[step] == 1)
        def _():
            accumulate(acc_out, jnp.dot(yt_ref[...], dy_ref[kc], preferred_element_type=F32))

        @pl.when(is_out[step] == 0)
        def _():
            accumulate(acc_in, jnp.dot(ht_ref[kc], dz_ref[...], preferred_element_type=F32))

        for k, (kind, r, sibling) in enumerate(WGRAD_ROLES):
            @pl.when((step == k) & last_kc)
            def _(kind=kind, r=r, sibling=sibling):
                acc, p1, out, red = pay[kind]
                if sibling:
                    red.send_to_sibling(r, p1, acc[...])
                else:
                    chip_sum = red.combine(r, acc[...])
                    if r == 0:
                        out[...] = chip_sum

        @pl.when((step == n_steps - 1) & last_kc)
        def _():
            for kind in ("out", "in"):
                _, _, out, red = pay[kind]
                out[...] = red.finish(out[...])
            for a in range(3):
                ag.from_sibling(a)
            for a in range(3):
                ag.sent(a)
            tot_small = _sum_blocks(small_g)
            osmall_ref[...] = tot_small
            oloss_ref[...] = jnp.full(oloss_ref.shape, (0.5 / D) * jnp.sum(tot_small[ROW_LOSS:ROW_LOSS + 1, :]), F32)
            tot_gws = gws_g[0].astype(F32)
            for b in range(1, N_DEV):
                tot_gws = tot_gws + gws_g[b].astype(F32)
            ogws_ref[...] = tot_gws
            obada_ref[...] = _sum_blocks(dmod_g)
            for b in range(N_DEV):
                odmod_ref[b:b + 1, :] = dmod_g[b, 0:1, :]

    def kc_of(working, step, kc, hold_first):
        held = jnp.where(step < hold_first, 0, n_kc - 1)
        return jnp.where(working, kc, held)

    out_kc = lambda i, kc, ob, ib, io: kc_of(io[i] == 1, i, kc, 0)
    in_kc = lambda i, kc, ob, ib, io: kc_of(io[i] == 0, i, kc, first_in)
    sems = lambda n: [pltpu.SemaphoreType.DMA((n,)), pltpu.SemaphoreType.DMA((n,))]
    grid_spec = pltpu.PrefetchScalarGridSpec(
        num_scalar_prefetch=3, grid=(n_steps, n_kc),
        in_specs=[VM,
                  pl.BlockSpec((WGRAD_K, W_IN_BLK), lambda i, kc, ob, ib, io: (in_kc(i, kc, ob, ib, io), ib[i])),
                  pl.BlockSpec((W_OUT_BLK, WGRAD_K), lambda i, kc, ob, ib, io: (ob[i], out_kc(i, kc, ob, ib, io))),
                  VM,
                  VM, VM, VM],
        out_specs=(VM,) * 7,
        scratch_shapes=[pltpu.VMEM(blk_in, F32), pltpu.VMEM(blk_out, F32),
                        pltpu.VMEM(blk_in, BF16), pltpu.VMEM(blk_out, BF16),
                        pltpu.VMEM((4,) + blk_in, BF16), pltpu.VMEM((4,) + blk_out, BF16),
                        pltpu.VMEM((3,) + blk_in, BF16), pltpu.VMEM((3,) + blk_out, BF16),
                        pltpu.VMEM((N_DEV,) + small.shape, F32), pltpu.VMEM((N_DEV,) + gws.shape, BF16),
                        pltpu.VMEM((N_DEV,) + dmod.shape, F32)]
        + sems(4) + sems(3) + sems(4) + sems(3) + sems(21))
    return pl.pallas_call(
        body, name="wgrad_reduce", grid_spec=grid_spec,
        out_shape=(jax.ShapeDtypeStruct(blk_in, F32), jax.ShapeDtypeStruct(blk_out, F32),
                   jax.ShapeDtypeStruct(small.shape, F32), jax.ShapeDtypeStruct(gws.shape, F32),
                   jax.ShapeDtypeStruct((N_DEV, 3 * D), F32), jax.ShapeDtypeStruct((8, 3 * D), F32),
                   jax.ShapeDtypeStruct((8, 128), F32)),
        compiler_params=_params(2, vmem=WGRAD_VMEM_LIMIT),
    )(*_wgrad_schedule(), ht, dz, ycatt, dy.reshape(n_kc, WGRAD_K, D), small, gws, dmod)


def _adamw_math(w, g, m, v):
    m = ADAM_B1 * m + (1.0 - ADAM_B1) * g
    v = ADAM_B2 * v + (1.0 - ADAM_B2) * (g * g)
    m_hat = m / (1.0 - ADAM_B1 ** ADAM_STEP)
    v_hat = v / (1.0 - ADAM_B2 ** ADAM_STEP)
    delta = -ADAM_LR * (m_hat / (jnp.sqrt(v_hat) + ADAM_EPS) + ADAM_WD * w)
    return delta, m, v


VECTORS = ["norm_g", "conv_b", "conv_ln_g", "conv_ln_b", "sg_ln_g", "sg_ln_b", "final_g", "b_s"]
VECTOR_ROWS = [ROW_NORM_G, ROW_CONV_B, ROW_CLN_G, ROW_CLN_B, ROW_SLN_G, ROW_SLN_B, ROW_FINAL_G, ROW_B_S]
ADAM_STEPS = 4


def _adamw_all(me, g_w_in, g_w_out, g_w_s, small_sum, dmod_all, bada8, c_all, matrices, b_ada, conv_w, vectors):
    mat_shapes = [(D, W_IN_BLK), (D, W_ADA_BLK), (W_OUT_BLK, D), (HEADS * CHUNK, CHUNK)]
    mat_blocks = [(sh[0] // ADAM_STEPS, sh[1]) for sh in mat_shapes]
    n_small = 2 + len(VECTORS)

    def body(me_ref, gin_ref, gout_ref, gws_ref, taps_ref, small_ref, dmc_ref, bada_ref, c_ref, *refs):
        params = refs[:3 * (4 + n_small)]
        outs = refs[3 * (4 + n_small):-2]
        act, gada = refs[-2:]
        mat_out, gada_out, small_out = outs[:12], outs[12], outs[13:]
        i = pl.program_id(0)

        @pl.when(i == 0)
        def _():
            for b in range(N_DEV):
                cb = c_ref[b, 0:1, :]
                act[b:b + 1, :] = cb * _sigmoid(cb)
            gada[...] = lax.dot_general(act[...], dmc_ref[...], (((0,), (0,)), ((), ())), preferred_element_type=F32,
                                        precision=lax.Precision.HIGHEST)
            small_grads = [bada_ref[0:1, :], taps_ref[0:CONV_W, :]] + [small_ref[r:r + 1, :] for r in VECTOR_ROWS]
            for k, g in enumerate(small_grads):
                w_ref, m_ref, v_ref = params[3 * (4 + k):3 * (5 + k)]
                o = small_out[4 * k:4 * k + 4]
                o[0][...] = g
                o[1][...], o[2][...], o[3][...] = _adamw_math(w_ref[...], g, m_ref[...], v_ref[...])

        rows = pl.ds(pl.multiple_of(i * mat_blocks[1][0], mat_blocks[1][0]), mat_blocks[1][0])
        g_ada = gada[rows, :]
        gada_out[...] = g_ada
        for k, g in enumerate([gin_ref[...], g_ada, gout_ref[...], gws_ref[...]]):
            w_ref, m_ref, v_ref = params[3 * k:3 * k + 3]
            o = mat_out[3 * k:3 * k + 3]
            o[0][...], o[1][...], o[2][...] = _adamw_math(w_ref[...], g, m_ref[...], v_ref[...])

    rows_of = lambda blk: pl.BlockSpec(blk, lambda i, me_ref: (i, 0))
    mat_specs = [rows_of(b) for b in mat_blocks]
    grid_spec = pltpu.PrefetchScalarGridSpec(
        num_scalar_prefetch=1, grid=(ADAM_STEPS,),
        in_specs=[mat_specs[0], mat_specs[2], mat_specs[3],
                  pl.BlockSpec((CONV_ROWS, CONV_BLK), lambda i, me_ref: (0, me_ref[0])), VM,
                  pl.BlockSpec((N_DEV, W_ADA_BLK), lambda i, me_ref: (0, me_ref[0])), VM, VM]
        + [s for s in mat_specs for _ in range(3)] + [VM] * (3 * n_small),
        out_specs=tuple([s for s in mat_specs for _ in range(3)] + [mat_specs[1]] + [VM] * (4 * n_small)),
        scratch_shapes=[pltpu.VMEM((N_DEV, D), F32), pltpu.VMEM((D, W_ADA_BLK), F32)])
    small_shapes = [b_ada[0].shape, conv_w[0].shape] + [(1, D)] * len(VECTORS)
    out_shape = tuple([jax.ShapeDtypeStruct(sh, F32) for sh in mat_shapes for _ in range(3)]
                      + [jax.ShapeDtypeStruct(mat_shapes[1], F32)]
                      + [jax.ShapeDtypeStruct(sh, F32) for sh in small_shapes for _ in range(4)])
    flat = [a for group in matrices for a in group] + list(b_ada) + list(conv_w) + [a for group in vectors for a in group]
    return pl.pallas_call(body, name="adamw_all", grid_spec=grid_spec, out_shape=out_shape,
                          compiler_params=_params(1))(
        me, g_w_in, g_w_out, g_w_s, small_sum, small_sum, dmod_all, bada8, c_all, *flat)


def kernel(x, c, w_ada, b_ada, norm_g, w_in, conv_w, conv_b, conv_ln_g, conv_ln_b, sg_ln_g, sg_ln_b, w_s, b_s, w_out, final_g, loss_target, m_w_ada, m_b_ada, m_norm_g, m_w_in, m_conv_w, m_conv_b, m_conv_ln_g, m_conv_ln_b, m_sg_ln_g, m_sg_ln_b, m_w_s, m_b_s, m_w_out, m_final_g, v_w_ada, v_b_ada, v_norm_g, v_w_in, v_conv_w, v_conv_b, v_conv_ln_g, v_conv_ln_b, v_sg_ln_g, v_sg_ln_b, v_w_s, v_b_s, v_w_out, v_final_g):
    me = 4 * lax.axis_index("x") + 2 * lax.axis_index("y") + lax.axis_index("c")
    x2d, tgt2d = x[0], loss_target[0]
    row1 = lambda a: a.reshape(1, D)
    taps = lambda a: jnp.pad(a.reshape(CONV_W, CONV_BLK), ((0, CONV_ROWS - CONV_W), (0, 0)))

    z, ht, win_g, wout_g, cw_g, c_all, mod = _fwd_in_proj(
        x2d, jnp.broadcast_to(c, (8, D)), w_ada[0], b_ada, norm_g, w_in[0], w_out[0], taps(conv_w))
    ws_b = w_s[0].astype(BF16)
    wst_b = jnp.swapaxes(w_s[0], 1, 2).astype(BF16)
    bs_full = jnp.repeat(b_s[0].T, HEAD_DIM, axis=1)

    dx2, ycatt, dy, dcv, dz_rest, acc_a, gws, gbs = _mix_and_head(
        z, x2d, tgt2d, mod, cw_g, conv_b, conv_ln_g, conv_ln_b, sg_ln_g, sg_ln_b, row1(final_g), ws_b, wst_b, bs_full,
        wout_g.reshape(D_MIX, D))
    grad_x, dz, acc_b, gcw = _bwd_in_proj(z, dcv, dz_rest, x2d, dx2, mod, norm_g, cw_g, win_g)

    small = jnp.concatenate(
        [jnp.transpose(gcw, (1, 0, 2)).reshape(CONV_ROWS, D), acc_b[0:1], acc_a[0:1], gbs.T.reshape(1, D), acc_a[2:8],
         jnp.zeros((SMALL_ROWS - ROW_LOSS - 1, D), F32)], axis=0)
    dmod_row = jnp.concatenate([acc_b[1:2], acc_b[2:3], acc_a[1:2]], axis=1)
    g_w_in, g_w_out, small_sum, g_w_s, dmod_all, bada8, loss_tile = _wgrad_reduce(
        ht, dz, ycatt, dy, small, gws.reshape(HEADS * CHUNK, CHUNK), jnp.broadcast_to(dmod_row, (8, 3 * D)))

    given = dict(w_ada=(w_ada, m_w_ada, v_w_ada), b_ada=(b_ada, m_b_ada, v_b_ada), norm_g=(norm_g, m_norm_g, v_norm_g),
                 w_in=(w_in, m_w_in, v_w_in), conv_w=(conv_w, m_conv_w, v_conv_w), conv_b=(conv_b, m_conv_b, v_conv_b),
                 conv_ln_g=(conv_ln_g, m_conv_ln_g, v_conv_ln_g), conv_ln_b=(conv_ln_b, m_conv_ln_b, v_conv_ln_b),
                 sg_ln_g=(sg_ln_g, m_sg_ln_g, v_sg_ln_g), sg_ln_b=(sg_ln_b, m_sg_ln_b, v_sg_ln_b),
                 w_s=(w_s, m_w_s, v_w_s), b_s=(b_s, m_b_s, v_b_s), w_out=(w_out, m_w_out, v_w_out),
                 final_g=(final_g, m_final_g, v_final_g))
    as2d = lambda name, shape: tuple(a.reshape(shape) for a in given[name])
    res = _adamw_all(
        jnp.reshape(me, (1,)).astype(jnp.int32), g_w_in, g_w_out, g_w_s, small_sum, dmod_all, bada8, c_all,
        [as2d("w_in", (D, W_IN_BLK)), as2d("w_ada", (D, W_ADA_BLK)), as2d("w_out", (W_OUT_BLK, D)),
         as2d("w_s", (HEADS * CHUNK, CHUNK))],
        given["b_ada"], as2d("conv_w", (CONV_W, CONV_BLK)), [as2d(n, (1, D)) for n in VECTORS])
    out = {}
    for k, name in enumerate(["w_in", "w_ada", "w_out", "w_s"]):
        out[name] = [None] + list(res[3 * k:3 * k + 3])
    out["w_in"][0], out["w_ada"][0], out["w_out"][0], out["w_s"][0] = g_w_in, res[12], g_w_out, g_w_s
    for k, name in enumerate(["b_ada", "conv_w"] + VECTORS):
        out[name] = list(res[13 + 4 * k:17 + 4 * k])
    order = ["w_ada", "b_ada", "norm_g", "w_in", "conv_w", "conv_b", "conv_ln_g", "conv_ln_b", "sg_ln_g", "sg_ln_b",
             "w_s", "b_s", "w_out", "final_g"]
    outs = [loss_tile[0, 0], grad_x.reshape(x.shape)]
    for kind in range(4):
        outs += [out[n][kind].reshape(given[n][0].shape) for n in order]
    return tuple(outs)
```

```python
import functools

import jax
import jax.numpy as jnp
from jax import lax
from jax.experimental import pallas as pl
from jax.experimental.pallas import tpu as pltpu

F32 = jnp.float32
BF16 = jnp.bfloat16
MESH = pl.DeviceIdType.MESH

D = 1024
D_IN = 6 * D
D_MIX = 2 * D
N_DEV = 8
W_IN_BLK = D_IN // N_DEV
W_OUT_BLK = D_MIX // N_DEV
W_ADA_BLK = 3 * D // N_DEV
CONV_BLK = D // N_DEV
CONV_W = 31
CONV_HALF = CONV_W // 2
CONV_ROWS = 32
HALO = 16
CHUNK = 128
HEADS = 8
HEAD_DIM = 128
EPS = 1e-6
ROW_TILE = 256
VMEM_LIMIT = 56 * 1024 * 1024

ADAM_LR = 0.001
ADAM_B1 = 0.9
ADAM_B2 = 0.999
ADAM_EPS = 1e-08
ADAM_WD = 0.01
ADAM_STEP = 10

VM = pl.BlockSpec(memory_space=pltpu.VMEM)


def _params(grid_rank=0, vmem=VMEM_LIMIT):
    sem = ("arbitrary",) * grid_rank if grid_rank else None
    return pltpu.CompilerParams(dimension_semantics=sem, vmem_limit_bytes=vmem)


def _sigmoid(t):
    return jax.nn.sigmoid(t)


def _dsilu(t, sig):
    return sig * (1.0 + t * (1.0 - sig))


def _mesh_pos():
    return lax.axis_index("x"), lax.axis_index("y"), lax.axis_index("c")


def _neighbours():
    x, y, c = _mesh_pos()
    nb1 = (x + c * (1 - 2 * x), y + (1 - c) * (1 - 2 * y))
    nb2 = (x + (1 - c) * (1 - 2 * x), y + c * (1 - 2 * y))
    return (x, y, c), (x, y, 1 - c), [nb1, nb2, (1 - x, 1 - y)], c


def _sibling_slot(j):
    return j if j == 2 else 1 - j


class _AllGather:
    def __init__(self, bufs, send_sems, recv_sems):
        self.bufs, self.send_sems, self.recv_sems = bufs, send_sems, recv_sems
        self.me, self.sib, self.chips, self.c = _neighbours()

    def _copy(self, a, k, block, to):
        px, py, pc = block
        ref = self.bufs[a].at[4 * px + 2 * py + pc]
        return pltpu.make_async_remote_copy(
            src_ref=ref, dst_ref=ref, send_sem=self.send_sems.at[7 * a + k], recv_sem=self.recv_sems.at[7 * a + k],
            device_id=to, device_id_type=MESH)

    def _outgoing(self, a, k):
        if k == 0:
            return self._copy(a, 0, self.me, self.sib)
        if k <= 2:
            return self._copy(a, k, self.me, (*self.chips[k - 1], self.c))
        if k == 3:
            return self._copy(a, 3, (*self.chips[0], self.c), (*self.chips[1], self.c))
        return self._copy(a, k, (*self.chips[_sibling_slot(k - 4)], self.c), self.sib)

    def source(self, k):
        if k == 0:
            return self.sib
        return (*self.chips[(k - 1) % 3], self.c if k <= 3 else 1 - self.c)

    def block_index(self, k):
        px, py, pc = self.source(k)
        return 4 * px + 2 * py + pc

    def send_own(self, a):
        for k in range(3):
            self._outgoing(a, k).start()

    def arrived(self, a, k):
        self._copy(a, k, self.source(k), self.me).wait_recv()

    def relay(self, a):
        self._outgoing(a, 3).start()

    def pass_on(self, a, j):
        self._outgoing(a, 4 + _sibling_slot(j)).start()

    def other_chips(self, a):
        self.arrived(a, 1)
        self.relay(a)
        self.pass_on(a, 0)
        for j in (1, 2):
            self.arrived(a, 1 + j)
            self.pass_on(a, j)

    def from_sibling(self, a):
        for k in (0, 4, 5, 6):
            self.arrived(a, k)

    def sent(self, a):
        for k in range(7):
            self._outgoing(a, k).wait_send()


def _shard_index(chip_of, c, r, sibling):
    if sibling and r:
        r = 1 + _sibling_slot(r - 1)
    cx, cy = chip_of[r]
    return 4 * cx + 2 * cy + ((1 - c) if sibling else c)


class _ChipReduce:
    def __init__(self, l1, l2, s1_send, s1_recv, s2_send, s2_recv):
        self.l1, self.l2 = l1, l2
        self.s1_send, self.s1_recv, self.s2_send, self.s2_recv = s1_send, s1_recv, s2_send, s2_recv
        me, self.sib, chips, self.c = _neighbours()
        self.chip_of = [me[:2]] + chips

    def block(self, r, sibling):
        return _shard_index(self.chip_of, self.c, r, sibling)

    def to_sibling(self, r, src=None):
        return pltpu.make_async_remote_copy(
            src_ref=self.l1.at[r] if src is None else src, dst_ref=self.l1.at[r], send_sem=self.s1_send.at[r],
            recv_sem=self.s1_recv.at[r], device_id=self.sib, device_id_type=MESH)

    ORDER = (3, 1, 2, 0)

    def send_to_sibling(self, r, stage, value):
        k = self.ORDER.index(r)
        if k:
            self.to_sibling(self.ORDER[k - 1], stage).wait_send()
        stage[...] = value.astype(BF16)
        self.to_sibling(r, stage).start()

    def to_chip(self, r):
        target = self.chip_of[1] if r == 3 else self.chip_of[r]
        return pltpu.make_async_remote_copy(
            src_ref=self.l1.at[r], dst_ref=self.l2.at[r - 1], send_sem=self.s2_send.at[r - 1],
            recv_sem=self.s2_recv.at[r - 1], device_id=(*target, self.c), device_id_type=MESH)

    def combine(self, r, mine):
        self.to_sibling(r).wait_recv()
        both = mine + self.l1[r].astype(F32)
        if r == 0:
            return both
        if r == 2:
            self.to_chip(3).wait_recv()
            both = both + self.l2[2].astype(F32)
        self.l1[r] = both.astype(BF16)
        self.to_chip(r).start()
        return None

    def finish(self, own_chip_sum):
        total = own_chip_sum
        for r in (1, 2):
            self.to_chip(r).wait_recv()
            total = total + self.l2[r - 1].astype(F32)
        self.to_sibling(self.ORDER[-1]).wait_send()
        for r in (1, 2, 3):
            self.to_chip(r).wait_send()
        return total


def _my_block():
    x, y, c = _mesh_pos()
    return 4 * x + 2 * y + c


def _modulation(c_g, w_ref, b_ref, mod_ref, part, land, send_sems, recv_sems):
    x, y, c = _mesh_pos()
    me = 4 * x + 2 * y + c
    w = w_ref[...]
    for b in range(N_DEV):
        cb = c_g[b]
        part[b] = jnp.dot(cb * _sigmoid(cb), w, preferred_element_type=F32, precision=lax.Precision.HIGHEST)
    land[me] = part[me]

    def copy(b):
        return pltpu.make_async_remote_copy(
            src_ref=part.at[b], dst_ref=land.at[me], send_sem=send_sems.at[b], recv_sem=recv_sems.at[me],
            device_id=(b // 4, (b // 2) % 2, b % 2), device_id_type=MESH)

    def arrival(b):
        return pltpu.make_async_remote_copy(
            src_ref=part.at[b], dst_ref=land.at[b], send_sem=send_sems.at[b], recv_sem=recv_sems.at[b],
            device_id=(b // 4, (b // 2) % 2, b % 2), device_id_type=MESH)

    for b in range(N_DEV):
        @pl.when(b != me)
        def _():
            copy(b).start()
    for b in range(N_DEV):
        @pl.when(b != me)
        def _():
            arrival(b).wait_recv()
            copy(b).wait_send()
    for b in range(N_DEV):
        cols = slice(b * W_ADA_BLK, (b + 1) * W_ADA_BLK)
        mod_ref[:, cols] = land[b] + b_ref[:, cols]


def _rms_modulate(x, mod_ref):
    shift = mod_ref[0:1, 0:D]
    scale = mod_ref[0:1, D:2 * D]
    r = lax.rsqrt(jnp.mean(x * x, axis=-1, keepdims=True) + EPS)
    xn = x * r
    return xn, r, shift, scale


FWD_TILE = 512
FWD_Z_TILE = 2048


def _fwd_in_proj(x, c_rep, w_ada, b_ada, norm_g, w_in, w_out, conv_w):
    s = x.shape[0]
    t, tz = FWD_TILE, FWD_Z_TILE
    n_tiles = s // t
    C_PAY, WIN_PAY, WOUT_PAY, CW_PAY = 0, 1, 2, 3

    def body(x_hbm, c_ref, wada_ref, bada_ref, ng_ref, win_ref, wout_ref, cw_ref,
             z_hbm, ht_ref, win_g, wout_g, cw_g, c_g, mod_ref,
             h, xbuf, zbuf, part, land, ag_send, ag_recv, mod_send, mod_recv, x_sem, z_sem):
        me = _my_block()
        c_g[me] = c_ref[...]
        ag = _AllGather([c_g, win_g, wout_g, cw_g], ag_send, ag_recv)
        ag.send_own(C_PAY)

        def x_copy(i):
            return pltpu.make_async_copy(x_hbm.at[pl.ds(i * t, t), :], xbuf.at[i % 2], x_sem.at[i % 2])

        x_copy(0).start()
        win_g[me] = win_ref[...].astype(BF16)
        wout_g[me] = wout_ref[...].astype(BF16)
        cw_g[me] = cw_ref[...]

        ag.other_chips(C_PAY)
        ag.from_sibling(C_PAY)
        _modulation(c_g, wada_ref, bada_ref, mod_ref, part, land, mod_send, mod_recv)
        for a in (WIN_PAY, WOUT_PAY, CW_PAY):
            ag.send_own(a)

        def z_copy(slot, row0, col0):
            return pltpu.make_async_copy(zbuf.at[slot], z_hbm.at[pl.ds(row0, tz), pl.ds(col0, W_IN_BLK)], z_sem.at[slot])

        done = [0]

        def z_tile(blk, j):
            col0 = pl.multiple_of(blk * W_IN_BLK, 128)
            slot = done[0] % 2
            if done[0] >= 2:
                z_copy(slot, 0, 0).wait()
            zbuf[slot] = jnp.dot(h[j * tz:(j + 1) * tz, :], win_g[blk], preferred_element_type=F32)
            z_copy(slot, j * tz, col0).start()
            done[0] += 1

        def z_block(blk):
            for j in range(s // tz):
                z_tile(blk, j)

        for i in range(n_tiles):
            x_copy(i).wait()
            if i + 1 < n_tiles:
                x_copy(i + 1).start()
            xn, _, shift, scale = _rms_modulate(xbuf[i % 2], mod_ref)
            hh = xn * ng_ref[...] * (1.0 + scale) + shift
            h[i * t:(i + 1) * t, :] = hh.astype(BF16)
            ht_ref[(i * t) // WGRAD_K, :, (i * t) % WGRAD_K:(i * t) % WGRAD_K + t] = hh.T.astype(BF16)
            if ((i + 1) * t) % tz == 0:
                z_tile(me, (i + 1) * t // tz - 1)

        ag.arrived(WIN_PAY, 0)
        z_block(ag.block_index(0))
        ag.arrived(WIN_PAY, 1)
        ag.relay(WIN_PAY)
        ag.pass_on(WIN_PAY, 0)
        ag.arrived(WIN_PAY, 2)
        ag.pass_on(WIN_PAY, 1)
        for k in (1, 2, 4, 5):
            if k >= 4:
                ag.arrived(WIN_PAY, k)
            z_block(ag.block_index(k))
        ag.arrived(WIN_PAY, 3)
        ag.pass_on(WIN_PAY, 2)
        z_block(ag.block_index(3))
        ag.arrived(WIN_PAY, 6)
        z_block(ag.block_index(6))

        for a in (WOUT_PAY, CW_PAY):
            ag.other_chips(a)
        for a in (WOUT_PAY, CW_PAY):
            ag.from_sibling(a)
        for a in (C_PAY, WIN_PAY, WOUT_PAY, CW_PAY):
            ag.sent(a)
        z_copy(0, 0, 0).wait()
        z_copy(1, 0, 0).wait()

    any_spec = pl.BlockSpec(memory_space=pl.ANY)
    return pl.pallas_call(
        body, name="fwd_in_proj",
        out_shape=(jax.ShapeDtypeStruct((s, D_IN), F32), jax.ShapeDtypeStruct((s // WGRAD_K, D, WGRAD_K), BF16),
                   jax.ShapeDtypeStruct((N_DEV,) + w_in.shape, BF16), jax.ShapeDtypeStruct((N_DEV,) + w_out.shape, BF16),
                   jax.ShapeDtypeStruct((N_DEV,) + conv_w.shape, F32), jax.ShapeDtypeStruct((N_DEV,) + c_rep.shape, F32),
                   jax.ShapeDtypeStruct((8, 3 * D), F32)),
        in_specs=[any_spec] + [VM] * 7, out_specs=(any_spec,) + (VM,) * 6,
        scratch_shapes=[pltpu.VMEM((s, D), BF16), pltpu.VMEM((2, t, D), F32), pltpu.VMEM((2, tz, W_IN_BLK), F32),
                        pltpu.VMEM((N_DEV, 8, W_ADA_BLK), F32), pltpu.VMEM((N_DEV, 8, W_ADA_BLK), F32),
                        pltpu.SemaphoreType.DMA((28,)), pltpu.SemaphoreType.DMA((28,)),
                        pltpu.SemaphoreType.DMA((N_DEV,)), pltpu.SemaphoreType.DMA((N_DEV,)),
                        pltpu.SemaphoreType.DMA((2,)), pltpu.SemaphoreType.DMA((2,))],
        compiler_params=_params(vmem=WGRAD_VMEM_LIMIT),
    )(x, c_rep, w_ada, b_ada, norm_g, w_in, w_out, conv_w)


def _halo_specs(t, s, width):
    per = t // HALO
    last = s // HALO - 1
    prev = pl.BlockSpec((HALO, width), lambda i: (jnp.maximum(i * per - 1, 0), 0))
    nxt = pl.BlockSpec((HALO, width), lambda i: (jnp.minimum((i + 1) * per, last), 0))
    return prev, nxt


def _glu(ref):
    return ref[:, 0:D] * _sigmoid(ref[:, D:2 * D])


SUB = 8
CONV_PHASE_ROWS = ROW_TILE + SUB


def _conv_taps(ext, qbuf, cols, tap, t):
    out = None
    for b in range(SUB):
        q = None
        for a in range((CONV_W - b + SUB - 1) // SUB):
            term = ext[SUB * a:SUB * a + t + SUB, cols] * tap(SUB * a + b)
            q = term if q is None else q + term
        qbuf[b] = q
        shifted = qbuf[b, pl.ds(b + 1, t), :]
        out = shifted if out is None else out + shifted
    return out


def _layer_norm_stats(v):
    mu = jnp.mean(v, axis=-1, keepdims=True)
    cen = v - mu
    rstd = lax.rsqrt(jnp.mean(cen * cen, axis=-1, keepdims=True) + EPS)
    return cen * rstd, rstd


def _layer_norm_bwd(dy_hat, hat, rstd):
    m1 = jnp.mean(dy_hat, axis=-1, keepdims=True)
    m2 = jnp.mean(dy_hat * hat, axis=-1, keepdims=True)
    return rstd * (dy_hat - m1 - hat * m2)


def _colsum(v):
    return jnp.sum(v, axis=0, keepdims=True)


def _mix_and_head(z, x, tgt, mod, conv_w_g, conv_b, cln_g, cln_b, sln_g, sln_b, final_g, ws_b, wst_b, bs_full, wout):
    s = x.shape[0]
    t = ROW_TILE
    n_chunks = t // CHUNK
    n_steps = s // t
    prev_spec, next_spec = _halo_specs(t, s, 2 * D)

    def body(z_ref, zp_ref, zn_ref, x_ref, tgt_ref, mod_ref, cw_ref, cb_ref, clg_ref, clb_ref, slg_ref, slb_ref, fg_ref,
             ws_ref, wst_ref, bs_ref, wout_ref,
             dx2_ref, ycatt_ref, dy_ref, dcv_ref, dzr_ref, acc_ref, gws_ref, gbs_ref,
             gext, cv, vs, dvn, ycat, gbs_acc, qbuf):
        i = pl.program_id(0)

        @pl.when(i == 0)
        def _():
            acc_ref[...] = jnp.zeros_like(acc_ref)
            gws_ref[...] = jnp.zeros_like(gws_ref)
            gbs_acc[...] = jnp.zeros_like(gbs_acc)

        gext[0:HALO, :] = jnp.where(i > 0, _glu(zp_ref), 0.0)
        gext[HALO:HALO + t, :] = _glu(z_ref)
        gext[HALO + t:2 * HALO + t, :] = jnp.where(i < n_steps - 1, _glu(zn_ref), 0.0)
        for blk in range(N_DEV):
            cols = slice(blk * CONV_BLK, (blk + 1) * CONV_BLK)
            cv[:, cols] = _conv_taps(gext, qbuf, cols, lambda k, blk=blk: cw_ref[blk, k:k + 1, :], t) + cb_ref[:, cols]
        ln_hat, ln_rstd = _layer_norm_stats(cv[...])
        ln_a = ln_hat * clg_ref[...] + clb_ref[...]
        sig_ln = _sigmoid(ln_a)
        sa = ln_a * sig_ln
        a_gate = z_ref[:, 2 * D:3 * D]
        sig_ag = _sigmoid(a_gate)
        s_gate = a_gate * sig_ag
        ya = sa * s_gate

        v_hat, v_rstd = _layer_norm_stats(z_ref[:, 4 * D:5 * D])
        vn = v_hat * slg_ref[...] + slb_ref[...]
        vnb = vn.astype(BF16)
        for n in range(n_chunks):
            rows = slice(n * CHUNK, (n + 1) * CHUNK)
            for h in range(HEADS):
                cols = slice(h * HEAD_DIM, (h + 1) * HEAD_DIM)
                vs[rows, cols] = jnp.dot(ws_ref[h], vnb[rows, cols], preferred_element_type=F32) + bs_ref[:, cols]
        u = z_ref[:, 3 * D:4 * D]
        b_gate = z_ref[:, 5 * D:6 * D]
        sig_bg = _sigmoid(b_gate)
        s_bg = b_gate * sig_bg
        vsv = vs[...]
        yb = u * vsv * s_bg

        ycat[:, 0:D] = ya.astype(BF16)
        ycat[:, D:2 * D] = yb.astype(BF16)
        ycatt_ref[0:D, :] = ya.T.astype(BF16)
        ycatt_ref[D:2 * D, :] = yb.T.astype(BF16)
        y = jnp.dot(ycat[...], wout_ref[...], preferred_element_type=F32)
        gate = mod_ref[0:1, 2 * D:3 * D]
        x2 = x_ref[...] + gate * y
        r2 = lax.rsqrt(jnp.mean(x2 * x2, axis=-1, keepdims=True) + EPS)
        x2n = x2 * r2
        fg = fg_ref[...]
        diff = x2n * fg - tgt_ref[...]
        acc_ref[7:8, :] += _colsum(diff * diff)
        dout = diff * (1.0 / D)
        acc_ref[0:1, :] += _colsum(dout * x2n)
        dx2n = dout * fg
        dx2 = r2 * (dx2n - x2n * jnp.mean(dx2n * x2n, axis=-1, keepdims=True))
        dx2_ref[...] = dx2
        acc_ref[1:2, :] += _colsum(dx2 * y)
        dyb16 = (dx2 * gate).astype(BF16)
        dy_ref[...] = dyb16
        dycat = lax.dot_general(dyb16, wout_ref[...], (((1,), (1,)), ((), ())), preferred_element_type=F32)
        dya = dycat[:, 0:D]
        dyb = dycat[:, D:2 * D]

        du = dyb * vsv * s_bg
        dvs = dyb * u * s_bg
        dbg = dyb * u * vsv * _dsilu(b_gate, sig_bg)
        dvsb = dvs.astype(BF16)
        gbs = gbs_acc[...]
        for n in range(n_chunks):
            rows = slice(n * CHUNK, (n + 1) * CHUNK)
            gbs = gbs + dvs[rows, :]
            for h in range(HEADS):
                cols = slice(h * HEAD_DIM, (h + 1) * HEAD_DIM)
                gws_ref[h] += lax.dot_general(dvsb[rows, cols], vnb[rows, cols], (((1,), (1,)), ((), ())),
                                              preferred_element_type=F32)
                dvn[rows, cols] = jnp.dot(wst_ref[h], dvsb[rows, cols], preferred_element_type=F32)
        gbs_acc[...] = gbs

        @pl.when(i == n_steps - 1)
        def _():
            for h in range(HEADS):
                gbs_ref[:, h:h + 1] = jnp.sum(gbs_acc[:, h * HEAD_DIM:(h + 1) * HEAD_DIM], axis=1, keepdims=True)

        dvnv = dvn[...]
        acc_ref[5:6, :] += _colsum(dvnv * v_hat)
        acc_ref[6:7, :] += _colsum(dvnv)
        dv = _layer_norm_bwd(dvnv * slg_ref[...], v_hat, v_rstd)

        dsa = dya * s_gate
        dagate = dya * sa * _dsilu(a_gate, sig_ag)
        dln = dsa * _dsilu(ln_a, sig_ln)
        acc_ref[3:4, :] += _colsum(dln * ln_hat)
        acc_ref[4:5, :] += _colsum(dln)
        dcv = _layer_norm_bwd(dln * clg_ref[...], ln_hat, ln_rstd)
        acc_ref[2:3, :] += _colsum(dcv)
        dcv_ref[...] = dcv

        dzr_ref[:, 0:D] = dagate.astype(BF16)
        dzr_ref[:, D:2 * D] = du.astype(BF16)
        dzr_ref[:, 2 * D:3 * D] = dv.astype(BF16)
        dzr_ref[:, 3 * D:4 * D] = dbg.astype(BF16)

    row = lambda w: pl.BlockSpec((t, w), lambda i: (i, 0))
    const = lambda shape: pl.BlockSpec(shape, lambda i: (0,) * len(shape))
    return pl.pallas_call(
        body, name="mix_and_head", grid=(n_steps,),
        out_shape=(jax.ShapeDtypeStruct((s, D), F32),
                   jax.ShapeDtypeStruct((D_MIX, s), BF16),
                   jax.ShapeDtypeStruct((s, D), BF16),
                   jax.ShapeDtypeStruct((s, D), F32),
                   jax.ShapeDtypeStruct((s, 4 * D), BF16),
                   jax.ShapeDtypeStruct((8, D), F32),
                   jax.ShapeDtypeStruct((HEADS, CHUNK, CHUNK), F32),
                   jax.ShapeDtypeStruct((CHUNK, HEADS), F32)),
        in_specs=[row(D_IN), prev_spec, next_spec, row(D), row(D)] + [VM] * 12,
        out_specs=(row(D), pl.BlockSpec((D_MIX, t), lambda i: (0, i)), row(D), row(D), row(4 * D),
                   const((8, D)), const((HEADS, CHUNK, CHUNK)), const((CHUNK, HEADS))),
        scratch_shapes=[pltpu.VMEM((t + 2 * HALO, D), F32), pltpu.VMEM((t, D), F32), pltpu.VMEM((t, D), F32),
                        pltpu.VMEM((t, D), F32), pltpu.VMEM((t, D_MIX), BF16), pltpu.VMEM((CHUNK, D), F32),
                        pltpu.VMEM((SUB, CONV_PHASE_ROWS, CONV_BLK), F32)],
        compiler_params=_params(1),
    )(z, z, z, x, tgt, mod, conv_w_g, conv_b, cln_g, cln_b, sln_g, sln_b, final_g, ws_b, wst_b, bs_full, wout)


def _bwd_in_proj(z, dcv, dz_rest, x, dx2, mod, norm_g, conv_w_g, win_g):
    s = x.shape[0]
    t = ROW_TILE
    n_steps = s // t
    zp_spec, zn_spec = _halo_specs(t, s, 2 * D)
    dp_spec, dn_spec = _halo_specs(t, s, D)

    def body(z_ref, zp_ref, zn_ref, dcv_ref, dcvp_ref, dcvn_ref, dzr_ref, x_ref, dx2_ref, mod_ref, ng_ref, cw_ref, w_ref,
             gx_ref, dz_ref, acc_ref, gcw_ref,
             gext, dext, dg, taps, dpad, qbuf):
        i = pl.program_id(0)

        @pl.when(i == 0)
        def _():
            acc_ref[...] = jnp.zeros_like(acc_ref)
            gcw_ref[...] = jnp.zeros_like(gcw_ref)
            taps[...] = jnp.zeros_like(taps)

        not_first = i > 0
        not_last = i < n_steps - 1
        gext[0:HALO, :] = jnp.where(not_first, _glu(zp_ref), 0.0)
        gext[HALO:HALO + t, :] = _glu(z_ref)
        gext[HALO + t:2 * HALO + t, :] = jnp.where(not_last, _glu(zn_ref), 0.0)
        dext[0:HALO, :] = jnp.where(not_first, dcvp_ref[...], 0.0)
        dext[HALO:HALO + t, :] = dcv_ref[...]
        dext[HALO + t:2 * HALO + t, :] = jnp.where(not_last, dcvn_ref[...], 0.0)

        dpad[0:SUB, :] = jnp.zeros((SUB, D), F32)
        dpad[SUB:SUB + t, :] = dcv_ref[...]
        dpad[SUB + t:2 * SUB + t, :] = jnp.zeros((SUB, D), F32)
        for blk in range(N_DEV):
            cols = slice(blk * CONV_BLK, (blk + 1) * CONV_BLK)
            dg[:, cols] = _conv_taps(dext, qbuf, cols, lambda k, blk=blk: cw_ref[blk, CONV_W - 1 - k:CONV_W - k, :], t)
            for b in range(SUB):
                dshift = dpad[pl.ds(SUB - 1 - b, t + SUB), cols]
                for a in range((CONV_W - b + SUB - 1) // SUB):
                    k = SUB * a + b
                    prod = gext[SUB * a:SUB * a + t + SUB, cols] * dshift
                    part = prod[0:SUB]
                    for j in range(1, (t + SUB) // SUB):
                        part = part + prod[SUB * j:SUB * (j + 1)]
                    taps[blk, k] += part

        @pl.when(i == n_steps - 1)
        def _():
            for blk in range(N_DEV):
                for k in range(CONV_W):
                    gcw_ref[blk, k:k + 1, :] = _colsum(taps[blk, k])

        a = z_ref[:, 0:D]
        sig = _sigmoid(z_ref[:, D:2 * D])
        dgv = dg[...]
        dz_ref[:, 0:D] = (dgv * sig).astype(BF16)
        dz_ref[:, D:2 * D] = (dgv * a * sig * (1.0 - sig)).astype(BF16)
        dz_ref[:, 2 * D:6 * D] = dzr_ref[...]

        dh = jnp.zeros((t, D), F32)
        for j in range(N_DEV):
            dh = dh + lax.dot_general(dz_ref[:, j * W_IN_BLK:(j + 1) * W_IN_BLK], w_ref[j], (((1,), (1,)), ((), ())),
                                      preferred_element_type=F32)

        xn, r, _, scale = _rms_modulate(x_ref[...], mod_ref)
        ng = ng_ref[...]
        one_scale = 1.0 + scale
        dh_xn = dh * xn
        acc_ref[0:1, :] += _colsum(dh_xn * one_scale)
        acc_ref[1:2, :] += _colsum(dh)
        acc_ref[2:3, :] += _colsum(dh_xn * ng)
        dxn = dh * (ng * one_scale)
        gx_ref[...] = dx2_ref[...] + r * (dxn - xn * jnp.mean(dxn * xn, axis=-1, keepdims=True))

    row = lambda w: pl.BlockSpec((t, w), lambda i: (i, 0))
    const = lambda shape: pl.BlockSpec(shape, lambda i: (0,) * len(shape))
    return pl.pallas_call(
        body, name="bwd_in_proj", grid=(n_steps,),
        out_shape=(jax.ShapeDtypeStruct((s, D), F32), jax.ShapeDtypeStruct((s, D_IN), BF16),
                   jax.ShapeDtypeStruct((8, D), F32), jax.ShapeDtypeStruct((N_DEV, CONV_ROWS, CONV_BLK), F32)),
        in_specs=[pl.BlockSpec((t, 2 * D), lambda i: (i, 0)), zp_spec, zn_spec, row(D), dp_spec, dn_spec, row(4 * D),
                  row(D), row(D), VM, VM, VM, VM],
        out_specs=(row(D), row(D_IN), const((8, D)), const((N_DEV, CONV_ROWS, CONV_BLK))),
        scratch_shapes=[pltpu.VMEM((t + 2 * HALO, D), F32), pltpu.VMEM((t + 2 * HALO, D), F32), pltpu.VMEM((t, D), F32),
                        pltpu.VMEM((N_DEV, CONV_ROWS, SUB, CONV_BLK), F32), pltpu.VMEM((t + 2 * SUB, D), F32),
                        pltpu.VMEM((SUB, CONV_PHASE_ROWS, CONV_BLK), F32)],
        compiler_params=_params(1),
    )(z, z, z, dcv, dcv, dcv, dz_rest, x, dx2, mod, norm_g, conv_w_g, win_g)


WGRAD_K = 2048
WGRAD_VMEM_LIMIT = 62 * 1024 * 1024
_OTHER_CHIPS = [(3, True), (3, False), (1, True), (1, False), (2, True), (2, False)]
_OWN_CHIP = [(0, True), (0, False)]
WGRAD_ROLES = ([("out",) + e for e in _OTHER_CHIPS] + [("in",) + e for e in _OTHER_CHIPS]
               + [("out",) + e for e in _OWN_CHIP] + [("in",) + e for e in _OWN_CHIP])


def _wgrad_schedule():
    me, _, chips, c = _neighbours()
    chip_of = [me[:2]] + chips
    blk = lambda r, sibling: _shard_index(chip_of, c, r, sibling)
    out_blk, in_blk, is_out = [], [], []
    last = {"out": blk(*_OTHER_CHIPS[0]), "in": blk(*_OTHER_CHIPS[0])}
    for kind, r, sibling in WGRAD_ROLES:
        last[kind] = blk(r, sibling)
        out_blk.append(last["out"])
        in_blk.append(last["in"])
        is_out.append(1 if kind == "out" else 0)
    as_vec = lambda v: jnp.stack([jnp.asarray(e, jnp.int32) for e in v])
    return as_vec(out_blk), as_vec(in_blk), as_vec(is_out)


def _sum_blocks(gathered):
    total = gathered[0]
    for b in range(1, N_DEV):
        total = total + gathered[b]
    return total


SMALL_ROWS = 48
ROW_NORM_G, ROW_FINAL_G, ROW_B_S, ROW_CONV_B, ROW_CLN_G, ROW_CLN_B, ROW_SLN_G, ROW_SLN_B, ROW_LOSS = range(32, 41)


def _wgrad_reduce(ht, dz, ycatt, dy, small, gws, dmod):
    s = dz.shape[0]
    n_kc = s // WGRAD_K
    n_steps = len(WGRAD_ROLES)
    first_in = [k for k, role in enumerate(WGRAD_ROLES) if role[0] == "in"][0]
    blk_in, blk_out = (D, W_IN_BLK), (W_OUT_BLK, D)

    def body(out_blk, in_blk, is_out, ht_ref, dz_ref, yt_ref, dy_ref, small_ref, gws_ref, dmod_ref,
             oin_ref, oout_ref, osmall_ref, ogws_ref, odmod_ref, obada_ref, oloss_ref,
             acc_in, acc_out, p1_in, p1_out, l1_in, l1_out, l2_in, l2_out, small_g, gws_g, dmod_g,
             s1_send, s1_recv, s2_send, s2_recv, t1_send, t1_recv, t2_send, t2_recv, ag_send, ag_recv):
        step, kc = pl.program_id(0), pl.program_id(1)
        pay = {"in": (acc_in, p1_in, oin_ref, _ChipReduce(l1_in, l2_in, s1_send, s1_recv, s2_send, s2_recv)),
               "out": (acc_out, p1_out, oout_ref, _ChipReduce(l1_out, l2_out, t1_send, t1_recv, t2_send, t2_recv))}
        ag = _AllGather([dmod_g, small_g, gws_g], ag_send, ag_recv)
        last_kc = kc == n_kc - 1

        @pl.when((step == 0) & (kc == 0))
        def _():
            me = _my_block()
            small_g[me] = small_ref[...]
            gws_g[me] = gws_ref[...].astype(BF16)
            dmod_g[me] = dmod_ref[...]
            for a in range(3):
                ag.send_own(a)

        @pl.when((step == 4) & (kc == 0))
        def _():
            for a in range(3):
                ag.other_chips(a)

        def accumulate(acc, prod):
            @pl.when(kc == 0)
            def _():
                acc[...] = prod

            @pl.when(kc != 0)
            def _():
                acc[...] += prod

        @pl.when(is_out[step] == 1)
        def _():
            accumulate(acc_out, jnp.dot(yt_ref[...], dy_ref[kc], preferred_element_type=F32))

        @pl.when(is_out[step] == 0)
        def _():
            accumulate(acc_in, jnp.dot(ht_ref[kc], dz_ref[...], preferred_element_type=F32))

        for k, (kind, r, sibling) in enumerate(WGRAD_ROLES):
            @pl.when((step == k) & last_kc)
            def _(kind=kind, r=r, sibling=sibling):
                acc, p1, out, red = pay[kind]
                if sibling:
                    red.send_to_sibling(r, p1, acc[...])
                else:
                    chip_sum = red.combine(r, acc[...])
                    if r == 0:
                        out[...] = chip_sum

        @pl.when((step == n_steps - 1) & last_kc)
        def _():
            for kind in ("out", "in"):
                _, _, out, red = pay[kind]
                out[...] = red.finish(out[...])
            for a in range(3):
                ag.from_sibling(a)
            for a in range(3):
                ag.sent(a)
            tot_small = _sum_blocks(small_g)
            osmall_ref[...] = tot_small
            oloss_ref[...] = jnp.full(oloss_ref.shape, (0.5 / D) * jnp.sum(tot_small[ROW_LOSS:ROW_LOSS + 1, :]), F32)
            tot_gws = gws_g[0].astype(F32)
            for b in range(1, N_DEV):
                tot_gws = tot_gws + gws_g[b].astype(F32)
            ogws_ref[...] = tot_gws
            obada_ref[...] = _sum_blocks(dmod_g)
            for b in range(N_DEV):
                odmod_ref[b:b + 1, :] = dmod_g[b, 0:1, :]

    def kc_of(working, step, kc, hold_first):
        held = jnp.where(step < hold_first, 0, n_kc - 1)
        return jnp.where(working, kc, held)

    out_kc = lambda i, kc, ob, ib, io: kc_of(io[i] == 1, i, kc, 0)
    in_kc = lambda i, kc, ob, ib, io: kc_of(io[i] == 0, i, kc, first_in)
    sems = lambda n: [pltpu.SemaphoreType.DMA((n,)), pltpu.SemaphoreType.DMA((n,))]
    grid_spec = pltpu.PrefetchScalarGridSpec(
        num_scalar_prefetch=3, grid=(n_steps, n_kc),
        in_specs=[VM,
                  pl.BlockSpec((WGRAD_K, W_IN_BLK), lambda i, kc, ob, ib, io: (in_kc(i, kc, ob, ib, io), ib[i])),
                  pl.BlockSpec((W_OUT_BLK, WGRAD_K), lambda i, kc, ob, ib, io: (ob[i], out_kc(i, kc, ob, ib, io))),
                  VM,
                  VM, VM, VM],
        out_specs=(VM,) * 7,
        scratch_shapes=[pltpu.VMEM(blk_in, F32), pltpu.VMEM(blk_out, F32),
                        pltpu.VMEM(blk_in, BF16), pltpu.VMEM(blk_out, BF16),
                        pltpu.VMEM((4,) + blk_in, BF16), pltpu.VMEM((4,) + blk_out, BF16),
                        pltpu.VMEM((3,) + blk_in, BF16), pltpu.VMEM((3,) + blk_out, BF16),
                        pltpu.VMEM((N_DEV,) + small.shape, F32), pltpu.VMEM((N_DEV,) + gws.shape, BF16),
                        pltpu.VMEM((N_DEV,) + dmod.shape, F32)]
        + sems(4) + sems(3) + sems(4) + sems(3) + sems(21))
    return pl.pallas_call(
        body, name="wgrad_reduce", grid_spec=grid_spec,
        out_shape=(jax.ShapeDtypeStruct(blk_in, F32), jax.ShapeDtypeStruct(blk_out, F32),
                   jax.ShapeDtypeStruct(small.shape, F32), jax.ShapeDtypeStruct(gws.shape, F32),
                   jax.ShapeDtypeStruct((N_DEV, 3 * D), F32), jax.ShapeDtypeStruct((8, 3 * D), F32),
                   jax.ShapeDtypeStruct((8, 128), F32)),
        compiler_params=_params(2, vmem=WGRAD_VMEM_LIMIT),
    )(*_wgrad_schedule(), ht, dz, ycatt, dy.reshape(n_kc, WGRAD_K, D), small, gws, dmod)


def _adamw_math(w, g, m, v):
    m = ADAM_B1 * m + (1.0 - ADAM_B1) * g
    v = ADAM_B2 * v + (1.0 - ADAM_B2) * (g * g)
    m_hat = m / (1.0 - ADAM_B1 ** ADAM_STEP)
    v_hat = v / (1.0 - ADAM_B2 ** ADAM_STEP)
    delta = -ADAM_LR * (m_hat / (jnp.sqrt(v_hat) + ADAM_EPS) + ADAM_WD * w)
    return delta, m, v


VECTORS = ["norm_g", "conv_b", "conv_ln_g", "conv_ln_b", "sg_ln_g", "sg_ln_b", "final_g", "b_s"]
VECTOR_ROWS = [ROW_NORM_G, ROW_CONV_B, ROW_CLN_G, ROW_CLN_B, ROW_SLN_G, ROW_SLN_B, ROW_FINAL_G, ROW_B_S]
ADAM_STEPS = 4


def _adamw_all(me, g_w_in, g_w_out, g_w_s, small_sum, dmod_all, bada8, c_all, matrices, b_ada, conv_w, vectors):
    mat_shapes = [(D, W_IN_BLK), (D, W_ADA_BLK), (W_OUT_BLK, D), (HEADS * CHUNK, CHUNK)]
    mat_blocks = [(sh[0] // ADAM_STEPS, sh[1]) for sh in mat_shapes]
    n_small = 2 + len(VECTORS)

    def body(me_ref, gin_ref, gout_ref, gws_ref, taps_ref, small_ref, dmc_ref, bada_ref, c_ref, *refs):
        params = refs[:3 * (4 + n_small)]
        outs = refs[3 * (4 + n_small):-2]
        act, gada = refs[-2:]
        mat_out, gada_out, small_out = outs[:12], outs[12], outs[13:]
        i = pl.program_id(0)

        @pl.when(i == 0)
        def _():
            for b in range(N_DEV):
                cb = c_ref[b, 0:1, :]
                act[b:b + 1, :] = cb * _sigmoid(cb)
            gada[...] = lax.dot_general(act[...], dmc_ref[...], (((0,), (0,)), ((), ())), preferred_element_type=F32,
                                        precision=lax.Precision.HIGHEST)
            small_grads = [bada_ref[0:1, :], taps_ref[0:CONV_W, :]] + [small_ref[r:r + 1, :] for r in VECTOR_ROWS]
            for k, g in enumerate(small_grads):
                w_ref, m_ref, v_ref = params[3 * (4 + k):3 * (5 + k)]
                o = small_out[4 * k:4 * k + 4]
                o[0][...] = g
                o[1][...], o[2][...], o[3][...] = _adamw_math(w_ref[...], g, m_ref[...], v_ref[...])

        rows = pl.ds(pl.multiple_of(i * mat_blocks[1][0], mat_blocks[1][0]), mat_blocks[1][0])
        g_ada = gada[rows, :]
        gada_out[...] = g_ada
        for k, g in enumerate([gin_ref[...], g_ada, gout_ref[...], gws_ref[...]]):
            w_ref, m_ref, v_ref = params[3 * k:3 * k + 3]
            o = mat_out[3 * k:3 * k + 3]
            o[0][...], o[1][...], o[2][...] = _adamw_math(w_ref[...], g, m_ref[...], v_ref[...])

    rows_of = lambda blk: pl.BlockSpec(blk, lambda i, me_ref: (i, 0))
    mat_specs = [rows_of(b) for b in mat_blocks]
    grid_spec = pltpu.PrefetchScalarGridSpec(
        num_scalar_prefetch=1, grid=(ADAM_STEPS,),
        in_specs=[mat_specs[0], mat_specs[2], mat_specs[3],
                  pl.BlockSpec((CONV_ROWS, CONV_BLK), lambda i, me_ref: (0, me_ref[0])), VM,
                  pl.BlockSpec((N_DEV, W_ADA_BLK), lambda i, me_ref: (0, me_ref[0])), VM, VM]
        + [s for s in mat_specs for _ in range(3)] + [VM] * (3 * n_small),
        out_specs=tuple([s for s in mat_specs for _ in range(3)] + [mat_specs[1]] + [VM] * (4 * n_small)),
        scratch_shapes=[pltpu.VMEM((N_DEV, D), F32), pltpu.VMEM((D, W_ADA_BLK), F32)])
    small_shapes = [b_ada[0].shape, conv_w[0].shape] + [(1, D)] * len(VECTORS)
    out_shape = tuple([jax.ShapeDtypeStruct(sh, F32) for sh in mat_shapes for _ in range(3)]
                      + [jax.ShapeDtypeStruct(mat_shapes[1], F32)]
                      + [jax.ShapeDtypeStruct(sh, F32) for sh in small_shapes for _ in range(4)])
    flat = [a for group in matrices for a in group] + list(b_ada) + list(conv_w) + [a for group in vectors for a in group]
    return pl.pallas_call(body, name="adamw_all", grid_spec=grid_spec, out_shape=out_shape,
                          compiler_params=_params(1))(
        me, g_w_in, g_w_out, g_w_s, small_sum, small_sum, dmod_all, bada8, c_all, *flat)


def kernel(x, c, w_ada, b_ada, norm_g, w_in, conv_w, conv_b, conv_ln_g, conv_ln_b, sg_ln_g, sg_ln_b, w_s, b_s, w_out, final_g, loss_target, m_w_ada, m_b_ada, m_norm_g, m_w_in, m_conv_w, m_conv_b, m_conv_ln_g, m_conv_ln_b, m_sg_ln_g, m_sg_ln_b, m_w_s, m_b_s, m_w_out, m_final_g, v_w_ada, v_b_ada, v_norm_g, v_w_in, v_conv_w, v_conv_b, v_conv_ln_g, v_conv_ln_b, v_sg_ln_g, v_sg_ln_b, v_w_s, v_b_s, v_w_out, v_final_g):
    me = 4 * lax.axis_index("x") + 2 * lax.axis_index("y") + lax.axis_index("c")
    x2d, tgt2d = x[0], loss_target[0]
    row1 = lambda a: a.reshape(1, D)
    taps = lambda a: jnp.pad(a.reshape(CONV_W, CONV_BLK), ((0, CONV_ROWS - CONV_W), (0, 0)))

    z, ht, win_g, wout_g, cw_g, c_all, mod = _fwd_in_proj(
        x2d, jnp.broadcast_to(c, (8, D)), w_ada[0], b_ada, norm_g, w_in[0], w_out[0], taps(conv_w))
    ws_b = w_s[0].astype(BF16)
    wst_b = jnp.swapaxes(w_s[0], 1, 2).astype(BF16)
    bs_full = jnp.repeat(b_s[0].T, HEAD_DIM, axis=1)

    dx2, ycatt, dy, dcv, dz_rest, acc_a, gws, gbs = _mix_and_head(
        z, x2d, tgt2d, mod, cw_g, conv_b, conv_ln_g, conv_ln_b, sg_ln_g, sg_ln_b, row1(final_g), ws_b, wst_b, bs_full,
        wout_g.reshape(D_MIX, D))
    grad_x, dz, acc_b, gcw = _bwd_in_proj(z, dcv, dz_rest, x2d, dx2, mod, norm_g, cw_g, win_g)

    small = jnp.concatenate(
        [jnp.transpose(gcw, (1, 0, 2)).reshape(CONV_ROWS, D), acc_b[0:1], acc_a[0:1], gbs.T.reshape(1, D), acc_a[2:8],
         jnp.zeros((SMALL_ROWS - ROW_LOSS - 1, D), F32)], axis=0)
    dmod_row = jnp.concatenate([acc_b[1:2], acc_b[2:3], acc_a[1:2]], axis=1)
    g_w_in, g_w_out, small_sum, g_w_s, dmod_all, bada8, loss_tile = _wgrad_reduce(
        ht, dz, ycatt, dy, small, gws.reshape(HEADS * CHUNK, CHUNK), jnp.broadcast_to(dmod_row, (8, 3 * D)))

    given = dict(w_ada=(w_ada, m_w_ada, v_w_ada), b_ada=(b_ada, m_b_ada, v_b_ada), norm_g=(norm_g, m_norm_g, v_norm_g),
                 w_in=(w_in, m_w_in, v_w_in), conv_w=(conv_w, m_conv_w, v_conv_w), conv_b=(conv_b, m_conv_b, v_conv_b),
                 conv_ln_g=(conv_ln_g, m_conv_ln_g, v_conv_ln_g), conv_ln_b=(conv_ln_b, m_conv_ln_b, v_conv_ln_b),
                 sg_ln_g=(sg_ln_g, m_sg_ln_g, v_sg_ln_g), sg_ln_b=(sg_ln_b, m_sg_ln_b, v_sg_ln_b),
                 w_s=(w_s, m_w_s, v_w_s), b_s=(b_s, m_b_s, v_b_s), w_out=(w_out, m_w_out, v_w_out),
                 final_g=(final_g, m_final_g, v_final_g))
    as2d = lambda name, shape: tuple(a.reshape(shape) for a in given[name])
    res = _adamw_all(
        jnp.reshape(me, (1,)).astype(jnp.int32), g_w_in, g_w_out, g_w_s, small_sum, dmod_all, bada8, c_all,
        [as2d("w_in", (D, W_IN_BLK)), as2d("w_ada", (D, W_ADA_BLK)), as2d("w_out", (W_OUT_BLK, D)),
         as2d("w_s", (HEADS * CHUNK, CHUNK))],
        given["b_ada"], as2d("conv_w", (CONV_W, CONV_BLK)), [as2d(n, (1, D)) for n in VECTORS])
    out = {}
    for k, name in enumerate(["w_in", "w_ada", "w_out", "w_s"]):
        out[name] = [None] + list(res[3 * k:3 * k + 3])
    out["w_in"][0], out["w_ada"][0], out["w_out"][0], out["w_s"][0] = g_w_in, res[12], g_w_out, g_w_s
    for k, name in enumerate(["b_ada", "conv_w"] + VECTORS):
        out[name] = list(res[13 + 4 * k:17 + 4 * k])
    order = ["w_ada", "b_ada", "norm_g", "w_in", "conv_w", "conv_b", "conv_ln_g", "conv_ln_b", "sg_ln_g", "sg_ln_b",
             "w_s", "b_s", "w_out", "final_g"]
    outs = [loss_tile[0, 0], grad_x.reshape(x.shape)]
    for kind in range(4):
        outs += [out[n][kind].reshape(given[n][0].shape) for n in order]
    return tuple(outs)
```

```python
import functools

import jax
import jax.numpy as jnp
from jax import lax
from jax.experimental import pallas as pl
from jax.experimental.pallas import tpu as pltpu

F32 = jnp.float32
BF16 = jnp.bfloat16
MESH = pl.DeviceIdType.MESH

D = 1024
D_IN = 6 * D
D_MIX = 2 * D
N_DEV = 8
W_IN_BLK = D_IN // N_DEV
W_OUT_BLK = D_MIX // N_DEV
W_ADA_BLK = 3 * D // N_DEV
CONV_BLK = D // N_DEV
CONV_W = 31
CONV_HALF = CONV_W // 2
CONV_ROWS = 32
HALO = 16
CHUNK = 128
HEADS = 8
HEAD_DIM = 128
EPS = 1e-6
ROW_TILE = 256
VMEM_LIMIT = 56 * 1024 * 1024

ADAM_LR = 0.001
ADAM_B1 = 0.9
ADAM_B2 = 0.999
ADAM_EPS = 1e-08
ADAM_WD = 0.01
ADAM_STEP = 10

VM = pl.BlockSpec(memory_space=pltpu.VMEM)


def _params(grid_rank=0, vmem=VMEM_LIMIT):
    sem = ("arbitrary",) * grid_rank if grid_rank else None
    return pltpu.CompilerParams(dimension_semantics=sem, vmem_limit_bytes=vmem)


def _sigmoid(t):
    return jax.nn.sigmoid(t)


def _dsilu(t, sig):
    return sig * (1.0 + t * (1.0 - sig))


def _mesh_pos():
    return lax.axis_index("x"), lax.axis_index("y"), lax.axis_index("c")


def _neighbours():
    x, y, c = _mesh_pos()
    nb1 = (x + c * (1 - 2 * x), y + (1 - c) * (1 - 2 * y))
    nb2 = (x + (1 - c) * (1 - 2 * x), y + c * (1 - 2 * y))
    return (x, y, c), (x, y, 1 - c), [nb1, nb2, (1 - x, 1 - y)], c


def _sibling_slot(j):
    return j if j == 2 else 1 - j


class _AllGather:
    def __init__(self, bufs, send_sems, recv_sems):
        self.bufs, self.send_sems, self.recv_sems = bufs, send_sems, recv_sems
        self.me, self.sib, self.chips, self.c = _neighbours()

    def _copy(self, a, k, block, to):
        px, py, pc = block
        ref = self.bufs[a].at[4 * px + 2 * py + pc]
        return pltpu.make_async_remote_copy(
            src_ref=ref, dst_ref=ref, send_sem=self.send_sems.at[7 * a + k], recv_sem=self.recv_sems.at[7 * a + k],
            device_id=to, device_id_type=MESH)

    def _outgoing(self, a, k):
        if k == 0:
            return self._copy(a, 0, self.me, self.sib)
        if k <= 2:
            return self._copy(a, k, self.me, (*self.chips[k - 1], self.c))
        if k == 3:
            return self._copy(a, 3, (*self.chips[0], self.c), (*self.chips[1], self.c))
        return self._copy(a, k, (*self.chips[_sibling_slot(k - 4)], self.c), self.sib)

    def source(self, k):
        if k == 0:
            return self.sib
        return (*self.chips[(k - 1) % 3], self.c if k <= 3 else 1 - self.c)

    def block_index(self, k):
        px, py, pc = self.source(k)
        return 4 * px + 2 * py + pc

    def send_own(self, a):
        for k in range(3):
            self._outgoing(a, k).start()

    def arrived(self, a, k):
        self._copy(a, k, self.source(k), self.me).wait_recv()

    def relay(self, a):
        self._outgoing(a, 3).start()

    def pass_on(self, a, j):
        self._outgoing(a, 4 + _sibling_slot(j)).start()

    def other_chips(self, a):
        self.arrived(a, 1)
        self.relay(a)
        self.pass_on(a, 0)
        for j in (1, 2):
            self.arrived(a, 1 + j)
            self.pass_on(a, j)

    def from_sibling(self, a):
        for k in (0, 4, 5, 6):
            self.arrived(a, k)

    def sent(self, a):
        for k in range(7):
            self._outgoing(a, k).wait_send()


def _shard_index(chip_of, c, r, sibling):
    if sibling and r:
        r = 1 + _sibling_slot(r - 1)
    cx, cy = chip_of[r]
    return 4 * cx + 2 * cy + ((1 - c) if sibling else c)


class _ChipReduce:
    def __init__(self, l1, l2, s1_send, s1_recv, s2_send, s2_recv):
        self.l1, self.l2 = l1, l2
        self.s1_send, self.s1_recv, self.s2_send, self.s2_recv = s1_send, s1_recv, s2_send, s2_recv
        me, self.sib, chips, self.c = _neighbours()
        self.chip_of = [me[:2]] + chips

    def block(self, r, sibling):
        return _shard_index(self.chip_of, self.c, r, sibling)

    def to_sibling(self, r, src=None):
        return pltpu.make_async_remote_copy(
            src_ref=self.l1.at[r] if src is None else src, dst_ref=self.l1.at[r], send_sem=self.s1_send.at[r],
            recv_sem=self.s1_recv.at[r], device_id=self.sib, device_id_type=MESH)

    ORDER = (3, 1, 2, 0)

    def send_to_sibling(self, r, stage, value):
        k = self.ORDER.index(r)
        if k:
            self.to_sibling(self.ORDER[k - 1], stage).wait_send()
        stage[...] = value.astype(BF16)
        self.to_sibling(r, stage).start()

    def to_chip(self, r):
        target = self.chip_of[1] if r == 3 else self.chip_of[r]
        return pltpu.make_async_remote_copy(
            src_ref=self.l1.at[r], dst_ref=self.l2.at[r - 1], send_sem=self.s2_send.at[r - 1],
            recv_sem=self.s2_recv.at[r - 1], device_id=(*target, self.c), device_id_type=MESH)

    def combine(self, r, mine):
        self.to_sibling(r).wait_recv()
        both = mine + self.l1[r].astype(F32)
        if r == 0:
            return both
        if r == 2:
            self.to_chip(3).wait_recv()
            both = both + self.l2[2].astype(F32)
        self.l1[r] = both.astype(BF16)
        self.to_chip(r).start()
        return None

    def finish(self, own_chip_sum):
        total = own_chip_sum
        for r in (1, 2):
            self.to_chip(r).wait_recv()
            total = total + self.l2[r - 1].astype(F32)
        self.to_sibling(self.ORDER[-1]).wait_send()
        for r in (1, 2, 3):
            self.to_chip(r).wait_send()
        return total


def _my_block():
    x, y, c = _mesh_pos()
    return 4 * x + 2 * y + c


def _modulation(c_g, w_ref, b_ref, mod_ref, part, land, send_sems, recv_sems):
    x, y, c = _mesh_pos()
    me = 4 * x + 2 * y + c
    c_rows = c_g[...].reshape(N_DEV * 8, D)
    parts = jnp.dot(c_rows * _sigmoid(c_rows), w_ref[...], preferred_element_type=F32, precision=lax.Precision.HIGHEST)
    for b in range(N_DEV):
        part[b] = parts[8 * b:8 * b + 8]
    land[me] = part[me]

    def copy(b):
        return pltpu.make_async_remote_copy(
            src_ref=part.at[b], dst_ref=land.at[me], send_sem=send_sems.at[b], recv_sem=recv_sems.at[me],
            device_id=(b // 4, (b // 2) % 2, b % 2), device_id_type=MESH)

    def arrival(b):
        return pltpu.make_async_remote_copy(
            src_ref=part.at[b], dst_ref=land.at[b], send_sem=send_sems.at[b], recv_sem=recv_sems.at[b],
            device_id=(b // 4, (b // 2) % 2, b % 2), device_id_type=MESH)

    for b in range(N_DEV):
        @pl.when(b != me)
        def _():
            copy(b).start()
    for b in range(N_DEV):
        @pl.when(b != me)
        def _():
            arrival(b).wait_recv()
            copy(b).wait_send()
    for b in range(N_DEV):
        cols = slice(b * W_ADA_BLK, (b + 1) * W_ADA_BLK)
        mod_ref[:, cols] = land[b] + b_ref[:, cols]


def _rms_modulate(x, mod_ref):
    shift = mod_ref[0:1, 0:D]
    scale = mod_ref[0:1, D:2 * D]
    r = lax.rsqrt(jnp.mean(x * x, axis=-1, keepdims=True) + EPS)
    xn = x * r
    return xn, r, shift, scale


FWD_TILE = 512
FWD_Z_TILE = 1024


def _fwd_in_proj(x, c_rep, w_ada, b_ada, norm_g, w_in, w_out, conv_w):
    s = x.shape[0]
    t, tz = FWD_TILE, FWD_Z_TILE
    n_tiles = s // t
    C_PAY, WIN_PAY, WOUT_PAY, CW_PAY = 0, 1, 2, 3

    def body(x_hbm, c_ref, wada_ref, bada_ref, ng_ref, win_ref, wout_ref, cw_ref,
             z_hbm, ht_ref, win_g, wout_g, cw_g, c_g, mod_ref,
             h, xbuf, zbuf, part, land, ag_send, ag_recv, mod_send, mod_recv, x_sem, z_sem):
        me = _my_block()
        c_g[me] = c_ref[...]
        ag = _AllGather([c_g, win_g, wout_g, cw_g], ag_send, ag_recv)
        ag.send_own(C_PAY)

        def x_copy(i):
            return pltpu.make_async_copy(x_hbm.at[pl.ds(i * t, t), :], xbuf.at[i % 2], x_sem.at[i % 2])

        x_copy(0).start()
        win_g[me] = win_ref[...].astype(BF16)
        wout_g[me] = wout_ref[...].astype(BF16)
        cw_g[me] = cw_ref[...]

        ag.other_chips(C_PAY)
        ag.from_sibling(C_PAY)
        _modulation(c_g, wada_ref, bada_ref, mod_ref, part, land, mod_send, mod_recv)
        for a in (WIN_PAY, WOUT_PAY, CW_PAY):
            ag.send_own(a)

        for i in range(n_tiles):
            x_copy(i).wait()
            if i + 1 < n_tiles:
                x_copy(i + 1).start()
            xn, _, shift, scale = _rms_modulate(xbuf[i % 2], mod_ref)
            hh = xn * ng_ref[...] * (1.0 + scale) + shift
            h[i * t:(i + 1) * t, :] = hh.astype(BF16)
            ht_ref[(i * t) // WGRAD_K, :, (i * t) % WGRAD_K:(i * t) % WGRAD_K + t] = hh.T.astype(BF16)

        def z_copy(slot, row0, col0):
            return pltpu.make_async_copy(zbuf.at[slot], z_hbm.at[pl.ds(row0, tz), pl.ds(col0, W_IN_BLK)], z_sem.at[slot])

        done = [0]

        def z_block(blk):
            col0 = pl.multiple_of(blk * W_IN_BLK, 128)
            for i in range(s // tz):
                slot = done[0] % 2
                if done[0] >= 2:
                    z_copy(slot, 0, 0).wait()
                zbuf[slot] = jnp.dot(h[i * tz:(i + 1) * tz, :], win_g[blk], preferred_element_type=F32)
                z_copy(slot, i * tz, col0).start()
                done[0] += 1

        z_block(me)
        ag.arrived(WIN_PAY, 0)
        z_block(ag.block_index(0))
        ag.arrived(WIN_PAY, 1)
        ag.relay(WIN_PAY)
        ag.pass_on(WIN_PAY, 0)
        ag.arrived(WIN_PAY, 2)
        ag.pass_on(WIN_PAY, 1)
        for k in (1, 2, 4, 5):
            if k >= 4:
                ag.arrived(WIN_PAY, k)
            z_block(ag.block_index(k))
        ag.arrived(WIN_PAY, 3)
        ag.pass_on(WIN_PAY, 2)
        z_block(ag.block_index(3))
        ag.arrived(WIN_PAY, 6)
        z_block(ag.block_index(6))

        for a in (WOUT_PAY, CW_PAY):
            ag.other_chips(a)
        for a in (WOUT_PAY, CW_PAY):
            ag.from_sibling(a)
        for a in (C_PAY, WIN_PAY, WOUT_PAY, CW_PAY):
            ag.sent(a)
        z_copy(0, 0, 0).wait()
        z_copy(1, 0, 0).wait()

    any_spec = pl.BlockSpec(memory_space=pl.ANY)
    return pl.pallas_call(
        body, name="fwd_in_proj",
        out_shape=(jax.ShapeDtypeStruct((s, D_IN), F32), jax.ShapeDtypeStruct((s // WGRAD_K, D, WGRAD_K), BF16),
                   jax.ShapeDtypeStruct((N_DEV,) + w_in.shape, BF16), jax.ShapeDtypeStruct((N_DEV,) + w_out.shape, BF16),
                   jax.ShapeDtypeStruct((N_DEV,) + conv_w.shape, F32), jax.ShapeDtypeStruct((N_DEV,) + c_rep.shape, F32),
                   jax.ShapeDtypeStruct((8, 3 * D), F32)),
        in_specs=[any_spec] + [VM] * 7, out_specs=(any_spec,) + (VM,) * 6,
        scratch_shapes=[pltpu.VMEM((s, D), BF16), pltpu.VMEM((2, t, D), F32), pltpu.VMEM((2, tz, W_IN_BLK), F32),
                        pltpu.VMEM((N_DEV, 8, W_ADA_BLK), F32), pltpu.VMEM((N_DEV, 8, W_ADA_BLK), F32),
                        pltpu.SemaphoreType.DMA((28,)), pltpu.SemaphoreType.DMA((28,)),
                        pltpu.SemaphoreType.DMA((N_DEV,)), pltpu.SemaphoreType.DMA((N_DEV,)),
                        pltpu.SemaphoreType.DMA((2,)), pltpu.SemaphoreType.DMA((2,))],
        compiler_params=_params(),
    )(x, c_rep, w_ada, b_ada, norm_g, w_in, w_out, conv_w)


def _halo_specs(t, s, width):
    per = t // HALO
    last = s // HALO - 1
    prev = pl.BlockSpec((HALO, width), lambda i: (jnp.maximum(i * per - 1, 0), 0))
    nxt = pl.BlockSpec((HALO, width), lambda i: (jnp.minimum((i + 1) * per, last), 0))
    return prev, nxt


def _glu(ref):
    return ref[:, 0:D] * _sigmoid(ref[:, D:2 * D])


SUB = 8
CONV_PHASE_ROWS = ROW_TILE + SUB


def _conv_taps(ext, qbuf, cols, tap, t):
    out = None
    for b in range(SUB):
        q = None
        for a in range((CONV_W - b + SUB - 1) // SUB):
            term = ext[SUB * a:SUB * a + t + SUB, cols] * tap(SUB * a + b)
            q = term if q is None else q + term
        qbuf[b] = q
        shifted = qbuf[b, pl.ds(b + 1, t), :]
        out = shifted if out is None else out + shifted
    return out


def _layer_norm_stats(v):
    mu = jnp.mean(v, axis=-1, keepdims=True)
    cen = v - mu
    rstd = lax.rsqrt(jnp.mean(cen * cen, axis=-1, keepdims=True) + EPS)
    return cen * rstd, rstd


def _layer_norm_bwd(dy_hat, hat, rstd):
    m1 = jnp.mean(dy_hat, axis=-1, keepdims=True)
    m2 = jnp.mean(dy_hat * hat, axis=-1, keepdims=True)
    return rstd * (dy_hat - m1 - hat * m2)


def _colsum(v):
    return jnp.sum(v, axis=0, keepdims=True)


def _mix_and_head(z, x, tgt, mod, conv_w_g, conv_b, cln_g, cln_b, sln_g, sln_b, final_g, ws_b, wst_b, bs_full, wout):
    s = x.shape[0]
    t = ROW_TILE
    n_chunks = t // CHUNK
    n_steps = s // t
    prev_spec, next_spec = _halo_specs(t, s, 2 * D)

    def body(z_ref, zp_ref, zn_ref, x_ref, tgt_ref, mod_ref, cw_ref, cb_ref, clg_ref, clb_ref, slg_ref, slb_ref, fg_ref,
             ws_ref, wst_ref, bs_ref, wout_ref,
             dx2_ref, ycatt_ref, dy_ref, dcv_ref, dzr_ref, acc_ref, gws_ref, gbs_ref,
             gext, cv, vs, dvn, ycat, gbs_acc, qbuf):
        i = pl.program_id(0)

        @pl.when(i == 0)
        def _():
            acc_ref[...] = jnp.zeros_like(acc_ref)
            gws_ref[...] = jnp.zeros_like(gws_ref)
            gbs_acc[...] = jnp.zeros_like(gbs_acc)

        gext[0:HALO, :] = jnp.where(i > 0, _glu(zp_ref), 0.0)
        gext[HALO:HALO + t, :] = _glu(z_ref)
        gext[HALO + t:2 * HALO + t, :] = jnp.where(i < n_steps - 1, _glu(zn_ref), 0.0)
        for blk in range(N_DEV):
            cols = slice(blk * CONV_BLK, (blk + 1) * CONV_BLK)
            cv[:, cols] = _conv_taps(gext, qbuf, cols, lambda k, blk=blk: cw_ref[blk, k:k + 1, :], t) + cb_ref[:, cols]
        ln_hat, ln_rstd = _layer_norm_stats(cv[...])
        ln_a = ln_hat * clg_ref[...] + clb_ref[...]
        sig_ln = _sigmoid(ln_a)
        sa = ln_a * sig_ln
        a_gate = z_ref[:, 2 * D:3 * D]
        sig_ag = _sigmoid(a_gate)
        s_gate = a_gate * sig_ag
        ya = sa * s_gate

        v_hat, v_rstd = _layer_norm_stats(z_ref[:, 4 * D:5 * D])
        vn = v_hat * slg_ref[...] + slb_ref[...]
        vnb = vn.astype(BF16)
        for n in range(n_chunks):
            rows = slice(n * CHUNK, (n + 1) * CHUNK)
            for h in range(HEADS):
                cols = slice(h * HEAD_DIM, (h + 1) * HEAD_DIM)
                vs[rows, cols] = jnp.dot(ws_ref[h], vnb[rows, cols], preferred_element_type=F32) + bs_ref[:, cols]
        u = z_ref[:, 3 * D:4 * D]
        b_gate = z_ref[:, 5 * D:6 * D]
        sig_bg = _sigmoid(b_gate)
        s_bg = b_gate * sig_bg
        vsv = vs[...]
        yb = u * vsv * s_bg

        ycat[:, 0:D] = ya.astype(BF16)
        ycat[:, D:2 * D] = yb.astype(BF16)
        ycatt_ref[0:D, :] = ya.T.astype(BF16)
        ycatt_ref[D:2 * D, :] = yb.T.astype(BF16)
        y = jnp.dot(ycat[...], wout_ref[...], preferred_element_type=F32)
        gate = mod_ref[0:1, 2 * D:3 * D]
        x2 = x_ref[...] + gate * y
        r2 = lax.rsqrt(jnp.mean(x2 * x2, axis=-1, keepdims=True) + EPS)
        x2n = x2 * r2
        fg = fg_ref[...]
        diff = x2n * fg - tgt_ref[...]
        acc_ref[7:8, :] += _colsum(diff * diff)
        dout = diff * (1.0 / D)
        acc_ref[0:1, :] += _colsum(dout * x2n)
        dx2n = dout * fg
        dx2 = r2 * (dx2n - x2n * jnp.mean(dx2n * x2n, axis=-1, keepdims=True))
        dx2_ref[...] = dx2
        acc_ref[1:2, :] += _colsum(dx2 * y)
        dyb16 = (dx2 * gate).astype(BF16)
        dy_ref[...] = dyb16
        dycat = lax.dot_general(dyb16, wout_ref[...], (((1,), (1,)), ((), ())), preferred_element_type=F32)
        dya = dycat[:, 0:D]
        dyb = dycat[:, D:2 * D]

        du = dyb * vsv * s_bg
        dvs = dyb * u * s_bg
        dbg = dyb * u * vsv * _dsilu(b_gate, sig_bg)
        dvsb = dvs.astype(BF16)
        gbs = gbs_acc[...]
        for n in range(n_chunks):
            rows = slice(n * CHUNK, (n + 1) * CHUNK)
            gbs = gbs + dvs[rows, :]
            for h in range(HEADS):
                cols = slice(h * HEAD_DIM, (h + 1) * HEAD_DIM)
                gws_ref[h] += lax.dot_general(dvsb[rows, cols], vnb[rows, cols], (((1,), (1,)), ((), ())),
                                              preferred_element_type=F32)
                dvn[rows, cols] = jnp.dot(wst_ref[h], dvsb[rows, cols], preferred_element_type=F32)
        gbs_acc[...] = gbs

        @pl.when(i == n_steps - 1)
        def _():
            for h in range(HEADS):
                gbs_ref[:, h:h + 1] = jnp.sum(gbs_acc[:, h * HEAD_DIM:(h + 1) * HEAD_DIM], axis=1, keepdims=True)

        dvnv = dvn[...]
        acc_ref[5:6, :] += _colsum(dvnv * v_hat)
        acc_ref[6:7, :] += _colsum(dvnv)
        dv = _layer_norm_bwd(dvnv * slg_ref[...], v_hat, v_rstd)

        dsa = dya * s_gate
        dagate = dya * sa * _dsilu(a_gate, sig_ag)
        dln = dsa * _dsilu(ln_a, sig_ln)
        acc_ref[3:4, :] += _colsum(dln * ln_hat)
        acc_ref[4:5, :] += _colsum(dln)
        dcv = _layer_norm_bwd(dln * clg_ref[...], ln_hat, ln_rstd)
        acc_ref[2:3, :] += _colsum(dcv)
        dcv_ref[...] = dcv

        dzr_ref[:, 0:D] = dagate.astype(BF16)
        dzr_ref[:, D:2 * D] = du.astype(BF16)
        dzr_ref[:, 2 * D:3 * D] = dv.astype(BF16)
        dzr_ref[:, 3 * D:4 * D] = dbg.astype(BF16)

    row = lambda w: pl.BlockSpec((t, w), lambda i: (i, 0))
    const = lambda shape: pl.BlockSpec(shape, lambda i: (0,) * len(shape))
    return pl.pallas_call(
        body, name="mix_and_head", grid=(n_steps,),
        out_shape=(jax.ShapeDtypeStruct((s, D), F32),
                   jax.ShapeDtypeStruct((D_MIX, s), BF16),
                   jax.ShapeDtypeStruct((s, D), BF16),
                   jax.ShapeDtypeStruct((s, D), F32),
                   jax.ShapeDtypeStruct((s, 4 * D), BF16),
                   jax.ShapeDtypeStruct((8, D), F32),
                   jax.ShapeDtypeStruct((HEADS, CHUNK, CHUNK), F32),
                   jax.ShapeDtypeStruct((CHUNK, HEADS), F32)),
        in_specs=[row(D_IN), prev_spec, next_spec, row(D), row(D)] + [VM] * 12,
        out_specs=(row(D), pl.BlockSpec((D_MIX, t), lambda i: (0, i)), row(D), row(D), row(4 * D),
                   const((8, D)), const((HEADS, CHUNK, CHUNK)), const((CHUNK, HEADS))),
        scratch_shapes=[pltpu.VMEM((t + 2 * HALO, D), F32), pltpu.VMEM((t, D), F32), pltpu.VMEM((t, D), F32),
                        pltpu.VMEM((t, D), F32), pltpu.VMEM((t, D_MIX), BF16), pltpu.VMEM((CHUNK, D), F32),
                        pltpu.VMEM((SUB, CONV_PHASE_ROWS, CONV_BLK), F32)],
        compiler_params=_params(1),
    )(z, z, z, x, tgt, mod, conv_w_g, conv_b, cln_g, cln_b, sln_g, sln_b, final_g, ws_b, wst_b, bs_full, wout)


def _bwd_in_proj(z, dcv, dz_rest, x, dx2, mod, norm_g, conv_w_g, win_g):
    s = x.shape[0]
    t = ROW_TILE
    n_steps = s // t
    zp_spec, zn_spec = _halo_specs(t, s, 2 * D)
    dp_spec, dn_spec = _halo_specs(t, s, D)

    def body(z_ref, zp_ref, zn_ref, dcv_ref, dcvp_ref, dcvn_ref, dzr_ref, x_ref, dx2_ref, mod_ref, ng_ref, cw_ref, w_ref,
             gx_ref, dz_ref, acc_ref, gcw_ref,
             gext, dext, dg, taps, dpad, qbuf):
        i = pl.program_id(0)

        @pl.when(i == 0)
        def _():
            acc_ref[...] = jnp.zeros_like(acc_ref)
            gcw_ref[...] = jnp.zeros_like(gcw_ref)

        not_first = i > 0
        not_last = i < n_steps - 1
        a_val = z_ref[:, 0:D]
        sig = _sigmoid(z_ref[:, D:2 * D])
        gext[0:HALO, :] = jnp.where(not_first, _glu(zp_ref), 0.0)
        gext[HALO:HALO + t, :] = a_val * sig
        gext[HALO + t:2 * HALO + t, :] = jnp.where(not_last, _glu(zn_ref), 0.0)
        dext[0:HALO, :] = jnp.where(not_first, dcvp_ref[...], 0.0)
        dext[HALO:HALO + t, :] = dcv_ref[...]
        dext[HALO + t:2 * HALO + t, :] = jnp.where(not_last, dcvn_ref[...], 0.0)

        taps[...] = jnp.zeros_like(taps)
        dpad[0:SUB, :] = jnp.zeros((SUB, D), F32)
        dpad[SUB:SUB + t, :] = dcv_ref[...]
        dpad[SUB + t:2 * SUB + t, :] = jnp.zeros((SUB, D), F32)
        for blk in range(N_DEV):
            cols = slice(blk * CONV_BLK, (blk + 1) * CONV_BLK)
            dg[:, cols] = _conv_taps(dext, qbuf, cols, lambda k, blk=blk: cw_ref[blk, CONV_W - 1 - k:CONV_W - k, :], t)
            for b in range(SUB):
                dshift = dpad[pl.ds(SUB - 1 - b, t + SUB), cols]
                for a in range((CONV_W - b + SUB - 1) // SUB):
                    k = SUB * a + b
                    taps[k:k + 1, :] = _colsum(gext[SUB * a:SUB * a + t + SUB, cols] * dshift)
            gcw_ref[blk] += taps[...]

        dgv = dg[...]
        dz_ref[:, 0:D] = (dgv * sig).astype(BF16)
        dz_ref[:, D:2 * D] = (dgv * a_val * sig * (1.0 - sig)).astype(BF16)
        dz_ref[:, 2 * D:6 * D] = dzr_ref[...]

        dh = None
        for j in range(N_DEV):
            term = lax.dot_general(dz_ref[:, j * W_IN_BLK:(j + 1) * W_IN_BLK], w_ref[j], (((1,), (1,)), ((), ())),
                                   preferred_element_type=F32)
            dh = term if dh is None else dh + term

        xn, r, _, scale = _rms_modulate(x_ref[...], mod_ref)
        ng = ng_ref[...]
        one_scale = 1.0 + scale
        dh_xn = dh * xn
        acc_ref[0:1, :] += _colsum(dh_xn * one_scale)
        acc_ref[1:2, :] += _colsum(dh)
        acc_ref[2:3, :] += _colsum(dh_xn * ng)
        dxn = dh * (ng * one_scale)
        gx_ref[...] = dx2_ref[...] + r * (dxn - xn * jnp.mean(dxn * xn, axis=-1, keepdims=True))

    row = lambda w: pl.BlockSpec((t, w), lambda i: (i, 0))
    const = lambda shape: pl.BlockSpec(shape, lambda i: (0,) * len(shape))
    return pl.pallas_call(
        body, name="bwd_in_proj", grid=(n_steps,),
        out_shape=(jax.ShapeDtypeStruct((s, D), F32), jax.ShapeDtypeStruct((s, D_IN), BF16),
                   jax.ShapeDtypeStruct((8, D), F32), jax.ShapeDtypeStruct((N_DEV, CONV_ROWS, CONV_BLK), F32)),
        in_specs=[pl.BlockSpec((t, 2 * D), lambda i: (i, 0)), zp_spec, zn_spec, row(D), dp_spec, dn_spec, row(4 * D),
                  row(D), row(D), VM, VM, VM, VM],
        out_specs=(row(D), row(D_IN), const((8, D)), const((N_DEV, CONV_ROWS, CONV_BLK))),
        scratch_shapes=[pltpu.VMEM((t + 2 * HALO, D), F32), pltpu.VMEM((t + 2 * HALO, D), F32), pltpu.VMEM((t, D), F32),
                        pltpu.VMEM((CONV_ROWS, CONV_BLK), F32), pltpu.VMEM((t + 2 * SUB, D), F32),
                        pltpu.VMEM((SUB, CONV_PHASE_ROWS, CONV_BLK), F32)],
        compiler_params=_params(1),
    )(z, z, z, dcv, dcv, dcv, dz_rest, x, dx2, mod, norm_g, conv_w_g, win_g)


WGRAD_K = 2048
WGRAD_VMEM_LIMIT = 62 * 1024 * 1024
_OTHER_CHIPS = [(3, True), (3, False), (1, True), (1, False), (2, True), (2, False)]
_OWN_CHIP = [(0, True), (0, False)]
WGRAD_ROLES = ([("out",) + e for e in _OTHER_CHIPS] + [("in",) + e for e in _OTHER_CHIPS]
               + [("out",) + e for e in _OWN_CHIP] + [("in",) + e for e in _OWN_CHIP])


def _wgrad_schedule():
    me, _, chips, c = _neighbours()
    chip_of = [me[:2]] + chips
    blk = lambda r, sibling: _shard_index(chip_of, c, r, sibling)
    out_blk, in_blk, is_out = [], [], []
    last = {"out": blk(*_OTHER_CHIPS[0]), "in": blk(*_OTHER_CHIPS[0])}
    for kind, r, sibling in WGRAD_ROLES:
        last[kind] = blk(r, sibling)
        out_blk.append(last["out"])
        in_blk.append(last["in"])
        is_out.append(1 if kind == "out" else 0)
    as_vec = lambda v: jnp.stack([jnp.asarray(e, jnp.int32) for e in v])
    return as_vec(out_blk), as_vec(in_blk), as_vec(is_out)


def _sum_blocks(gathered):
    total = gathered[0]
    for b in range(1, N_DEV):
        total = total + gathered[b]
    return total


SMALL_ROWS = 48
ROW_NORM_G, ROW_FINAL_G, ROW_B_S, ROW_CONV_B, ROW_CLN_G, ROW_CLN_B, ROW_SLN_G, ROW_SLN_B, ROW_LOSS = range(32, 41)


def _wgrad_reduce(ht, dz, ycatt, dy, small, gws, dmod):
    s = dz.shape[0]
    n_kc = s // WGRAD_K
    n_steps = len(WGRAD_ROLES)
    first_in = [k for k, role in enumerate(WGRAD_ROLES) if role[0] == "in"][0]
    blk_in, blk_out = (D, W_IN_BLK), (W_OUT_BLK, D)

    def body(out_blk, in_blk, is_out, ht_ref, dz_ref, yt_ref, dy_ref, small_ref, gws_ref, dmod_ref,
             oin_ref, oout_ref, osmall_ref, ogws_ref, odmod_ref, obada_ref, oloss_ref,
             acc_in, acc_out, p1_in, p1_out, l1_in, l1_out, l2_in, l2_out, small_g, gws_g, dmod_g,
             s1_send, s1_recv, s2_send, s2_recv, t1_send, t1_recv, t2_send, t2_recv, ag_send, ag_recv):
        step, kc = pl.program_id(0), pl.program_id(1)
        pay = {"in": (acc_in, p1_in, oin_ref, _ChipReduce(l1_in, l2_in, s1_send, s1_recv, s2_send, s2_recv)),
               "out": (acc_out, p1_out, oout_ref, _ChipReduce(l1_out, l2_out, t1_send, t1_recv, t2_send, t2_recv))}
        ag = _AllGather([dmod_g, small_g, gws_g], ag_send, ag_recv)
        last_kc = kc == n_kc - 1

        @pl.when((step == 0) & (kc == 0))
        def _():
            me = _my_block()
            small_g[me] = small_ref[...]
            gws_g[me] = gws_ref[...].astype(BF16)
            dmod_g[me] = dmod_ref[...]
            for a in range(3):
                ag.send_own(a)

        @pl.when((step == 4) & (kc == 0))
        def _():
            for a in range(3):
                ag.other_chips(a)

        def accumulate(acc, prod):
            @pl.when(kc == 0)
            def _():
                acc[...] = prod

            @pl.when(kc != 0)
            def _():
                acc[...] += prod

        @pl.when(is_out[step] == 1)
        def _():
            accumulate(acc_out, jnp.dot(yt_ref[...], dy_ref[kc], preferred_element_type=F32))

        @pl.when(is_out[step] == 0)
        def _():
            accumulate(acc_in, jnp.dot(ht_ref[kc], dz_ref[...], preferred_element_type=F32))

        for k, (kind, r, sibling) in enumerate(WGRAD_ROLES):
            @pl.when((step == k) & last_kc)
            def _(kind=kind, r=r, sibling=sibling):
                acc, p1, out, red = pay[kind]
                if sibling:
                    red.send_to_sibling(r, p1, acc[...])
                else:
                    chip_sum = red.combine(r, acc[...])
                    if r == 0:
                        out[...] = chip_sum

        @pl.when((step == n_steps - 1) & last_kc)
        def _():
            for kind in ("out", "in"):
                _, _, out, red = pay[kind]
                out[...] = red.finish(out[...])
            for a in range(3):
                ag.from_sibling(a)
            for a in range(3):
                ag.sent(a)
            tot_small = _sum_blocks(small_g)
            osmall_ref[...] = tot_small
            oloss_ref[...] = jnp.full(oloss_ref.shape, (0.5 / D) * jnp.sum(tot_small[ROW_LOSS:ROW_LOSS + 1, :]), F32)
            tot_gws = gws_g[0].astype(F32)
            for b in range(1, N_DEV):
                tot_gws = tot_gws + gws_g[b].astype(F32)
            ogws_ref[...] = tot_gws
            obada_ref[...] = _sum_blocks(dmod_g)
            for b in range(N_DEV):
                odmod_ref[b:b + 1, :] = dmod_g[b, 0:1, :]

    def kc_of(working, step, kc, hold_first):
        held = jnp.where(step < hold_first, 0, n_kc - 1)
        return jnp.where(working, kc, held)

    out_kc = lambda i, kc, ob, ib, io: kc_of(io[i] == 1, i, kc, 0)
    in_kc = lambda i, kc, ob, ib, io: kc_of(io[i] == 0, i, kc, first_in)
    sems = lambda n: [pltpu.SemaphoreType.DMA((n,)), pltpu.SemaphoreType.DMA((n,))]
    grid_spec = pltpu.PrefetchScalarGridSpec(
        num_scalar_prefetch=3, grid=(n_steps, n_kc),
        in_specs=[VM,
                  pl.BlockSpec((WGRAD_K, W_IN_BLK), lambda i, kc, ob, ib, io: (in_kc(i, kc, ob, ib, io), ib[i])),
                  pl.BlockSpec((W_OUT_BLK, WGRAD_K), lambda i, kc, ob, ib, io: (ob[i], out_kc(i, kc, ob, ib, io))),
                  VM,
                  VM, VM, VM],
        out_specs=(VM,) * 7,
        scratch_shapes=[pltpu.VMEM(blk_in, F32), pltpu.VMEM(blk_out, F32),
                        pltpu.VMEM(blk_in, BF16), pltpu.VMEM(blk_out, BF16),
                        pltpu.VMEM((4,) + blk_in, BF16), pltpu.VMEM((4,) + blk_out, BF16),
                        pltpu.VMEM((3,) + blk_in, BF16), pltpu.VMEM((3,) + blk_out, BF16),
                        pltpu.VMEM((N_DEV,) + small.shape, F32), pltpu.VMEM((N_DEV,) + gws.shape, BF16),
                        pltpu.VMEM((N_DEV,) + dmod.shape, F32)]
        + sems(4) + sems(3) + sems(4) + sems(3) + sems(21))
    return pl.pallas_call(
        body, name="wgrad_reduce", grid_spec=grid_spec,
        out_shape=(jax.ShapeDtypeStruct(blk_in, F32), jax.ShapeDtypeStruct(blk_out, F32),
                   jax.ShapeDtypeStruct(small.shape, F32), jax.ShapeDtypeStruct(gws.shape, F32),
                   jax.ShapeDtypeStruct((N_DEV, 3 * D), F32), jax.ShapeDtypeStruct((8, 3 * D), F32),
                   jax.ShapeDtypeStruct((8, 128), F32)),
        compiler_params=_params(2, vmem=WGRAD_VMEM_LIMIT),
    )(*_wgrad_schedule(), ht, dz, ycatt, dy.reshape(n_kc, WGRAD_K, D), small, gws, dmod)


def _adamw_math(w, g, m, v):
    m = ADAM_B1 * m + (1.0 - ADAM_B1) * g
    v = ADAM_B2 * v + (1.0 - ADAM_B2) * (g * g)
    m_hat = m / (1.0 - ADAM_B1 ** ADAM_STEP)
    v_hat = v / (1.0 - ADAM_B2 ** ADAM_STEP)
    delta = -ADAM_LR * (m_hat / (jnp.sqrt(v_hat) + ADAM_EPS) + ADAM_WD * w)
    return delta, m, v


VECTORS = ["norm_g", "conv_b", "conv_ln_g", "conv_ln_b", "sg_ln_g", "sg_ln_b", "final_g", "b_s"]
VECTOR_ROWS = [ROW_NORM_G, ROW_CONV_B, ROW_CLN_G, ROW_CLN_B, ROW_SLN_G, ROW_SLN_B, ROW_FINAL_G, ROW_B_S]
ADAM_STEPS = 4


def _adamw_all(me, g_w_in, g_w_out, g_w_s, small_sum, dmod_all, bada8, c_all, matrices, b_ada, conv_w, vectors):
    mat_shapes = [(D, W_IN_BLK), (D, W_ADA_BLK), (W_OUT_BLK, D), (HEADS * CHUNK, CHUNK)]
    mat_blocks = [(sh[0] // ADAM_STEPS, sh[1]) for sh in mat_shapes]
    n_small = 2 + len(VECTORS)

    def body(me_ref, gin_ref, gout_ref, gws_ref, taps_ref, small_ref, dmc_ref, bada_ref, c_ref, *refs):
        params = refs[:3 * (4 + n_small)]
        outs = refs[3 * (4 + n_small):-2]
        act, gada = refs[-2:]
        mat_out, gada_out, small_out = outs[:12], outs[12], outs[13:]
        i = pl.program_id(0)

        @pl.when(i == 0)
        def _():
            for b in range(N_DEV):
                cb = c_ref[b, 0:1, :]
                act[b:b + 1, :] = cb * _sigmoid(cb)
            gada[...] = lax.dot_general(act[...], dmc_ref[...], (((0,), (0,)), ((), ())), preferred_element_type=F32,
                                        precision=lax.Precision.HIGHEST)
            small_grads = [bada_ref[0:1, :], taps_ref[0:CONV_W, :]] + [small_ref[r:r + 1, :] for r in VECTOR_ROWS]
            for k, g in enumerate(small_grads):
                w_ref, m_ref, v_ref = params[3 * (4 + k):3 * (5 + k)]
                o = small_out[4 * k:4 * k + 4]
                o[0][...] = g
                o[1][...], o[2][...], o[3][...] = _adamw_math(w_ref[...], g, m_ref[...], v_ref[...])

        rows = pl.ds(pl.multiple_of(i * mat_blocks[1][0], mat_blocks[1][0]), mat_blocks[1][0])
        g_ada = gada[rows, :]
        gada_out[...] = g_ada
        for k, g in enumerate([gin_ref[...], g_ada, gout_ref[...], gws_ref[...]]):
            w_ref, m_ref, v_ref = params[3 * k:3 * k + 3]
            o = mat_out[3 * k:3 * k + 3]
            o[0][...], o[1][...], o[2][...] = _adamw_math(w_ref[...], g, m_ref[...], v_ref[...])

    rows_of = lambda blk: pl.BlockSpec(blk, lambda i, me_ref: (i, 0))
    mat_specs = [rows_of(b) for b in mat_blocks]
    grid_spec = pltpu.PrefetchScalarGridSpec(
        num_scalar_prefetch=1, grid=(ADAM_STEPS,),
        in_specs=[mat_specs[0], mat_specs[2], mat_specs[3],
                  pl.BlockSpec((CONV_ROWS, CONV_BLK), lambda i, me_ref: (0, me_ref[0])), VM,
                  pl.BlockSpec((N_DEV, W_ADA_BLK), lambda i, me_ref: (0, me_ref[0])), VM, VM]
        + [s for s in mat_specs for _ in range(3)] + [VM] * (3 * n_small),
        out_specs=tuple([s for s in mat_specs for _ in range(3)] + [mat_specs[1]] + [VM] * (4 * n_small)),
        scratch_shapes=[pltpu.VMEM((N_DEV, D), F32), pltpu.VMEM((D, W_ADA_BLK), F32)])
    small_shapes = [b_ada[0].shape, conv_w[0].shape] + [(1, D)] * len(VECTORS)
    out_shape = tuple([jax.ShapeDtypeStruct(sh, F32) for sh in mat_shapes for _ in range(3)]
                      + [jax.ShapeDtypeStruct(mat_shapes[1], F32)]
                      + [jax.ShapeDtypeStruct(sh, F32) for sh in small_shapes for _ in range(4)])
    flat = [a for group in matrices for a in group] + list(b_ada) + list(conv_w) + [a for group in vectors for a in group]
    return pl.pallas_call(body, name="adamw_all", grid_spec=grid_spec, out_shape=out_shape,
                          compiler_params=_params(1))(
        me, g_w_in, g_w_out, g_w_s, small_sum, small_sum, dmod_all, bada8, c_all, *flat)


def kernel(x, c, w_ada, b_ada, norm_g, w_in, conv_w, conv_b, conv_ln_g, conv_ln_b, sg_ln_g, sg_ln_b, w_s, b_s, w_out, final_g, loss_target, m_w_ada, m_b_ada, m_norm_g, m_w_in, m_conv_w, m_conv_b, m_conv_ln_g, m_conv_ln_b, m_sg_ln_g, m_sg_ln_b, m_w_s, m_b_s, m_w_out, m_final_g, v_w_ada, v_b_ada, v_norm_g, v_w_in, v_conv_w, v_conv_b, v_conv_ln_g, v_conv_ln_b, v_sg_ln_g, v_sg_ln_b, v_w_s, v_b_s, v_w_out, v_final_g):
    me = 4 * lax.axis_index("x") + 2 * lax.axis_index("y") + lax.axis_index("c")
    x2d, tgt2d = x[0], loss_target[0]
    row1 = lambda a: a.reshape(1, D)
    taps = lambda a: jnp.pad(a.reshape(CONV_W, CONV_BLK), ((0, CONV_ROWS - CONV_W), (0, 0)))

    z, ht, win_g, wout_g, cw_g, c_all, mod = _fwd_in_proj(
        x2d, jnp.broadcast_to(c, (8, D)), w_ada[0], b_ada, norm_g, w_in[0], w_out[0], taps(conv_w))
    ws_b = w_s[0].astype(BF16)
    wst_b = jnp.swapaxes(w_s[0], 1, 2).astype(BF16)
    bs_full = jnp.repeat(b_s[0].T, HEAD_DIM, axis=1)

    dx2, ycatt, dy, dcv, dz_rest, acc_a, gws, gbs = _mix_and_head(
        z, x2d, tgt2d, mod, cw_g, conv_b, conv_ln_g, conv_ln_b, sg_ln_g, sg_ln_b, row1(final_g), ws_b, wst_b, bs_full,
        wout_g.reshape(D_MIX, D))
    grad_x, dz, acc_b, gcw = _bwd_in_proj(z, dcv, dz_rest, x2d, dx2, mod, norm_g, cw_g, win_g)

    small = jnp.concatenate(
        [jnp.transpose(gcw, (1, 0, 2)).reshape(CONV_ROWS, D), acc_b[0:1], acc_a[0:1], gbs.T.reshape(1, D), acc_a[2:8],
         jnp.zeros((SMALL_ROWS - ROW_LOSS - 1, D), F32)], axis=0)
    dmod_row = jnp.concatenate([acc_b[1:2], acc_b[2:3], acc_a[1:2]], axis=1)
    g_w_in, g_w_out, small_sum, g_w_s, dmod_all, bada8, loss_tile = _wgrad_reduce(
        ht, dz, ycatt, dy, small, gws.reshape(HEADS * CHUNK, CHUNK), jnp.broadcast_to(dmod_row, (8, 3 * D)))

    given = dict(w_ada=(w_ada, m_w_ada, v_w_ada), b_ada=(b_ada, m_b_ada, v_b_ada), norm_g=(norm_g, m_norm_g, v_norm_g),
                 w_in=(w_in, m_w_in, v_w_in), conv_w=(conv_w, m_conv_w, v_conv_w), conv_b=(conv_b, m_conv_b, v_conv_b),
                 conv_ln_g=(conv_ln_g, m_conv_ln_g, v_conv_ln_g), conv_ln_b=(conv_ln_b, m_conv_ln_b, v_conv_ln_b),
                 sg_ln_g=(sg_ln_g, m_sg_ln_g, v_sg_ln_g), sg_ln_b=(sg_ln_b, m_sg_ln_b, v_sg_ln_b),
                 w_s=(w_s, m_w_s, v_w_s), b_s=(b_s, m_b_s, v_b_s), w_out=(w_out, m_w_out, v_w_out),
                 final_g=(final_g, m_final_g, v_final_g))
    as2d = lambda name, shape: tuple(a.reshape(shape) for a in given[name])
    res = _adamw_all(
        jnp.reshape(me, (1,)).astype(jnp.int32), g_w_in, g_w_out, g_w_s, small_sum, dmod_all, bada8, c_all,
        [as2d("w_in", (D, W_IN_BLK)), as2d("w_ada", (D, W_ADA_BLK)), as2d("w_out", (W_OUT_BLK, D)),
         as2d("w_s", (HEADS * CHUNK, CHUNK))],
        given["b_ada"], as2d("conv_w", (CONV_W, CONV_BLK)), [as2d(n, (1, D)) for n in VECTORS])
    out = {}
    for k, name in enumerate(["w_in", "w_ada", "w_out", "w_s"]):
        out[name] = [None] + list(res[3 * k:3 * k + 3])
    out["w_in"][0], out["w_ada"][0], out["w_out"][0], out["w_s"][0] = g_w_in, res[12], g_w_out, g_w_s
    for k, name in enumerate(["b_ada", "conv_w"] + VECTORS):
        out[name] = list(res[13 + 4 * k:17 + 4 * k])
    order = ["w_ada", "b_ada", "norm_g", "w_in", "conv_w", "conv_b", "conv_ln_g", "conv_ln_b", "sg_ln_g", "sg_ln_b",
             "w_s", "b_s", "w_out", "final_g"]
    outs = [loss_tile[0, 0], grad_x.reshape(x.shape)]
    for kind in range(4):
        outs += [out[n][kind].reshape(given[n][0].shape) for n in order]
    return tuple(outs)
```

```python
import functools

import jax
import jax.numpy as jnp
from jax import lax
from jax.experimental import pallas as pl
from jax.experimental.pallas import tpu as pltpu

F32 = jnp.float32
BF16 = jnp.bfloat16
MESH = pl.DeviceIdType.MESH

D = 1024
D_IN = 6 * D
D_MIX = 2 * D
N_DEV = 8
W_IN_BLK = D_IN // N_DEV
W_OUT_BLK = D_MIX // N_DEV
W_ADA_BLK = 3 * D // N_DEV
CONV_BLK = D // N_DEV
CONV_W = 31
CONV_HALF = CONV_W // 2
CONV_ROWS = 32
HALO = 16
CHUNK = 128
HEADS = 8
HEAD_DIM = 128
EPS = 1e-6
ROW_TILE = 256
VMEM_LIMIT = 56 * 1024 * 1024

ADAM_LR = 0.001
ADAM_B1 = 0.9
ADAM_B2 = 0.999
ADAM_EPS = 1e-08
ADAM_WD = 0.01
ADAM_STEP = 10

VM = pl.BlockSpec(memory_space=pltpu.VMEM)


def _params(grid_rank=0, vmem=VMEM_LIMIT):
    sem = ("arbitrary",) * grid_rank if grid_rank else None
    return pltpu.CompilerParams(dimension_semantics=sem, vmem_limit_bytes=vmem)


def _sigmoid(t):
    return jax.nn.sigmoid(t)


def _dsilu(t, sig):
    return sig * (1.0 + t * (1.0 - sig))


def _mesh_pos():
    return lax.axis_index("x"), lax.axis_index("y"), lax.axis_index("c")


def _neighbours():
    x, y, c = _mesh_pos()
    nb1 = (x + c * (1 - 2 * x), y + (1 - c) * (1 - 2 * y))
    nb2 = (x + (1 - c) * (1 - 2 * x), y + c * (1 - 2 * y))
    return (x, y, c), (x, y, 1 - c), [nb1, nb2, (1 - x, 1 - y)], c


def _sibling_slot(j):
    return j if j == 2 else 1 - j


class _AllGather:
    def __init__(self, bufs, send_sems, recv_sems):
        self.bufs, self.send_sems, self.recv_sems = bufs, send_sems, recv_sems
        self.me, self.sib, self.chips, self.c = _neighbours()

    def _copy(self, a, k, block, to):
        px, py, pc = block
        ref = self.bufs[a].at[4 * px + 2 * py + pc]
        return pltpu.make_async_remote_copy(
            src_ref=ref, dst_ref=ref, send_sem=self.send_sems.at[7 * a + k], recv_sem=self.recv_sems.at[7 * a + k],
            device_id=to, device_id_type=MESH)

    def _outgoing(self, a, k):
        if k == 0:
            return self._copy(a, 0, self.me, self.sib)
        if k <= 2:
            return self._copy(a, k, self.me, (*self.chips[k - 1], self.c))
        if k == 3:
            return self._copy(a, 3, (*self.chips[0], self.c), (*self.chips[1], self.c))
        return self._copy(a, k, (*self.chips[_sibling_slot(k - 4)], self.c), self.sib)

    def source(self, k):
        if k == 0:
            return self.sib
        return (*self.chips[(k - 1) % 3], self.c if k <= 3 else 1 - self.c)

    def block_index(self, k):
        px, py, pc = self.source(k)
        return 4 * px + 2 * py + pc

    def send_own(self, a):
        for k in range(3):
            self._outgoing(a, k).start()

    def arrived(self, a, k):
        self._copy(a, k, self.source(k), self.me).wait_recv()

    def relay(self, a):
        self._outgoing(a, 3).start()

    def pass_on(self, a, j):
        self._outgoing(a, 4 + _sibling_slot(j)).start()

    def other_chips(self, a):
        self.arrived(a, 1)
        self.relay(a)
        self.pass_on(a, 0)
        for j in (1, 2):
            self.arrived(a, 1 + j)
            self.pass_on(a, j)

    def from_sibling(self, a):
        for k in (0, 4, 5, 6):
            self.arrived(a, k)

    def sent(self, a):
        for k in range(7):
            self._outgoing(a, k).wait_send()


def _shard_index(chip_of, c, r, sibling):
    if sibling and r:
        r = 1 + _sibling_slot(r - 1)
    cx, cy = chip_of[r]
    return 4 * cx + 2 * cy + ((1 - c) if sibling else c)


class _ChipReduce:
    def __init__(self, l1, l2, s1_send, s1_recv, s2_send, s2_recv):
        self.l1, self.l2 = l1, l2
        self.s1_send, self.s1_recv, self.s2_send, self.s2_recv = s1_send, s1_recv, s2_send, s2_recv
        me, self.sib, chips, self.c = _neighbours()
        self.chip_of = [me[:2]] + chips

    def block(self, r, sibling):
        return _shard_index(self.chip_of, self.c, r, sibling)

    def to_sibling(self, r, src=None):
        return pltpu.make_async_remote_copy(
            src_ref=self.l1.at[r] if src is None else src, dst_ref=self.l1.at[r], send_sem=self.s1_send.at[r],
            recv_sem=self.s1_recv.at[r], device_id=self.sib, device_id_type=MESH)

    ORDER = (3, 1, 2, 0)

    def send_to_sibling(self, r, stage, value):
        k = self.ORDER.index(r)
        if k:
            self.to_sibling(self.ORDER[k - 1], stage).wait_send()
        stage[...] = value.astype(BF16)
        self.to_sibling(r, stage).start()

    def to_chip(self, r):
        target = self.chip_of[1] if r == 3 else self.chip_of[r]
        return pltpu.make_async_remote_copy(
            src_ref=self.l1.at[r], dst_ref=self.l2.at[r - 1], send_sem=self.s2_send.at[r - 1],
            recv_sem=self.s2_recv.at[r - 1], device_id=(*target, self.c), device_id_type=MESH)

    def combine(self, r, mine):
        self.to_sibling(r).wait_recv()
        both = mine + self.l1[r].astype(F32)
        if r == 0:
            return both
        if r == 2:
            self.to_chip(3).wait_recv()
            both = both + self.l2[2].astype(F32)
        self.l1[r] = both.astype(BF16)
        self.to_chip(r).start()
        return None

    def finish(self, own_chip_sum):
        total = own_chip_sum
        for r in (1, 2):
            self.to_chip(r).wait_recv()
            total = total + self.l2[r - 1].astype(F32)
        self.to_sibling(self.ORDER[-1]).wait_send()
        for r in (1, 2, 3):
            self.to_chip(r).wait_send()
        return total


def _my_block():
    x, y, c = _mesh_pos()
    return 4 * x + 2 * y + c


def _modulation(c_g, w_ref, b_ref, mod_ref, part, land, send_sems, recv_sems):
    x, y, c = _mesh_pos()
    me = 4 * x + 2 * y + c
    c_rows = c_g[...].reshape(N_DEV * 8, D)
    parts = jnp.dot(c_rows * _sigmoid(c_rows), w_ref[...], preferred_element_type=F32, precision=lax.Precision.HIGHEST)
    for b in range(N_DEV):
        part[b] = parts[8 * b:8 * b + 8]
    land[me] = part[me]

    def copy(b):
        return pltpu.make_async_remote_copy(
            src_ref=part.at[b], dst_ref=land.at[me], send_sem=send_sems.at[b], recv_sem=recv_sems.at[me],
            device_id=(b // 4, (b // 2) % 2, b % 2), device_id_type=MESH)

    def arrival(b):
        return pltpu.make_async_remote_copy(
            src_ref=part.at[b], dst_ref=land.at[b], send_sem=send_sems.at[b], recv_sem=recv_sems.at[b],
            device_id=(b // 4, (b // 2) % 2, b % 2), device_id_type=MESH)

    for b in range(N_DEV):
        @pl.when(b != me)
        def _():
            copy(b).start()
    for b in range(N_DEV):
        @pl.when(b != me)
        def _():
            arrival(b).wait_recv()
            copy(b).wait_send()
    for b in range(N_DEV):
        cols = slice(b * W_ADA_BLK, (b + 1) * W_ADA_BLK)
        mod_ref[:, cols] = land[b] + b_ref[:, cols]


def _rms_modulate(x, mod_ref):
    shift = mod_ref[0:1, 0:D]
    scale = mod_ref[0:1, D:2 * D]
    r = lax.rsqrt(jnp.mean(x * x, axis=-1, keepdims=True) + EPS)
    xn = x * r
    return xn, r, shift, scale


FWD_TILE = 512
FWD_Z_TILE = 1024


def _fwd_in_proj(x, c_rep, w_ada, b_ada, norm_g, w_in, w_out, conv_w):
    s = x.shape[0]
    t, tz = FWD_TILE, FWD_Z_TILE
    n_tiles = s // t
    C_PAY, WIN_PAY, WOUT_PAY, CW_PAY = 0, 1, 2, 3

    def body(x_hbm, c_ref, wada_ref, bada_ref, ng_ref, win_ref, wout_ref, cw_ref,
             z_hbm, ht_ref, win_g, wout_g, cw_g, c_g, mod_ref,
             h, xbuf, zbuf, part, land, ag_send, ag_recv, mod_send, mod_recv, x_sem, z_sem):
        me = _my_block()
        c_g[me] = c_ref[...]
        ag = _AllGather([c_g, win_g, wout_g, cw_g], ag_send, ag_recv)
        ag.send_own(C_PAY)

        def x_copy(i):
            return pltpu.make_async_copy(x_hbm.at[pl.ds(i * t, t), :], xbuf.at[i % 2], x_sem.at[i % 2])

        x_copy(0).start()
        win_g[me] = win_ref[...].astype(BF16)
        wout_g[me] = wout_ref[...].astype(BF16)
        cw_g[me] = cw_ref[...]

        ag.other_chips(C_PAY)
        ag.from_sibling(C_PAY)
        _modulation(c_g, wada_ref, bada_ref, mod_ref, part, land, mod_send, mod_recv)
        for a in (WIN_PAY, WOUT_PAY, CW_PAY):
            ag.send_own(a)

        for i in range(n_tiles):
            x_copy(i).wait()
            if i + 1 < n_tiles:
                x_copy(i + 1).start()
            xn, _, shift, scale = _rms_modulate(xbuf[i % 2], mod_ref)
            hh = xn * ng_ref[...] * (1.0 + scale) + shift
            h[i * t:(i + 1) * t, :] = hh.astype(BF16)
            ht_ref[(i * t) // WGRAD_K, :, (i * t) % WGRAD_K:(i * t) % WGRAD_K + t] = hh.T.astype(BF16)

        def z_copy(slot, row0, col0):
            return pltpu.make_async_copy(zbuf.at[slot], z_hbm.at[pl.ds(row0, tz), pl.ds(col0, W_IN_BLK)], z_sem.at[slot])

        done = [0]

        def z_block(blk):
            col0 = pl.multiple_of(blk * W_IN_BLK, 128)
            for i in range(s // tz):
                slot = done[0] % 2
                if done[0] >= 2:
                    z_copy(slot, 0, 0).wait()
                zbuf[slot] = jnp.dot(h[i * tz:(i + 1) * tz, :], win_g[blk], preferred_element_type=F32)
                z_copy(slot, i * tz, col0).start()
                done[0] += 1

        z_block(me)
        ag.arrived(WIN_PAY, 0)
        z_block(ag.block_index(0))
        ag.arrived(WIN_PAY, 1)
        ag.relay(WIN_PAY)
        ag.pass_on(WIN_PAY, 0)
        ag.arrived(WIN_PAY, 2)
        ag.pass_on(WIN_PAY, 1)
        for k in (1, 2, 4, 5):
            if k >= 4:
                ag.arrived(WIN_PAY, k)
            z_block(ag.block_index(k))
        ag.arrived(WIN_PAY, 3)
        ag.pass_on(WIN_PAY, 2)
        z_block(ag.block_index(3))
        ag.arrived(WIN_PAY, 6)
        z_block(ag.block_index(6))

        for a in (WOUT_PAY, CW_PAY):
            ag.other_chips(a)
        for a in (WOUT_PAY, CW_PAY):
            ag.from_sibling(a)
        for a in (C_PAY, WIN_PAY, WOUT_PAY, CW_PAY):
            ag.sent(a)
        z_copy(0, 0, 0).wait()
        z_copy(1, 0, 0).wait()

    any_spec = pl.BlockSpec(memory_space=pl.ANY)
    return pl.pallas_call(
        body, name="fwd_in_proj",
        out_shape=(jax.ShapeDtypeStruct((s, D_IN), F32), jax.ShapeDtypeStruct((s // WGRAD_K, D, WGRAD_K), BF16),
                   jax.ShapeDtypeStruct((N_DEV,) + w_in.shape, BF16), jax.ShapeDtypeStruct((N_DEV,) + w_out.shape, BF16),
                   jax.ShapeDtypeStruct((N_DEV,) + conv_w.shape, F32), jax.ShapeDtypeStruct((N_DEV,) + c_rep.shape, F32),
                   jax.ShapeDtypeStruct((8, 3 * D), F32)),
        in_specs=[any_spec] + [VM] * 7, out_specs=(any_spec,) + (VM,) * 6,
        scratch_shapes=[pltpu.VMEM((s, D), BF16), pltpu.VMEM((2, t, D), F32), pltpu.VMEM((2, tz, W_IN_BLK), F32),
                        pltpu.VMEM((N_DEV, 8, W_ADA_BLK), F32), pltpu.VMEM((N_DEV, 8, W_ADA_BLK), F32),
                        pltpu.SemaphoreType.DMA((28,)), pltpu.SemaphoreType.DMA((28,)),
                        pltpu.SemaphoreType.DMA((N_DEV,)), pltpu.SemaphoreType.DMA((N_DEV,)),
                        pltpu.SemaphoreType.DMA((2,)), pltpu.SemaphoreType.DMA((2,))],
        compiler_params=_params(),
    )(x, c_rep, w_ada, b_ada, norm_g, w_in, w_out, conv_w)


def _halo_specs(t, s, width):
    per = t // HALO
    last = s // HALO - 1
    prev = pl.BlockSpec((HALO, width), lambda i: (jnp.maximum(i * per - 1, 0), 0))
    nxt = pl.BlockSpec((HALO, width), lambda i: (jnp.minimum((i + 1) * per, last), 0))
    return prev, nxt


def _glu(ref):
    return ref[:, 0:D] * _sigmoid(ref[:, D:2 * D])


SUB = 8
CONV_PHASE_ROWS = ROW_TILE + SUB


def _conv_taps(ext, qbuf, cols, tap, t):
    out = None
    for b in range(SUB):
        q = None
        for a in range((CONV_W - b + SUB - 1) // SUB):
            term = ext[SUB * a:SUB * a + t + SUB, cols] * tap(SUB * a + b)
            q = term if q is None else q + term
        qbuf[b] = q
        shifted = qbuf[b, pl.ds(b + 1, t), :]
        out = shifted if out is None else out + shifted
    return out


def _layer_norm_stats(v):
    mu = jnp.mean(v, axis=-1, keepdims=True)
    cen = v - mu
    rstd = lax.rsqrt(jnp.mean(cen * cen, axis=-1, keepdims=True) + EPS)
    return cen * rstd, rstd


def _layer_norm_bwd(dy_hat, hat, rstd):
    m1 = jnp.mean(dy_hat, axis=-1, keepdims=True)
    m2 = jnp.mean(dy_hat * hat, axis=-1, keepdims=True)
    return rstd * (dy_hat - m1 - hat * m2)


def _colsum(v):
    return jnp.sum(v, axis=0, keepdims=True)


def _mix_and_head(z, x, tgt, mod, conv_w_g, conv_b, cln_g, cln_b, sln_g, sln_b, final_g, ws_b, wst_b, bs_full, wout):
    s = x.shape[0]
    t = ROW_TILE
    n_chunks = t // CHUNK
    n_steps = s // t
    prev_spec, next_spec = _halo_specs(t, s, 2 * D)

    def body(z_ref, zp_ref, zn_ref, x_ref, tgt_ref, mod_ref, cw_ref, cb_ref, clg_ref, clb_ref, slg_ref, slb_ref, fg_ref,
             ws_ref, wst_ref, bs_ref, wout_ref,
             dx2_ref, ycatt_ref, dy_ref, dcv_ref, dzr_ref, acc_ref, gws_ref, gbs_ref,
             gext, cv, vs, dvn, ycat, gbs_acc, qbuf):
        i = pl.program_id(0)

        @pl.when(i == 0)
        def _():
            acc_ref[...] = jnp.zeros_like(acc_ref)
            gws_ref[...] = jnp.zeros_like(gws_ref)
            gbs_acc[...] = jnp.zeros_like(gbs_acc)

        gext[0:HALO, :] = jnp.where(i > 0, _glu(zp_ref), 0.0)
        gext[HALO:HALO + t, :] = _glu(z_ref)
        gext[HALO + t:2 * HALO + t, :] = jnp.where(i < n_steps - 1, _glu(zn_ref), 0.0)
        for blk in range(N_DEV):
            cols = slice(blk * CONV_BLK, (blk + 1) * CONV_BLK)
            cv[:, cols] = _conv_taps(gext, qbuf, cols, lambda k, blk=blk: cw_ref[blk, k:k + 1, :], t) + cb_ref[:, cols]
        ln_hat, ln_rstd = _layer_norm_stats(cv[...])
        ln_a = ln_hat * clg_ref[...] + clb_ref[...]
        sig_ln = _sigmoid(ln_a)
        sa = ln_a * sig_ln
        a_gate = z_ref[:, 2 * D:3 * D]
        sig_ag = _sigmoid(a_gate)
        s_gate = a_gate * sig_ag
        ya = sa * s_gate

        v_hat, v_rstd = _layer_norm_stats(z_ref[:, 4 * D:5 * D])
        vn = v_hat * slg_ref[...] + slb_ref[...]
        vnb = vn.astype(BF16)
        for n in range(n_chunks):
            rows = slice(n * CHUNK, (n + 1) * CHUNK)
            for h in range(HEADS):
                cols = slice(h * HEAD_DIM, (h + 1) * HEAD_DIM)
                vs[rows, cols] = jnp.dot(ws_ref[h], vnb[rows, cols], preferred_element_type=F32) + bs_ref[:, cols]
        u = z_ref[:, 3 * D:4 * D]
        b_gate = z_ref[:, 5 * D:6 * D]
        sig_bg = _sigmoid(b_gate)
        s_bg = b_gate * sig_bg
        vsv = vs[...]
        yb = u * vsv * s_bg

        ycat[:, 0:D] = ya.astype(BF16)
        ycat[:, D:2 * D] = yb.astype(BF16)
        ycatt_ref[0:D, :] = ya.T.astype(BF16)
        ycatt_ref[D:2 * D, :] = yb.T.astype(BF16)
        y = jnp.dot(ycat[...], wout_ref[...], preferred_element_type=F32)
        gate = mod_ref[0:1, 2 * D:3 * D]
        x2 = x_ref[...] + gate * y
        r2 = lax.rsqrt(jnp.mean(x2 * x2, axis=-1, keepdims=True) + EPS)
        x2n = x2 * r2
        fg = fg_ref[...]
        diff = x2n * fg - tgt_ref[...]
        acc_ref[7:8, :] += _colsum(diff * diff)
        dout = diff * (1.0 / D)
        acc_ref[0:1, :] += _colsum(dout * x2n)
        dx2n = dout * fg
        dx2 = r2 * (dx2n - x2n * jnp.mean(dx2n * x2n, axis=-1, keepdims=True))
        dx2_ref[...] = dx2
        acc_ref[1:2, :] += _colsum(dx2 * y)
        dyb16 = (dx2 * gate).astype(BF16)
        dy_ref[...] = dyb16
        dycat = lax.dot_general(dyb16, wout_ref[...], (((1,), (1,)), ((), ())), preferred_element_type=F32)
        dya = dycat[:, 0:D]
        dyb = dycat[:, D:2 * D]

        du = dyb * vsv * s_bg
        dvs = dyb * u * s_bg
        dbg = dyb * u * vsv * _dsilu(b_gate, sig_bg)
        dvsb = dvs.astype(BF16)
        gbs = gbs_acc[...]
        for n in range(n_chunks):
            rows = slice(n * CHUNK, (n + 1) * CHUNK)
            gbs = gbs + dvs[rows, :]
            for h in range(HEADS):
                cols = slice(h * HEAD_DIM, (h + 1) * HEAD_DIM)
                gws_ref[h] += lax.dot_general(dvsb[rows, cols], vnb[rows, cols], (((1,), (1,)), ((), ())),
                                              preferred_element_type=F32)
                dvn[rows, cols] = jnp.dot(wst_ref[h], dvsb[rows, cols], preferred_element_type=F32)
        gbs_acc[...] = gbs

        @pl.when(i == n_steps - 1)
        def _():
            for h in range(HEADS):
                gbs_ref[:, h:h + 1] = jnp.sum(gbs_acc[:, h * HEAD_DIM:(h + 1) * HEAD_DIM], axis=1, keepdims=True)

        dvnv = dvn[...]
        acc_ref[5:6, :] += _colsum(dvnv * v_hat)
        acc_ref[6:7, :] += _colsum(dvnv)
        dv = _layer_norm_bwd(dvnv * slg_ref[...], v_hat, v_rstd)

        dsa = dya * s_gate
        dagate = dya * sa * _dsilu(a_gate, sig_ag)
        dln = dsa * _dsilu(ln_a, sig_ln)
        acc_ref[3:4, :] += _colsum(dln * ln_hat)
        acc_ref[4:5, :] += _colsum(dln)
        dcv = _layer_norm_bwd(dln * clg_ref[...], ln_hat, ln_rstd)
        acc_ref[2:3, :] += _colsum(dcv)
        dcv_ref[...] = dcv

        dzr_ref[:, 0:D] = dagate.astype(BF16)
        dzr_ref[:, D:2 * D] = du.astype(BF16)
        dzr_ref[:, 2 * D:3 * D] = dv.astype(BF16)
        dzr_ref[:, 3 * D:4 * D] = dbg.astype(BF16)

    row = lambda w: pl.BlockSpec((t, w), lambda i: (i, 0))
    const = lambda shape: pl.BlockSpec(shape, lambda i: (0,) * len(shape))
    return pl.pallas_call(
        body, name="mix_and_head", grid=(n_steps,),
        out_shape=(jax.ShapeDtypeStruct((s, D), F32),
                   jax.ShapeDtypeStruct((D_MIX, s), BF16),
                   jax.ShapeDtypeStruct((s, D), BF16),
                   jax.ShapeDtypeStruct((s, D), F32),
                   jax.ShapeDtypeStruct((s, 4 * D), BF16),
                   jax.ShapeDtypeStruct((8, D), F32),
                   jax.ShapeDtypeStruct((HEADS, CHUNK, CHUNK), F32),
                   jax.ShapeDtypeStruct((CHUNK, HEADS), F32)),
        in_specs=[row(D_IN), prev_spec, next_spec, row(D), row(D)] + [VM] * 12,
        out_specs=(row(D), pl.BlockSpec((D_MIX, t), lambda i: (0, i)), row(D), row(D), row(4 * D),
                   const((8, D)), const((HEADS, CHUNK, CHUNK)), const((CHUNK, HEADS))),
        scratch_shapes=[pltpu.VMEM((t + 2 * HALO, D), F32), pltpu.VMEM((t, D), F32), pltpu.VMEM((t, D), F32),
                        pltpu.VMEM((t, D), F32), pltpu.VMEM((t, D_MIX), BF16), pltpu.VMEM((CHUNK, D), F32),
                        pltpu.VMEM((SUB, CONV_PHASE_ROWS, CONV_BLK), F32)],
        compiler_params=_params(1),
    )(z, z, z, x, tgt, mod, conv_w_g, conv_b, cln_g, cln_b, sln_g, sln_b, final_g, ws_b, wst_b, bs_full, wout)


def _bwd_in_proj(z, dcv, dz_rest, x, dx2, mod, norm_g, conv_w_g, win_g):
    s = x.shape[0]
    t = ROW_TILE
    n_steps = s // t
    zp_spec, zn_spec = _halo_specs(t, s, 2 * D)
    dp_spec, dn_spec = _halo_specs(t, s, D)

    def body(z_ref, zp_ref, zn_ref, dcv_ref, dcvp_ref, dcvn_ref, dzr_ref, x_ref, dx2_ref, mod_ref, ng_ref, cw_ref, w_ref,
             gx_ref, dz_ref, acc_ref, gcw_ref,
             gext, dext, dg, taps, dpad, qbuf):
        i = pl.program_id(0)

        @pl.when(i == 0)
        def _():
            acc_ref[...] = jnp.zeros_like(acc_ref)
            gcw_ref[...] = jnp.zeros_like(gcw_ref)

        not_first = i > 0
        not_last = i < n_steps - 1
        gext[0:HALO, :] = jnp.where(not_first, _glu(zp_ref), 0.0)
        gext[HALO:HALO + t, :] = _glu(z_ref)
        gext[HALO + t:2 * HALO + t, :] = jnp.where(not_last, _glu(zn_ref), 0.0)
        dext[0:HALO, :] = jnp.where(not_first, dcvp_ref[...], 0.0)
        dext[HALO:HALO + t, :] = dcv_ref[...]
        dext[HALO + t:2 * HALO + t, :] = jnp.where(not_last, dcvn_ref[...], 0.0)

        taps[...] = jnp.zeros_like(taps)
        dpad[0:SUB, :] = jnp.zeros((SUB, D), F32)
        dpad[SUB:SUB + t, :] = dcv_ref[...]
        dpad[SUB + t:2 * SUB + t, :] = jnp.zeros((SUB, D), F32)
        for blk in range(N_DEV):
            cols = slice(blk * CONV_BLK, (blk + 1) * CONV_BLK)
            dg[:, cols] = _conv_taps(dext, qbuf, cols, lambda k, blk=blk: cw_ref[blk, CONV_W - 1 - k:CONV_W - k, :], t)
            for b in range(SUB):
                dshift = dpad[pl.ds(SUB - 1 - b, t + SUB), cols]
                for a in range((CONV_W - b + SUB - 1) // SUB):
                    k = SUB * a + b
                    taps[k:k + 1, :] = _colsum(gext[SUB * a:SUB * a + t + SUB, cols] * dshift)
            gcw_ref[blk] += taps[...]

        a_val = z_ref[:, 0:D]
        sig = _sigmoid(z_ref[:, D:2 * D])
        dgv = dg[...]
        dz_ref[:, 0:D] = (dgv * sig).astype(BF16)
        dz_ref[:, D:2 * D] = (dgv * a_val * sig * (1.0 - sig)).astype(BF16)
        dz_ref[:, 2 * D:6 * D] = dzr_ref[...]

        dh = None
        for j in range(N_DEV):
            term = lax.dot_general(dz_ref[:, j * W_IN_BLK:(j + 1) * W_IN_BLK], w_ref[j], (((1,), (1,)), ((), ())),
                                   preferred_element_type=F32)
            dh = term if dh is None else dh + term

        xn, r, _, scale = _rms_modulate(x_ref[...], mod_ref)
        ng = ng_ref[...]
        one_scale = 1.0 + scale
        dh_xn = dh * xn
        acc_ref[0:1, :] += _colsum(dh_xn * one_scale)
        acc_ref[1:2, :] += _colsum(dh)
        acc_ref[2:3, :] += _colsum(dh_xn * ng)
        dxn = dh * (ng * one_scale)
        gx_ref[...] = dx2_ref[...] + r * (dxn - xn * jnp.mean(dxn * xn, axis=-1, keepdims=True))

    row = lambda w: pl.BlockSpec((t, w), lambda i: (i, 0))
    const = lambda shape: pl.BlockSpec(shape, lambda i: (0,) * len(shape))
    return pl.pallas_call(
        body, name="bwd_in_proj", grid=(n_steps,),
        out_shape=(jax.ShapeDtypeStruct((s, D), F32), jax.ShapeDtypeStruct((s, D_IN), BF16),
                   jax.ShapeDtypeStruct((8, D), F32), jax.ShapeDtypeStruct((N_DEV, CONV_ROWS, CONV_BLK), F32)),
        in_specs=[pl.BlockSpec((t, 2 * D), lambda i: (i, 0)), zp_spec, zn_spec, row(D), dp_spec, dn_spec, row(4 * D),
                  row(D), row(D), VM, VM, VM, VM],
        out_specs=(row(D), row(D_IN), const((8, D)), const((N_DEV, CONV_ROWS, CONV_BLK))),
        scratch_shapes=[pltpu.VMEM((t + 2 * HALO, D), F32), pltpu.VMEM((t + 2 * HALO, D), F32), pltpu.VMEM((t, D), F32),
                        pltpu.VMEM((CONV_ROWS, CONV_BLK), F32), pltpu.VMEM((t + 2 * SUB, D), F32),
                        pltpu.VMEM((SUB, CONV_PHASE_ROWS, CONV_BLK), F32)],
        compiler_params=_params(1),
    )(z, z, z, dcv, dcv, dcv, dz_rest, x, dx2, mod, norm_g, conv_w_g, win_g)


WGRAD_K = 2048
WGRAD_VMEM_LIMIT = 62 * 1024 * 1024
_OTHER_CHIPS = [(3, True), (3, False), (1, True), (1, False), (2, True), (2, False)]
_OWN_CHIP = [(0, True), (0, False)]
WGRAD_ROLES = ([("out",) + e for e in _OTHER_CHIPS] + [("in",) + e for e in _OTHER_CHIPS]
               + [("out",) + e for e in _OWN_CHIP] + [("in",) + e for e in _OWN_CHIP])


def _wgrad_schedule():
    me, _, chips, c = _neighbours()
    chip_of = [me[:2]] + chips
    blk = lambda r, sibling: _shard_index(chip_of, c, r, sibling)
    out_blk, in_blk, is_out = [], [], []
    last = {"out": blk(*_OTHER_CHIPS[0]), "in": blk(*_OTHER_CHIPS[0])}
    for kind, r, sibling in WGRAD_ROLES:
        last[kind] = blk(r, sibling)
        out_blk.append(last["out"])
        in_blk.append(last["in"])
        is_out.append(1 if kind == "out" else 0)
    as_vec = lambda v: jnp.stack([jnp.asarray(e, jnp.int32) for e in v])
    return as_vec(out_blk), as_vec(in_blk), as_vec(is_out)


def _sum_blocks(gathered):
    total = gathered[0]
    for b in range(1, N_DEV):
        total = total + gathered[b]
    return total


SMALL_ROWS = 48
ROW_NORM_G, ROW_FINAL_G, ROW_B_S, ROW_CONV_B, ROW_CLN_G, ROW_CLN_B, ROW_SLN_G, ROW_SLN_B, ROW_LOSS = range(32, 41)


def _wgrad_reduce(ht, dz, ycatt, dy, small, gws, dmod):
    s = dz.shape[0]
    n_kc = s // WGRAD_K
    n_steps = len(WGRAD_ROLES)
    first_in = [k for k, role in enumerate(WGRAD_ROLES) if role[0] == "in"][0]
    blk_in, blk_out = (D, W_IN_BLK), (W_OUT_BLK, D)

    def body(out_blk, in_blk, is_out, ht_ref, dz_ref, yt_ref, dy_ref, small_ref, gws_ref, dmod_ref,
             oin_ref, oout_ref, osmall_ref, ogws_ref, odmod_ref, obada_ref, oloss_ref,
             acc_in, acc_out, p1_in, p1_out, l1_in, l1_out, l2_in, l2_out, small_g, gws_g, dmod_g,
             s1_send, s1_recv, s2_send, s2_recv, t1_send, t1_recv, t2_send, t2_recv, ag_send, ag_recv):
        step, kc = pl.program_id(0), pl.program_id(1)
        pay = {"in": (acc_in, p1_in, oin_ref, _ChipReduce(l1_in, l2_in, s1_send, s1_recv, s2_send, s2_recv)),
               "out": (acc_out, p1_out, oout_ref, _ChipReduce(l1_out, l2_out, t1_send, t1_recv, t2_send, t2_recv))}
        ag = _AllGather([dmod_g, small_g, gws_g], ag_send, ag_recv)
        last_kc = kc == n_kc - 1

        @pl.when((step == 0) & (kc == 0))
        def _():
            me = _my_block()
            small_g[me] = small_ref[...]
            gws_g[me] = gws_ref[...].astype(BF16)
            dmod_g[me] = dmod_ref[...]
            for a in range(3):
                ag.send_own(a)

        @pl.when((step == 4) & (kc == 0))
        def _():
            for a in range(3):
                ag.other_chips(a)

        def accumulate(acc, prod):
            @pl.when(kc == 0)
            def _():
                acc[...] = prod

            @pl.when(kc != 0)
            def _():
                acc[...] += prod

        @pl.when(is_out[step] == 1)
        def _():
            accumulate(acc_out, jnp.dot(yt_ref[...], dy_ref[kc], preferred_element_type=F32))

        @pl.when(is_out[step] == 0)
        def _():
            accumulate(acc_in, jnp.dot(ht_ref[kc], dz_ref[...], preferred_element_type=F32))

        for k, (kind, r, sibling) in enumerate(WGRAD_ROLES):
            @pl.when((step == k) & last_kc)
            def _(kind=kind, r=r, sibling=sibling):
                acc, p1, out, red = pay[kind]
                if sibling:
                    red.send_to_sibling(r, p1, acc[...])
                else:
                    chip_sum = red.combine(r, acc[...])
                    if r == 0:
                        out[...] = chip_sum

        @pl.when((step == n_steps - 1) & last_kc)
        def _():
            for kind in ("out", "in"):
                _, _, out, red = pay[kind]
                out[...] = red.finish(out[...])
            for a in range(3):
                ag.from_sibling(a)
            for a in range(3):
                ag.sent(a)
            tot_small = _sum_blocks(small_g)
            osmall_ref[...] = tot_small
            oloss_ref[...] = jnp.full(oloss_ref.shape, (0.5 / D) * jnp.sum(tot_small[ROW_LOSS:ROW_LOSS + 1, :]), F32)
            tot_gws = gws_g[0].astype(F32)
            for b in range(1, N_DEV):
                tot_gws = tot_gws + gws_g[b].astype(F32)
            ogws_ref[...] = tot_gws
            obada_ref[...] = _sum_blocks(dmod_g)
            for b in range(N_DEV):
                odmod_ref[b:b + 1, :] = dmod_g[b, 0:1, :]

    def kc_of(working, step, kc, hold_first):
        held = jnp.where(step < hold_first, 0, n_kc - 1)
        return jnp.where(working, kc, held)

    out_kc = lambda i, kc, ob, ib, io: kc_of(io[i] == 1, i, kc, 0)
    in_kc = lambda i, kc, ob, ib, io: kc_of(io[i] == 0, i, kc, first_in)
    sems = lambda n: [pltpu.SemaphoreType.DMA((n,)), pltpu.SemaphoreType.DMA((n,))]
    grid_spec = pltpu.PrefetchScalarGridSpec(
        num_scalar_prefetch=3, grid=(n_steps, n_kc),
        in_specs=[VM,
                  pl.BlockSpec((WGRAD_K, W_IN_BLK), lambda i, kc, ob, ib, io: (in_kc(i, kc, ob, ib, io), ib[i])),
                  pl.BlockSpec((W_OUT_BLK, WGRAD_K), lambda i, kc, ob, ib, io: (ob[i], out_kc(i, kc, ob, ib, io))),
                  VM,
                  VM, VM, VM],
        out_specs=(VM,) * 7,
        scratch_shapes=[pltpu.VMEM(blk_in, F32), pltpu.VMEM(blk_out, F32),
                        pltpu.VMEM(blk_in, BF16), pltpu.VMEM(blk_out, BF16),
                        pltpu.VMEM((4,) + blk_in, BF16), pltpu.VMEM((4,) + blk_out, BF16),
                        pltpu.VMEM((3,) + blk_in, BF16), pltpu.VMEM((3,) + blk_out, BF16),
                        pltpu.VMEM((N_DEV,) + small.shape, F32), pltpu.VMEM((N_DEV,) + gws.shape, BF16),
                        pltpu.VMEM((N_DEV,) + dmod.shape, F32)]
        + sems(4) + sems(3) + sems(4) + sems(3) + sems(21))
    return pl.pallas_call(
        body, name="wgrad_reduce", grid_spec=grid_spec,
        out_shape=(jax.ShapeDtypeStruct(blk_in, F32), jax.ShapeDtypeStruct(blk_out, F32),
                   jax.ShapeDtypeStruct(small.shape, F32), jax.ShapeDtypeStruct(gws.shape, F32),
                   jax.ShapeDtypeStruct((N_DEV, 3 * D), F32), jax.ShapeDtypeStruct((8, 3 * D), F32),
                   jax.ShapeDtypeStruct((8, 128), F32)),
        compiler_params=_params(2, vmem=WGRAD_VMEM_LIMIT),
    )(*_wgrad_schedule(), ht, dz, ycatt, dy.reshape(n_kc, WGRAD_K, D), small, gws, dmod)


def _adamw_math(w, g, m, v):
    m = ADAM_B1 * m + (1.0 - ADAM_B1) * g
    v = ADAM_B2 * v + (1.0 - ADAM_B2) * (g * g)
    m_hat = m / (1.0 - ADAM_B1 ** ADAM_STEP)
    v_hat = v / (1.0 - ADAM_B2 ** ADAM_STEP)
    delta = -ADAM_LR * (m_hat / (jnp.sqrt(v_hat) + ADAM_EPS) + ADAM_WD * w)
    return delta, m, v


VECTORS = ["norm_g", "conv_b", "conv_ln_g", "conv_ln_b", "sg_ln_g", "sg_ln_b", "final_g", "b_s"]
VECTOR_ROWS = [ROW_NORM_G, ROW_CONV_B, ROW_CLN_G, ROW_CLN_B, ROW_SLN_G, ROW_SLN_B, ROW_FINAL_G, ROW_B_S]
ADAM_STEPS = 4


def _adamw_all(me, g_w_in, g_w_out, g_w_s, small_sum, dmod_all, bada8, c_all, matrices, b_ada, conv_w, vectors):
    mat_shapes = [(D, W_IN_BLK), (D, W_ADA_BLK), (W_OUT_BLK, D), (HEADS * CHUNK, CHUNK)]
    mat_blocks = [(sh[0] // ADAM_STEPS, sh[1]) for sh in mat_shapes]
    n_small = 2 + len(VECTORS)

    def body(me_ref, gin_ref, gout_ref, gws_ref, taps_ref, small_ref, dmc_ref, bada_ref, c_ref, *refs):
        params = refs[:3 * (4 + n_small)]
        outs = refs[3 * (4 + n_small):-2]
        act, gada = refs[-2:]
        mat_out, gada_out, small_out = outs[:12], outs[12], outs[13:]
        i = pl.program_id(0)

        @pl.when(i == 0)
        def _():
            for b in range(N_DEV):
                cb = c_ref[b, 0:1, :]
                act[b:b + 1, :] = cb * _sigmoid(cb)
            gada[...] = lax.dot_general(act[...], dmc_ref[...], (((0,), (0,)), ((), ())), preferred_element_type=F32,
                                        precision=lax.Precision.HIGHEST)
            small_grads = [bada_ref[0:1, :], taps_ref[0:CONV_W, :]] + [small_ref[r:r + 1, :] for r in VECTOR_ROWS]
            for k, g in enumerate(small_grads):
                w_ref, m_ref, v_ref = params[3 * (4 + k):3 * (5 + k)]
                o = small_out[4 * k:4 * k + 4]
                o[0][...] = g
                o[1][...], o[2][...], o[3][...] = _adamw_math(w_ref[...], g, m_ref[...], v_ref[...])

        rows = pl.ds(pl.multiple_of(i * mat_blocks[1][0], mat_blocks[1][0]), mat_blocks[1][0])
        g_ada = gada[rows, :]
        gada_out[...] = g_ada
        for k, g in enumerate([gin_ref[...], g_ada, gout_ref[...], gws_ref[...]]):
            w_ref, m_ref, v_ref = params[3 * k:3 * k + 3]
            o = mat_out[3 * k:3 * k + 3]
            o[0][...], o[1][...], o[2][...] = _adamw_math(w_ref[...], g, m_ref[...], v_ref[...])

    rows_of = lambda blk: pl.BlockSpec(blk, lambda i, me_ref: (i, 0))
    mat_specs = [rows_of(b) for b in mat_blocks]
    grid_spec = pltpu.PrefetchScalarGridSpec(
        num_scalar_prefetch=1, grid=(ADAM_STEPS,),
        in_specs=[mat_specs[0], mat_specs[2], mat_specs[3],
                  pl.BlockSpec((CONV_ROWS, CONV_BLK), lambda i, me_ref: (0, me_ref[0])), VM,
                  pl.BlockSpec((N_DEV, W_ADA_BLK), lambda i, me_ref: (0, me_ref[0])), VM, VM]
        + [s for s in mat_specs for _ in range(3)] + [VM] * (3 * n_small),
        out_specs=tuple([s for s in mat_specs for _ in range(3)] + [mat_specs[1]] + [VM] * (4 * n_small)),
        scratch_shapes=[pltpu.VMEM((N_DEV, D), F32), pltpu.VMEM((D, W_ADA_BLK), F32)])
    small_shapes = [b_ada[0].shape, conv_w[0].shape] + [(1, D)] * len(VECTORS)
    out_shape = tuple([jax.ShapeDtypeStruct(sh, F32) for sh in mat_shapes for _ in range(3)]
                      + [jax.ShapeDtypeStruct(mat_shapes[1], F32)]
                      + [jax.ShapeDtypeStruct(sh, F32) for sh in small_shapes for _ in range(4)])
    flat = [a for group in matrices for a in group] + list(b_ada) + list(conv_w) + [a for group in vectors for a in group]
    return pl.pallas_call(body, name="adamw_all", grid_spec=grid_spec, out_shape=out_shape,
                          compiler_params=_params(1))(
        me, g_w_in, g_w_out, g_w_s, small_sum, small_sum, dmod_all, bada8, c_all, *flat)


def kernel(x, c, w_ada, b_ada, norm_g, w_in, conv_w, conv_b, conv_ln_g, conv_ln_b, sg_ln_g, sg_ln_b, w_s, b_s, w_out, final_g, loss_target, m_w_ada, m_b_ada, m_norm_g, m_w_in, m_conv_w, m_conv_b, m_conv_ln_g, m_conv_ln_b, m_sg_ln_g, m_sg_ln_b, m_w_s, m_b_s, m_w_out, m_final_g, v_w_ada, v_b_ada, v_norm_g, v_w_in, v_conv_w, v_conv_b, v_conv_ln_g, v_conv_ln_b, v_sg_ln_g, v_sg_ln_b, v_w_s, v_b_s, v_w_out, v_final_g):
    me = 4 * lax.axis_index("x") + 2 * lax.axis_index("y") + lax.axis_index("c")
    x2d, tgt2d = x[0], loss_target[0]
    row1 = lambda a: a.reshape(1, D)
    taps = lambda a: jnp.pad(a.reshape(CONV_W, CONV_BLK), ((0, CONV_ROWS - CONV_W), (0, 0)))

    z, ht, win_g, wout_g, cw_g, c_all, mod = _fwd_in_proj(
        x2d, jnp.broadcast_to(c, (8, D)), w_ada[0], b_ada, norm_g, w_in[0], w_out[0], taps(conv_w))
    ws_b = w_s[0].astype(BF16)
    wst_b = jnp.swapaxes(w_s[0], 1, 2).astype(BF16)
    bs_full = jnp.repeat(b_s[0].T, HEAD_DIM, axis=1)

    dx2, ycatt, dy, dcv, dz_rest, acc_a, gws, gbs = _mix_and_head(
        z, x2d, tgt2d, mod, cw_g, conv_b, conv_ln_g, conv_ln_b, sg_ln_g, sg_ln_b, row1(final_g), ws_b, wst_b, bs_full,
        wout_g.reshape(D_MIX, D))
    grad_x, dz, acc_b, gcw = _bwd_in_proj(z, dcv, dz_rest, x2d, dx2, mod, norm_g, cw_g, win_g)

    small = jnp.concatenate(
        [jnp.transpose(gcw, (1, 0, 2)).reshape(CONV_ROWS, D), acc_b[0:1], acc_a[0:1], gbs.T.reshape(1, D), acc_a[2:8],
         jnp.zeros((SMALL_ROWS - ROW_LOSS - 1, D), F32)], axis=0)
    dmod_row = jnp.concatenate([acc_b[1:2], acc_b[2:3], acc_a[1:2]], axis=1)
    g_w_in, g_w_out, small_sum, g_w_s, dmod_all, bada8, loss_tile = _wgrad_reduce(
        ht, dz, ycatt, dy, small, gws.reshape(HEADS * CHUNK, CHUNK), jnp.broadcast_to(dmod_row, (8, 3 * D)))

    given = dict(w_ada=(w_ada, m_w_ada, v_w_ada), b_ada=(b_ada, m_b_ada, v_b_ada), norm_g=(norm_g, m_norm_g, v_norm_g),
                 w_in=(w_in, m_w_in, v_w_in), conv_w=(conv_w, m_conv_w, v_conv_w), conv_b=(conv_b, m_conv_b, v_conv_b),
                 conv_ln_g=(conv_ln_g, m_conv_ln_g, v_conv_ln_g), conv_ln_b=(conv_ln_b, m_conv_ln_b, v_conv_ln_b),
                 sg_ln_g=(sg_ln_g, m_sg_ln_g, v_sg_ln_g), sg_ln_b=(sg_ln_b, m_sg_ln_b, v_sg_ln_b),
                 w_s=(w_s, m_w_s, v_w_s), b_s=(b_s, m_b_s, v_b_s), w_out=(w_out, m_w_out, v_w_out),
                 final_g=(final_g, m_final_g, v_final_g))
    as2d = lambda name, shape: tuple(a.reshape(shape) for a in given[name])
    res = _adamw_all(
        jnp.reshape(me, (1,)).astype(jnp.int32), g_w_in, g_w_out, g_w_s, small_sum, dmod_all, bada8, c_all,
        [as2d("w_in", (D, W_IN_BLK)), as2d("w_ada", (D, W_ADA_BLK)), as2d("w_out", (W_OUT_BLK, D)),
         as2d("w_s", (HEADS * CHUNK, CHUNK))],
        given["b_ada"], as2d("conv_w", (CONV_W, CONV_BLK)), [as2d(n, (1, D)) for n in VECTORS])
    out = {}
    for k, name in enumerate(["w_in", "w_ada", "w_out", "w_s"]):
        out[name] = [None] + list(res[3 * k:3 * k + 3])
    out["w_in"][0], out["w_ada"][0], out["w_out"][0], out["w_s"][0] = g_w_in, res[12], g_w_out, g_w_s
    for k, name in enumerate(["b_ada", "conv_w"] + VECTORS):
        out[name] = list(res[13 + 4 * k:17 + 4 * k])
    order = ["w_ada", "b_ada", "norm_g", "w_in", "conv_w", "conv_b", "conv_ln_g", "conv_ln_b", "sg_ln_g", "sg_ln_b",
             "w_s", "b_s", "w_out", "final_g"]
    outs = [loss_tile[0, 0], grad_x.reshape(x.shape)]
    for kind in range(4):
        outs += [out[n][kind].reshape(given[n][0].shape) for n in order]
    return tuple(outs)
```

```python
import functools

import jax
import jax.numpy as jnp
from jax import lax
from jax.experimental import pallas as pl
from jax.experimental.pallas import tpu as pltpu

F32 = jnp.float32
BF16 = jnp.bfloat16
MESH = pl.DeviceIdType.MESH

D = 1024
D_IN = 6 * D
D_MIX = 2 * D
N_DEV = 8
W_IN_BLK = D_IN // N_DEV
W_OUT_BLK = D_MIX // N_DEV
W_ADA_BLK = 3 * D // N_DEV
CONV_BLK = D // N_DEV
CONV_W = 31
CONV_HALF = CONV_W // 2
CONV_ROWS = 32
HALO = 16
CHUNK = 128
HEADS = 8
HEAD_DIM = 128
EPS = 1e-6
ROW_TILE = 256
VMEM_LIMIT = 56 * 1024 * 1024

ADAM_LR = 0.001
ADAM_B1 = 0.9
ADAM_B2 = 0.999
ADAM_EPS = 1e-08
ADAM_WD = 0.01
ADAM_STEP = 10

VM = pl.BlockSpec(memory_space=pltpu.VMEM)


def _params(grid_rank=0, vmem=VMEM_LIMIT):
    sem = ("arbitrary",) * grid_rank if grid_rank else None
    return pltpu.CompilerParams(dimension_semantics=sem, vmem_limit_bytes=vmem)


def _sigmoid(t):
    return jax.nn.sigmoid(t)


def _dsilu(t, sig):
    return sig * (1.0 + t * (1.0 - sig))


def _mesh_pos():
    return lax.axis_index("x"), lax.axis_index("y"), lax.axis_index("c")


def _neighbours():
    x, y, c = _mesh_pos()
    nb1 = (x + c * (1 - 2 * x), y + (1 - c) * (1 - 2 * y))
    nb2 = (x + (1 - c) * (1 - 2 * x), y + c * (1 - 2 * y))
    return (x, y, c), (x, y, 1 - c), [nb1, nb2, (1 - x, 1 - y)], c


def _sibling_slot(j):
    return j if j == 2 else 1 - j


class _AllGather:
    def __init__(self, bufs, send_sems, recv_sems):
        self.bufs, self.send_sems, self.recv_sems = bufs, send_sems, recv_sems
        self.me, self.sib, self.chips, self.c = _neighbours()

    def _copy(self, a, k, block, to):
        px, py, pc = block
        ref = self.bufs[a].at[4 * px + 2 * py + pc]
        return pltpu.make_async_remote_copy(
            src_ref=ref, dst_ref=ref, send_sem=self.send_sems.at[7 * a + k], recv_sem=self.recv_sems.at[7 * a + k],
            device_id=to, device_id_type=MESH)

    def _outgoing(self, a, k):
        if k == 0:
            return self._copy(a, 0, self.me, self.sib)
        if k <= 2:
            return self._copy(a, k, self.me, (*self.chips[k - 1], self.c))
        if k == 3:
            return self._copy(a, 3, (*self.chips[0], self.c), (*self.chips[1], self.c))
        return self._copy(a, k, (*self.chips[_sibling_slot(k - 4)], self.c), self.sib)

    def source(self, k):
        if k == 0:
            return self.sib
        return (*self.chips[(k - 1) % 3], self.c if k <= 3 else 1 - self.c)

    def block_index(self, k):
        px, py, pc = self.source(k)
        return 4 * px + 2 * py + pc

    def send_own(self, a):
        for k in range(3):
            self._outgoing(a, k).start()

    def arrived(self, a, k):
        self._copy(a, k, self.source(k), self.me).wait_recv()

    def relay(self, a):
        self._outgoing(a, 3).start()

    def pass_on(self, a, j):
        self._outgoing(a, 4 + _sibling_slot(j)).start()

    def other_chips(self, a):
        self.arrived(a, 1)
        self.relay(a)
        self.pass_on(a, 0)
        for j in (1, 2):
            self.arrived(a, 1 + j)
            self.pass_on(a, j)

    def from_sibling(self, a):
        for k in (0, 4, 5, 6):
            self.arrived(a, k)

    def sent(self, a):
        for k in range(7):
            self._outgoing(a, k).wait_send()


def _shard_index(chip_of, c, r, sibling):
    if sibling and r:
        r = 1 + _sibling_slot(r - 1)
    cx, cy = chip_of[r]
    return 4 * cx + 2 * cy + ((1 - c) if sibling else c)


class _ChipReduce:
    def __init__(self, l1, l2, s1_send, s1_recv, s2_send, s2_recv):
        self.l1, self.l2 = l1, l2
        self.s1_send, self.s1_recv, self.s2_send, self.s2_recv = s1_send, s1_recv, s2_send, s2_recv
        me, self.sib, chips, self.c = _neighbours()
        self.chip_of = [me[:2]] + chips

    def block(self, r, sibling):
        return _shard_index(self.chip_of, self.c, r, sibling)

    def to_sibling(self, r, src=None):
        return pltpu.make_async_remote_copy(
            src_ref=self.l1.at[r] if src is None else src, dst_ref=self.l1.at[r], send_sem=self.s1_send.at[r],
            recv_sem=self.s1_recv.at[r], device_id=self.sib, device_id_type=MESH)

    ORDER = (3, 1, 2, 0)

    def send_to_sibling(self, r, stage, value):
        k = self.ORDER.index(r)
        if k:
            self.to_sibling(self.ORDER[k - 1], stage).wait_send()
        stage[...] = value.astype(BF16)
        self.to_sibling(r, stage).start()

    def to_chip(self, r):
        target = self.chip_of[1] if r == 3 else self.chip_of[r]
        return pltpu.make_async_remote_copy(
            src_ref=self.l1.at[r], dst_ref=self.l2.at[r - 1], send_sem=self.s2_send.at[r - 1],
            recv_sem=self.s2_recv.at[r - 1], device_id=(*target, self.c), device_id_type=MESH)

    def combine(self, r, mine):
        self.to_sibling(r).wait_recv()
        both = mine + self.l1[r].astype(F32)
        if r == 0:
            return both
        if r == 2:
            self.to_chip(3).wait_recv()
            both = both + self.l2[2].astype(F32)
        self.l1[r] = both.astype(BF16)
        self.to_chip(r).start()
        return None

    def finish(self, own_chip_sum):
        total = own_chip_sum
        for r in (1, 2):
            self.to_chip(r).wait_recv()
            total = total + self.l2[r - 1].astype(F32)
        self.to_sibling(self.ORDER[-1]).wait_send()
        for r in (1, 2, 3):
            self.to_chip(r).wait_send()
        return total


def _my_block():
    x, y, c = _mesh_pos()
    return 4 * x + 2 * y + c


def _modulation(c_g, w_ref, b_ref, mod_ref, part, land, send_sems, recv_sems):
    x, y, c = _mesh_pos()
    me = 4 * x + 2 * y + c
    c_rows = c_g[...].reshape(N_DEV * 8, D)
    parts = jnp.dot(c_rows * _sigmoid(c_rows), w_ref[...], preferred_element_type=F32, precision=lax.Precision.HIGHEST)
    for b in range(N_DEV):
        part[b] = parts[8 * b:8 * b + 8]
    land[me] = part[me]

    def copy(b):
        return pltpu.make_async_remote_copy(
            src_ref=part.at[b], dst_ref=land.at[me], send_sem=send_sems.at[b], recv_sem=recv_sems.at[me],
            device_id=(b // 4, (b // 2) % 2, b % 2), device_id_type=MESH)

    def arrival(b):
        return pltpu.make_async_remote_copy(
            src_ref=part.at[b], dst_ref=land.at[b], send_sem=send_sems.at[b], recv_sem=recv_sems.at[b],
            device_id=(b // 4, (b // 2) % 2, b % 2), device_id_type=MESH)

    for b in range(N_DEV):
        @pl.when(b != me)
        def _():
            copy(b).start()
    for b in range(N_DEV):
        @pl.when(b != me)
        def _():
            arrival(b).wait_recv()
            copy(b).wait_send()
    for b in range(N_DEV):
        cols = slice(b * W_ADA_BLK, (b + 1) * W_ADA_BLK)
        mod_ref[:, cols] = land[b] + b_ref[:, cols]


def _rms_modulate(x, mod_ref):
    shift = mod_ref[0:1, 0:D]
    scale = mod_ref[0:1, D:2 * D]
    r = lax.rsqrt(jnp.mean(x * x, axis=-1, keepdims=True) + EPS)
    xn = x * r
    return xn, r, shift, scale


FWD_TILE = 512
FWD_Z_TILE = 1024


def _fwd_in_proj(x, c_rep, w_ada, b_ada, norm_g, w_in, w_out, conv_w):
    s = x.shape[0]
    t, tz = FWD_TILE, FWD_Z_TILE
    n_tiles = s // t
    C_PAY, WIN_PAY, WOUT_PAY, CW_PAY = 0, 1, 2, 3

    def body(x_hbm, c_ref, wada_ref, bada_ref, ng_ref, win_ref, wout_ref, cw_ref,
             z_hbm, ht_ref, win_g, wout_g, cw_g, c_g, mod_ref,
             h, xbuf, zbuf, part, land, ag_send, ag_recv, mod_send, mod_recv, x_sem, z_sem):
        me = _my_block()
        c_g[me] = c_ref[...]
        ag = _AllGather([c_g, win_g, wout_g, cw_g], ag_send, ag_recv)
        ag.send_own(C_PAY)

        def x_copy(i):
            return pltpu.make_async_copy(x_hbm.at[pl.ds(i * t, t), :], xbuf.at[i % 2], x_sem.at[i % 2])

        x_copy(0).start()
        win_g[me] = win_ref[...].astype(BF16)
        wout_g[me] = wout_ref[...].astype(BF16)
        cw_g[me] = cw_ref[...]

        ag.other_chips(C_PAY)
        ag.from_sibling(C_PAY)
        _modulation(c_g, wada_ref, bada_ref, mod_ref, part, land, mod_send, mod_recv)
        for a in (WIN_PAY, WOUT_PAY, CW_PAY):
            ag.send_own(a)

        for i in range(n_tiles):
            x_copy(i).wait()
            if i + 1 < n_tiles:
                x_copy(i + 1).start()
            xn, _, shift, scale = _rms_modulate(xbuf[i % 2], mod_ref)
            hh = xn * ng_ref[...] * (1.0 + scale) + shift
            h[i * t:(i + 1) * t, :] = hh.astype(BF16)
            ht_ref[(i * t) // WGRAD_K, :, (i * t) % WGRAD_K:(i * t) % WGRAD_K + t] = hh.T.astype(BF16)

        def z_copy(slot, row0, col0):
            return pltpu.make_async_copy(zbuf.at[slot], z_hbm.at[pl.ds(row0, tz), pl.ds(col0, W_IN_BLK)], z_sem.at[slot])

        done = [0]

        def z_block(blk):
            col0 = pl.multiple_of(blk * W_IN_BLK, 128)
            for i in range(s // tz):
                slot = done[0] % 2
                if done[0] >= 2:
                    z_copy(slot, 0, 0).wait()
                zbuf[slot] = jnp.dot(h[i * tz:(i + 1) * tz, :], win_g[blk], preferred_element_type=F32)
                z_copy(slot, i * tz, col0).start()
                done[0] += 1

        z_block(me)
        ag.arrived(WIN_PAY, 0)
        z_block(ag.block_index(0))
        ag.arrived(WIN_PAY, 1)
        ag.relay(WIN_PAY)
        ag.pass_on(WIN_PAY, 0)
        ag.arrived(WIN_PAY, 2)
        ag.pass_on(WIN_PAY, 1)
        for k in (1, 2, 4, 5):
            if k >= 4:
                ag.arrived(WIN_PAY, k)
            z_block(ag.block_index(k))
        ag.arrived(WIN_PAY, 3)
        ag.pass_on(WIN_PAY, 2)
        z_block(ag.block_index(3))
        ag.arrived(WIN_PAY, 6)
        z_block(ag.block_index(6))

        for a in (WOUT_PAY, CW_PAY):
            ag.other_chips(a)
        for a in (WOUT_PAY, CW_PAY):
            ag.from_sibling(a)
        for a in (C_PAY, WIN_PAY, WOUT_PAY, CW_PAY):
            ag.sent(a)
        z_copy(0, 0, 0).wait()
        z_copy(1, 0, 0).wait()

    any_spec = pl.BlockSpec(memory_space=pl.ANY)
    return pl.pallas_call(
        body, name="fwd_in_proj",
        out_shape=(jax.ShapeDtypeStruct((s, D_IN), F32), jax.ShapeDtypeStruct((s // WGRAD_K, D, WGRAD_K), BF16),
                   jax.ShapeDtypeStruct((N_DEV,) + w_in.shape, BF16), jax.ShapeDtypeStruct((N_DEV,) + w_out.shape, BF16),
                   jax.ShapeDtypeStruct((N_DEV,) + conv_w.shape, F32), jax.ShapeDtypeStruct((N_DEV,) + c_rep.shape, F32),
                   jax.ShapeDtypeStruct((8, 3 * D), F32)),
        in_specs=[any_spec] + [VM] * 7, out_specs=(any_spec,) + (VM,) * 6,
        scratch_shapes=[pltpu.VMEM((s, D), BF16), pltpu.VMEM((2, t, D), F32), pltpu.VMEM((2, tz, W_IN_BLK), F32),
                        pltpu.VMEM((N_DEV, 8, W_ADA_BLK), F32), pltpu.VMEM((N_DEV, 8, W_ADA_BLK), F32),
                        pltpu.SemaphoreType.DMA((28,)), pltpu.SemaphoreType.DMA((28,)),
                        pltpu.SemaphoreType.DMA((N_DEV,)), pltpu.SemaphoreType.DMA((N_DEV,)),
                        pltpu.SemaphoreType.DMA((2,)), pltpu.SemaphoreType.DMA((2,))],
        compiler_params=_params(),
    )(x, c_rep, w_ada, b_ada, norm_g, w_in, w_out, conv_w)


def _halo_specs(t, s, width):
    per = t // HALO
    last = s // HALO - 1
    prev = pl.BlockSpec((HALO, width), lambda i: (jnp.maximum(i * per - 1, 0), 0))
    nxt = pl.BlockSpec((HALO, width), lambda i: (jnp.minimum((i + 1) * per, last), 0))
    return prev, nxt


def _glu(ref):
    return ref[:, 0:D] * _sigmoid(ref[:, D:2 * D])


SUB = 8
HEAD_ROWS = 16
CONV_PHASE_ROWS = ROW_TILE + SUB


def _conv_taps(ext, qbuf, cols, tap, t):
    out = None
    for b in range(SUB):
        q = None
        for a in range((CONV_W - b + SUB - 1) // SUB):
            term = ext[SUB * a:SUB * a + t + SUB, cols] * tap(SUB * a + b)
            q = term if q is None else q + term
        qbuf[b] = q
        shifted = qbuf[b, pl.ds(b + 1, t), :]
        out = shifted if out is None else out + shifted
    return out


def _layer_norm_stats(v):
    mu = jnp.mean(v, axis=-1, keepdims=True)
    cen = v - mu
    rstd = lax.rsqrt(jnp.mean(cen * cen, axis=-1, keepdims=True) + EPS)
    return cen * rstd, rstd


def _layer_norm_bwd(dy_hat, hat, rstd):
    m1 = jnp.mean(dy_hat, axis=-1, keepdims=True)
    m2 = jnp.mean(dy_hat * hat, axis=-1, keepdims=True)
    return rstd * (dy_hat - m1 - hat * m2)


def _colsum(v):
    return jnp.sum(v, axis=0, keepdims=True)


def _mix_and_head(z, x, tgt, mod, conv_w_g, conv_b, cln_g, cln_b, sln_g, sln_b, final_g, ws_b, wst_b, bs_full, wout):
    s = x.shape[0]
    t = ROW_TILE
    n_chunks = t // CHUNK
    n_steps = s // t
    prev_spec, next_spec = _halo_specs(t, s, 2 * D)

    def body(z_ref, zp_ref, zn_ref, x_ref, tgt_ref, mod_ref, cw_ref, cb_ref, clg_ref, clb_ref, slg_ref, slb_ref, fg_ref,
             ws_ref, wst_ref, bs_ref, wout_ref,
             dx2_ref, ycatt_ref, dy_ref, dcv_ref, dzr_ref, acc_ref, gws_ref, gbs_ref,
             gext, cv, vs, dvn, ycat, gbs_acc, qbuf):
        i = pl.program_id(0)

        @pl.when(i == 0)
        def _():
            acc_ref[...] = jnp.zeros_like(acc_ref)
            gws_ref[...] = jnp.zeros_like(gws_ref)
            gbs_acc[...] = jnp.zeros_like(gbs_acc)

        gext[0:HALO, :] = jnp.where(i > 0, _glu(zp_ref), 0.0)
        gext[HALO:HALO + t, :] = _glu(z_ref)
        gext[HALO + t:2 * HALO + t, :] = jnp.where(i < n_steps - 1, _glu(zn_ref), 0.0)
        for blk in range(N_DEV):
            cols = slice(blk * CONV_BLK, (blk + 1) * CONV_BLK)
            cv[:, cols] = _conv_taps(gext, qbuf, cols, lambda k, blk=blk: cw_ref[blk, k:k + 1, :], t) + cb_ref[:, cols]
        ln_hat, ln_rstd = _layer_norm_stats(cv[...])
        ln_a = ln_hat * clg_ref[...] + clb_ref[...]
        sig_ln = _sigmoid(ln_a)
        sa = ln_a * sig_ln
        a_gate = z_ref[:, 2 * D:3 * D]
        sig_ag = _sigmoid(a_gate)
        s_gate = a_gate * sig_ag
        ya = sa * s_gate

        v_hat, v_rstd = _layer_norm_stats(z_ref[:, 4 * D:5 * D])
        vn = v_hat * slg_ref[...] + slb_ref[...]
        vnb = vn.astype(BF16)
        for n in range(n_chunks):
            rows = slice(n * CHUNK, (n + 1) * CHUNK)
            for h in range(HEADS):
                cols = slice(h * HEAD_DIM, (h + 1) * HEAD_DIM)
                vs[rows, cols] = jnp.dot(ws_ref[h], vnb[rows, cols], preferred_element_type=F32) + bs_ref[:, cols]
        u = z_ref[:, 3 * D:4 * D]
        b_gate = z_ref[:, 5 * D:6 * D]
        sig_bg = _sigmoid(b_gate)
        s_bg = b_gate * sig_bg
        vsv = vs[...]
        yb = u * vsv * s_bg

        ycat[:, 0:D] = ya.astype(BF16)
        ycat[:, D:2 * D] = yb.astype(BF16)
        ycatt_ref[0:D, :] = ya.T.astype(BF16)
        ycatt_ref[D:2 * D, :] = yb.T.astype(BF16)
        y = jnp.dot(ycat[...], wout_ref[...], preferred_element_type=F32)
        gate = mod_ref[0:1, 2 * D:3 * D]
        fg = fg_ref[...]
        dvn[...] = y
        sums = [jnp.zeros((SUB, D), F32)] * 3
        fold = lambda v: v[0:SUB] + v[SUB:2 * SUB]
        for rc in range(t // HEAD_ROWS):
            rows = slice(rc * HEAD_ROWS, (rc + 1) * HEAD_ROWS)
            yc = dvn[rows, :]
            x2 = x_ref[rows, :] + gate * yc
            r2 = lax.rsqrt(jnp.mean(x2 * x2, axis=-1, keepdims=True) + EPS)
            x2n = x2 * r2
            diff = x2n * fg - tgt_ref[rows, :]
            dout = diff * (1.0 / D)
            dx2n = dout * fg
            dx2 = r2 * (dx2n - x2n * jnp.mean(dx2n * x2n, axis=-1, keepdims=True))
            dx2_ref[rows, :] = dx2
            dy_ref[rows, :] = (dx2 * gate).astype(BF16)
            sums = [sums[0] + fold(diff * diff), sums[1] + fold(dout * x2n), sums[2] + fold(dx2 * yc)]
        acc_ref[7:8, :] += _colsum(sums[0])
        acc_ref[0:1, :] += _colsum(sums[1])
        acc_ref[1:2, :] += _colsum(sums[2])
        dyb16 = dy_ref[...]
        dycat = lax.dot_general(dyb16, wout_ref[...], (((1,), (1,)), ((), ())), preferred_element_type=F32)
        dya = dycat[:, 0:D]
        dyb = dycat[:, D:2 * D]

        du = dyb * vsv * s_bg
        dvs = dyb * u * s_bg
        dbg = dyb * u * vsv * _dsilu(b_gate, sig_bg)
        dvsb = dvs.astype(BF16)
        gbs = gbs_acc[...]
        for n in range(n_chunks):
            rows = slice(n * CHUNK, (n + 1) * CHUNK)
            gbs = gbs + dvs[rows, :]
            for h in range(HEADS):
                cols = slice(h * HEAD_DIM, (h + 1) * HEAD_DIM)
                gws_ref[h] += lax.dot_general(dvsb[rows, cols], vnb[rows, cols], (((1,), (1,)), ((), ())),
                                              preferred_element_type=F32)
                dvn[rows, cols] = jnp.dot(wst_ref[h], dvsb[rows, cols], preferred_element_type=F32)
        gbs_acc[...] = gbs

        @pl.when(i == n_steps - 1)
        def _():
            for h in range(HEADS):
                gbs_ref[:, h:h + 1] = jnp.sum(gbs_acc[:, h * HEAD_DIM:(h + 1) * HEAD_DIM], axis=1, keepdims=True)

        dvnv = dvn[...]
        acc_ref[5:6, :] += _colsum(dvnv * v_hat)
        acc_ref[6:7, :] += _colsum(dvnv)
        dv = _layer_norm_bwd(dvnv * slg_ref[...], v_hat, v_rstd)

        dsa = dya * s_gate
        dagate = dya * sa * _dsilu(a_gate, sig_ag)
        dln = dsa * _dsilu(ln_a, sig_ln)
        acc_ref[3:4, :] += _colsum(dln * ln_hat)
        acc_ref[4:5, :] += _colsum(dln)
        dcv = _layer_norm_bwd(dln * clg_ref[...], ln_hat, ln_rstd)
        acc_ref[2:3, :] += _colsum(dcv)
        dcv_ref[...] = dcv

        dzr_ref[:, 0:D] = dagate.astype(BF16)
        dzr_ref[:, D:2 * D] = du.astype(BF16)
        dzr_ref[:, 2 * D:3 * D] = dv.astype(BF16)
        dzr_ref[:, 3 * D:4 * D] = dbg.astype(BF16)

    row = lambda w: pl.BlockSpec((t, w), lambda i: (i, 0))
    const = lambda shape: pl.BlockSpec(shape, lambda i: (0,) * len(shape))
    return pl.pallas_call(
        body, name="mix_and_head", grid=(n_steps,),
        out_shape=(jax.ShapeDtypeStruct((s, D), F32),
                   jax.ShapeDtypeStruct((D_MIX, s), BF16),
                   jax.ShapeDtypeStruct((s, D), BF16),
                   jax.ShapeDtypeStruct((s, D), F32),
                   jax.ShapeDtypeStruct((s, 4 * D), BF16),
                   jax.ShapeDtypeStruct((8, D), F32),
                   jax.ShapeDtypeStruct((HEADS, CHUNK, CHUNK), F32),
                   jax.ShapeDtypeStruct((CHUNK, HEADS), F32)),
        in_specs=[row(D_IN), prev_spec, next_spec, row(D), row(D)] + [VM] * 12,
        out_specs=(row(D), pl.BlockSpec((D_MIX, t), lambda i: (0, i)), row(D), row(D), row(4 * D),
                   const((8, D)), const((HEADS, CHUNK, CHUNK)), const((CHUNK, HEADS))),
        scratch_shapes=[pltpu.VMEM((t + 2 * HALO, D), F32), pltpu.VMEM((t, D), F32), pltpu.VMEM((t, D), F32),
                        pltpu.VMEM((t, D), F32), pltpu.VMEM((t, D_MIX), BF16), pltpu.VMEM((CHUNK, D), F32),
                        pltpu.VMEM((SUB, CONV_PHASE_ROWS, CONV_BLK), F32)],
        compiler_params=_params(1),
    )(z, z, z, x, tgt, mod, conv_w_g, conv_b, cln_g, cln_b, sln_g, sln_b, final_g, ws_b, wst_b, bs_full, wout)


def _bwd_in_proj(z, dcv, dz_rest, x, dx2, mod, norm_g, conv_w_g, win_g):
    s = x.shape[0]
    t = ROW_TILE
    n_steps = s // t
    zp_spec, zn_spec = _halo_specs(t, s, 2 * D)
    dp_spec, dn_spec = _halo_specs(t, s, D)

    def body(z_ref, zp_ref, zn_ref, dcv_ref, dcvp_ref, dcvn_ref, dzr_ref, x_ref, dx2_ref, mod_ref, ng_ref, cw_ref, w_ref,
             gx_ref, dz_ref, acc_ref, gcw_ref,
             gext, dext, dg, taps, dpad, qbuf):
        i = pl.program_id(0)

        @pl.when(i == 0)
        def _():
            acc_ref[...] = jnp.zeros_like(acc_ref)
            gcw_ref[...] = jnp.zeros_like(gcw_ref)

        not_first = i > 0
        not_last = i < n_steps - 1
        gext[0:HALO, :] = jnp.where(not_first, _glu(zp_ref), 0.0)
        gext[HALO:HALO + t, :] = _glu(z_ref)
        gext[HALO + t:2 * HALO + t, :] = jnp.where(not_last, _glu(zn_ref), 0.0)
        dext[0:HALO, :] = jnp.where(not_first, dcvp_ref[...], 0.0)
        dext[HALO:HALO + t, :] = dcv_ref[...]
        dext[HALO + t:2 * HALO + t, :] = jnp.where(not_last, dcvn_ref[...], 0.0)

        taps[...] = jnp.zeros_like(taps)
        dpad[0:SUB, :] = jnp.zeros((SUB, D), F32)
        dpad[SUB:SUB + t, :] = dcv_ref[...]
        dpad[SUB + t:2 * SUB + t, :] = jnp.zeros((SUB, D), F32)
        for blk in range(N_DEV):
            cols = slice(blk * CONV_BLK, (blk + 1) * CONV_BLK)
            dg[:, cols] = _conv_taps(dext, qbuf, cols, lambda k, blk=blk: cw_ref[blk, CONV_W - 1 - k:CONV_W - k, :], t)
            for b in range(SUB):
                dshift = dpad[pl.ds(SUB - 1 - b, t + SUB), cols]
                for a in range((CONV_W - b + SUB - 1) // SUB):
                    k = SUB * a + b
                    taps[k:k + 1, :] = _colsum(gext[SUB * a:SUB * a + t + SUB, cols] * dshift)
            gcw_ref[blk] += taps[...]

        a_val = z_ref[:, 0:D]
        sig = _sigmoid(z_ref[:, D:2 * D])
        dgv = dg[...]
        dz_ref[:, 0:D] = (dgv * sig).astype(BF16)
        dz_ref[:, D:2 * D] = (dgv * a_val * sig * (1.0 - sig)).astype(BF16)
        dz_ref[:, 2 * D:6 * D] = dzr_ref[...]

        dh = None
        for j in range(N_DEV):
            term = lax.dot_general(dz_ref[:, j * W_IN_BLK:(j + 1) * W_IN_BLK], w_ref[j], (((1,), (1,)), ((), ())),
                                   preferred_element_type=F32)
            dh = term if dh is None else dh + term

        xn, r, _, scale = _rms_modulate(x_ref[...], mod_ref)
        ng = ng_ref[...]
        one_scale = 1.0 + scale
        dh_xn = dh * xn
        acc_ref[0:1, :] += _colsum(dh_xn * one_scale)
        acc_ref[1:2, :] += _colsum(dh)
        acc_ref[2:3, :] += _colsum(dh_xn * ng)
        dxn = dh * (ng * one_scale)
        gx_ref[...] = dx2_ref[...] + r * (dxn - xn * jnp.mean(dxn * xn, axis=-1, keepdims=True))

    row = lambda w: pl.BlockSpec((t, w), lambda i: (i, 0))
    const = lambda shape: pl.BlockSpec(shape, lambda i: (0,) * len(shape))
    return pl.pallas_call(
        body, name="bwd_in_proj", grid=(n_steps,),
        out_shape=(jax.ShapeDtypeStruct((s, D), F32), jax.ShapeDtypeStruct((s, D_IN), BF16),
                   jax.ShapeDtypeStruct((8, D), F32), jax.ShapeDtypeStruct((N_DEV, CONV_ROWS, CONV_BLK), F32)),
        in_specs=[pl.BlockSpec((t, 2 * D), lambda i: (i, 0)), zp_spec, zn_spec, row(D), dp_spec, dn_spec, row(4 * D),
                  row(D), row(D), VM, VM, VM, VM],
        out_specs=(row(D), row(D_IN), const((8, D)), const((N_DEV, CONV_ROWS, CONV_BLK))),
        scratch_shapes=[pltpu.VMEM((t + 2 * HALO, D), F32), pltpu.VMEM((t + 2 * HALO, D), F32), pltpu.VMEM((t, D), F32),
                        pltpu.VMEM((CONV_ROWS, CONV_BLK), F32), pltpu.VMEM((t + 2 * SUB, D), F32),
                        pltpu.VMEM((SUB, CONV_PHASE_ROWS, CONV_BLK), F32)],
        compiler_params=_params(1),
    )(z, z, z, dcv, dcv, dcv, dz_rest, x, dx2, mod, norm_g, conv_w_g, win_g)


WGRAD_K = 2048
WGRAD_VMEM_LIMIT = 62 * 1024 * 1024
_OTHER_CHIPS = [(3, True), (3, False), (1, True), (1, False), (2, True), (2, False)]
_OWN_CHIP = [(0, True), (0, False)]
WGRAD_ROLES = ([("out",) + e for e in _OTHER_CHIPS] + [("in",) + e for e in _OTHER_CHIPS]
               + [("out",) + e for e in _OWN_CHIP] + [("in",) + e for e in _OWN_CHIP])


def _wgrad_schedule():
    me, _, chips, c = _neighbours()
    chip_of = [me[:2]] + chips
    blk = lambda r, sibling: _shard_index(chip_of, c, r, sibling)
    out_blk, in_blk, is_out = [], [], []
    last = {"out": blk(*_OTHER_CHIPS[0]), "in": blk(*_OTHER_CHIPS[0])}
    for kind, r, sibling in WGRAD_ROLES:
        last[kind] = blk(r, sibling)
        out_blk.append(last["out"])
        in_blk.append(last["in"])
        is_out.append(1 if kind == "out" else 0)
    as_vec = lambda v: jnp.stack([jnp.asarray(e, jnp.int32) for e in v])
    return as_vec(out_blk), as_vec(in_blk), as_vec(is_out)


def _sum_blocks(gathered):
    total = gathered[0]
    for b in range(1, N_DEV):
        total = total + gathered[b]
    return total


SMALL_ROWS = 48
ROW_NORM_G, ROW_FINAL_G, ROW_B_S, ROW_CONV_B, ROW_CLN_G, ROW_CLN_B, ROW_SLN_G, ROW_SLN_B, ROW_LOSS = range(32, 41)


def _wgrad_reduce(ht, dz, ycatt, dy, small, gws, dmod):
    s = dz.shape[0]
    n_kc = s // WGRAD_K
    n_steps = len(WGRAD_ROLES)
    first_in = [k for k, role in enumerate(WGRAD_ROLES) if role[0] == "in"][0]
    blk_in, blk_out = (D, W_IN_BLK), (W_OUT_BLK, D)

    def body(out_blk, in_blk, is_out, ht_ref, dz_ref, yt_ref, dy_ref, small_ref, gws_ref, dmod_ref,
             oin_ref, oout_ref, osmall_ref, ogws_ref, odmod_ref, obada_ref, oloss_ref,
             acc_in, acc_out, p1_in, p1_out, l1_in, l1_out, l2_in, l2_out, small_g, gws_g, dmod_g,
             s1_send, s1_recv, s2_send, s2_recv, t1_send, t1_recv, t2_send, t2_recv, ag_send, ag_recv):
        step, kc = pl.program_id(0), pl.program_id(1)
        pay = {"in": (acc_in, p1_in, oin_ref, _ChipReduce(l1_in, l2_in, s1_send, s1_recv, s2_send, s2_recv)),
               "out": (acc_out, p1_out, oout_ref, _ChipReduce(l1_out, l2_out, t1_send, t1_recv, t2_send, t2_recv))}
        ag = _AllGather([dmod_g, small_g, gws_g], ag_send, ag_recv)
        last_kc = kc == n_kc - 1

        @pl.when((step == 0) & (kc == 0))
        def _():
            me = _my_block()
            small_g[me] = small_ref[...]
            gws_g[me] = gws_ref[...].astype(BF16)
            dmod_g[me] = dmod_ref[...]
            for a in range(3):
                ag.send_own(a)

        @pl.when((step == 4) & (kc == 0))
        def _():
            for a in range(3):
                ag.other_chips(a)

        def accumulate(acc, prod):
            @pl.when(kc == 0)
            def _():
                acc[...] = prod

            @pl.when(kc != 0)
            def _():
                acc[...] += prod

        @pl.when(is_out[step] == 1)
        def _():
            accumulate(acc_out, jnp.dot(yt_ref[...], dy_ref[kc], preferred_element_type=F32))

        @pl.when(is_out[step] == 0)
        def _():
            accumulate(acc_in, jnp.dot(ht_ref[kc], dz_ref[...], preferred_element_type=F32))

        for k, (kind, r, sibling) in enumerate(WGRAD_ROLES):
            @pl.when((step == k) & last_kc)
            def _(kind=kind, r=r, sibling=sibling):
                acc, p1, out, red = pay[kind]
                if sibling:
                    red.send_to_sibling(r, p1, acc[...])
                else:
                    chip_sum = red.combine(r, acc[...])
                    if r == 0:
                        out[...] = chip_sum

        @pl.when((step == n_steps - 1) & last_kc)
        def _():
            for kind in ("out", "in"):
                _, _, out, red = pay[kind]
                out[...] = red.finish(out[...])
            for a in range(3):
                ag.from_sibling(a)
            for a in range(3):
                ag.sent(a)
            tot_small = _sum_blocks(small_g)
            osmall_ref[...] = tot_small
            oloss_ref[...] = jnp.full(oloss_ref.shape, (0.5 / D) * jnp.sum(tot_small[ROW_LOSS:ROW_LOSS + 1, :]), F32)
            tot_gws = gws_g[0].astype(F32)
            for b in range(1, N_DEV):
                tot_gws = tot_gws + gws_g[b].astype(F32)
            ogws_ref[...] = tot_gws
            obada_ref[...] = _sum_blocks(dmod_g)
            for b in range(N_DEV):
                odmod_ref[b:b + 1, :] = dmod_g[b, 0:1, :]

    def kc_of(working, step, kc, hold_first):
        held = jnp.where(step < hold_first, 0, n_kc - 1)
        return jnp.where(working, kc, held)

    out_kc = lambda i, kc, ob, ib, io: kc_of(io[i] == 1, i, kc, 0)
    in_kc = lambda i, kc, ob, ib, io: kc_of(io[i] == 0, i, kc, first_in)
    sems = lambda n: [pltpu.SemaphoreType.DMA((n,)), pltpu.SemaphoreType.DMA((n,))]
    grid_spec = pltpu.PrefetchScalarGridSpec(
        num_scalar_prefetch=3, grid=(n_steps, n_kc),
        in_specs=[VM,
                  pl.BlockSpec((WGRAD_K, W_IN_BLK), lambda i, kc, ob, ib, io: (in_kc(i, kc, ob, ib, io), ib[i])),
                  pl.BlockSpec((W_OUT_BLK, WGRAD_K), lambda i, kc, ob, ib, io: (ob[i], out_kc(i, kc, ob, ib, io))),
                  VM,
                  VM, VM, VM],
        out_specs=(VM,) * 7,
        scratch_shapes=[pltpu.VMEM(blk_in, F32), pltpu.VMEM(blk_out, F32),
                        pltpu.VMEM(blk_in, BF16), pltpu.VMEM(blk_out, BF16),
                        pltpu.VMEM((4,) + blk_in, BF16), pltpu.VMEM((4,) + blk_out, BF16),
                        pltpu.VMEM((3,) + blk_in, BF16), pltpu.VMEM((3,) + blk_out, BF16),
                        pltpu.VMEM((N_DEV,) + small.shape, F32), pltpu.VMEM((N_DEV,) + gws.shape, BF16),
                        pltpu.VMEM((N_DEV,) + dmod.shape, F32)]
        + sems(4) + sems(3) + sems(4) + sems(3) + sems(21))
    return pl.pallas_call(
        body, name="wgrad_reduce", grid_spec=grid_spec,
        out_shape=(jax.ShapeDtypeStruct(blk_in, F32), jax.ShapeDtypeStruct(blk_out, F32),
                   jax.ShapeDtypeStruct(small.shape, F32), jax.ShapeDtypeStruct(gws.shape, F32),
                   jax.ShapeDtypeStruct((N_DEV, 3 * D), F32), jax.ShapeDtypeStruct((8, 3 * D), F32),
                   jax.ShapeDtypeStruct((8, 128), F32)),
        compiler_params=_params(2, vmem=WGRAD_VMEM_LIMIT),
    )(*_wgrad_schedule(), ht, dz, ycatt, dy.reshape(n_kc, WGRAD_K, D), small, gws, dmod)


def _adamw_math(w, g, m, v):
    m = ADAM_B1 * m + (1.0 - ADAM_B1) * g
    v = ADAM_B2 * v + (1.0 - ADAM_B2) * (g * g)
    m_hat = m / (1.0 - ADAM_B1 ** ADAM_STEP)
    v_hat = v / (1.0 - ADAM_B2 ** ADAM_STEP)
    delta = -ADAM_LR * (m_hat / (jnp.sqrt(v_hat) + ADAM_EPS) + ADAM_WD * w)
    return delta, m, v


VECTORS = ["norm_g", "conv_b", "conv_ln_g", "conv_ln_b", "sg_ln_g", "sg_ln_b", "final_g", "b_s"]
VECTOR_ROWS = [ROW_NORM_G, ROW_CONV_B, ROW_CLN_G, ROW_CLN_B, ROW_SLN_G, ROW_SLN_B, ROW_FINAL_G, ROW_B_S]
ADAM_STEPS = 4


def _adamw_all(me, g_w_in, g_w_out, g_w_s, small_sum, dmod_all, bada8, c_all, matrices, b_ada, conv_w, vectors):
    mat_shapes = [(D, W_IN_BLK), (D, W_ADA_BLK), (W_OUT_BLK, D), (HEADS * CHUNK, CHUNK)]
    mat_blocks = [(sh[0] // ADAM_STEPS, sh[1]) for sh in mat_shapes]
    n_small = 2 + len(VECTORS)

    def body(me_ref, gin_ref, gout_ref, gws_ref, taps_ref, small_ref, dmc_ref, bada_ref, c_ref, *refs):
        params = refs[:3 * (4 + n_small)]
        outs = refs[3 * (4 + n_small):-2]
        act, gada = refs[-2:]
        mat_out, gada_out, small_out = outs[:12], outs[12], outs[13:]
        i = pl.program_id(0)

        @pl.when(i == 0)
        def _():
            for b in range(N_DEV):
                cb = c_ref[b, 0:1, :]
                act[b:b + 1, :] = cb * _sigmoid(cb)
            gada[...] = lax.dot_general(act[...], dmc_ref[...], (((0,), (0,)), ((), ())), preferred_element_type=F32,
                                        precision=lax.Precision.HIGHEST)
            small_grads = [bada_ref[0:1, :], taps_ref[0:CONV_W, :]] + [small_ref[r:r + 1, :] for r in VECTOR_ROWS]
            for k, g in enumerate(small_grads):
                w_ref, m_ref, v_ref = params[3 * (4 + k):3 * (5 + k)]
                o = small_out[4 * k:4 * k + 4]
                o[0][...] = g
                o[1][...], o[2][...], o[3][...] = _adamw_math(w_ref[...], g, m_ref[...], v_ref[...])

        rows = pl.ds(pl.multiple_of(i * mat_blocks[1][0], mat_blocks[1][0]), mat_blocks[1][0])
        g_ada = gada[rows, :]
        gada_out[...] = g_ada
        for k, g in enumerate([gin_ref[...], g_ada, gout_ref[...], gws_ref[...]]):
            w_ref, m_ref, v_ref = params[3 * k:3 * k + 3]
            o = mat_out[3 * k:3 * k + 3]
            o[0][...], o[1][...], o[2][...] = _adamw_math(w_ref[...], g, m_ref[...], v_ref[...])

    rows_of = lambda blk: pl.BlockSpec(blk, lambda i, me_ref: (i, 0))
    mat_specs = [rows_of(b) for b in mat_blocks]
    grid_spec = pltpu.PrefetchScalarGridSpec(
        num_scalar_prefetch=1, grid=(ADAM_STEPS,),
        in_specs=[mat_specs[0], mat_specs[2], mat_specs[3],
                  pl.BlockSpec((CONV_ROWS, CONV_BLK), lambda i, me_ref: (0, me_ref[0])), VM,
                  pl.BlockSpec((N_DEV, W_ADA_BLK), lambda i, me_ref: (0, me_ref[0])), VM, VM]
        + [s for s in mat_specs for _ in range(3)] + [VM] * (3 * n_small),
        out_specs=tuple([s for s in mat_specs for _ in range(3)] + [mat_specs[1]] + [VM] * (4 * n_small)),
        scratch_shapes=[pltpu.VMEM((N_DEV, D), F32), pltpu.VMEM((D, W_ADA_BLK), F32)])
    small_shapes = [b_ada[0].shape, conv_w[0].shape] + [(1, D)] * len(VECTORS)
    out_shape = tuple([jax.ShapeDtypeStruct(sh, F32) for sh in mat_shapes for _ in range(3)]
                      + [jax.ShapeDtypeStruct(mat_shapes[1], F32)]
                      + [jax.ShapeDtypeStruct(sh, F32) for sh in small_shapes for _ in range(4)])
    flat = [a for group in matrices for a in group] + list(b_ada) + list(conv_w) + [a for group in vectors for a in group]
    return pl.pallas_call(body, name="adamw_all", grid_spec=grid_spec, out_shape=out_shape,
                          compiler_params=_params(1))(
        me, g_w_in, g_w_out, g_w_s, small_sum, small_sum, dmod_all, bada8, c_all, *flat)


def kernel(x, c, w_ada, b_ada, norm_g, w_in, conv_w, conv_b, conv_ln_g, conv_ln_b, sg_ln_g, sg_ln_b, w_s, b_s, w_out, final_g, loss_target, m_w_ada, m_b_ada, m_norm_g, m_w_in, m_conv_w, m_conv_b, m_conv_ln_g, m_conv_ln_b, m_sg_ln_g, m_sg_ln_b, m_w_s, m_b_s, m_w_out, m_final_g, v_w_ada, v_b_ada, v_norm_g, v_w_in, v_conv_w, v_conv_b, v_conv_ln_g, v_conv_ln_b, v_sg_ln_g, v_sg_ln_b, v_w_s, v_b_s, v_w_out, v_final_g):
    me = 4 * lax.axis_index("x") + 2 * lax.axis_index("y") + lax.axis_index("c")
    x2d, tgt2d = x[0], loss_target[0]
    row1 = lambda a: a.reshape(1, D)
    taps = lambda a: jnp.pad(a.reshape(CONV_W, CONV_BLK), ((0, CONV_ROWS - CONV_W), (0, 0)))

    z, ht, win_g, wout_g, cw_g, c_all, mod = _fwd_in_proj(
        x2d, jnp.broadcast_to(c, (8, D)), w_ada[0], b_ada, norm_g, w_in[0], w_out[0], taps(conv_w))
    ws_b = w_s[0].astype(BF16)
    wst_b = jnp.swapaxes(w_s[0], 1, 2).astype(BF16)
    bs_full = jnp.repeat(b_s[0].T, HEAD_DIM, axis=1)

    dx2, ycatt, dy, dcv, dz_rest, acc_a, gws, gbs = _mix_and_head(
        z, x2d, tgt2d, mod, cw_g, conv_b, conv_ln_g, conv_ln_b, sg_ln_g, sg_ln_b, row1(final_g), ws_b, wst_b, bs_full,
        wout_g.reshape(D_MIX, D))
    grad_x, dz, acc_b, gcw = _bwd_in_proj(z, dcv, dz_rest, x2d, dx2, mod, norm_g, cw_g, win_g)

    small = jnp.concatenate(
        [jnp.transpose(gcw, (1, 0, 2)).reshape(CONV_ROWS, D), acc_b[0:1], acc_a[0:1], gbs.T.reshape(1, D), acc_a[2:8],
         jnp.zeros((SMALL_ROWS - ROW_LOSS - 1, D), F32)], axis=0)
    dmod_row = jnp.concatenate([acc_b[1:2], acc_b[2:3], acc_a[1:2]], axis=1)
    g_w_in, g_w_out, small_sum, g_w_s, dmod_all, bada8, loss_tile = _wgrad_reduce(
        ht, dz, ycatt, dy, small, gws.reshape(HEADS * CHUNK, CHUNK), jnp.broadcast_to(dmod_row, (8, 3 * D)))

    given = dict(w_ada=(w_ada, m_w_ada, v_w_ada), b_ada=(b_ada, m_b_ada, v_b_ada), norm_g=(norm_g, m_norm_g, v_norm_g),
                 w_in=(w_in, m_w_in, v_w_in), conv_w=(conv_w, m_conv_w, v_conv_w), conv_b=(conv_b, m_conv_b, v_conv_b),
                 conv_ln_g=(conv_ln_g, m_conv_ln_g, v_conv_ln_g), conv_ln_b=(conv_ln_b, m_conv_ln_b, v_conv_ln_b),
                 sg_ln_g=(sg_ln_g, m_sg_ln_g, v_sg_ln_g), sg_ln_b=(sg_ln_b, m_sg_ln_b, v_sg_ln_b),
                 w_s=(w_s, m_w_s, v_w_s), b_s=(b_s, m_b_s, v_b_s), w_out=(w_out, m_w_out, v_w_out),
                 final_g=(final_g, m_final_g, v_final_g))
    as2d = lambda name, shape: tuple(a.reshape(shape) for a in given[name])
    res = _adamw_all(
        jnp.reshape(me, (1,)).astype(jnp.int32), g_w_in, g_w_out, g_w_s, small_sum, dmod_all, bada8, c_all,
        [as2d("w_in", (D, W_IN_BLK)), as2d("w_ada", (D, W_ADA_BLK)), as2d("w_out", (W_OUT_BLK, D)),
         as2d("w_s", (HEADS * CHUNK, CHUNK))],
        given["b_ada"], as2d("conv_w", (CONV_W, CONV_BLK)), [as2d(n, (1, D)) for n in VECTORS])
    out = {}
    for k, name in enumerate(["w_in", "w_ada", "w_out", "w_s"]):
        out[name] = [None] + list(res[3 * k:3 * k + 3])
    out["w_in"][0], out["w_ada"][0], out["w_out"][0], out["w_s"][0] = g_w_in, res[12], g_w_out, g_w_s
    for k, name in enumerate(["b_ada", "conv_w"] + VECTORS):
        out[name] = list(res[13 + 4 * k:17 + 4 * k])
    order = ["w_ada", "b_ada", "norm_g", "w_in", "conv_w", "conv_b", "conv_ln_g", "conv_ln_b", "sg_ln_g", "sg_ln_b",
             "w_s", "b_s", "w_out", "final_g"]
    outs = [loss_tile[0, 0], grad_x.reshape(x.shape)]
    for kind in range(4):
        outs += [out[n][kind].reshape(given[n][0].shape) for n in order]
    return tuple(outs)
```
